```python
import jax, jax.numpy as jnp
from jax import lax
import numpy as np

D_MODEL = 1024
BATCH = 8
SEQ = 2048
DEPTH = 1
DEC_BATCH = 128
DEC_SEQ = 8
PAST_LEN = 16384
PAGE_SIZE = 128

W_LRU = D_MODEL
N_LRU_BLOCKS = 8
LRU_BLOCK = W_LRU // N_LRU_BLOCKS
CONV_WIDTH = 4
LRU_C = 8.0
RWKV_HEAD = 64
N_RWKV_HEADS = D_MODEL // RWKV_HEAD
W_RWKV = N_RWKV_HEADS * RWKV_HEAD
LORA_W = 64
LORA_A = 64
LORA_G = 160
RWKV_COLS = 3 * W_RWKV + LORA_W + LORA_A + LORA_G
GN_EPS = 64e-5
IN_COLS = 2 * W_LRU + RWKV_COLS + 2 * D_MODEL
N_GROUPS = 4
EXPERTS_PER_GROUP = 8
N_EXPERTS = N_GROUPS * EXPERTS_PER_GROUP
D_EXPERT = 256
TOP_K_INNER = 2
NORM_EPS = 1e-6

kernel_name = 'hawk_rwkv7_hier_moe_step'


def _rmsnorm(x, gain):
    xf = x.astype(jnp.float32)
    return xf * lax.rsqrt(jnp.mean(xf * xf, axis=-1, keepdims=True) + NORM_EPS) * gain


def _causal_conv(xa, buf, w, b):
    T = xa.shape[1]
    xp = jnp.concatenate([buf.astype(jnp.float32), xa.astype(jnp.float32)], axis=1)
    out = b + sum(w[j] * xp[:, j:j + T] for j in range(CONV_WIDTH))
    return out, xp[:, T:]


def _lin_combine(left, right):
    a_l, b_l = left
    a_r, b_r = right
    return a_l * a_r, a_r * b_l + b_r


def _rglru(x, h0, wx, bx, wa, ba, a_param, reset_first):
    B, T, _ = x.shape
    x = x.astype(jnp.float32)
    xb = x.reshape(B, T, N_LRU_BLOCKS, LRU_BLOCK)
    gate_x = jax.nn.sigmoid(jnp.einsum('btnd,nde->btne', xb, wx).reshape(B, T, W_LRU) + bx)
    gate_a = jax.nn.sigmoid(jnp.einsum('btnd,nde->btne', xb, wa).reshape(B, T, W_LRU) + ba)
    log_a = LRU_C * gate_a * jax.nn.log_sigmoid(a_param)
    a = jnp.exp(log_a)
    mult = jnp.sqrt(-jnp.expm1(2.0 * log_a))
    if reset_first:
        mult = mult.at[:, 0].set(1.0)
    bterm = x * gate_x * mult
    a_cum, b_cum = lax.associative_scan(_lin_combine, (a, bterm), axis=1)
    h = b_cum + a_cum * h0.astype(jnp.float32)[:, None]
    return h, h[:, -1]


def _wkv_scan(r, decay, k, v, aa, bb, S0):
    def step(S, inp):
        r_t, w_t, k_t, v_t, a_t, b_t = inp
        Sa = jnp.einsum('bhij,bhj->bhi', S, a_t)
        S = S * w_t[:, :, None, :] + Sa[..., None] * b_t[:, :, None, :] + v_t[..., None] * k_t[:, :, None, :]
        y = jnp.einsum('bhij,bhj->bhi', S, r_t)
        return S, y
    xs = tuple(jnp.moveaxis(z, 1, 0) for z in (r, decay, k, v, aa, bb))
    S, ys = lax.scan(step, S0, xs)
    return jnp.moveaxis(ys, 0, 1), S


def _rwkv7(P, prev, S0, mu, w0, w2, a0, a2, g2, k_k, k_a, r_k, ln_w, ln_b):
    B, T, _ = P.shape
    P = P.astype(jnp.float32)
    P_prev = jnp.concatenate([prev.astype(jnp.float32)[:, None], P[:, :-1]], axis=1)
    Pm = P + (P_prev - P) * mu
    c1, c2, c3 = W_RWKV, 2 * W_RWKV, 3 * W_RWKV
    r, k, v, xw, xa, xg = jnp.split(Pm, [c1, c2, c3, c3 + LORA_W, c3 + LORA_W + LORA_A], axis=-1)
    w_log = -jax.nn.softplus(-(w0 + jnp.tanh(xw) @ w2)) - 0.5
    decay = jnp.exp(-jnp.exp(w_log))
    a = jax.nn.sigmoid(a0 + xa @ a2)
    g = jax.nn.sigmoid(xg) @ g2
    hs = (B, T, N_RWKV_HEADS, RWKV_HEAD)
    kk = (k * k_k).reshape(hs)
    kk = kk / jnp.maximum(jnp.sqrt(jnp.sum(kk * kk, axis=-1, keepdims=True)), 1e-12)
    k = k * (1.0 + (a - 1.0) * k_a)
    rh, kh, vh, ah = r.reshape(hs), k.reshape(hs), v.reshape(hs), a.reshape(hs)
    y, S = _wkv_scan(rh, decay.reshape(hs), kh, vh, -kk, kk * ah, S0.astype(jnp.float32))
    mean = jnp.mean(y, axis=-1, keepdims=True)
    var = jnp.mean(jnp.square(y - mean), axis=-1, keepdims=True)
    yn = ((y - mean) * lax.rsqrt(var + GN_EPS)).reshape(B, T, W_RWKV) * ln_w + ln_b
    bonus = jnp.sum(rh * kh * r_k, axis=-1, keepdims=True) * vh
    out = (yn + bonus.reshape(B, T, W_RWKV)) * g
    return out, P[:, -1], S


def _hier_moe(x, rg, rgb, re, reb, wg, wu, wd):
    B, T, D = x.shape
    t = x.reshape(B * T, D)
    pg = jax.nn.softmax((t @ rg + rgb).astype(jnp.float32), axis=-1)
    p_top, g_idx = lax.top_k(pg, 1)
    le = (t @ re + reb).astype(jnp.float32).reshape(-1, N_GROUPS, EXPERTS_PER_GROUP)
    le_g = jnp.take_along_axis(le, g_idx[:, :, None], axis=1)[:, 0]
    q, e_idx = lax.top_k(jax.nn.softmax(le_g, axis=-1), TOP_K_INNER)
    q = q / jnp.sum(q, axis=-1, keepdims=True)
    w_grp = jnp.sum(jax.nn.one_hot(e_idx, EXPERTS_PER_GROUP) * q[..., None], axis=1)
    comb = (jax.nn.one_hot(g_idx[:, 0], N_GROUPS)[:, :, None] * w_grp[:, None, :] * p_top[:, :, None]).reshape(-1, N_EXPERTS)
    h = jax.nn.silu(jnp.einsum('nd,edf->nef', t, wg)) * jnp.einsum('nd,edf->nef', t, wu)
    h = h * comb[:, :, None]
    y = jnp.einsum('nef,efd->nd', h, wd)
    return y.reshape(B, T, D)


def _layer(x, conv_buf, h0, shift_prev, wkv0, reset_first, norm1, w_in, conv_w, conv_b, lru_wx, lru_bx, lru_wa, lru_ba, lru_a_param, rwkv_mu, rwkv_w0, rwkv_w2, rwkv_a0, rwkv_a2, rwkv_g2, rwkv_k_k, rwkv_k_a, rwkv_r_k, rwkv_ln_w, rwkv_ln_b, w_out, norm2, router_group, router_group_b, router_expert, router_expert_b, exp_gate, exp_up, exp_down):
    xn = _rmsnorm(x, norm1)
    proj = xn @ w_in
    s1 = W_LRU
    s2 = 2 * W_LRU
    s3 = s2 + RWKV_COLS
    s4 = s3 + D_MODEL
    lru_x, lru_y, rw, gate_lru, gate_rwkv = jnp.split(proj, [s1, s2, s3, s4], axis=-1)
    xc, new_conv = _causal_conv(lru_x, conv_buf, conv_w, conv_b)
    h, h_last = _rglru(xc, h0, lru_wx, lru_bx, lru_wa, lru_ba, lru_a_param, reset_first)
    out_lru = h * jax.nn.gelu(lru_y)
    out_rwkv, new_shift, new_wkv = _rwkv7(rw, shift_prev, wkv0, rwkv_mu, rwkv_w0, rwkv_w2, rwkv_a0, rwkv_a2, rwkv_g2, rwkv_k_k, rwkv_k_a, rwkv_r_k, rwkv_ln_w, rwkv_ln_b)
    merged = jax.nn.sigmoid(gate_lru) * out_lru + jax.nn.sigmoid(gate_rwkv) * out_rwkv
    x = x + merged @ w_out
    x = x + _hier_moe(_rmsnorm(x, norm2), router_group, router_group_b, router_expert, router_expert_b, exp_gate, exp_up, exp_down)
    return x, new_conv, h_last, new_shift, new_wkv


def setup_inputs(seed: int = 0) -> dict:
    key = jax.random.key(seed)
    ks = iter(jax.random.split(key, 64))
    f32 = jnp.float32

    def nrm(shape, scale):
        return scale * jax.random.normal(next(ks), shape, f32)

    def uni(shape, lo, hi):
        return jax.random.uniform(next(ks), shape, f32, lo, hi)

    L = DEPTH
    a_lru = uni((L, W_LRU), 0.9, 0.999) ** (1.0 / LRU_C)
    return {
        'x_prompt': nrm((BATCH, SEQ, D_MODEL), 1.0),
        'x_sample': nrm((DEC_BATCH, DEC_SEQ, D_MODEL), 1.0),
        'state_conv': nrm((L, DEC_BATCH, CONV_WIDTH - 1, W_LRU), 1.0),
        'state_lru': nrm((L, DEC_BATCH, W_LRU), 0.5),
        'state_shift': nrm((L, DEC_BATCH, RWKV_COLS), 1.0),
        'state_wkv': nrm((L, DEC_BATCH, N_RWKV_HEADS, RWKV_HEAD, RWKV_HEAD), 0.3),
        'norm1': 1.0 + nrm((L, D_MODEL), 0.05),
        'w_in': nrm((L, D_MODEL, IN_COLS), D_MODEL ** -0.5),
        'conv_w': nrm((L, CONV_WIDTH, W_LRU), CONV_WIDTH ** -0.5),
        'conv_b': nrm((L, W_LRU), 0.02),
        'lru_wx': nrm((L, N_LRU_BLOCKS, LRU_BLOCK, LRU_BLOCK), LRU_BLOCK ** -0.5),
        'lru_bx': nrm((L, W_LRU), 0.02),
        'lru_wa': nrm((L, N_LRU_BLOCKS, LRU_BLOCK, LRU_BLOCK), LRU_BLOCK ** -0.5),
        'lru_ba': nrm((L, W_LRU), 0.02),
        'lru_a_param': jnp.log(a_lru) - jnp.log1p(-a_lru),
        'rwkv_mu': uni((L, RWKV_COLS), 0.0, 1.0),
        'rwkv_w0': uni((L, W_RWKV), -6.0, 1.0),
        'rwkv_w2': nrm((L, LORA_W, W_RWKV), 0.5 * LORA_W ** -0.5),
        'rwkv_a0': nrm((L, W_RWKV), 0.1),
        'rwkv_a2': nrm((L, LORA_A, W_RWKV), 0.5 * LORA_A ** -0.5),
        'rwkv_g2': nrm((L, LORA_G, W_RWKV), LORA_G ** -0.5),
        'rwkv_k_k': 0.85 + nrm((L, W_RWKV), 0.05),
        'rwkv_k_a': 1.0 + nrm((L, W_RWKV), 0.05),
        'rwkv_r_k': nrm((L, N_RWKV_HEADS, RWKV_HEAD), 0.1),
        'rwkv_ln_w': 1.0 + nrm((L, W_RWKV), 0.05),
        'rwkv_ln_b': nrm((L, W_RWKV), 0.02),
        'w_out': nrm((L, D_MODEL, D_MODEL), D_MODEL ** -0.5),
        'norm2': 1.0 + nrm((L, D_MODEL), 0.05),
        'router_group': nrm((L, D_MODEL, N_GROUPS), D_MODEL ** -0.5),
        'router_group_b': nrm((L, N_GROUPS), 0.01),
        'router_expert': nrm((L, D_MODEL, N_EXPERTS), D_MODEL ** -0.5),
        'router_expert_b': nrm((L, N_EXPERTS), 0.01),
        'exp_gate': nrm((L, N_EXPERTS, D_MODEL, D_EXPERT), D_MODEL ** -0.5),
        'exp_up': nrm((L, N_EXPERTS, D_MODEL, D_EXPERT), D_MODEL ** -0.5),
        'exp_down': nrm((L, N_EXPERTS, D_EXPERT, D_MODEL), D_EXPERT ** -0.5),
        'final_norm': 1.0 + nrm((D_MODEL,), 0.05),
    }


def reference(x_prompt, x_sample, state_conv, state_lru, state_shift, state_wkv, norm1, w_in, conv_w, conv_b, lru_wx, lru_bx, lru_wa, lru_ba, lru_a_param, rwkv_mu, rwkv_w0, rwkv_w2, rwkv_a0, rwkv_a2, rwkv_g2, rwkv_k_k, rwkv_k_a, rwkv_r_k, rwkv_ln_w, rwkv_ln_b, w_out, norm2, router_group, router_group_b, router_expert, router_expert_b, exp_gate, exp_up, exp_down, final_norm):
    f32 = jnp.float32
    hp = x_prompt
    hs = x_sample
    conv_p, lru_p, shift_p, wkv_p = [], [], [], []
    conv_s, lru_s, shift_s, wkv_s = [], [], [], []
    for l in range(DEPTH):
        lp = (norm1[l], w_in[l], conv_w[l], conv_b[l], lru_wx[l], lru_bx[l], lru_wa[l], lru_ba[l], lru_a_param[l], rwkv_mu[l], rwkv_w0[l], rwkv_w2[l], rwkv_a0[l], rwkv_a2[l], rwkv_g2[l], rwkv_k_k[l], rwkv_k_a[l], rwkv_r_k[l], rwkv_ln_w[l], rwkv_ln_b[l], w_out[l], norm2[l], router_group[l], router_group_b[l], router_expert[l], router_expert_b[l], exp_gate[l], exp_up[l], exp_down[l])
        hp, c, h, s, w = _layer(hp, jnp.zeros((BATCH, CONV_WIDTH - 1, W_LRU), f32), jnp.zeros((BATCH, W_LRU), f32), jnp.zeros((BATCH, RWKV_COLS), f32), jnp.zeros((BATCH, N_RWKV_HEADS, RWKV_HEAD, RWKV_HEAD), f32), True, *lp)
        conv_p.append(c)
        lru_p.append(h)
        shift_p.append(s)
        wkv_p.append(w)
        hs, c, h, s, w = _layer(hs, state_conv[l], state_lru[l], state_shift[l], state_wkv[l], False, *lp)
        conv_s.append(c)
        lru_s.append(h)
        shift_s.append(s)
        wkv_s.append(w)
    y_prompt = _rmsnorm(hp, final_norm)
    y_sample = _rmsnorm(hs, final_norm)
    return (y_prompt, y_sample, jnp.stack(conv_p), jnp.stack(lru_p), jnp.stack(shift_p), jnp.stack(wkv_p), jnp.stack(conv_s), jnp.stack(lru_s), jnp.stack(shift_s), jnp.stack(wkv_s))
```

```python
import functools

import jax
import jax.numpy as jnp
from jax import lax
from jax.experimental import pallas as pl
from jax.experimental.pallas import tpu as pltpu

f32 = jnp.float32
bf16 = jnp.bfloat16

D = 1024
HEADS = 16
HD = 64
LANES = 128
SUBLANES = 8
LORA_W = 64
LORA_A = 64
LORA_G = 160
LORA = LORA_W + LORA_A + LORA_G
LORA_PAD = 512
N_GROUPS = 4
EPG = 8
N_EXPERTS = N_GROUPS * EPG
D_EXPERT = 256
LRU_C = 8.0
GN_EPS = 64e-5
NORM_EPS = 1e-6
PROJ_COLS = 7 * D + LORA_PAD
COL_LRU_X, COL_LRU_Y, COL_R, COL_K, COL_V, COL_GL, COL_GR = range(7)
COL_LORA = 7 * D // LORA_PAD
ROUTER_ROWS = 48


def _inproj_kernel(x_ref, g_ref, w_ref, o_ref, xn_ref):
    @pl.when(pl.program_id(1) == 0)
    def _():
        x = x_ref[...]
        ms = jnp.mean(x * x, axis=-1, keepdims=True)
        xn_ref[...] = (x * lax.rsqrt(ms + NORM_EPS) * g_ref[...]).astype(bf16)

    o_ref[...] = jnp.dot(xn_ref[...], w_ref[...], preferred_element_type=f32)


def _inproj(x, gain, w, tm=1024, tn=1280):
    n = x.shape[0]
    return pl.pallas_call(
        _inproj_kernel,
        grid=(n // tm, PROJ_COLS // tn),
        in_specs=[
            pl.BlockSpec((tm, D), lambda i, j: (i, 0)),
            pl.BlockSpec((1, D), lambda i, j: (0, 0)),
            pl.BlockSpec((D, tn), lambda i, j: (0, j)),
        ],
        out_specs=pl.BlockSpec((tm, tn), lambda i, j: (i, j)),
        out_shape=jax.ShapeDtypeStruct((n, PROJ_COLS), f32),
        scratch_shapes=[pltpu.VMEM((tm, D), bf16)],
        compiler_params=pltpu.CompilerParams(dimension_semantics=("arbitrary", "arbitrary"),
                                             vmem_limit_bytes=48 * 1024 * 1024),
        name="inproj",
    )(x, gain, w)


def _lru_kernel(x_ref, y_ref, gl_ref, cinit_ref, h0_ref, cw_ref, cb_ref, wxa_ref, bx_ref, ba_ref, ap_ref,
                o_ref, hl_ref, xs_ref, a_ref, b_ref, h_ref, *, batch, reset_first, chunk):
    rows = x_ref.shape[0]
    nt = rows // batch
    hist = 3 * batch
    pid = pl.program_id(0)

    @pl.when(pid == 0)
    def _():
        xs_ref[0:hist, :] = cinit_ref[...]
        h_ref[...] = h0_ref[...]

    xs_ref[hist:hist + rows, :] = x_ref[...]
    logsig = -jax.nn.softplus(-ap_ref[...])

    def gates(c, _):
        r0 = pl.multiple_of(c * chunk, chunk)
        xc = (cb_ref[...]
              + cw_ref[3:4, :] * xs_ref[pl.ds(pl.multiple_of(r0 + hist, SUBLANES), chunk), :]
              + cw_ref[2:3, :] * xs_ref[pl.ds(pl.multiple_of(r0 + 2 * batch, SUBLANES), chunk), :]
              + cw_ref[1:2, :] * xs_ref[pl.ds(pl.multiple_of(r0 + batch, SUBLANES), chunk), :]
              + cw_ref[0:1, :] * xs_ref[pl.ds(r0, chunk), :])
        if reset_first:
            grow = lax.broadcasted_iota(jnp.int32, (chunk, LANES), 0) + (r0 + pid * rows)
            first = grow < batch
        for n in range(D // LANES):
            sl = slice(n * LANES, (n + 1) * LANES)
            xn = xc[:, sl]
            g2 = jnp.dot(xn.astype(bf16), wxa_ref[n], preferred_element_type=f32)
            gate_x = jax.nn.sigmoid(g2[:, :LANES] + bx_ref[:, sl])
            gate_a = jax.nn.sigmoid(g2[:, LANES:] + ba_ref[:, sl])
            log_a = LRU_C * gate_a * logsig[:, sl]
            a = jnp.exp(log_a)
            mult = jnp.sqrt(-jnp.tanh(log_a) * (a * a + 1.0))
            if reset_first:
                mult = jnp.where(first, 1.0, mult)
            a_ref[pl.ds(r0, chunk), sl] = a
            b_ref[pl.ds(r0, chunk), sl] = xn * gate_x * mult
        return 0

    lax.fori_loop(0, rows // chunk, gates, 0)

    def scan(t, h):
        r0 = pl.multiple_of(t * batch, batch)
        h = a_ref[pl.ds(r0, batch), :] * h + b_ref[pl.ds(r0, batch), :]
        b_ref[pl.ds(r0, batch), :] = h
        return h

    h = lax.fori_loop(0, nt, scan, h_ref[...], unroll=(8 if nt >= 8 and batch == SUBLANES else 1))
    h_ref[...] = h
    hl_ref[...] = h
    xs_ref[0:hist, :] = xs_ref[rows:rows + hist, :]

    def outp(c, _):
        r0 = pl.multiple_of(c * chunk, chunk)
        o_ref[pl.ds(r0, chunk), :] = (b_ref[pl.ds(r0, chunk), :] * jax.nn.gelu(y_ref[pl.ds(r0, chunk), :])
                                      * jax.nn.sigmoid(gl_ref[pl.ds(r0, chunk), :]))
        return 0

    lax.fori_loop(0, rows // chunk, outp, 0)


def _lru(proj, row_off, n_rows, batch, rows_blk, reset_first, conv_init, h0, cw, cb, wxa, bx, ba, ap):
    ob = row_off // rows_blk
    full = lambda shape: pl.BlockSpec(shape, lambda i: tuple(0 for _ in shape))
    kern = functools.partial(_lru_kernel, batch=batch, reset_first=reset_first, chunk=128)
    return pl.pallas_call(
        kern,
        grid=(n_rows // rows_blk,),
        in_specs=[
            pl.BlockSpec((rows_blk, D), lambda i: (ob + i, COL_LRU_X)),
            pl.BlockSpec((rows_blk, D), lambda i: (ob + i, COL_LRU_Y)),
            pl.BlockSpec((rows_blk, D), lambda i: (ob + i, COL_GL)),
            full((3 * batch, D)), full((batch, D)), full((4, D)), full((1, D)),
            full((D // LANES, LANES, 2 * LANES)), full((1, D)), full((1, D)), full((1, D)),
        ],
        out_specs=[pl.BlockSpec((rows_blk, D), lambda i: (i, 0)), full((batch, D))],
        out_shape=[jax.ShapeDtypeStruct((n_rows, D), f32), jax.ShapeDtypeStruct((batch, D), f32)],
        scratch_shapes=[pltpu.VMEM((rows_blk + 3 * batch, D), f32), pltpu.VMEM((rows_blk, D), f32),
                        pltpu.VMEM((rows_blk, D), f32), pltpu.VMEM((batch, D), f32)],
        compiler_params=pltpu.CompilerParams(dimension_semantics=("arbitrary",),
                                             vmem_limit_bytes=48 * 1024 * 1024),
        name="lru",
    )(proj, proj, proj, conv_init, h0, cw, cb, wxa, bx, ba, ap)


def _rwkv_pre_kernel(r_ref, k_ref, v_ref, l_ref, sinit_ref, mu_ref, w0_ref, a0_ref, w2_ref, a2_ref, g2_ref,
                     ro_ref, wo_ref, ko_ref, vo_ref, ao_ref, go_ref, ps_ref, *, batch, chunk):
    rows = r_ref.shape[0]

    @pl.when(pl.program_id(0) == 0)
    def _():
        ps_ref[0:batch, :] = sinit_ref[...]

    ps_ref[batch:batch + rows, 0:D] = r_ref[...]
    ps_ref[batch:batch + rows, D:2 * D] = k_ref[...]
    ps_ref[batch:batch + rows, 2 * D:3 * D] = v_ref[...]
    ps_ref[batch:batch + rows, 3 * D:3 * D + LORA_PAD] = l_ref[...]

    def body(c, _):
        r0 = pl.multiple_of(c * chunk, chunk)

        def mixed(lo, hi):
            cur = ps_ref[pl.ds(pl.multiple_of(r0 + batch, SUBLANES), chunk), lo:hi]
            prev = ps_ref[pl.ds(r0, chunk), lo:hi]
            return cur + (prev - cur) * mu_ref[:, lo:hi]

        ro_ref[pl.ds(r0, chunk), :] = mixed(0, D)
        ko_ref[pl.ds(r0, chunk), :] = mixed(D, 2 * D)
        vo_ref[pl.ds(r0, chunk), :] = mixed(2 * D, 3 * D)
        lm = mixed(3 * D, 3 * D + LORA_PAD)
        xwa = lm[:, 0:LANES]
        xg = lm[:, LANES:3 * LANES]
        lw = jnp.dot(jnp.tanh(xwa).astype(bf16), w2_ref[...], preferred_element_type=f32)
        w_log = -jax.nn.softplus(-(w0_ref[...] + lw)) - 0.5
        wo_ref[pl.ds(r0, chunk), :] = jnp.exp(-jnp.exp(w_log))
        la = jnp.dot(xwa.astype(bf16), a2_ref[...], preferred_element_type=f32)
        ao_ref[pl.ds(r0, chunk), :] = jax.nn.sigmoid(a0_ref[...] + la)
        go_ref[pl.ds(r0, chunk), :] = jnp.dot(jax.nn.sigmoid(xg).astype(bf16), g2_ref[...],
                                              preferred_element_type=f32)
        return 0

    lax.fori_loop(0, rows // chunk, body, 0)
    ps_ref[0:batch, :] = ps_ref[rows:rows + batch, :]


def _rwkv_pre(proj, row_off, n_rows, batch, rows_blk, shift_init, mu, w0, a0, w2p, a2p, g2p):
    ob = row_off // rows_blk
    full = lambda shape: pl.BlockSpec(shape, lambda i: tuple(0 for _ in shape))
    width = 3 * D + LORA_PAD
    kern = functools.partial(_rwkv_pre_kernel, batch=batch, chunk=128)
    out = jax.ShapeDtypeStruct((n_rows, D), f32)
    return pl.pallas_call(
        kern,
        grid=(n_rows // rows_blk,),
        in_specs=[
            pl.BlockSpec((rows_blk, D), lambda i: (ob + i, COL_R)),
            pl.BlockSpec((rows_blk, D), lambda i: (ob + i, COL_K)),
            pl.BlockSpec((rows_blk, D), lambda i: (ob + i, COL_V)),
            pl.BlockSpec((rows_blk, LORA_PAD), lambda i: (ob + i, COL_LORA)),
            full((batch, width)), full((1, width)), full((1, D)), full((1, D)),
            full((LANES, D)), full((LANES, D)), full((2 * LANES, D)),
        ],
        out_specs=[pl.BlockSpec((rows_blk, D), lambda i: (i, 0))] * 6,
        out_shape=[out] * 6,
        scratch_shapes=[pltpu.VMEM((rows_blk + batch, width), f32)],
        compiler_params=pltpu.CompilerParams(dimension_semantics=("arbitrary",),
                                             vmem_limit_bytes=56 * 1024 * 1024),
        name="rwkv_pre",
    )(proj, proj, proj, proj, shift_init, mu, w0, a0, w2p, a2p, g2p)


TILE_ALPHA, TILE_BETA, TILE_W, TILE_KP, TILE_WR, TILE_V = range(6)
ROW_BR, ROW_KR, ROW_BONUS = range(3)
CONST_KK, CONST_KA, CONST_RK, CONST_LNW, CONST_LNB = range(5)


def _wkv_prep(tiles_ref, rows_ref, t, r, w, k, v, a, const):
    kk = k * const(CONST_KK)
    nrm = jnp.sqrt(jnp.sum(kk * kk, axis=0, keepdims=True))
    kk = kk / jnp.maximum(nrm, 1e-12)
    beta = kk * a
    kp = k * (1.0 + (a - 1.0) * const(CONST_KA))
    tiles_ref[t, TILE_ALPHA] = -kk
    tiles_ref[t, TILE_BETA] = beta
    tiles_ref[t, TILE_W] = w
    tiles_ref[t, TILE_KP] = kp
    tiles_ref[t, TILE_WR] = w * r
    tiles_ref[t, TILE_V] = v
    rows_ref[t, ROW_BR:ROW_BR + 1, :] = jnp.sum(beta * r, axis=0, keepdims=True)
    rows_ref[t, ROW_KR:ROW_KR + 1, :] = jnp.sum(kp * r, axis=0, keepdims=True)
    rows_ref[t, ROW_BONUS:ROW_BONUS + 1, :] = jnp.sum(r * kp * const(CONST_RK), axis=0, keepdims=True)


def _wkv_step(s_ref, base, tiles_ref, rows_ref, y_ref, t):
    br = rows_ref[t, ROW_BR:ROW_BR + 1, :]
    kr = rows_ref[t, ROW_KR:ROW_KR + 1, :]

    def body(i, _):
        r0 = pl.multiple_of(base + i * HD, HD)
        s = s_ref[pl.ds(r0, HD), :]
        sa = jnp.sum(s * tiles_ref[t, TILE_ALPHA], axis=0, keepdims=True)
        y0 = jnp.sum(s * tiles_ref[t, TILE_WR], axis=0, keepdims=True)
        vi = tiles_ref[t, TILE_V, pl.ds(i, 1), :]
        s_ref[pl.ds(r0, HD), :] = s * tiles_ref[t, TILE_W] + sa * tiles_ref[t, TILE_BETA] + vi * tiles_ref[t, TILE_KP]
        y_ref[t, pl.ds(i, 1), :] = y0 + sa * br + vi * kr
        return 0

    lax.fori_loop(0, HD, body, 0, unroll=2)


def _wkv_post(tiles_ref, rows_ref, y_ref, t, const):
    y = y_ref[t]
    mean = jnp.sum(y, axis=0, keepdims=True) * (1.0 / HD)
    d = y - mean
    var = jnp.sum(d * d, axis=0, keepdims=True) * (1.0 / HD)
    yn = d * lax.rsqrt(var + GN_EPS)
    return yn * const(CONST_LNW) + const(CONST_LNB) + rows_ref[t, ROW_BONUS:ROW_BONUS + 1, :] * tiles_ref[t, TILE_V]


def _wkv_prompt_kernel(r_ref, w_ref, k_ref, v_ref, a_ref, c_ref, o_ref, sfin_ref,
                       s_ref, tiles_ref, rows_ref, y_ref, *, steps):
    pid = pl.program_id(0)
    half = LANES // 2
    lane = lax.broadcasted_iota(jnp.int32, (HD, LANES), 1)
    const = lambda i: c_ref[i]

    @pl.when(pid == 0)
    def _():
        s_ref[...] = jnp.zeros_like(s_ref)

    def load_pair(ref, tp):
        zz = jnp.concatenate(
            [ref[pl.ds(pl.multiple_of(tp * 2 * SUBLANES + tt * SUBLANES, SUBLANES), SUBLANES),
                 hp * LANES:(hp + 1) * LANES] for tt in range(2) for hp in range(D // LANES)], axis=0)
        tr = zz.T
        top, bot = tr[0:HD], tr[HD:2 * HD]
        d0 = jnp.where(lane < half, top, pltpu.roll(bot, half, axis=1))
        d1 = jnp.where(lane < half, pltpu.roll(top, half, axis=1), bot)
        return d0, d1

    def prep(tp, _):
        pairs = [load_pair(ref, tp) for ref in (r_ref, w_ref, k_ref, v_ref, a_ref)]
        for tt in range(2):
            _wkv_prep(tiles_ref, rows_ref, 2 * tp + tt, *[p[tt] for p in pairs], const)
        return 0

    lax.fori_loop(0, steps // 2, prep, 0)

    def step(t, _):
        _wkv_step(s_ref, 0, tiles_ref, rows_ref, y_ref, t)
        return 0

    lax.fori_loop(0, steps, step, 0)

    def post(tp, _):
        o0 = _wkv_post(tiles_ref, rows_ref, y_ref, 2 * tp, const)
        o1 = _wkv_post(tiles_ref, rows_ref, y_ref, 2 * tp + 1, const)
        top = jnp.where(lane < half, o0, pltpu.roll(o1, half, axis=1))
        bot = jnp.where(lane < half, pltpu.roll(o0, half, axis=1), o1)
        zz = jnp.concatenate([top, bot], axis=0).T
        for tt in range(2):
            for hp in range(D // LANES):
                row = pl.multiple_of(tp * 2 * SUBLANES + tt * SUBLANES, SUBLANES)
                src = tt * HD + hp * SUBLANES
                o_ref[pl.ds(row, SUBLANES), hp * LANES:(hp + 1) * LANES] = zz[src:src + SUBLANES, :]
        return 0

    lax.fori_loop(0, steps // 2, post, 0)

    @pl.when(pid == pl.num_programs(0) - 1)
    def _():
        def fin(c, _):
            c0 = pl.multiple_of(c * LANES, LANES)
            sfin_ref[:, pl.ds(c0, LANES)] = s_ref[pl.ds(c0, LANES), :].T
            return 0

        lax.fori_loop(0, HD * HD // LANES, fin, 0)


def _wkv_prompt(r, w, k, v, a, consts, n_rows, steps=16):
    rows_blk = steps * SUBLANES
    blk = pl.BlockSpec((rows_blk, D), lambda i: (i, 0))
    kern = functools.partial(_wkv_prompt_kernel, steps=steps)
    return pl.pallas_call(
        kern,
        grid=(n_rows // rows_blk,),
        in_specs=[blk] * 5 + [pl.BlockSpec((5, HD, LANES), lambda i: (0, 0, 0))],
        out_specs=[blk, pl.BlockSpec((LANES, HD * HD), lambda i: (0, 0))],
        out_shape=[jax.ShapeDtypeStruct((n_rows, D), f32), jax.ShapeDtypeStruct((LANES, HD * HD), f32)],
        scratch_shapes=[pltpu.VMEM((HD * HD, LANES), f32), pltpu.VMEM((steps, 6, HD, LANES), f32),
                        pltpu.VMEM((steps, SUBLANES, LANES), f32), pltpu.VMEM((steps, HD, LANES), f32)],
        compiler_params=pltpu.CompilerParams(dimension_semantics=("arbitrary",),
                                             vmem_limit_bytes=40 * 1024 * 1024),
        name="wkv_prompt",
    )(r, w, k, v, a, consts)


def _wkv_sample_kernel(r_ref, w_ref, k_ref, v_ref, a_ref, c_ref, st_ref, o_ref, sfin_ref,
                       s_ref, tiles_ref, rows_ref, y_ref, *, steps, batch):
    nchunk = 2 * HD * HD // LANES

    def init(c, _):
        c0 = pl.multiple_of(c * LANES, LANES)
        s_ref[pl.ds(c0, LANES), :] = st_ref[:, pl.ds(c0, LANES)].T
        return 0

    lax.fori_loop(0, nchunk, init, 0)

    for h2 in range(2):
        const = lambda i, h2=h2: c_ref[0, i, h2]

        def prep(t, _, h2=h2, const=const):
            r0 = pl.multiple_of(t * batch, batch)
            vals = [ref[pl.ds(r0, batch), :].T[h2 * HD:(h2 + 1) * HD] for ref in (r_ref, w_ref, k_ref, v_ref, a_ref)]
            _wkv_prep(tiles_ref, rows_ref, t, *vals, const)
            return 0

        lax.fori_loop(0, steps, prep, 0)

        def step(t, _, h2=h2):
            _wkv_step(s_ref, h2 * HD * HD, tiles_ref, rows_ref, y_ref.at[h2], t)
            return 0

        lax.fori_loop(0, steps, step, 0)

        def post(t, _, h2=h2, const=const):
            y_ref[h2, t] = _wkv_post(tiles_ref, rows_ref, y_ref.at[h2], t, const)
            return 0

        lax.fori_loop(0, steps, post, 0)

    def outp(t, _):
        r0 = pl.multiple_of(t * batch, batch)
        o_ref[pl.ds(r0, batch), :] = jnp.concatenate([y_ref[0, t], y_ref[1, t]], axis=0).T
        return 0

    lax.fori_loop(0, steps, outp, 0)

    def fin(c, _):
        c0 = pl.multiple_of(c * LANES, LANES)
        sfin_ref[:, pl.ds(c0, LANES)] = s_ref[pl.ds(c0, LANES), :].T
        return 0

    lax.fori_loop(0, nchunk, fin, 0)


def _wkv_sample(r, w, k, v, a, consts, state, steps, batch):
    n_rows = steps * batch
    blk = pl.BlockSpec((n_rows, LANES), lambda i: (0, i))
    sblk = pl.BlockSpec((batch, 2 * HD * HD), lambda i: (0, i))
    kern = functools.partial(_wkv_sample_kernel, steps=steps, batch=batch)
    return pl.pallas_call(
        kern,
        grid=(HEADS // 2,),
        in_specs=[blk] * 5 + [pl.BlockSpec((1, 5, 2, HD, LANES), lambda i: (i, 0, 0, 0, 0)), sblk],
        out_specs=[blk, sblk],
        out_shape=[jax.ShapeDtypeStruct((n_rows, D), f32), jax.ShapeDtypeStruct((batch, HEADS * HD * HD), f32)],
        scratch_shapes=[pltpu.VMEM((2 * HD * HD, LANES), f32), pltpu.VMEM((steps, 6, HD, LANES), f32),
                        pltpu.VMEM((steps, SUBLANES, LANES), f32), pltpu.VMEM((2, steps, HD, LANES), f32)],
        compiler_params=pltpu.CompilerParams(dimension_semantics=("arbitrary",),
                                             vmem_limit_bytes=48 * 1024 * 1024),
        name="wkv_sample",
    )(r, w, k, v, a, consts, state)


def _outproj_kernel(ml_ref, op_ref, g_ref, gr_ref, x_ref, wo_ref, n2_ref, rh_ref, rl_ref, rb_ref,
                    x1_ref, t_ref, comb_ref):
    tm = x_ref.shape[0]
    merged = ml_ref[...] + jax.nn.sigmoid(gr_ref[...]) * (op_ref[...] * g_ref[...])
    x1 = x_ref[...] + jnp.dot(merged.astype(bf16), wo_ref[...], preferred_element_type=f32)
    x1_ref[...] = x1
    t = x1 * lax.rsqrt(jnp.mean(x1 * x1, axis=-1, keepdims=True) + NORM_EPS) * n2_ref[...]
    th = t.astype(bf16)
    t_ref[...] = th
    tl = (t - th.astype(f32)).astype(bf16)
    nt_dims = (((1,), (1,)), ((), ()))
    lg = (lax.dot_general(rh_ref[...], th, nt_dims, preferred_element_type=f32)
          + lax.dot_general(rh_ref[...], tl, nt_dims, preferred_element_type=f32)
          + lax.dot_general(rl_ref[...], th, nt_dims, preferred_element_type=f32)) + rb_ref[...]
    row = lax.broadcasted_iota(jnp.int32, (EPG, tm), 0).astype(f32)
    neg = jnp.float32(-jnp.inf)
    glog = jnp.where(row < N_GROUPS, lg[0:EPG], neg)
    ge = jnp.exp(glog - jnp.max(glog, axis=0, keepdims=True))
    pg = ge / jnp.sum(ge, axis=0, keepdims=True)
    p_top = jnp.max(pg, axis=0, keepdims=True)
    g_idx = jnp.min(jnp.where(pg == p_top, row, EPG), axis=0, keepdims=True)
    le = jnp.zeros((EPG, tm), f32)
    for g in range(N_GROUPS):
        le = jnp.where(g_idx == g, lg[EPG * (g + 1):EPG * (g + 2)], le)
    qe = jnp.exp(le - jnp.max(le, axis=0, keepdims=True))
    q = qe / jnp.sum(qe, axis=0, keepdims=True)
    q1 = jnp.max(q, axis=0, keepdims=True)
    i1 = jnp.min(jnp.where(q == q1, row, EPG), axis=0, keepdims=True)
    qm = jnp.where(row == i1, -1.0, q)
    q2 = jnp.max(qm, axis=0, keepdims=True)
    i2 = jnp.min(jnp.where(qm == q2, row, EPG), axis=0, keepdims=True)
    qs = q1 + q2
    w_grp = jnp.where(row == i1, q1 / qs, 0.0) + jnp.where(row == i2, q2 / qs, 0.0)
    wp = w_grp * p_top
    zero = jnp.zeros((EPG, tm), f32)
    comb_t = jnp.concatenate([jnp.where(g_idx == g, wp, zero) for g in range(N_GROUPS)]
                             + [jnp.zeros((LANES - N_EXPERTS, tm), f32)], axis=0)
    for c in range(tm // LANES):
        comb_ref[c * LANES:(c + 1) * LANES, :] = comb_t[:, c * LANES:(c + 1) * LANES].T


def _outproj(ml, op, g, proj, x, wo, n2, rh, rl, rb, tm=512):
    n = x.shape[0]
    blk = pl.BlockSpec((tm, D), lambda i: (i, 0))
    full = lambda shape: pl.BlockSpec(shape, lambda i: tuple(0 for _ in shape))
    return pl.pallas_call(
        _outproj_kernel,
        grid=(n // tm,),
        in_specs=[blk, blk, blk, pl.BlockSpec((tm, D), lambda i: (i, COL_GR)), blk,
                  full((D, D)), full((1, D)), full((ROUTER_ROWS, D)), full((ROUTER_ROWS, D)), full((ROUTER_ROWS, 1))],
        out_specs=[blk, blk, pl.BlockSpec((tm, LANES), lambda i: (i, 0))],
        out_shape=[jax.ShapeDtypeStruct((n, D), f32), jax.ShapeDtypeStruct((n, D), bf16),
                   jax.ShapeDtypeStruct((n, LANES), f32)],
        compiler_params=pltpu.CompilerParams(dimension_semantics=("arbitrary",),
                                             vmem_limit_bytes=48 * 1024 * 1024),
        name="outproj_router",
    )(ml, op, g, proj, x, wo, n2, rh, rl, rb)


def _moe_kernel(t_ref, comb_ref, wgu_ref, wd_ref, x1_ref, fn_ref, o_ref, acc_ref):
    e = pl.program_id(1)
    tm = t_ref.shape[0]

    @pl.when(e == 0)
    def _():
        acc_ref[...] = jnp.zeros_like(acc_ref)

    gu = jnp.dot(t_ref[...], wgu_ref[0], preferred_element_type=f32)
    lane = lax.broadcasted_iota(jnp.int32, (tm, LANES), 1)
    c = jnp.sum(jnp.where(lane == e, comb_ref[...], 0.0), axis=1, keepdims=True)
    h = jax.nn.silu(gu[:, :D_EXPERT]) * gu[:, D_EXPERT:] * c
    acc_ref[...] += jnp.dot(h.astype(bf16), wd_ref[0], preferred_element_type=f32)

    @pl.when(e == pl.num_programs(1) - 1)
    def _():
        xo = x1_ref[...] + acc_ref[...]
        o_ref[...] = xo * lax.rsqrt(jnp.mean(xo * xo, axis=-1, keepdims=True) + NORM_EPS) * fn_ref[...]


def _moe(t, comb, wgu, wd, x1, fn, tm=1024):
    n = t.shape[0]
    return pl.pallas_call(
        _moe_kernel,
        grid=(n // tm, N_EXPERTS),
        in_specs=[
            pl.BlockSpec((tm, D), lambda i, e: (i, 0)),
            pl.BlockSpec((tm, LANES), lambda i, e: (i, 0)),
            pl.BlockSpec((1, D, 2 * D_EXPERT), lambda i, e: (e, 0, 0)),
            pl.BlockSpec((1, D_EXPERT, D), lambda i, e: (e, 0, 0)),
            pl.BlockSpec((tm, D), lambda i, e: (i, 0)),
            pl.BlockSpec((1, D), lambda i, e: (0, 0)),
        ],
        out_specs=pl.BlockSpec((tm, D), lambda i, e: (i, 0)),
        out_shape=jax.ShapeDtypeStruct((n, D), f32),
        scratch_shapes=[pltpu.VMEM((tm, D), f32)],
        compiler_params=pltpu.CompilerParams(dimension_semantics=("arbitrary", "arbitrary"),
                                             vmem_limit_bytes=48 * 1024 * 1024),
        name="moe",
    )(t, comb, wgu, wd, x1, fn)


def _chain_tiles_prompt(vec, batch):
    t = vec.reshape(HEADS // 2, 2, HD)
    t = jnp.transpose(t, (2, 1, 0))
    return jnp.broadcast_to(t[..., None], (HD, 2, HEADS // 2, batch)).reshape(HD, LANES)


def _chain_tiles_sample(vec):
    t = vec.reshape(HEADS // 2, 2, HD)
    return jnp.broadcast_to(t[..., None], (HEADS // 2, 2, HD, LANES))


def kernel(x_prompt, x_sample, state_conv, state_lru, state_shift, state_wkv, norm1, w_in, conv_w, conv_b, lru_wx, lru_bx, lru_wa, lru_ba, lru_a_param, rwkv_mu, rwkv_w0, rwkv_w2, rwkv_a0, rwkv_a2, rwkv_g2, rwkv_k_k, rwkv_k_a, rwkv_r_k, rwkv_ln_w, rwkv_ln_b, w_out, norm2, router_group, router_group_b, router_expert, router_expert_b, exp_gate, exp_up, exp_down, final_norm):
    bp, tp, _ = x_prompt.shape
    bs, ts, _ = x_sample.shape
    assert norm1.shape[0] == 1 and bp * HEADS == LANES and bs == LANES and tp % 64 == 0
    n_p, n_s = bp * tp, bs * ts
    c_rw = 2 * D
    c_lora = c_rw + 3 * D
    c_gl = c_lora + LORA

    w = w_in[0]
    w_all = jnp.concatenate([w[:, :c_lora], w[:, c_gl:], w[:, c_lora:c_gl],
                             jnp.zeros((D, LORA_PAD - LORA), f32)], axis=1).astype(bf16)
    mu = rwkv_mu[0]
    mu_all = jnp.concatenate([mu[:3 * D], mu[3 * D:], jnp.zeros((LORA_PAD - LORA,), f32)])[None]
    w2p = jnp.concatenate([rwkv_w2[0], jnp.zeros((LANES - LORA_W, D), f32)]).astype(bf16)
    a2p = jnp.concatenate([jnp.zeros((LORA_W, D), f32), rwkv_a2[0]]).astype(bf16)
    g2p = jnp.concatenate([rwkv_g2[0], jnp.zeros((2 * LANES - LORA_G, D), f32)]).astype(bf16)
    wxa = jnp.concatenate([lru_wx[0], lru_wa[0]], axis=-1).astype(bf16)
    row = lambda v: v.reshape(1, -1)
    rk = rwkv_r_k[0].reshape(D)
    cvecs = (rwkv_k_k[0], rwkv_k_a[0], rk, rwkv_ln_w[0], rwkv_ln_b[0])
    consts_p = jnp.stack([_chain_tiles_prompt(v, bp) for v in cvecs])
    consts_s = jnp.stack([_chain_tiles_sample(v) for v in cvecs], axis=1)
    rw = jnp.zeros((ROUTER_ROWS, D), f32)
    rw = rw.at[0:N_GROUPS].set(router_group[0].T).at[EPG:EPG + N_EXPERTS].set(router_expert[0].T)
    rh = rw.astype(bf16)
    rl = (rw - rh.astype(f32)).astype(bf16)
    rb = jnp.zeros((ROUTER_ROWS, 1), f32)
    rb = rb.at[0:N_GROUPS, 0].set(router_group_b[0]).at[EPG:EPG + N_EXPERTS, 0].set(router_expert_b[0])
    wgu = jnp.concatenate([exp_gate[0], exp_up[0]], axis=-1).astype(bf16)
    wd = exp_down[0].astype(bf16)
    wo = w_out[0].astype(bf16)

    x = jnp.concatenate([jnp.transpose(x_prompt, (1, 0, 2)).reshape(n_p, D),
                         jnp.transpose(x_sample, (1, 0, 2)).reshape(n_s, D)])
    proj = _inproj(x, row(norm1[0]), w_all)

    lru_args = (conv_w[0], row(conv_b[0]), wxa, row(lru_bx[0]), row(lru_ba[0]), row(lru_a_param[0]))
    conv_init_s = jnp.transpose(state_conv[0], (1, 0, 2)).reshape(3 * bs, D)
    ml_p, hl_p = _lru(proj, 0, n_p, bp, 512, True, jnp.zeros((3 * bp, D), f32), jnp.zeros((bp, D), f32), *lru_args)
    ml_s, hl_s = _lru(proj, n_p, n_s, bs, 512, False, conv_init_s, state_lru[0], *lru_args)

    sh = state_shift[0]
    shift_init_s = jnp.concatenate([sh, jnp.zeros((bs, LORA_PAD - LORA), f32)], axis=1)
    pre_args = (mu_all, row(rwkv_w0[0]), row(rwkv_a0[0]), w2p, a2p, g2p)
    r_p, w_p, k_p, v_p, a_p, g_p = _rwkv_pre(proj, 0, n_p, bp, 256, jnp.zeros((bp, 3 * D + LORA_PAD), f32), *pre_args)
    r_s, w_s, k_s, v_s, a_s, g_s = _rwkv_pre(proj, n_p, n_s, bs, 256, shift_init_s, *pre_args)

    op_p, sfin_p = _wkv_prompt(r_p, w_p, k_p, v_p, a_p, consts_p, n_p)
    op_s, sfin_s = _wkv_sample(r_s, w_s, k_s, v_s, a_s, consts_s, state_wkv[0].reshape(bs, HEADS * HD * HD), ts, bs)

    cat = lambda a, b: jnp.concatenate([a, b])
    x1, t, comb = _outproj(cat(ml_p, ml_s), cat(op_p, op_s), cat(g_p, g_s), proj, x, wo, row(norm2[0]), rh, rl, rb)
    y = _moe(t, comb, wgu, wd, x1, row(final_norm))

    y_prompt = jnp.transpose(y[:n_p].reshape(tp, bp, D), (1, 0, 2))
    y_sample = jnp.transpose(y[n_p:].reshape(ts, bs, D), (1, 0, 2))

    def last_rows(lo, hi, n_end, batch, steps):
        return proj[n_end - steps * batch:n_end, lo:hi].reshape(steps, batch, hi - lo)

    conv_p = jnp.transpose(last_rows(0, D, n_p, bp, 3), (1, 0, 2))[None]
    conv_s = jnp.transpose(last_rows(0, D, n_p + n_s, bs, 3), (1, 0, 2))[None]

    def shift_rows(n_end, batch):
        return jnp.concatenate([last_rows(2 * D, 5 * D, n_end, batch, 1)[0],
                                last_rows(7 * D, 7 * D + LORA, n_end, batch, 1)[0]], axis=1)[None]

    wkv_p = jnp.transpose(sfin_p.reshape(2, HEADS // 2, bp, HD, HD), (2, 1, 0, 3, 4)).reshape(1, bp, HEADS, HD, HD)
    wkv_s = sfin_s.reshape(1, bs, HEADS, HD, HD)
    return (y_prompt, y_sample, conv_p, hl_p[None], shift_rows(n_p, bp), wkv_p,
            conv_s, hl_s[None], shift_rows(n_p + n_s, bs), wkv_s)
```

```python
import functools

import jax
import jax.numpy as jnp
from jax import lax
from jax.experimental import pallas as pl
from jax.experimental.pallas import tpu as pltpu

f32 = jnp.float32
bf16 = jnp.bfloat16

D = 1024
HEADS = 16
HD = 64
LANES = 128
SUBLANES = 8
LORA_W = 64
LORA_A = 64
LORA_G = 160
LORA = LORA_W + LORA_A + LORA_G
LORA_PAD = 512
N_GROUPS = 4
EPG = 8
N_EXPERTS = N_GROUPS * EPG
D_EXPERT = 256
LRU_C = 8.0
GN_EPS = 64e-5
NORM_EPS = 1e-6
PROJ_COLS = 7 * D + LORA_PAD
COL_LRU_X, COL_LRU_Y, COL_R, COL_K, COL_V, COL_GL, COL_GR = range(7)
COL_LORA = 7 * D // LORA_PAD
ROUTER_ROWS = 48
MIB = 1024 * 1024


def _full(shape, grid_rank=1):
    zeros = tuple(0 for _ in shape)
    if grid_rank == 1:
        return pl.BlockSpec(shape, lambda i: zeros)
    return pl.BlockSpec(shape, lambda i, j: zeros)


def _inproj_kernel(x_ref, g_ref, w_ref, o_ref, xn_ref):
    @pl.when(pl.program_id(1) == 0)
    def _():
        x = x_ref[...]
        ms = jnp.mean(x * x, axis=-1, keepdims=True)
        xn_ref[...] = (x * lax.rsqrt(ms + NORM_EPS) * g_ref[...]).astype(bf16)

    o_ref[...] = jnp.dot(xn_ref[...], w_ref[...], preferred_element_type=f32)


def _inproj(x, gain, w, tm=1024, tn=1280):
    n = x.shape[0]
    return pl.pallas_call(
        _inproj_kernel,
        grid=(n // tm, PROJ_COLS // tn),
        in_specs=[
            pl.BlockSpec((tm, D), lambda i, j: (i, 0)),
            pl.BlockSpec((1, D), lambda i, j: (0, 0)),
            pl.BlockSpec((D, tn), lambda i, j: (0, j)),
        ],
        out_specs=pl.BlockSpec((tm, tn), lambda i, j: (i, j)),
        out_shape=jax.ShapeDtypeStruct((n, PROJ_COLS), f32),
        scratch_shapes=[pltpu.VMEM((tm, D), bf16)],
        compiler_params=pltpu.CompilerParams(dimension_semantics=("arbitrary", "arbitrary"),
                                             vmem_limit_bytes=48 * MIB),
        name="inproj",
    )(x, gain, w)


def _lru_kernel(x_ref, y_ref, gl_ref, cinit_ref, h0_ref, cw_ref, cb_ref, wxa_ref, bx_ref, ba_ref, ap_ref,
                o_ref, hl_ref, xs_ref, a_ref, b_ref, h_ref, *, batch, reset_first, chunk):
    rows = x_ref.shape[0]
    nt = rows // batch
    hist = 3 * batch
    pid = pl.program_id(0)

    @pl.when(pid == 0)
    def _():
        xs_ref[0:hist, :] = cinit_ref[...]
        h_ref[...] = h0_ref[...]

    xs_ref[hist:hist + rows, :] = x_ref[...]
    logsig = -jax.nn.softplus(-ap_ref[...])

    def gates(c, _):
        r0 = pl.multiple_of(c * chunk, chunk)
        xc = (cb_ref[...]
              + cw_ref[3:4, :] * xs_ref[pl.ds(pl.multiple_of(r0 + hist, SUBLANES), chunk), :]
              + cw_ref[2:3, :] * xs_ref[pl.ds(pl.multiple_of(r0 + 2 * batch, SUBLANES), chunk), :]
              + cw_ref[1:2, :] * xs_ref[pl.ds(pl.multiple_of(r0 + batch, SUBLANES), chunk), :]
              + cw_ref[0:1, :] * xs_ref[pl.ds(r0, chunk), :])
        if reset_first:
            grow = lax.broadcasted_iota(jnp.int32, (chunk, LANES), 0) + (r0 + pid * rows)
            first = grow < batch
        for n in range(D // LANES):
            sl = slice(n * LANES, (n + 1) * LANES)
            xn = xc[:, sl]
            g2 = jnp.dot(xn.astype(bf16), wxa_ref[n], preferred_element_type=f32)
            gate_x = jax.nn.sigmoid(g2[:, :LANES] + bx_ref[:, sl])
            gate_a = jax.nn.sigmoid(g2[:, LANES:] + ba_ref[:, sl])
            log_a = LRU_C * gate_a * logsig[:, sl]
            a = jnp.exp(log_a)
            mult = jnp.sqrt(-jnp.tanh(log_a) * (a * a + 1.0))
            if reset_first:
                mult = jnp.where(first, 1.0, mult)
            a_ref[pl.ds(r0, chunk), sl] = a
            b_ref[pl.ds(r0, chunk), sl] = xn * gate_x * mult
        return 0

    lax.fori_loop(0, rows // chunk, gates, 0)

    def scan(t, h):
        r0 = pl.multiple_of(t * batch, batch)
        h = a_ref[pl.ds(r0, batch), :] * h + b_ref[pl.ds(r0, batch), :]
        b_ref[pl.ds(r0, batch), :] = h
        return h

    h = lax.fori_loop(0, nt, scan, h_ref[...], unroll=(8 if nt >= 8 and batch == SUBLANES else 1))
    h_ref[...] = h
    hl_ref[...] = h
    xs_ref[0:hist, :] = xs_ref[rows:rows + hist, :]

    def outp(c, _):
        r0 = pl.multiple_of(c * chunk, chunk)
        o_ref[pl.ds(r0, chunk), :] = (b_ref[pl.ds(r0, chunk), :] * jax.nn.gelu(y_ref[pl.ds(r0, chunk), :])
                                      * jax.nn.sigmoid(gl_ref[pl.ds(r0, chunk), :]))
        return 0

    lax.fori_loop(0, rows // chunk, outp, 0)


def _lru(proj, row_off, n_rows, batch, rows_blk, reset_first, conv_init, h0, cw, cb, wxa, bx, ba, ap):
    ob = row_off // rows_blk
    kern = functools.partial(_lru_kernel, batch=batch, reset_first=reset_first, chunk=128)
    return pl.pallas_call(
        kern,
        grid=(n_rows // rows_blk,),
        in_specs=[
            pl.BlockSpec((rows_blk, D), lambda i: (ob + i, COL_LRU_X)),
            pl.BlockSpec((rows_blk, D), lambda i: (ob + i, COL_LRU_Y)),
            pl.BlockSpec((rows_blk, D), lambda i: (ob + i, COL_GL)),
            _full((3 * batch, D)), _full((batch, D)), _full((4, D)), _full((1, D)),
            _full((D // LANES, LANES, 2 * LANES)), _full((1, D)), _full((1, D)), _full((1, D)),
        ],
        out_specs=[pl.BlockSpec((rows_blk, D), lambda i: (i, 0)), _full((batch, D))],
        out_shape=[jax.ShapeDtypeStruct((n_rows, D), f32), jax.ShapeDtypeStruct((batch, D), f32)],
        scratch_shapes=[pltpu.VMEM((rows_blk + 3 * batch, D), f32), pltpu.VMEM((rows_blk, D), f32),
                        pltpu.VMEM((rows_blk, D), f32), pltpu.VMEM((batch, D), f32)],
        compiler_params=pltpu.CompilerParams(dimension_semantics=("arbitrary",), vmem_limit_bytes=48 * MIB),
        name="lru",
    )(proj, proj, proj, conv_init, h0, cw, cb, wxa, bx, ba, ap)


TILE_ALPHA, TILE_BETA, TILE_W, TILE_KP, TILE_WR, TILE_V = range(6)
ROW_BR, ROW_KR, ROW_BONUS = range(3)
CONST_KK, CONST_KA, CONST_RK, CONST_LNW, CONST_LNB = range(5)
NAT_R, NAT_W, NAT_K, NAT_V, NAT_A = range(5)


def _wkv_prep(tiles_ref, rows_ref, t, g, r, w, k, v, a, const):
    kk = k * const(CONST_KK)
    nrm = jnp.sqrt(jnp.sum(kk * kk, axis=0, keepdims=True))
    kk = kk * (1.0 / jnp.maximum(nrm, 1e-12))
    beta = kk * a
    kp = k * (1.0 + (a - 1.0) * const(CONST_KA))
    tiles_ref[t, g, TILE_ALPHA] = -kk
    tiles_ref[t, g, TILE_BETA] = beta
    tiles_ref[t, g, TILE_W] = w
    tiles_ref[t, g, TILE_KP] = kp
    tiles_ref[t, g, TILE_WR] = w * r
    tiles_ref[t, g, TILE_V] = v
    rows_ref[t, g, ROW_BR:ROW_BR + 1, :] = jnp.sum(beta * r, axis=0, keepdims=True)
    rows_ref[t, g, ROW_KR:ROW_KR + 1, :] = jnp.sum(kp * r, axis=0, keepdims=True)
    rows_ref[t, g, ROW_BONUS:ROW_BONUS + 1, :] = jnp.sum(r * kp * const(CONST_RK), axis=0, keepdims=True)


def _rwkv_pre_kernel(r_ref, k_ref, v_ref, l_ref, sinit_ref, mu_ref, w0_ref, a0_ref, w2_ref, a2_ref, g2_ref, c_ref,
                     tiles_ref, rows_ref, go_ref, ps_ref, nat_ref, *, batch, chunk, chains_are_heads):
    rows = r_ref.shape[0]

    @pl.when(pl.program_id(0) == 0)
    def _():
        ps_ref[0:batch, :] = sinit_ref[...]

    ps_ref[batch:batch + rows, 0:D] = r_ref[...]
    ps_ref[batch:batch + rows, D:2 * D] = k_ref[...]
    ps_ref[batch:batch + rows, 2 * D:3 * D] = v_ref[...]
    ps_ref[batch:batch + rows, 3 * D:3 * D + LORA_PAD] = l_ref[...]

    def body(c, _):
        r0 = pl.multiple_of(c * chunk, chunk)

        def mixed(lo, hi):
            cur = ps_ref[pl.ds(pl.multiple_of(r0 + batch, SUBLANES), chunk), lo:hi]
            prev = ps_ref[pl.ds(r0, chunk), lo:hi]
            return cur + (prev - cur) * mu_ref[:, lo:hi]

        nat_ref[NAT_R, pl.ds(r0, chunk), :] = mixed(0, D)
        nat_ref[NAT_K, pl.ds(r0, chunk), :] = mixed(D, 2 * D)
        nat_ref[NAT_V, pl.ds(r0, chunk), :] = mixed(2 * D, 3 * D)
        lm = mixed(3 * D, 3 * D + LORA_PAD)
        xwa = lm[:, 0:LANES]
        xg = lm[:, LANES:3 * LANES]
        lw = jnp.dot(jnp.tanh(xwa).astype(bf16), w2_ref[...], preferred_element_type=f32)
        w_log = -jax.nn.softplus(-(w0_ref[...] + lw)) - 0.5
        nat_ref[NAT_W, pl.ds(r0, chunk), :] = jnp.exp(-jnp.exp(w_log))
        la = jnp.dot(xwa.astype(bf16), a2_ref[...], preferred_element_type=f32)
        nat_ref[NAT_A, pl.ds(r0, chunk), :] = jax.nn.sigmoid(a0_ref[...] + la)
        go_ref[pl.ds(r0, chunk), :] = jnp.dot(jax.nn.sigmoid(xg).astype(bf16), g2_ref[...],
                                              preferred_element_type=f32)
        return 0

    lax.fori_loop(0, rows // chunk, body, 0)
    ps_ref[0:batch, :] = ps_ref[rows:rows + batch, :]

    if chains_are_heads:
        def tstep(t, _):
            r0 = pl.multiple_of(t * batch, batch)
            for hp in range(D // LANES):
                trs = [nat_ref[q, pl.ds(r0, batch), hp * LANES:(hp + 1) * LANES].T for q in range(5)]
                for h2 in range(2):
                    h = 2 * hp + h2
                    _wkv_prep(tiles_ref, rows_ref, t, h, *[tr[h2 * HD:(h2 + 1) * HD] for tr in trs],
                              lambda i, h=h: c_ref[h, i])
            return 0

        lax.fori_loop(0, rows // batch, tstep, 0)
    else:
        half = LANES // 2
        lane = lax.broadcasted_iota(jnp.int32, (HD, LANES), 1)

        def load_pair(q, tp):
            zz = jnp.concatenate(
                [nat_ref[q, pl.ds(pl.multiple_of(tp * 2 * SUBLANES + tt * SUBLANES, SUBLANES), SUBLANES),
                         hp * LANES:(hp + 1) * LANES] for tt in range(2) for hp in range(D // LANES)], axis=0)
            tr = zz.T
            top, bot = tr[0:HD], tr[HD:2 * HD]
            d0 = jnp.where(lane < half, top, pltpu.roll(bot, half, axis=1))
            d1 = jnp.where(lane < half, pltpu.roll(top, half, axis=1), bot)
            return d0, d1

        def pair(tp, _):
            pairs = [load_pair(q, tp) for q in range(5)]
            for tt in range(2):
                _wkv_prep(tiles_ref, rows_ref, 2 * tp + tt, 0, *[p[tt] for p in pairs], lambda i: c_ref[0, i])
            return 0

        lax.fori_loop(0, rows // (2 * SUBLANES), pair, 0, unroll=2)


def _rwkv_pre(proj, row_off, n_rows, batch, rows_blk, groups, shift_init, consts, mu, w0, a0, w2p, a2p, g2p):
    ob = row_off // rows_blk
    width = 3 * D + LORA_PAD
    steps_blk = rows_blk // batch
    kern = functools.partial(_rwkv_pre_kernel, batch=batch, chunk=128, chains_are_heads=groups > 1)
    return pl.pallas_call(
        kern,
        grid=(n_rows // rows_blk,),
        in_specs=[
            pl.BlockSpec((rows_blk, D), lambda i: (ob + i, COL_R)),
            pl.BlockSpec((rows_blk, D), lambda i: (ob + i, COL_K)),
            pl.BlockSpec((rows_blk, D), lambda i: (ob + i, COL_V)),
            pl.BlockSpec((rows_blk, LORA_PAD), lambda i: (ob + i, COL_LORA)),
            _full((batch, width)), _full((1, width)), _full((1, D)), _full((1, D)),
            _full((LANES, D)), _full((LANES, D)), _full((2 * LANES, D)), _full((groups, 5, HD, LANES)),
        ],
        out_specs=[pl.BlockSpec((steps_blk, groups, 6, HD, LANES), lambda i: (i, 0, 0, 0, 0)),
                   pl.BlockSpec((steps_blk, groups, SUBLANES, LANES), lambda i: (i, 0, 0, 0)),
                   pl.BlockSpec((rows_blk, D), lambda i: (i, 0))],
        out_shape=[jax.ShapeDtypeStruct((n_rows // batch, groups, 6, HD, LANES), f32),
                   jax.ShapeDtypeStruct((n_rows // batch, groups, SUBLANES, LANES), f32),
                   jax.ShapeDtypeStruct((n_rows, D), f32)],
        scratch_shapes=[pltpu.VMEM((rows_blk + batch, width), f32), pltpu.VMEM((5, rows_blk, D), f32)],
        compiler_params=pltpu.CompilerParams(dimension_semantics=("arbitrary",), vmem_limit_bytes=56 * MIB),
        name="rwkv_pre",
    )(proj, proj, proj, proj, shift_init, mu, w0, a0, w2p, a2p, g2p, consts)


STATE_ROWS = HD * HD
VALUE_ROWS = 64


def _wkv_step(s_ref, tiles_ref, rows_ref, y_ref, t):
    for part in range(HD // VALUE_ROWS):
        lo = part * VALUE_ROWS
        sa = y0 = None
        for j in range(HD):
            s = s_ref[j * HD + lo:j * HD + lo + VALUE_ROWS, :]
            pa = s * tiles_ref[t, 0, TILE_ALPHA, j:j + 1, :]
            pr = s * tiles_ref[t, 0, TILE_WR, j:j + 1, :]
            sa = pa if sa is None else sa + pa
            y0 = pr if y0 is None else y0 + pr
        v = tiles_ref[t, 0, TILE_V, lo:lo + VALUE_ROWS, :]
        for j in range(HD):
            rows = slice(j * HD + lo, j * HD + lo + VALUE_ROWS)
            s_ref[rows, :] = (s_ref[rows, :] * tiles_ref[t, 0, TILE_W, j:j + 1, :]
                              + sa * tiles_ref[t, 0, TILE_BETA, j:j + 1, :]
                              + v * tiles_ref[t, 0, TILE_KP, j:j + 1, :])
        y_ref[t, 0, lo:lo + VALUE_ROWS, :] = (y0 + sa * rows_ref[t, 0, ROW_BR:ROW_BR + 1, :]
                                              + v * rows_ref[t, 0, ROW_KR:ROW_KR + 1, :])


def _wkv_scan_kernel(tiles_ref, rows_ref, st_ref, y_ref, sfin_ref, s_ref, *, steps):
    tb = pl.program_id(1)

    def value_rows(i):
        return pl.ds(i, HD, stride=HD)

    @pl.when(tb == 0)
    def _():
        def init(c, _):
            tr = st_ref[:, pl.ds(pl.multiple_of(c * LANES, LANES), LANES)].T
            for i2 in range(2):
                s_ref[value_rows(2 * c + i2), :] = tr[i2 * HD:(i2 + 1) * HD]
            return 0

        lax.fori_loop(0, STATE_ROWS // LANES, init, 0)

    def step(t, _):
        _wkv_step(s_ref, tiles_ref, rows_ref, y_ref, t)
        return 0

    lax.fori_loop(0, steps, step, 0)

    @pl.when(tb == pl.num_programs(1) - 1)
    def _():
        def fin(c, _):
            pair = jnp.concatenate([s_ref[value_rows(2 * c + i2), :] for i2 in range(2)], axis=0)
            sfin_ref[:, pl.ds(pl.multiple_of(c * LANES, LANES), LANES)] = pair.T
            return 0

        lax.fori_loop(0, STATE_ROWS // LANES, fin, 0)


def _wkv_scan(tiles, rows, state, steps_blk):
    n_steps, groups = tiles.shape[0], tiles.shape[1]
    kern = functools.partial(_wkv_scan_kernel, steps=steps_blk)
    sblk = pl.BlockSpec((LANES, STATE_ROWS), lambda g, i: (0, g))
    return pl.pallas_call(
        kern,
        grid=(groups, n_steps // steps_blk),
        in_specs=[pl.BlockSpec((steps_blk, 1, 6, HD, LANES), lambda g, i: (i, g, 0, 0, 0)),
                  pl.BlockSpec((steps_blk, 1, SUBLANES, LANES), lambda g, i: (i, g, 0, 0)),
                  sblk],
        out_specs=[pl.BlockSpec((steps_blk, 1, HD, LANES), lambda g, i: (i, g, 0, 0)), sblk],
        out_shape=[jax.ShapeDtypeStruct((n_steps, groups, HD, LANES), f32),
                   jax.ShapeDtypeStruct((LANES, groups * STATE_ROWS), f32)],
        scratch_shapes=[pltpu.VMEM((STATE_ROWS, LANES), f32)],
        compiler_params=pltpu.CompilerParams(dimension_semantics=("arbitrary", "arbitrary"),
                                             vmem_limit_bytes=40 * MIB),
        name="wkv_scan",
    )(tiles, rows, state)


def _wkv_norm(y, v, bonus, lnw, lnb):
    mean = jnp.sum(y, axis=0, keepdims=True) * (1.0 / HD)
    d = y - mean
    var = jnp.sum(d * d, axis=0, keepdims=True) * (1.0 / HD)
    return d * lax.rsqrt(var + GN_EPS) * lnw + lnb + bonus * v


def _wkv_post_kernel(y_ref, v_ref, rows_ref, c_ref, o_ref, *, batch, chains_are_heads):
    steps = y_ref.shape[0]

    def normed(t, g):
        return _wkv_norm(y_ref[t, g], v_ref[t, g, 0], rows_ref[t, g, ROW_BONUS:ROW_BONUS + 1, :],
                         c_ref[g, CONST_LNW], c_ref[g, CONST_LNB])

    if chains_are_heads:
        def tstep(t, _):
            r0 = pl.multiple_of(t * batch, batch)
            for hp in range(D // LANES):
                two = jnp.concatenate([normed(t, 2 * hp), normed(t, 2 * hp + 1)], axis=0)
                o_ref[pl.ds(r0, batch), hp * LANES:(hp + 1) * LANES] = two.T
            return 0

        lax.fori_loop(0, steps, tstep, 0)
    else:
        half = LANES // 2
        lane = lax.broadcasted_iota(jnp.int32, (HD, LANES), 1)

        def pair(tp, _):
            o0 = normed(2 * tp, 0)
            o1 = normed(2 * tp + 1, 0)
            top = jnp.where(lane < half, o0, pltpu.roll(o1, half, axis=1))
            bot = jnp.where(lane < half, pltpu.roll(o0, half, axis=1), o1)
            zz = jnp.concatenate([top, bot], axis=0).T
            for tt in range(2):
                row = pl.multiple_of(tp * 2 * SUBLANES + tt * SUBLANES, SUBLANES)
                for hp in range(D // LANES):
                    src = tt * HD + hp * SUBLANES
                    o_ref[pl.ds(row, SUBLANES), hp * LANES:(hp + 1) * LANES] = zz[src:src + SUBLANES, :]
            return 0

        lax.fori_loop(0, steps // 2, pair, 0, unroll=2)


def _wkv_post(y, tiles, rows, consts, batch, steps_blk):
    n_steps, groups = y.shape[0], y.shape[1]
    kern = functools.partial(_wkv_post_kernel, batch=batch, chains_are_heads=groups > 1)
    return pl.pallas_call(
        kern,
        grid=(n_steps // steps_blk,),
        in_specs=[pl.BlockSpec((steps_blk, groups, HD, LANES), lambda i: (i, 0, 0, 0)),
                  pl.BlockSpec((steps_blk, groups, 1, HD, LANES), lambda i: (i, 0, TILE_V, 0, 0)),
                  pl.BlockSpec((steps_blk, groups, SUBLANES, LANES), lambda i: (i, 0, 0, 0)),
                  _full((groups, 5, HD, LANES))],
        out_specs=pl.BlockSpec((steps_blk * batch, D), lambda i: (i, 0)),
        out_shape=jax.ShapeDtypeStruct((n_steps * batch, D), f32),
        compiler_params=pltpu.CompilerParams(dimension_semantics=("arbitrary",), vmem_limit_bytes=40 * MIB),
        name="wkv_post",
    )(y, tiles, rows, consts)


def _outproj_kernel(ml_ref, op_ref, g_ref, gr_ref, x_ref, wo_ref, n2_ref, rh_ref, rl_ref, rb_ref,
                    x1_ref, t_ref, comb_ref):
    tm = x_ref.shape[0]
    merged = ml_ref[...] + jax.nn.sigmoid(gr_ref[...]) * (op_ref[...] * g_ref[...])
    x1 = x_ref[...] + jnp.dot(merged.astype(bf16), wo_ref[...], preferred_element_type=f32)
    x1_ref[...] = x1
    t = x1 * lax.rsqrt(jnp.mean(x1 * x1, axis=-1, keepdims=True) + NORM_EPS) * n2_ref[...]
    th = t.astype(bf16)
    t_ref[...] = th
    tl = (t - th.astype(f32)).astype(bf16)
    nt_dims = (((1,), (1,)), ((), ()))
    lg = (lax.dot_general(rh_ref[...], th, nt_dims, preferred_element_type=f32)
          + lax.dot_general(rh_ref[...], tl, nt_dims, preferred_element_type=f32)
          + lax.dot_general(rl_ref[...], th, nt_dims, preferred_element_type=f32)) + rb_ref[...]
    row = lax.broadcasted_iota(jnp.int32, (EPG, tm), 0).astype(f32)
    neg = jnp.float32(-jnp.inf)
    glog = jnp.where(row < N_GROUPS, lg[0:EPG], neg)
    ge = jnp.exp(glog - jnp.max(glog, axis=0, keepdims=True))
    pg = ge / jnp.sum(ge, axis=0, keepdims=True)
    p_top = jnp.max(pg, axis=0, keepdims=True)
    g_idx = jnp.min(jnp.where(pg == p_top, row, EPG), axis=0, keepdims=True)
    le = jnp.zeros((EPG, tm), f32)
    for g in range(N_GROUPS):
        le = jnp.where(g_idx == g, lg[EPG * (g + 1):EPG * (g + 2)], le)
    qe = jnp.exp(le - jnp.max(le, axis=0, keepdims=True))
    q = qe / jnp.sum(qe, axis=0, keepdims=True)
    q1 = jnp.max(q, axis=0, keepdims=True)
    i1 = jnp.min(jnp.where(q == q1, row, EPG), axis=0, keepdims=True)
    qm = jnp.where(row == i1, -1.0, q)
    q2 = jnp.max(qm, axis=0, keepdims=True)
    i2 = jnp.min(jnp.where(qm == q2, row, EPG), axis=0, keepdims=True)
    qs = q1 + q2
    w_grp = jnp.where(row == i1, q1 / qs, 0.0) + jnp.where(row == i2, q2 / qs, 0.0)
    wp = w_grp * p_top
    zero = jnp.zeros((EPG, tm), f32)
    comb_t = jnp.concatenate([jnp.where(g_idx == g, wp, zero) for g in range(N_GROUPS)]
                             + [jnp.zeros((LANES - N_EXPERTS, tm), f32)], axis=0)
    for c in range(tm // LANES):
        comb_ref[c * LANES:(c + 1) * LANES, :] = comb_t[:, c * LANES:(c + 1) * LANES].T


def _outproj(ml, op, g, proj, x, wo, n2, rh, rl, rb, tm=512):
    n = x.shape[0]
    blk = pl.BlockSpec((tm, D), lambda i: (i, 0))
    return pl.pallas_call(
        _outproj_kernel,
        grid=(n // tm,),
        in_specs=[blk, blk, blk, pl.BlockSpec((tm, D), lambda i: (i, COL_GR)), blk,
                  _full((D, D)), _full((1, D)), _full((ROUTER_ROWS, D)), _full((ROUTER_ROWS, D)),
                  _full((ROUTER_ROWS, 1))],
        out_specs=[blk, blk, pl.BlockSpec((tm, LANES), lambda i: (i, 0))],
        out_shape=[jax.ShapeDtypeStruct((n, D), f32), jax.ShapeDtypeStruct((n, D), bf16),
                   jax.ShapeDtypeStruct((n, LANES), f32)],
        compiler_params=pltpu.CompilerParams(dimension_semantics=("arbitrary",), vmem_limit_bytes=48 * MIB),
        name="outproj_router",
    )(ml, op, g, proj, x, wo, n2, rh, rl, rb)


def _moe_kernel(t_ref, comb_ref, wgu_ref, wd_ref, x1_ref, fn_ref, o_ref, acc_ref):
    e = pl.program_id(1)
    tm = t_ref.shape[0]

    @pl.when(e == 0)
    def _():
        acc_ref[...] = jnp.zeros_like(acc_ref)

    gu = jnp.dot(t_ref[...], wgu_ref[0], preferred_element_type=f32)
    lane = lax.broadcasted_iota(jnp.int32, (tm, LANES), 1)
    c = jnp.sum(jnp.where(lane == e, comb_ref[...], 0.0), axis=1, keepdims=True)
    h = jax.nn.silu(gu[:, :D_EXPERT]) * gu[:, D_EXPERT:] * c
    acc_ref[...] += jnp.dot(h.astype(bf16), wd_ref[0], preferred_element_type=f32)

    @pl.when(e == pl.num_programs(1) - 1)
    def _():
        xo = x1_ref[...] + acc_ref[...]
        o_ref[...] = xo * lax.rsqrt(jnp.mean(xo * xo, axis=-1, keepdims=True) + NORM_EPS) * fn_ref[...]


def _moe(t, comb, wgu, wd, x1, fn, tm=1024):
    n = t.shape[0]
    return pl.pallas_call(
        _moe_kernel,
        grid=(n // tm, N_EXPERTS),
        in_specs=[
            pl.BlockSpec((tm, D), lambda i, e: (i, 0)),
            pl.BlockSpec((tm, LANES), lambda i, e: (i, 0)),
            pl.BlockSpec((1, D, 2 * D_EXPERT), lambda i, e: (e, 0, 0)),
            pl.BlockSpec((1, D_EXPERT, D), lambda i, e: (e, 0, 0)),
            pl.BlockSpec((tm, D), lambda i, e: (i, 0)),
            pl.BlockSpec((1, D), lambda i, e: (0, 0)),
        ],
        out_specs=pl.BlockSpec((tm, D), lambda i, e: (i, 0)),
        out_shape=jax.ShapeDtypeStruct((n, D), f32),
        scratch_shapes=[pltpu.VMEM((tm, D), f32)],
        compiler_params=pltpu.CompilerParams(dimension_semantics=("arbitrary", "arbitrary"),
                                             vmem_limit_bytes=48 * MIB),
        name="moe",
    )(t, comb, wgu, wd, x1, fn)


def _chain_tiles_prompt(vec, batch):
    t = vec.reshape(HEADS // 2, 2, HD)
    t = jnp.transpose(t, (2, 1, 0))
    return jnp.broadcast_to(t[..., None], (HD, 2, HEADS // 2, batch)).reshape(HD, LANES)


def _chain_tiles_sample(vec):
    return jnp.broadcast_to(vec.reshape(HEADS, HD)[..., None], (HEADS, HD, LANES))


def kernel(x_prompt, x_sample, state_conv, state_lru, state_shift, state_wkv, norm1, w_in, conv_w, conv_b, lru_wx, lru_bx, lru_wa, lru_ba, lru_a_param, rwkv_mu, rwkv_w0, rwkv_w2, rwkv_a0, rwkv_a2, rwkv_g2, rwkv_k_k, rwkv_k_a, rwkv_r_k, rwkv_ln_w, rwkv_ln_b, w_out, norm2, router_group, router_group_b, router_expert, router_expert_b, exp_gate, exp_up, exp_down, final_norm):
    bp, tp, _ = x_prompt.shape
    bs, ts, _ = x_sample.shape
    assert norm1.shape[0] == 1 and bp * HEADS == LANES and bs == LANES and tp % 64 == 0 and ts % 2 == 0
    n_p, n_s = bp * tp, bs * ts
    c_rw = 2 * D
    c_lora = c_rw + 3 * D
    c_gl = c_lora + LORA

    w = w_in[0]
    w_all = jnp.concatenate([w[:, :c_lora], w[:, c_gl:], w[:, c_lora:c_gl],
                             jnp.zeros((D, LORA_PAD - LORA), f32)], axis=1).astype(bf16)
    mu = rwkv_mu[0]
    mu_all = jnp.concatenate([mu[:3 * D], mu[3 * D:], jnp.zeros((LORA_PAD - LORA,), f32)])[None]
    w2p = jnp.concatenate([rwkv_w2[0], jnp.zeros((LANES - LORA_W, D), f32)]).astype(bf16)
    a2p = jnp.concatenate([jnp.zeros((LORA_W, D), f32), rwkv_a2[0]]).astype(bf16)
    g2p = jnp.concatenate([rwkv_g2[0], jnp.zeros((2 * LANES - LORA_G, D), f32)]).astype(bf16)
    wxa = jnp.concatenate([lru_wx[0], lru_wa[0]], axis=-1).astype(bf16)
    row = lambda v: v.reshape(1, -1)
    rk = rwkv_r_k[0].reshape(D)
    cvecs = (rwkv_k_k[0], rwkv_k_a[0], rk, rwkv_ln_w[0], rwkv_ln_b[0])
    consts_p = jnp.stack([_chain_tiles_prompt(v, bp) for v in cvecs])[None]
    consts_s = jnp.stack([_chain_tiles_sample(v) for v in cvecs], axis=1)
    rw = jnp.zeros((ROUTER_ROWS, D), f32)
    rw = rw.at[0:N_GROUPS].set(router_group[0].T).at[EPG:EPG + N_EXPERTS].set(router_expert[0].T)
    rh = rw.astype(bf16)
    rl = (rw - rh.astype(f32)).astype(bf16)
    rb = jnp.zeros((ROUTER_ROWS, 1), f32)
    rb = rb.at[0:N_GROUPS, 0].set(router_group_b[0]).at[EPG:EPG + N_EXPERTS, 0].set(router_expert_b[0])
    wgu = jnp.concatenate([exp_gate[0], exp_up[0]], axis=-1).astype(bf16)
    wd = exp_down[0].astype(bf16)
    wo = w_out[0].astype(bf16)

    x = jnp.concatenate([jnp.transpose(x_prompt, (1, 0, 2)).reshape(n_p, D),
                         jnp.transpose(x_sample, (1, 0, 2)).reshape(n_s, D)])
    proj = _inproj(x, row(norm1[0]), w_all)

    lru_args = (conv_w[0], row(conv_b[0]), wxa, row(lru_bx[0]), row(lru_ba[0]), row(lru_a_param[0]))
    conv_init_s = jnp.transpose(state_conv[0], (1, 0, 2)).reshape(3 * bs, D)
    ml_p, hl_p = _lru(proj, 0, n_p, bp, 512, True, jnp.zeros((3 * bp, D), f32), jnp.zeros((bp, D), f32), *lru_args)
    ml_s, hl_s = _lru(proj, n_p, n_s, bs, 512, False, conv_init_s, state_lru[0], *lru_args)

    sh = state_shift[0]
    shift_init_s = jnp.concatenate([sh, jnp.zeros((bs, LORA_PAD - LORA), f32)], axis=1)
    pre_args = (mu_all, row(rwkv_w0[0]), row(rwkv_a0[0]), w2p, a2p, g2p)
    tiles_p, rows_p, g_p = _rwkv_pre(proj, 0, n_p, bp, 256, 1, jnp.zeros((bp, 3 * D + LORA_PAD), f32),
                                     consts_p, *pre_args)
    tiles_s, rows_s, g_s = _rwkv_pre(proj, n_p, n_s, bs, 256, HEADS, shift_init_s, consts_s, *pre_args)

    y_p, sfin_p = _wkv_scan(tiles_p, rows_p, jnp.zeros((LANES, STATE_ROWS), f32), 16)
    y_s, sfin_s = _wkv_scan(tiles_s, rows_s, state_wkv[0].reshape(bs, HEADS * STATE_ROWS), ts)
    op_p = _wkv_post(y_p, tiles_p, rows_p, consts_p, bp, 64)
    op_s = _wkv_post(y_s, tiles_s, rows_s, consts_s, bs, 2)

    cat = lambda a, b: jnp.concatenate([a, b])
    x1, t, comb = _outproj(cat(ml_p, ml_s), cat(op_p, op_s), cat(g_p, g_s), proj, x, wo, row(norm2[0]), rh, rl, rb)
    y = _moe(t, comb, wgu, wd, x1, row(final_norm))

    y_prompt = jnp.transpose(y[:n_p].reshape(tp, bp, D), (1, 0, 2))
    y_sample = jnp.transpose(y[n_p:].reshape(ts, bs, D), (1, 0, 2))

    def last_rows(lo, hi, n_end, batch, steps):
        return proj[n_end - steps * batch:n_end, lo:hi].reshape(steps, batch, hi - lo)

    conv_p = jnp.transpose(last_rows(0, D, n_p, bp, 3), (1, 0, 2))[None]
    conv_s = jnp.transpose(last_rows(0, D, n_p + n_s, bs, 3), (1, 0, 2))[None]

    def shift_rows(n_end, batch):
        return jnp.concatenate([last_rows(2 * D, 5 * D, n_end, batch, 1)[0],
                                last_rows(7 * D, 7 * D + LORA, n_end, batch, 1)[0]], axis=1)[None]

    wkv_p = jnp.transpose(sfin_p.reshape(2, HEADS // 2, bp, HD, HD), (2, 1, 0, 3, 4)).reshape(1, bp, HEADS, HD, HD)
    wkv_s = sfin_s.reshape(1, bs, HEADS, HD, HD)
    return (y_prompt, y_sample, conv_p, hl_p[None], shift_rows(n_p, bp), wkv_p,
            conv_s, hl_s[None], shift_rows(n_p + n_s, bs), wkv_s)
```

```python
import functools

import jax
import jax.numpy as jnp
from jax import lax
from jax.experimental import pallas as pl
from jax.experimental.pallas import tpu as pltpu

f32 = jnp.float32
bf16 = jnp.bfloat16

D = 1024
HEADS = 16
HD = 64
LANES = 128
SUBLANES = 8
LORA_W = 64
LORA_A = 64
LORA_G = 160
LORA = LORA_W + LORA_A + LORA_G
LORA_PAD = 512
N_GROUPS = 4
EPG = 8
N_EXPERTS = N_GROUPS * EPG
D_EXPERT = 256
LRU_C = 8.0
GN_EPS = 64e-5
NORM_EPS = 1e-6
PROJ_COLS = 7 * D + LORA_PAD
COL_LRU_X, COL_LRU_Y, COL_R, COL_K, COL_V, COL_GL, COL_GR = range(7)
COL_LORA = 7 * D // LORA_PAD
ROUTER_ROWS = 48
MIB = 1024 * 1024
TOKEN_TILE = 1024


def _full(shape, grid_rank=1):
    zeros = tuple(0 for _ in shape)
    if grid_rank == 1:
        return pl.BlockSpec(shape, lambda i: zeros)
    return pl.BlockSpec(shape, lambda i, j: zeros)


def _inproj_kernel(xp_ref, xs_ref, g_ref, w_ref, o_ref, xo_ref, xf_ref, xn_ref, *, prompt_tiles):
    i = pl.program_id(0)
    first_col = pl.program_id(1) == 0
    bp, tsteps = xp_ref.shape[0], xp_ref.shape[1]
    bs, ssteps = xs_ref.shape[0], xs_ref.shape[1]

    @pl.when(first_col & (i < prompt_tiles))
    def _():
        for b in range(bp):
            v = xp_ref[b]
            for cb in range(D // LANES):
                xf_ref[cb, pl.ds(b, tsteps, stride=bp), :] = v[:, cb * LANES:(cb + 1) * LANES]

    @pl.when(first_col & (i >= prompt_tiles))
    def _():
        for t in range(ssteps):
            v = xs_ref[:, t, :]
            for cb in range(D // LANES):
                xf_ref[cb, t * bs:(t + 1) * bs, :] = v[:, cb * LANES:(cb + 1) * LANES]

    @pl.when(first_col)
    def _():
        x = jnp.concatenate([xf_ref[cb] for cb in range(D // LANES)], axis=1)
        xo_ref[...] = x
        ms = jnp.mean(x * x, axis=-1, keepdims=True)
        xn_ref[...] = (x * lax.rsqrt(ms + NORM_EPS) * g_ref[...]).astype(bf16)

    o_ref[...] = jnp.dot(xn_ref[...], w_ref[...], preferred_element_type=f32)


def _inproj(x_prompt, x_sample, gain, w, tm, tn=1280):
    bp, tp, _ = x_prompt.shape
    bs, ts, _ = x_sample.shape
    assert tm % bp == 0 and tp % (tm // bp) == 0 and bs * ts == tm
    prompt_tiles = bp * tp // tm
    n = bp * tp + bs * ts
    kern = functools.partial(_inproj_kernel, prompt_tiles=prompt_tiles)
    return pl.pallas_call(
        kern,
        grid=(n // tm, PROJ_COLS // tn),
        in_specs=[
            pl.BlockSpec((bp, tm // bp, D), lambda i, j: (0, jnp.minimum(i, prompt_tiles - 1), 0)),
            pl.BlockSpec((bs, ts, D), lambda i, j: (0, 0, 0)),
            pl.BlockSpec((1, D), lambda i, j: (0, 0)),
            pl.BlockSpec((D, tn), lambda i, j: (0, j)),
        ],
        out_specs=[pl.BlockSpec((tm, tn), lambda i, j: (i, j)), pl.BlockSpec((tm, D), lambda i, j: (i, 0))],
        out_shape=[jax.ShapeDtypeStruct((n, PROJ_COLS), f32), jax.ShapeDtypeStruct((n, D), f32)],
        scratch_shapes=[pltpu.VMEM((D // LANES, tm, LANES), f32), pltpu.VMEM((tm, D), bf16)],
        compiler_params=pltpu.CompilerParams(dimension_semantics=("arbitrary", "arbitrary"),
                                             vmem_limit_bytes=56 * MIB),
        name="inproj",
    )(x_prompt, x_sample, gain, w)


def _lru_kernel(x_ref, y_ref, gl_ref, cinit_ref, h0_ref, cw_ref, cb_ref, wxa_ref, bx_ref, ba_ref, ap_ref,
                o_ref, hl_ref, xs_ref, a_ref, b_ref, h_ref, *, batch, reset_first, chunk):
    rows = x_ref.shape[0]
    nt = rows // batch
    hist = 3 * batch
    pid = pl.program_id(0)

    @pl.when(pid == 0)
    def _():
        xs_ref[0:hist, :] = cinit_ref[...]
        h_ref[...] = h0_ref[...]

    xs_ref[hist:hist + rows, :] = x_ref[...]
    logsig = -jax.nn.softplus(-ap_ref[...])

    def gates(c, _):
        r0 = pl.multiple_of(c * chunk, chunk)
        xc = (cb_ref[...]
              + cw_ref[3:4, :] * xs_ref[pl.ds(pl.multiple_of(r0 + hist, SUBLANES), chunk), :]
              + cw_ref[2:3, :] * xs_ref[pl.ds(pl.multiple_of(r0 + 2 * batch, SUBLANES), chunk), :]
              + cw_ref[1:2, :] * xs_ref[pl.ds(pl.multiple_of(r0 + batch, SUBLANES), chunk), :]
              + cw_ref[0:1, :] * xs_ref[pl.ds(r0, chunk), :])
        if reset_first:
            grow = lax.broadcasted_iota(jnp.int32, (chunk, LANES), 0) + (r0 + pid * rows)
            first = grow < batch
        for n in range(D // LANES):
            sl = slice(n * LANES, (n + 1) * LANES)
            xn = xc[:, sl]
            g2 = jnp.dot(xn.astype(bf16), wxa_ref[n], preferred_element_type=f32)
            gate_x = jax.nn.sigmoid(g2[:, :LANES] + bx_ref[:, sl])
            gate_a = jax.nn.sigmoid(g2[:, LANES:] + ba_ref[:, sl])
            log_a = LRU_C * gate_a * logsig[:, sl]
            a = jnp.exp(log_a)
            mult = jnp.sqrt(-jnp.tanh(log_a) * (a * a + 1.0))
            if reset_first:
                mult = jnp.where(first, 1.0, mult)
            a_ref[pl.ds(r0, chunk), sl] = a
            b_ref[pl.ds(r0, chunk), sl] = xn * gate_x * mult
        return 0

    lax.fori_loop(0, rows // chunk, gates, 0)

    def scan(t, h):
        r0 = pl.multiple_of(t * batch, batch)
        h = a_ref[pl.ds(r0, batch), :] * h + b_ref[pl.ds(r0, batch), :]
        b_ref[pl.ds(r0, batch), :] = h
        return h

    h = lax.fori_loop(0, nt, scan, h_ref[...], unroll=(8 if nt >= 8 and batch == SUBLANES else 1))
    h_ref[...] = h
    hl_ref[...] = h
    xs_ref[0:hist, :] = xs_ref[rows:rows + hist, :]

    def outp(c, _):
        r0 = pl.multiple_of(c * chunk, chunk)
        o_ref[pl.ds(r0, chunk), :] = (b_ref[pl.ds(r0, chunk), :] * jax.nn.gelu(y_ref[pl.ds(r0, chunk), :])
                                      * jax.nn.sigmoid(gl_ref[pl.ds(r0, chunk), :]))
        return 0

    lax.fori_loop(0, rows // chunk, outp, 0)


def _with_dst(kern, n_in):
    def wrapped(*refs):
        return kern(*refs[:n_in], *refs[n_in + 1:])
    return wrapped


def _shared_rows(dst, n_total):
    if dst is None:
        return [], [], jax.ShapeDtypeStruct((n_total, D), f32)
    return [pl.BlockSpec(memory_space=pl.ANY)], [dst], jax.ShapeDtypeStruct(dst.shape, dst.dtype)


def _lru(proj, row_off, n_rows, batch, rows_blk, reset_first, conv_init, h0, cw, cb, wxa, bx, ba, ap, dst=None):
    ob = row_off // rows_blk
    kern = functools.partial(_lru_kernel, batch=batch, reset_first=reset_first, chunk=128)
    n_in = 11
    dst_spec, dst_arg, out0 = _shared_rows(dst, proj.shape[0])
    return pl.pallas_call(
        kern if dst is None else _with_dst(kern, n_in),
        grid=(n_rows // rows_blk,),
        in_specs=[
            pl.BlockSpec((rows_blk, D), lambda i: (ob + i, COL_LRU_X)),
            pl.BlockSpec((rows_blk, D), lambda i: (ob + i, COL_LRU_Y)),
            pl.BlockSpec((rows_blk, D), lambda i: (ob + i, COL_GL)),
            _full((3 * batch, D)), _full((batch, D)), _full((4, D)), _full((1, D)),
            _full((D // LANES, LANES, 2 * LANES)), _full((1, D)), _full((1, D)), _full((1, D)),
        ] + dst_spec,
        out_specs=[pl.BlockSpec((rows_blk, D), lambda i: (ob + i, 0)), _full((batch, D))],
        out_shape=[out0, jax.ShapeDtypeStruct((batch, D), f32)],
        input_output_aliases={} if dst is None else {n_in: 0},
        scratch_shapes=[pltpu.VMEM((rows_blk + 3 * batch, D), f32), pltpu.VMEM((rows_blk, D), f32),
                        pltpu.VMEM((rows_blk, D), f32), pltpu.VMEM((batch, D), f32)],
        compiler_params=pltpu.CompilerParams(dimension_semantics=("arbitrary",), vmem_limit_bytes=48 * MIB),
        name="lru",
    )(proj, proj, proj, conv_init, h0, cw, cb, wxa, bx, ba, ap, *dst_arg)


TILE_ALPHA, TILE_BETA, TILE_W, TILE_KP, TILE_WR, TILE_V = range(6)
ROW_BR, ROW_KR, ROW_BONUS = range(3)
CONST_KK, CONST_KA, CONST_RK, CONST_LNW, CONST_LNB = range(5)
NAT_R, NAT_W, NAT_K, NAT_V, NAT_A = range(5)


def _wkv_prep(tiles_ref, rows_ref, t, g, r, w, k, v, a, const):
    kk = k * const(CONST_KK)
    nrm = jnp.sqrt(jnp.sum(kk * kk, axis=0, keepdims=True))
    kk = kk * (1.0 / jnp.maximum(nrm, 1e-12))
    beta = kk * a
    kp = k * (1.0 + (a - 1.0) * const(CONST_KA))
    tiles_ref[t, g, TILE_ALPHA] = -kk
    tiles_ref[t, g, TILE_BETA] = beta
    tiles_ref[t, g, TILE_W] = w
    tiles_ref[t, g, TILE_KP] = kp
    tiles_ref[t, g, TILE_WR] = w * r
    tiles_ref[t, g, TILE_V] = v
    rows_ref[t, g, ROW_BR:ROW_BR + 1, :] = jnp.sum(beta * r, axis=0, keepdims=True)
    rows_ref[t, g, ROW_KR:ROW_KR + 1, :] = jnp.sum(kp * r, axis=0, keepdims=True)
    rows_ref[t, g, ROW_BONUS:ROW_BONUS + 1, :] = jnp.sum(r * kp * const(CONST_RK), axis=0, keepdims=True)


def _rwkv_pre_kernel(r_ref, k_ref, v_ref, l_ref, sinit_ref, mu_ref, w0_ref, a0_ref, w2_ref, a2_ref, g2_ref, c_ref,
                     tiles_ref, rows_ref, go_ref, ps_ref, nat_ref, *, batch, chunk, chains_are_heads):
    rows = r_ref.shape[0]

    @pl.when(pl.program_id(0) == 0)
    def _():
        ps_ref[0:batch, :] = sinit_ref[...]

    ps_ref[batch:batch + rows, 0:D] = r_ref[...]
    ps_ref[batch:batch + rows, D:2 * D] = k_ref[...]
    ps_ref[batch:batch + rows, 2 * D:3 * D] = v_ref[...]
    ps_ref[batch:batch + rows, 3 * D:3 * D + LORA_PAD] = l_ref[...]

    def body(c, _):
        r0 = pl.multiple_of(c * chunk, chunk)

        def mixed(lo, hi):
            cur = ps_ref[pl.ds(pl.multiple_of(r0 + batch, SUBLANES), chunk), lo:hi]
            prev = ps_ref[pl.ds(r0, chunk), lo:hi]
            return cur + (prev - cur) * mu_ref[:, lo:hi]

        nat_ref[NAT_R, pl.ds(r0, chunk), :] = mixed(0, D)
        nat_ref[NAT_K, pl.ds(r0, chunk), :] = mixed(D, 2 * D)
        nat_ref[NAT_V, pl.ds(r0, chunk), :] = mixed(2 * D, 3 * D)
        lm = mixed(3 * D, 3 * D + LORA_PAD)
        xwa = lm[:, 0:LANES]
        xg = lm[:, LANES:3 * LANES]
        lw = jnp.dot(jnp.tanh(xwa).astype(bf16), w2_ref[...], preferred_element_type=f32)
        w_log = -jax.nn.softplus(-(w0_ref[...] + lw)) - 0.5
        nat_ref[NAT_W, pl.ds(r0, chunk), :] = jnp.exp(-jnp.exp(w_log))
        la = jnp.dot(xwa.astype(bf16), a2_ref[...], preferred_element_type=f32)
        nat_ref[NAT_A, pl.ds(r0, chunk), :] = jax.nn.sigmoid(a0_ref[...] + la)
        go_ref[pl.ds(r0, chunk), :] = jnp.dot(jax.nn.sigmoid(xg).astype(bf16), g2_ref[...],
                                              preferred_element_type=f32)
        return 0

    lax.fori_loop(0, rows // chunk, body, 0)
    ps_ref[0:batch, :] = ps_ref[rows:rows + batch, :]

    if chains_are_heads:
        def tstep(t, _):
            r0 = pl.multiple_of(t * batch, batch)
            for hp in range(D // LANES):
                trs = [nat_ref[q, pl.ds(r0, batch), hp * LANES:(hp + 1) * LANES].T for q in range(5)]
                for h2 in range(2):
                    h = 2 * hp + h2
                    _wkv_prep(tiles_ref, rows_ref, t, h, *[tr[h2 * HD:(h2 + 1) * HD] for tr in trs],
                              lambda i, h=h: c_ref[h, i])
            return 0

        lax.fori_loop(0, rows // batch, tstep, 0)
    else:
        half = LANES // 2
        lane = lax.broadcasted_iota(jnp.int32, (HD, LANES), 1)

        def load_pair(q, tp):
            zz = jnp.concatenate(
                [nat_ref[q, pl.ds(pl.multiple_of(tp * 2 * SUBLANES + tt * SUBLANES, SUBLANES), SUBLANES),
                         hp * LANES:(hp + 1) * LANES] for tt in range(2) for hp in range(D // LANES)], axis=0)
            tr = zz.T
            top, bot = tr[0:HD], tr[HD:2 * HD]
            d0 = jnp.where(lane < half, top, pltpu.roll(bot, half, axis=1))
            d1 = jnp.where(lane < half, pltpu.roll(top, half, axis=1), bot)
            return d0, d1

        def pair(tp, _):
            pairs = [load_pair(q, tp) for q in range(5)]
            for tt in range(2):
                _wkv_prep(tiles_ref, rows_ref, 2 * tp + tt, 0, *[p[tt] for p in pairs], lambda i: c_ref[0, i])
            return 0

        lax.fori_loop(0, rows // (2 * SUBLANES), pair, 0, unroll=2)


def _rwkv_pre(proj, row_off, n_rows, batch, rows_blk, groups, shift_init, consts, mu, w0, a0, w2p, a2p, g2p,
              dst=None):
    ob = row_off // rows_blk
    width = 3 * D + LORA_PAD
    steps_blk = rows_blk // batch
    kern = functools.partial(_rwkv_pre_kernel, batch=batch, chunk=128, chains_are_heads=groups > 1)
    n_in = 12
    dst_spec, dst_arg, out_g = _shared_rows(dst, proj.shape[0])
    return pl.pallas_call(
        kern if dst is None else _with_dst(kern, n_in),
        grid=(n_rows // rows_blk,),
        in_specs=[
            pl.BlockSpec((rows_blk, D), lambda i: (ob + i, COL_R)),
            pl.BlockSpec((rows_blk, D), lambda i: (ob + i, COL_K)),
            pl.BlockSpec((rows_blk, D), lambda i: (ob + i, COL_V)),
            pl.BlockSpec((rows_blk, LORA_PAD), lambda i: (ob + i, COL_LORA)),
            _full((batch, width)), _full((1, width)), _full((1, D)), _full((1, D)),
            _full((LANES, D)), _full((LANES, D)), _full((2 * LANES, D)), _full((groups, 5, HD, LANES)),
        ] + dst_spec,
        out_specs=[pl.BlockSpec((steps_blk, groups, 6, HD, LANES), lambda i: (i, 0, 0, 0, 0)),
                   pl.BlockSpec((steps_blk, groups, SUBLANES, LANES), lambda i: (i, 0, 0, 0)),
                   pl.BlockSpec((rows_blk, D), lambda i: (ob + i, 0))],
        out_shape=[jax.ShapeDtypeStruct((n_rows // batch, groups, 6, HD, LANES), f32),
                   jax.ShapeDtypeStruct((n_rows // batch, groups, SUBLANES, LANES), f32),
                   out_g],
        input_output_aliases={} if dst is None else {n_in: 2},
        scratch_shapes=[pltpu.VMEM((rows_blk + batch, width), f32), pltpu.VMEM((5, rows_blk, D), f32)],
        compiler_params=pltpu.CompilerParams(dimension_semantics=("arbitrary",), vmem_limit_bytes=56 * MIB),
        name="rwkv_pre",
    )(proj, proj, proj, proj, shift_init, mu, w0, a0, w2p, a2p, g2p, consts, *dst_arg)


STATE_ROWS = HD * HD
VALUE_ROWS = 64


def _wkv_step(s_ref, tiles_ref, rows_ref, y_ref, t):
    for part in range(HD // VALUE_ROWS):
        lo = part * VALUE_ROWS
        sa = y0 = None
        for j in range(HD):
            s = s_ref[j * HD + lo:j * HD + lo + VALUE_ROWS, :]
            pa = s * tiles_ref[t, 0, TILE_ALPHA, j:j + 1, :]
            pr = s * tiles_ref[t, 0, TILE_WR, j:j + 1, :]
            sa = pa if sa is None else sa + pa
            y0 = pr if y0 is None else y0 + pr
        v = tiles_ref[t, 0, TILE_V, lo:lo + VALUE_ROWS, :]
        for j in range(HD):
            rows = slice(j * HD + lo, j * HD + lo + VALUE_ROWS)
            s_ref[rows, :] = (s_ref[rows, :] * tiles_ref[t, 0, TILE_W, j:j + 1, :]
                              + sa * tiles_ref[t, 0, TILE_BETA, j:j + 1, :]
                              + v * tiles_ref[t, 0, TILE_KP, j:j + 1, :])
        y_ref[t, 0, lo:lo + VALUE_ROWS, :] = (y0 + sa * rows_ref[t, 0, ROW_BR:ROW_BR + 1, :]
                                              + v * rows_ref[t, 0, ROW_KR:ROW_KR + 1, :])


def _wkv_scan_kernel(tiles_ref, rows_ref, st_ref, y_ref, sfin_ref, s_ref, *, steps):
    tb = pl.program_id(1)

    def value_rows(i):
        return pl.ds(i, HD, stride=HD)

    @pl.when(tb == 0)
    def _():
        def init(c, _):
            tr = st_ref[:, pl.ds(pl.multiple_of(c * LANES, LANES), LANES)].T
            for i2 in range(2):
                s_ref[value_rows(2 * c + i2), :] = tr[i2 * HD:(i2 + 1) * HD]
            return 0

        lax.fori_loop(0, STATE_ROWS // LANES, init, 0)

    def step(t, _):
        _wkv_step(s_ref, tiles_ref, rows_ref, y_ref, t)
        return 0

    lax.fori_loop(0, steps, step, 0)

    @pl.when(tb == pl.num_programs(1) - 1)
    def _():
        def fin(c, _):
            pair = jnp.concatenate([s_ref[value_rows(2 * c + i2), :] for i2 in range(2)], axis=0)
            sfin_ref[:, pl.ds(pl.multiple_of(c * LANES, LANES), LANES)] = pair.T
            return 0

        lax.fori_loop(0, STATE_ROWS // LANES, fin, 0)


def _wkv_scan(tiles, rows, state, steps_blk):
    n_steps, groups = tiles.shape[0], tiles.shape[1]
    kern = functools.partial(_wkv_scan_kernel, steps=steps_blk)
    sblk = pl.BlockSpec((LANES, STATE_ROWS), lambda g, i: (0, g))
    return pl.pallas_call(
        kern,
        grid=(groups, n_steps // steps_blk),
        in_specs=[pl.BlockSpec((steps_blk, 1, 6, HD, LANES), lambda g, i: (i, g, 0, 0, 0)),
                  pl.BlockSpec((steps_blk, 1, SUBLANES, LANES), lambda g, i: (i, g, 0, 0)),
                  sblk],
        out_specs=[pl.BlockSpec((steps_blk, 1, HD, LANES), lambda g, i: (i, g, 0, 0)), sblk],
        out_shape=[jax.ShapeDtypeStruct((n_steps, groups, HD, LANES), f32),
                   jax.ShapeDtypeStruct((LANES, groups * STATE_ROWS), f32)],
        scratch_shapes=[pltpu.VMEM((STATE_ROWS, LANES), f32)],
        compiler_params=pltpu.CompilerParams(dimension_semantics=("arbitrary", "arbitrary"),
                                             vmem_limit_bytes=40 * MIB),
        name="wkv_scan",
    )(tiles, rows, state)


def _wkv_norm(y, v, bonus, lnw, lnb):
    mean = jnp.sum(y, axis=0, keepdims=True) * (1.0 / HD)
    d = y - mean
    var = jnp.sum(d * d, axis=0, keepdims=True) * (1.0 / HD)
    return d * lax.rsqrt(var + GN_EPS) * lnw + lnb + bonus * v


def _wkv_post_kernel(y_ref, v_ref, rows_ref, c_ref, o_ref, *, batch, chains_are_heads):
    steps = y_ref.shape[0]

    def normed(t, g):
        return _wkv_norm(y_ref[t, g], v_ref[t, g, 0], rows_ref[t, g, ROW_BONUS:ROW_BONUS + 1, :],
                         c_ref[g, CONST_LNW], c_ref[g, CONST_LNB])

    if chains_are_heads:
        def tstep(t, _):
            r0 = pl.multiple_of(t * batch, batch)
            for hp in range(D // LANES):
                two = jnp.concatenate([normed(t, 2 * hp), normed(t, 2 * hp + 1)], axis=0)
                o_ref[pl.ds(r0, batch), hp * LANES:(hp + 1) * LANES] = two.T
            return 0

        lax.fori_loop(0, steps, tstep, 0)
    else:
        half = LANES // 2
        lane = lax.broadcasted_iota(jnp.int32, (HD, LANES), 1)

        def pair(tp, _):
            o0 = normed(2 * tp, 0)
            o1 = normed(2 * tp + 1, 0)
            top = jnp.where(lane < half, o0, pltpu.roll(o1, half, axis=1))
            bot = jnp.where(lane < half, pltpu.roll(o0, half, axis=1), o1)
            zz = jnp.concatenate([top, bot], axis=0).T
            for tt in range(2):
                row = pl.multiple_of(tp * 2 * SUBLANES + tt * SUBLANES, SUBLANES)
                for hp in range(D // LANES):
                    src = tt * HD + hp * SUBLANES
                    o_ref[pl.ds(row, SUBLANES), hp * LANES:(hp + 1) * LANES] = zz[src:src + SUBLANES, :]
            return 0

        lax.fori_loop(0, steps // 2, pair, 0, unroll=2)


def _wkv_post(y, tiles, rows, consts, batch, steps_blk, row_off, n_total, dst=None):
    n_steps, groups = y.shape[0], y.shape[1]
    ob = row_off // (steps_blk * batch)
    kern = functools.partial(_wkv_post_kernel, batch=batch, chains_are_heads=groups > 1)
    n_in = 4
    dst_spec, dst_arg, out0 = _shared_rows(dst, n_total)
    return pl.pallas_call(
        kern if dst is None else _with_dst(kern, n_in),
        grid=(n_steps // steps_blk,),
        in_specs=[pl.BlockSpec((steps_blk, groups, HD, LANES), lambda i: (i, 0, 0, 0)),
                  pl.BlockSpec((steps_blk, groups, 1, HD, LANES), lambda i: (i, 0, TILE_V, 0, 0)),
                  pl.BlockSpec((steps_blk, groups, SUBLANES, LANES), lambda i: (i, 0, 0, 0)),
                  _full((groups, 5, HD, LANES))] + dst_spec,
        out_specs=pl.BlockSpec((steps_blk * batch, D), lambda i: (ob + i, 0)),
        out_shape=out0,
        input_output_aliases={} if dst is None else {n_in: 0},
        compiler_params=pltpu.CompilerParams(dimension_semantics=("arbitrary",), vmem_limit_bytes=40 * MIB),
        name="wkv_post",
    )(y, tiles, rows, consts, *dst_arg)


def _outproj_kernel(ml_ref, op_ref, g_ref, gr_ref, x_ref, wo_ref, n2_ref, rh_ref, rl_ref, rb_ref,
                    x1_ref, t_ref, comb_ref):
    tm = x_ref.shape[0]
    merged = ml_ref[...] + jax.nn.sigmoid(gr_ref[...]) * (op_ref[...] * g_ref[...])
    x1 = x_ref[...] + jnp.dot(merged.astype(bf16), wo_ref[...], preferred_element_type=f32)
    x1_ref[...] = x1
    t = x1 * lax.rsqrt(jnp.mean(x1 * x1, axis=-1, keepdims=True) + NORM_EPS) * n2_ref[...]
    th = t.astype(bf16)
    t_ref[...] = th
    tl = (t - th.astype(f32)).astype(bf16)
    nt_dims = (((1,), (1,)), ((), ()))
    lg = (lax.dot_general(rh_ref[...], th, nt_dims, preferred_element_type=f32)
          + lax.dot_general(rh_ref[...], tl, nt_dims, preferred_element_type=f32)
          + lax.dot_general(rl_ref[...], th, nt_dims, preferred_element_type=f32)) + rb_ref[...]
    row = lax.broadcasted_iota(jnp.int32, (EPG, tm), 0).astype(f32)
    neg = jnp.float32(-jnp.inf)
    glog = jnp.where(row < N_GROUPS, lg[0:EPG], neg)
    ge = jnp.exp(glog - jnp.max(glog, axis=0, keepdims=True))
    pg = ge / jnp.sum(ge, axis=0, keepdims=True)
    p_top = jnp.max(pg, axis=0, keepdims=True)
    g_idx = jnp.min(jnp.where(pg == p_top, row, EPG), axis=0, keepdims=True)
    le = jnp.zeros((EPG, tm), f32)
    for g in range(N_GROUPS):
        le = jnp.where(g_idx == g, lg[EPG * (g + 1):EPG * (g + 2)], le)
    qe = jnp.exp(le - jnp.max(le, axis=0, keepdims=True))
    q = qe / jnp.sum(qe, axis=0, keepdims=True)
    q1 = jnp.max(q, axis=0, keepdims=True)
    i1 = jnp.min(jnp.where(q == q1, row, EPG), axis=0, keepdims=True)
    qm = jnp.where(row == i1, -1.0, q)
    q2 = jnp.max(qm, axis=0, keepdims=True)
    i2 = jnp.min(jnp.where(qm == q2, row, EPG), axis=0, keepdims=True)
    qs = q1 + q2
    w_grp = jnp.where(row == i1, q1 / qs, 0.0) + jnp.where(row == i2, q2 / qs, 0.0)
    wp = w_grp * p_top
    zero = jnp.zeros((EPG, tm), f32)
    comb_t = jnp.concatenate([jnp.where(g_idx == g, wp, zero) for g in range(N_GROUPS)]
                             + [jnp.zeros((LANES - N_EXPERTS, tm), f32)], axis=0)
    for c in range(tm // LANES):
        comb_ref[c * LANES:(c + 1) * LANES, :] = comb_t[:, c * LANES:(c + 1) * LANES].T


def _outproj(ml, op, g, proj, x, wo, n2, rh, rl, rb, tm=512):
    n = x.shape[0]
    blk = pl.BlockSpec((tm, D), lambda i: (i, 0))
    return pl.pallas_call(
        _outproj_kernel,
        grid=(n // tm,),
        in_specs=[blk, blk, blk, pl.BlockSpec((tm, D), lambda i: (i, COL_GR)), blk,
                  _full((D, D)), _full((1, D)), _full((ROUTER_ROWS, D)), _full((ROUTER_ROWS, D)),
                  _full((ROUTER_ROWS, 1))],
        out_specs=[blk, blk, pl.BlockSpec((tm, LANES), lambda i: (i, 0))],
        out_shape=[jax.ShapeDtypeStruct((n, D), f32), jax.ShapeDtypeStruct((n, D), bf16),
                   jax.ShapeDtypeStruct((n, LANES), f32)],
        compiler_params=pltpu.CompilerParams(dimension_semantics=("arbitrary",), vmem_limit_bytes=48 * MIB),
        name="outproj_router",
    )(ml, op, g, proj, x, wo, n2, rh, rl, rb)


def _moe_kernel(t_ref, comb_ref, wgu_ref, wd_ref, x1_ref, fn_ref, yp_ref, ys_ref, acc_ref, yf_ref, *, prompt_tiles):
    i = pl.program_id(0)
    e = pl.program_id(1)
    tm = t_ref.shape[0]
    bp, tsteps = yp_ref.shape[0], yp_ref.shape[1]
    bs, ssteps = ys_ref.shape[0], ys_ref.shape[1]

    @pl.when(e == 0)
    def _():
        acc_ref[...] = jnp.zeros_like(acc_ref)

    gu = jnp.dot(t_ref[...], wgu_ref[0], preferred_element_type=f32)
    lane = lax.broadcasted_iota(jnp.int32, (tm, LANES), 1)
    c = jnp.sum(jnp.where(lane == e, comb_ref[...], 0.0), axis=1, keepdims=True)
    h = jax.nn.silu(gu[:, :D_EXPERT]) * gu[:, D_EXPERT:] * c
    acc_ref[...] += jnp.dot(h.astype(bf16), wd_ref[0], preferred_element_type=f32)

    last = e == pl.num_programs(1) - 1

    @pl.when(last)
    def _():
        xo = x1_ref[...] + acc_ref[...]
        y = xo * lax.rsqrt(jnp.mean(xo * xo, axis=-1, keepdims=True) + NORM_EPS) * fn_ref[...]
        for cb in range(D // LANES):
            yf_ref[cb] = y[:, cb * LANES:(cb + 1) * LANES]

    @pl.when(last & (i < prompt_tiles))
    def _():
        for b in range(bp):
            yp_ref[b] = jnp.concatenate([yf_ref[cb, pl.ds(b, tsteps, stride=bp), :] for cb in range(D // LANES)],
                                        axis=1)

    @pl.when(last & (i >= prompt_tiles))
    def _():
        for t in range(ssteps):
            ys_ref[:, t, :] = jnp.concatenate([yf_ref[cb, t * bs:(t + 1) * bs, :] for cb in range(D // LANES)],
                                              axis=1)


def _moe(t, comb, wgu, wd, x1, fn, prompt_shape, sample_shape, tm):
    n = t.shape[0]
    bp, tp, _ = prompt_shape
    bs, ts, _ = sample_shape
    assert tm % bp == 0 and tp % (tm // bp) == 0 and bs * ts == tm
    prompt_tiles = bp * tp // tm
    kern = functools.partial(_moe_kernel, prompt_tiles=prompt_tiles)
    return pl.pallas_call(
        kern,
        grid=(n // tm, N_EXPERTS),
        in_specs=[
            pl.BlockSpec((tm, D), lambda i, e: (i, 0)),
            pl.BlockSpec((tm, LANES), lambda i, e: (i, 0)),
            pl.BlockSpec((1, D, 2 * D_EXPERT), lambda i, e: (e, 0, 0)),
            pl.BlockSpec((1, D_EXPERT, D), lambda i, e: (e, 0, 0)),
            pl.BlockSpec((tm, D), lambda i, e: (i, 0)),
            pl.BlockSpec((1, D), lambda i, e: (0, 0)),
        ],
        out_specs=[pl.BlockSpec((bp, tm // bp, D), lambda i, e: (0, jnp.minimum(i, prompt_tiles - 1), 0)),
                   pl.BlockSpec((bs, ts, D), lambda i, e: (0, 0, 0))],
        out_shape=[jax.ShapeDtypeStruct(prompt_shape, f32), jax.ShapeDtypeStruct(sample_shape, f32)],
        scratch_shapes=[pltpu.VMEM((tm, D), f32), pltpu.VMEM((D // LANES, tm, LANES), f32)],
        compiler_params=pltpu.CompilerParams(dimension_semantics=("arbitrary", "arbitrary"),
                                             vmem_limit_bytes=58 * MIB),
        name="moe",
    )(t, comb, wgu, wd, x1, fn)


def _chain_tiles_prompt(vec, batch):
    t = vec.reshape(HEADS // 2, 2, HD)
    t = jnp.transpose(t, (2, 1, 0))
    return jnp.broadcast_to(t[..., None], (HD, 2, HEADS // 2, batch)).reshape(HD, LANES)


def _chain_tiles_sample(vec):
    return jnp.broadcast_to(vec.reshape(HEADS, HD)[..., None], (HEADS, HD, LANES))


def kernel(x_prompt, x_sample, state_conv, state_lru, state_shift, state_wkv, norm1, w_in, conv_w, conv_b, lru_wx, lru_bx, lru_wa, lru_ba, lru_a_param, rwkv_mu, rwkv_w0, rwkv_w2, rwkv_a0, rwkv_a2, rwkv_g2, rwkv_k_k, rwkv_k_a, rwkv_r_k, rwkv_ln_w, rwkv_ln_b, w_out, norm2, router_group, router_group_b, router_expert, router_expert_b, exp_gate, exp_up, exp_down, final_norm):
    bp, tp, _ = x_prompt.shape
    bs, ts, _ = x_sample.shape
    assert norm1.shape[0] == 1 and bp * HEADS == LANES and bs == LANES and tp % 64 == 0 and ts % 2 == 0
    n_p, n_s = bp * tp, bs * ts
    c_rw = 2 * D
    c_lora = c_rw + 3 * D
    c_gl = c_lora + LORA

    w = w_in[0]
    w_all = jnp.concatenate([w[:, :c_lora], w[:, c_gl:], w[:, c_lora:c_gl],
                             jnp.zeros((D, LORA_PAD - LORA), f32)], axis=1).astype(bf16)
    mu = rwkv_mu[0]
    mu_all = jnp.concatenate([mu[:3 * D], mu[3 * D:], jnp.zeros((LORA_PAD - LORA,), f32)])[None]
    w2p = jnp.concatenate([rwkv_w2[0], jnp.zeros((LANES - LORA_W, D), f32)]).astype(bf16)
    a2p = jnp.concatenate([jnp.zeros((LORA_W, D), f32), rwkv_a2[0]]).astype(bf16)
    g2p = jnp.concatenate([rwkv_g2[0], jnp.zeros((2 * LANES - LORA_G, D), f32)]).astype(bf16)
    wxa = jnp.concatenate([lru_wx[0], lru_wa[0]], axis=-1).astype(bf16)
    row = lambda v: v.reshape(1, -1)
    rk = rwkv_r_k[0].reshape(D)
    cvecs = (rwkv_k_k[0], rwkv_k_a[0], rk, rwkv_ln_w[0], rwkv_ln_b[0])
    consts_p = jnp.stack([_chain_tiles_prompt(v, bp) for v in cvecs])[None]
    consts_s = jnp.stack([_chain_tiles_sample(v) for v in cvecs], axis=1)
    rw = jnp.zeros((ROUTER_ROWS, D), f32)
    rw = rw.at[0:N_GROUPS].set(router_group[0].T).at[EPG:EPG + N_EXPERTS].set(router_expert[0].T)
    rh = rw.astype(bf16)
    rl = (rw - rh.astype(f32)).astype(bf16)
    rb = jnp.zeros((ROUTER_ROWS, 1), f32)
    rb = rb.at[0:N_GROUPS, 0].set(router_group_b[0]).at[EPG:EPG + N_EXPERTS, 0].set(router_expert_b[0])
    wgu = jnp.concatenate([exp_gate[0], exp_up[0]], axis=-1).astype(bf16)
    wd = exp_down[0].astype(bf16)
    wo = w_out[0].astype(bf16)

    n = n_p + n_s
    proj, x = _inproj(x_prompt, x_sample, row(norm1[0]), w_all, tm=TOKEN_TILE)

    lru_args = (conv_w[0], row(conv_b[0]), wxa, row(lru_bx[0]), row(lru_ba[0]), row(lru_a_param[0]))
    conv_init_s = jnp.transpose(state_conv[0], (1, 0, 2)).reshape(3 * bs, D)
    ml, hl_p = _lru(proj, 0, n_p, bp, 512, True, jnp.zeros((3 * bp, D), f32), jnp.zeros((bp, D), f32), *lru_args)
    ml, hl_s = _lru(proj, n_p, n_s, bs, 512, False, conv_init_s, state_lru[0], *lru_args, dst=ml)

    sh = state_shift[0]
    shift_init_s = jnp.concatenate([sh, jnp.zeros((bs, LORA_PAD - LORA), f32)], axis=1)
    pre_args = (mu_all, row(rwkv_w0[0]), row(rwkv_a0[0]), w2p, a2p, g2p)
    tiles_p, rows_p, g = _rwkv_pre(proj, 0, n_p, bp, 256, 1, jnp.zeros((bp, 3 * D + LORA_PAD), f32),
                                   consts_p, *pre_args)
    tiles_s, rows_s, g = _rwkv_pre(proj, n_p, n_s, bs, 256, HEADS, shift_init_s, consts_s, *pre_args, dst=g)

    y_p, sfin_p = _wkv_scan(tiles_p, rows_p, jnp.zeros((LANES, STATE_ROWS), f32), 16)
    y_s, sfin_s = _wkv_scan(tiles_s, rows_s, state_wkv[0].reshape(bs, HEADS * STATE_ROWS), ts)
    op = _wkv_post(y_p, tiles_p, rows_p, consts_p, bp, 64, 0, n)
    op = _wkv_post(y_s, tiles_s, rows_s, consts_s, bs, 2, n_p, n, dst=op)

    x1, t, comb = _outproj(ml, op, g, proj, x, wo, row(norm2[0]), rh, rl, rb)
    y_prompt, y_sample = _moe(t, comb, wgu, wd, x1, row(final_norm), x_prompt.shape, x_sample.shape, tm=TOKEN_TILE)

    def last_rows(lo, hi, n_end, batch, steps):
        return proj[n_end - steps * batch:n_end, lo:hi].reshape(steps, batch, hi - lo)

    conv_p = jnp.transpose(last_rows(0, D, n_p, bp, 3), (1, 0, 2))[None]
    conv_s = jnp.transpose(last_rows(0, D, n_p + n_s, bs, 3), (1, 0, 2))[None]

    def shift_rows(n_end, batch):
        return jnp.concatenate([last_rows(2 * D, 5 * D, n_end, batch, 1)[0],
                                last_rows(7 * D, 7 * D + LORA, n_end, batch, 1)[0]], axis=1)[None]

    wkv_p = jnp.transpose(sfin_p.reshape(2, HEADS // 2, bp, HD, HD), (2, 1, 0, 3, 4)).reshape(1, bp, HEADS, HD, HD)
    wkv_s = sfin_s.reshape(1, bs, HEADS, HD, HD)
    return (y_prompt, y_sample, conv_p, hl_p[None], shift_rows(n_p, bp), wkv_p,
            conv_s, hl_s[None], shift_rows(n_p + n_s, bs), wkv_s)
```

```python
import functools

import jax
import jax.numpy as jnp
from jax import lax
from jax.experimental import pallas as pl
from jax.experimental.pallas import tpu as pltpu

f32 = jnp.float32
bf16 = jnp.bfloat16

D = 1024
HEADS = 16
HD = 64
LANES = 128
SUBLANES = 8
LORA_W = 64
LORA_A = 64
LORA_G = 160
LORA = LORA_W + LORA_A + LORA_G
LORA_PAD = 512
N_GROUPS = 4
EPG = 8
N_EXPERTS = N_GROUPS * EPG
D_EXPERT = 256
LRU_C = 8.0
GN_EPS = 64e-5
NORM_EPS = 1e-6
PROJ_COLS = 7 * D + LORA_PAD
COL_LRU_X, COL_LRU_Y, COL_R, COL_K, COL_V, COL_GL, COL_GR = range(7)
COL_LORA = 7 * D // LORA_PAD
ROUTER_ROWS = 48
MIB = 1024 * 1024
TOKEN_TILE = 1024


def _full(shape, grid_rank=1):
    zeros = tuple(0 for _ in shape)
    if grid_rank == 1:
        return pl.BlockSpec(shape, lambda i: zeros)
    return pl.BlockSpec(shape, lambda i, j: zeros)


def _inproj_kernel(xp_ref, xs_ref, g_ref, w_ref, o_ref, xo_ref, xf_ref, xn_ref, *, prompt_tiles):
    i = pl.program_id(0)
    first_col = pl.program_id(1) == 0
    bp, tsteps = xp_ref.shape[0], xp_ref.shape[1]
    bs, ssteps = xs_ref.shape[0], xs_ref.shape[1]

    @pl.when(first_col & (i < prompt_tiles))
    def _():
        for b in range(bp):
            v = xp_ref[b]
            for cb in range(D // LANES):
                xf_ref[cb, pl.ds(b, tsteps, stride=bp), :] = v[:, cb * LANES:(cb + 1) * LANES]

    @pl.when(first_col & (i >= prompt_tiles))
    def _():
        for t in range(ssteps):
            v = xs_ref[:, t, :]
            for cb in range(D // LANES):
                xf_ref[cb, t * bs:(t + 1) * bs, :] = v[:, cb * LANES:(cb + 1) * LANES]

    @pl.when(first_col)
    def _():
        x = jnp.concatenate([xf_ref[cb] for cb in range(D // LANES)], axis=1)
        xo_ref[...] = x
        ms = jnp.mean(x * x, axis=-1, keepdims=True)
        xn_ref[...] = (x * lax.rsqrt(ms + NORM_EPS) * g_ref[...]).astype(bf16)

    o_ref[...] = jnp.dot(xn_ref[...], w_ref[...], preferred_element_type=f32)


def _inproj(x_prompt, x_sample, gain, w, tm, tn=1280):
    bp, tp, _ = x_prompt.shape
    bs, ts, _ = x_sample.shape
    assert tm % bp == 0 and tp % (tm // bp) == 0 and bs * ts == tm
    prompt_tiles = bp * tp // tm
    n = bp * tp + bs * ts
    kern = functools.partial(_inproj_kernel, prompt_tiles=prompt_tiles)
    return pl.pallas_call(
        kern,
        grid=(n // tm, PROJ_COLS // tn),
        in_specs=[
            pl.BlockSpec((bp, tm // bp, D), lambda i, j: (0, jnp.minimum(i, prompt_tiles - 1), 0)),
            pl.BlockSpec((bs, ts, D), lambda i, j: (0, 0, 0)),
            pl.BlockSpec((1, D), lambda i, j: (0, 0)),
            pl.BlockSpec((D, tn), lambda i, j: (0, j)),
        ],
        out_specs=[pl.BlockSpec((tm, tn), lambda i, j: (i, j)), pl.BlockSpec((tm, D), lambda i, j: (i, 0))],
        out_shape=[jax.ShapeDtypeStruct((n, PROJ_COLS), f32), jax.ShapeDtypeStruct((n, D), f32)],
        scratch_shapes=[pltpu.VMEM((D // LANES, tm, LANES), f32), pltpu.VMEM((tm, D), bf16)],
        compiler_params=pltpu.CompilerParams(dimension_semantics=("arbitrary", "arbitrary"),
                                             vmem_limit_bytes=56 * MIB),
        name="inproj",
    )(x_prompt, x_sample, gain, w)


def _lru_kernel(x_ref, y_ref, gl_ref, cinit_ref, h0_ref, cw_ref, cb_ref, wxa_ref, bx_ref, ba_ref, ap_ref,
                o_ref, hl_ref, xs_ref, a_ref, b_ref, h_ref, *, batch, reset_first, chunk):
    rows = x_ref.shape[0]
    nt = rows // batch
    hist = 3 * batch
    pid = pl.program_id(0)

    @pl.when(pid == 0)
    def _():
        xs_ref[0:hist, :] = cinit_ref[...]
        h_ref[...] = h0_ref[...]

    xs_ref[hist:hist + rows, :] = x_ref[...]
    logsig = -jax.nn.softplus(-ap_ref[...])

    def gates(c, _):
        r0 = pl.multiple_of(c * chunk, chunk)
        xc = (cb_ref[...]
              + cw_ref[3:4, :] * xs_ref[pl.ds(pl.multiple_of(r0 + hist, SUBLANES), chunk), :]
              + cw_ref[2:3, :] * xs_ref[pl.ds(pl.multiple_of(r0 + 2 * batch, SUBLANES), chunk), :]
              + cw_ref[1:2, :] * xs_ref[pl.ds(pl.multiple_of(r0 + batch, SUBLANES), chunk), :]
              + cw_ref[0:1, :] * xs_ref[pl.ds(r0, chunk), :])
        if reset_first:
            grow = lax.broadcasted_iota(jnp.int32, (chunk, LANES), 0) + (r0 + pid * rows)
            first = grow < batch
        for n in range(D // LANES):
            sl = slice(n * LANES, (n + 1) * LANES)
            xn = xc[:, sl]
            g2 = jnp.dot(xn.astype(bf16), wxa_ref[n], preferred_element_type=f32)
            gate_x = jax.nn.sigmoid(g2[:, :LANES] + bx_ref[:, sl])
            gate_a = jax.nn.sigmoid(g2[:, LANES:] + ba_ref[:, sl])
            log_a = LRU_C * gate_a * logsig[:, sl]
            a = jnp.exp(log_a)
            mult = jnp.sqrt(-jnp.tanh(log_a) * (a * a + 1.0))
            if reset_first:
                mult = jnp.where(first, 1.0, mult)
            a_ref[pl.ds(r0, chunk), sl] = a
            b_ref[pl.ds(r0, chunk), sl] = xn * gate_x * mult
        return 0

    lax.fori_loop(0, rows // chunk, gates, 0)

    def scan(t, h):
        r0 = pl.multiple_of(t * batch, batch)
        h = a_ref[pl.ds(r0, batch), :] * h + b_ref[pl.ds(r0, batch), :]
        b_ref[pl.ds(r0, batch), :] = h
        return h

    h = lax.fori_loop(0, nt, scan, h_ref[...], unroll=(8 if nt >= 8 and batch == SUBLANES else 1))
    h_ref[...] = h
    hl_ref[...] = h
    xs_ref[0:hist, :] = xs_ref[rows:rows + hist, :]

    def outp(c, _):
        r0 = pl.multiple_of(c * chunk, chunk)
        o_ref[pl.ds(r0, chunk), :] = (b_ref[pl.ds(r0, chunk), :] * jax.nn.gelu(y_ref[pl.ds(r0, chunk), :])
                                      * jax.nn.sigmoid(gl_ref[pl.ds(r0, chunk), :]))
        return 0

    lax.fori_loop(0, rows // chunk, outp, 0)


def _with_dst(kern, n_in):
    def wrapped(*refs):
        return kern(*refs[:n_in], *refs[n_in + 1:])
    return wrapped


def _shared_rows(dst, n_total):
    if dst is None:
        return [], [], jax.ShapeDtypeStruct((n_total, D), f32)
    return [pl.BlockSpec(memory_space=pl.ANY)], [dst], jax.ShapeDtypeStruct(dst.shape, dst.dtype)


def _lru(proj, row_off, n_rows, batch, rows_blk, reset_first, conv_init, h0, cw, cb, wxa, bx, ba, ap, dst=None):
    ob = row_off // rows_blk
    kern = functools.partial(_lru_kernel, batch=batch, reset_first=reset_first, chunk=128)
    n_in = 11
    dst_spec, dst_arg, out0 = _shared_rows(dst, proj.shape[0])
    return pl.pallas_call(
        kern if dst is None else _with_dst(kern, n_in),
        grid=(n_rows // rows_blk,),
        in_specs=[
            pl.BlockSpec((rows_blk, D), lambda i: (ob + i, COL_LRU_X)),
            pl.BlockSpec((rows_blk, D), lambda i: (ob + i, COL_LRU_Y)),
            pl.BlockSpec((rows_blk, D), lambda i: (ob + i, COL_GL)),
            _full((3 * batch, D)), _full((batch, D)), _full((4, D)), _full((1, D)),
            _full((D // LANES, LANES, 2 * LANES)), _full((1, D)), _full((1, D)), _full((1, D)),
        ] + dst_spec,
        out_specs=[pl.BlockSpec((rows_blk, D), lambda i: (ob + i, 0)), _full((batch, D))],
        out_shape=[out0, jax.ShapeDtypeStruct((batch, D), f32)],
        input_output_aliases={} if dst is None else {n_in: 0},
        scratch_shapes=[pltpu.VMEM((rows_blk + 3 * batch, D), f32), pltpu.VMEM((rows_blk, D), f32),
                        pltpu.VMEM((rows_blk, D), f32), pltpu.VMEM((batch, D), f32)],
        compiler_params=pltpu.CompilerParams(dimension_semantics=("arbitrary",), vmem_limit_bytes=48 * MIB),
        name="lru",
    )(proj, proj, proj, conv_init, h0, cw, cb, wxa, bx, ba, ap, *dst_arg)


TILE_ALPHA, TILE_BETA, TILE_W, TILE_KP, TILE_WR, TILE_V = range(6)
ROW_BR, ROW_KR, ROW_BONUS = range(3)
CONST_KK, CONST_KA, CONST_RK, CONST_LNW, CONST_LNB = range(5)
NAT_R, NAT_W, NAT_K, NAT_V, NAT_A = range(5)


def _wkv_prep(tiles_ref, rows_ref, t, g, r, w, k, v, a, const):
    kk = k * const(CONST_KK)
    nrm = jnp.sqrt(jnp.sum(kk * kk, axis=0, keepdims=True))
    kk = kk * (1.0 / jnp.maximum(nrm, 1e-12))
    beta = kk * a
    kp = k * (1.0 + (a - 1.0) * const(CONST_KA))
    tiles_ref[t, g, TILE_ALPHA] = -kk
    tiles_ref[t, g, TILE_BETA] = beta
    tiles_ref[t, g, TILE_W] = w
    tiles_ref[t, g, TILE_KP] = kp
    tiles_ref[t, g, TILE_WR] = w * r
    tiles_ref[t, g, TILE_V] = v
    rows_ref[t, g, ROW_BR:ROW_BR + 1, :] = jnp.sum(beta * r, axis=0, keepdims=True)
    rows_ref[t, g, ROW_KR:ROW_KR + 1, :] = jnp.sum(kp * r, axis=0, keepdims=True)
    rows_ref[t, g, ROW_BONUS:ROW_BONUS + 1, :] = jnp.sum(r * kp * const(CONST_RK), axis=0, keepdims=True)


def _rwkv_pre_kernel(r_ref, k_ref, v_ref, l_ref, sinit_ref, mu_ref, w0_ref, a0_ref, w2_ref, a2_ref, g2_ref, c_ref,
                     tiles_ref, rows_ref, go_ref, ps_ref, nat_ref, *, batch, chunk, chains_are_heads):
    rows = r_ref.shape[0]

    @pl.when(pl.program_id(0) == 0)
    def _():
        ps_ref[0:batch, :] = sinit_ref[...]

    ps_ref[batch:batch + rows, 0:D] = r_ref[...]
    ps_ref[batch:batch + rows, D:2 * D] = k_ref[...]
    ps_ref[batch:batch + rows, 2 * D:3 * D] = v_ref[...]
    ps_ref[batch:batch + rows, 3 * D:3 * D + LORA_PAD] = l_ref[...]

    def body(c, _):
        r0 = pl.multiple_of(c * chunk, chunk)

        def mixed(lo, hi):
            cur = ps_ref[pl.ds(pl.multiple_of(r0 + batch, SUBLANES), chunk), lo:hi]
            prev = ps_ref[pl.ds(r0, chunk), lo:hi]
            return cur + (prev - cur) * mu_ref[:, lo:hi]

        nat_ref[NAT_R, pl.ds(r0, chunk), :] = mixed(0, D)
        nat_ref[NAT_K, pl.ds(r0, chunk), :] = mixed(D, 2 * D)
        nat_ref[NAT_V, pl.ds(r0, chunk), :] = mixed(2 * D, 3 * D)
        lm = mixed(3 * D, 3 * D + LORA_PAD)
        xwa = lm[:, 0:LANES]
        xg = lm[:, LANES:3 * LANES]
        lw = jnp.dot(jnp.tanh(xwa).astype(bf16), w2_ref[...], preferred_element_type=f32)
        w_log = -jax.nn.softplus(-(w0_ref[...] + lw)) - 0.5
        nat_ref[NAT_W, pl.ds(r0, chunk), :] = jnp.exp(-jnp.exp(w_log))
        la = jnp.dot(xwa.astype(bf16), a2_ref[...], preferred_element_type=f32)
        nat_ref[NAT_A, pl.ds(r0, chunk), :] = jax.nn.sigmoid(a0_ref[...] + la)
        go_ref[pl.ds(r0, chunk), :] = jnp.dot(jax.nn.sigmoid(xg).astype(bf16), g2_ref[...],
                                              preferred_element_type=f32)
        return 0

    lax.fori_loop(0, rows // chunk, body, 0)
    ps_ref[0:batch, :] = ps_ref[rows:rows + batch, :]

    if chains_are_heads:
        def tstep(t, _):
            r0 = pl.multiple_of(t * batch, batch)
            for hp in range(D // LANES):
                trs = [nat_ref[q, pl.ds(r0, batch), hp * LANES:(hp + 1) * LANES].T for q in range(5)]
                for h2 in range(2):
                    h = 2 * hp + h2
                    _wkv_prep(tiles_ref, rows_ref, t, h, *[tr[h2 * HD:(h2 + 1) * HD] for tr in trs],
                              lambda i, h=h: c_ref[h, i])
            return 0

        lax.fori_loop(0, rows // batch, tstep, 0)
    else:
        half = LANES // 2
        lane = lax.broadcasted_iota(jnp.int32, (HD, LANES), 1)

        def load_pair(q, tp):
            zz = jnp.concatenate(
                [nat_ref[q, pl.ds(pl.multiple_of(tp * 2 * SUBLANES + tt * SUBLANES, SUBLANES), SUBLANES),
                         hp * LANES:(hp + 1) * LANES] for tt in range(2) for hp in range(D // LANES)], axis=0)
            tr = zz.T
            top, bot = tr[0:HD], tr[HD:2 * HD]
            d0 = jnp.where(lane < half, top, pltpu.roll(bot, half, axis=1))
            d1 = jnp.where(lane < half, pltpu.roll(top, half, axis=1), bot)
            return d0, d1

        def pair(tp, _):
            pairs = [load_pair(q, tp) for q in range(5)]
            for tt in range(2):
                _wkv_prep(tiles_ref, rows_ref, 2 * tp + tt, 0, *[p[tt] for p in pairs], lambda i: c_ref[0, i])
            return 0

        lax.fori_loop(0, rows // (2 * SUBLANES), pair, 0, unroll=4)


def _rwkv_pre(proj, row_off, n_rows, batch, rows_blk, groups, shift_init, consts, mu, w0, a0, w2p, a2p, g2p,
              dst=None):
    ob = row_off // rows_blk
    width = 3 * D + LORA_PAD
    steps_blk = rows_blk // batch
    kern = functools.partial(_rwkv_pre_kernel, batch=batch, chunk=128, chains_are_heads=groups > 1)
    n_in = 12
    dst_spec, dst_arg, out_g = _shared_rows(dst, proj.shape[0])
    return pl.pallas_call(
        kern if dst is None else _with_dst(kern, n_in),
        grid=(n_rows // rows_blk,),
        in_specs=[
            pl.BlockSpec((rows_blk, D), lambda i: (ob + i, COL_R)),
            pl.BlockSpec((rows_blk, D), lambda i: (ob + i, COL_K)),
            pl.BlockSpec((rows_blk, D), lambda i: (ob + i, COL_V)),
            pl.BlockSpec((rows_blk, LORA_PAD), lambda i: (ob + i, COL_LORA)),
            _full((batch, width)), _full((1, width)), _full((1, D)), _full((1, D)),
            _full((LANES, D)), _full((LANES, D)), _full((2 * LANES, D)), _full((groups, 5, HD, LANES)),
        ] + dst_spec,
        out_specs=[pl.BlockSpec((steps_blk, groups, 6, HD, LANES), lambda i: (i, 0, 0, 0, 0)),
                   pl.BlockSpec((steps_blk, groups, SUBLANES, LANES), lambda i: (i, 0, 0, 0)),
                   pl.BlockSpec((rows_blk, D), lambda i: (ob + i, 0))],
        out_shape=[jax.ShapeDtypeStruct((n_rows // batch, groups, 6, HD, LANES), f32),
                   jax.ShapeDtypeStruct((n_rows // batch, groups, SUBLANES, LANES), f32),
                   out_g],
        input_output_aliases={} if dst is None else {n_in: 2},
        scratch_shapes=[pltpu.VMEM((rows_blk + batch, width), f32), pltpu.VMEM((5, rows_blk, D), f32)],
        compiler_params=pltpu.CompilerParams(dimension_semantics=("arbitrary",), vmem_limit_bytes=56 * MIB),
        name="rwkv_pre",
    )(proj, proj, proj, proj, shift_init, mu, w0, a0, w2p, a2p, g2p, consts, *dst_arg)


STATE_ROWS = HD * HD
KEY_UNROLL = 32


def _wkv_step(s_ref, tiles_ref, rows_ref, y_ref, t):
    def key_row(tile, j):
        return tiles_ref[t, 0, tile, pl.ds(j, 1), :]

    def reduce_keys(j, acc):
        sa, y0 = acc
        s = s_ref[pl.ds(pl.multiple_of(j * HD, HD), HD), :]
        return sa + s * key_row(TILE_ALPHA, j), y0 + s * key_row(TILE_WR, j)

    zero = jnp.zeros((HD, LANES), f32)
    sa, y0 = lax.fori_loop(0, HD, reduce_keys, (zero, zero), unroll=KEY_UNROLL)
    v = tiles_ref[t, 0, TILE_V]

    def update_keys(j, _):
        rows = pl.ds(pl.multiple_of(j * HD, HD), HD)
        s_ref[rows, :] = (s_ref[rows, :] * key_row(TILE_W, j) + sa * key_row(TILE_BETA, j)
                          + v * key_row(TILE_KP, j))
        return 0

    lax.fori_loop(0, HD, update_keys, 0, unroll=KEY_UNROLL)
    y_ref[t, 0] = y0 + sa * rows_ref[t, 0, ROW_BR:ROW_BR + 1, :] + v * rows_ref[t, 0, ROW_KR:ROW_KR + 1, :]


def _wkv_scan_kernel(tiles_ref, rows_ref, st_ref, y_ref, sfin_ref, s_ref, *, steps):
    tb = pl.program_id(1)

    def value_rows(i):
        return pl.ds(i, HD, stride=HD)

    @pl.when(tb == 0)
    def _():
        def init(c, _):
            tr = st_ref[:, pl.ds(pl.multiple_of(c * LANES, LANES), LANES)].T
            for i2 in range(2):
                s_ref[value_rows(2 * c + i2), :] = tr[i2 * HD:(i2 + 1) * HD]
            return 0

        lax.fori_loop(0, STATE_ROWS // LANES, init, 0, unroll=4)

    def step(t, _):
        _wkv_step(s_ref, tiles_ref, rows_ref, y_ref, t)
        return 0

    lax.fori_loop(0, steps, step, 0)

    @pl.when(tb == pl.num_programs(1) - 1)
    def _():
        def fin(c, _):
            pair = jnp.concatenate([s_ref[value_rows(2 * c + i2), :] for i2 in range(2)], axis=0)
            sfin_ref[:, pl.ds(pl.multiple_of(c * LANES, LANES), LANES)] = pair.T
            return 0

        lax.fori_loop(0, STATE_ROWS // LANES, fin, 0, unroll=4)


def _wkv_scan(tiles, rows, state, steps_blk):
    n_steps, groups = tiles.shape[0], tiles.shape[1]
    kern = functools.partial(_wkv_scan_kernel, steps=steps_blk)
    sblk = pl.BlockSpec((LANES, STATE_ROWS), lambda g, i: (0, g))
    return pl.pallas_call(
        kern,
        grid=(groups, n_steps // steps_blk),
        in_specs=[pl.BlockSpec((steps_blk, 1, 6, HD, LANES), lambda g, i: (i, g, 0, 0, 0)),
                  pl.BlockSpec((steps_blk, 1, SUBLANES, LANES), lambda g, i: (i, g, 0, 0)),
                  sblk],
        out_specs=[pl.BlockSpec((steps_blk, 1, HD, LANES), lambda g, i: (i, g, 0, 0)), sblk],
        out_shape=[jax.ShapeDtypeStruct((n_steps, groups, HD, LANES), f32),
                   jax.ShapeDtypeStruct((LANES, groups * STATE_ROWS), f32)],
        scratch_shapes=[pltpu.VMEM((STATE_ROWS, LANES), f32)],
        compiler_params=pltpu.CompilerParams(dimension_semantics=("arbitrary", "arbitrary"),
                                             vmem_limit_bytes=40 * MIB),
        name="wkv_scan",
    )(tiles, rows, state)


def _wkv_norm(y, v, bonus, lnw, lnb):
    mean = jnp.sum(y, axis=0, keepdims=True) * (1.0 / HD)
    d = y - mean
    var = jnp.sum(d * d, axis=0, keepdims=True) * (1.0 / HD)
    return d * lax.rsqrt(var + GN_EPS) * lnw + lnb + bonus * v


def _wkv_post_kernel(y_ref, v_ref, rows_ref, c_ref, o_ref, *, batch, chains_are_heads):
    steps = y_ref.shape[0]

    def normed(t, g):
        return _wkv_norm(y_ref[t, g], v_ref[t, g, 0], rows_ref[t, g, ROW_BONUS:ROW_BONUS + 1, :],
                         c_ref[g, CONST_LNW], c_ref[g, CONST_LNB])

    if chains_are_heads:
        def tstep(t, _):
            r0 = pl.multiple_of(t * batch, batch)
            for hp in range(D // LANES):
                two = jnp.concatenate([normed(t, 2 * hp), normed(t, 2 * hp + 1)], axis=0)
                o_ref[pl.ds(r0, batch), hp * LANES:(hp + 1) * LANES] = two.T
            return 0

        lax.fori_loop(0, steps, tstep, 0)
    else:
        half = LANES // 2
        lane = lax.broadcasted_iota(jnp.int32, (HD, LANES), 1)

        def pair(tp, _):
            o0 = normed(2 * tp, 0)
            o1 = normed(2 * tp + 1, 0)
            top = jnp.where(lane < half, o0, pltpu.roll(o1, half, axis=1))
            bot = jnp.where(lane < half, pltpu.roll(o0, half, axis=1), o1)
            zz = jnp.concatenate([top, bot], axis=0).T
            for tt in range(2):
                row = pl.multiple_of(tp * 2 * SUBLANES + tt * SUBLANES, SUBLANES)
                for hp in range(D // LANES):
                    src = tt * HD + hp * SUBLANES
                    o_ref[pl.ds(row, SUBLANES), hp * LANES:(hp + 1) * LANES] = zz[src:src + SUBLANES, :]
            return 0

        lax.fori_loop(0, steps // 2, pair, 0, unroll=2)


def _wkv_post(y, tiles, rows, consts, batch, steps_blk, row_off, n_total, dst=None):
    n_steps, groups = y.shape[0], y.shape[1]
    ob = row_off // (steps_blk * batch)
    kern = functools.partial(_wkv_post_kernel, batch=batch, chains_are_heads=groups > 1)
    n_in = 4
    dst_spec, dst_arg, out0 = _shared_rows(dst, n_total)
    return pl.pallas_call(
        kern if dst is None else _with_dst(kern, n_in),
        grid=(n_steps // steps_blk,),
        in_specs=[pl.BlockSpec((steps_blk, groups, HD, LANES), lambda i: (i, 0, 0, 0)),
                  pl.BlockSpec((steps_blk, groups, 1, HD, LANES), lambda i: (i, 0, TILE_V, 0, 0)),
                  pl.BlockSpec((steps_blk, groups, SUBLANES, LANES), lambda i: (i, 0, 0, 0)),
                  _full((groups, 5, HD, LANES))] + dst_spec,
        out_specs=pl.BlockSpec((steps_blk * batch, D), lambda i: (ob + i, 0)),
        out_shape=out0,
        input_output_aliases={} if dst is None else {n_in: 0},
        compiler_params=pltpu.CompilerParams(dimension_semantics=("arbitrary",), vmem_limit_bytes=40 * MIB),
        name="wkv_post",
    )(y, tiles, rows, consts, *dst_arg)


def _outproj_kernel(ml_ref, op_ref, g_ref, gr_ref, x_ref, wo_ref, n2_ref, rh_ref, rl_ref, rb_ref,
                    x1_ref, t_ref, comb_ref):
    tm = x_ref.shape[0]
    merged = ml_ref[...] + jax.nn.sigmoid(gr_ref[...]) * (op_ref[...] * g_ref[...])
    x1 = x_ref[...] + jnp.dot(merged.astype(bf16), wo_ref[...], preferred_element_type=f32)
    x1_ref[...] = x1
    t = x1 * lax.rsqrt(jnp.mean(x1 * x1, axis=-1, keepdims=True) + NORM_EPS) * n2_ref[...]
    th = t.astype(bf16)
    t_ref[...] = th
    tl = (t - th.astype(f32)).astype(bf16)
    nt_dims = (((1,), (1,)), ((), ()))
    lg = (lax.dot_general(rh_ref[...], th, nt_dims, preferred_element_type=f32)
          + lax.dot_general(rh_ref[...], tl, nt_dims, preferred_element_type=f32)
          + lax.dot_general(rl_ref[...], th, nt_dims, preferred_element_type=f32)) + rb_ref[...]
    row = lax.broadcasted_iota(jnp.int32, (EPG, tm), 0).astype(f32)
    neg = jnp.float32(-jnp.inf)
    glog = jnp.where(row < N_GROUPS, lg[0:EPG], neg)
    ge = jnp.exp(glog - jnp.max(glog, axis=0, keepdims=True))
    pg = ge / jnp.sum(ge, axis=0, keepdims=True)
    p_top = jnp.max(pg, axis=0, keepdims=True)
    g_idx = jnp.min(jnp.where(pg == p_top, row, EPG), axis=0, keepdims=True)
    le = jnp.zeros((EPG, tm), f32)
    for g in range(N_GROUPS):
        le = jnp.where(g_idx == g, lg[EPG * (g + 1):EPG * (g + 2)], le)
    qe = jnp.exp(le - jnp.max(le, axis=0, keepdims=True))
    q = qe / jnp.sum(qe, axis=0, keepdims=True)
    q1 = jnp.max(q, axis=0, keepdims=True)
    i1 = jnp.min(jnp.where(q == q1, row, EPG), axis=0, keepdims=True)
    qm = jnp.where(row == i1, -1.0, q)
    q2 = jnp.max(qm, axis=0, keepdims=True)
    i2 = jnp.min(jnp.where(qm == q2, row, EPG), axis=0, keepdims=True)
    qs = q1 + q2
    w_grp = jnp.where(row == i1, q1 / qs, 0.0) + jnp.where(row == i2, q2 / qs, 0.0)
    wp = w_grp * p_top
    zero = jnp.zeros((EPG, tm), f32)
    comb_t = jnp.concatenate([jnp.where(g_idx == g, wp, zero) for g in range(N_GROUPS)]
                             + [jnp.zeros((LANES - N_EXPERTS, tm), f32)], axis=0)
    for c in range(tm // LANES):
        comb_ref[c * LANES:(c + 1) * LANES, :] = comb_t[:, c * LANES:(c + 1) * LANES].T


def _outproj(ml, op, g, proj, x, wo, n2, rh, rl, rb, tm=512):
    n = x.shape[0]
    blk = pl.BlockSpec((tm, D), lambda i: (i, 0))
    return pl.pallas_call(
        _outproj_kernel,
        grid=(n // tm,),
        in_specs=[blk, blk, blk, pl.BlockSpec((tm, D), lambda i: (i, COL_GR)), blk,
                  _full((D, D)), _full((1, D)), _full((ROUTER_ROWS, D)), _full((ROUTER_ROWS, D)),
                  _full((ROUTER_ROWS, 1))],
        out_specs=[blk, blk, pl.BlockSpec((tm, LANES), lambda i: (i, 0))],
        out_shape=[jax.ShapeDtypeStruct((n, D), f32), jax.ShapeDtypeStruct((n, D), bf16),
                   jax.ShapeDtypeStruct((n, LANES), f32)],
        compiler_params=pltpu.CompilerParams(dimension_semantics=("arbitrary",), vmem_limit_bytes=48 * MIB),
        name="outproj_router",
    )(ml, op, g, proj, x, wo, n2, rh, rl, rb)


def _moe_kernel(t_ref, comb_ref, wgu_ref, wd_ref, x1_ref, fn_ref, yp_ref, ys_ref, acc_ref, yf_ref, *, prompt_tiles):
    i = pl.program_id(0)
    e = pl.program_id(1)
    tm = t_ref.shape[0]
    bp, tsteps = yp_ref.shape[0], yp_ref.shape[1]
    bs, ssteps = ys_ref.shape[0], ys_ref.shape[1]

    @pl.when(e == 0)
    def _():
        acc_ref[...] = jnp.zeros_like(acc_ref)

    gu = jnp.dot(t_ref[...], wgu_ref[0], preferred_element_type=f32)
    lane = lax.broadcasted_iota(jnp.int32, (tm, LANES), 1)
    c = jnp.sum(jnp.where(lane == e, comb_ref[...], 0.0), axis=1, keepdims=True)
    h = jax.nn.silu(gu[:, :D_EXPERT]) * gu[:, D_EXPERT:] * c
    acc_ref[...] += jnp.dot(h.astype(bf16), wd_ref[0], preferred_element_type=f32)

    last = e == pl.num_programs(1) - 1

    @pl.when(last)
    def _():
        xo = x1_ref[...] + acc_ref[...]
        y = xo * lax.rsqrt(jnp.mean(xo * xo, axis=-1, keepdims=True) + NORM_EPS) * fn_ref[...]
        for cb in range(D // LANES):
            yf_ref[cb] = y[:, cb * LANES:(cb + 1) * LANES]

    @pl.when(last & (i < prompt_tiles))
    def _():
        for b in range(bp):
            yp_ref[b] = jnp.concatenate([yf_ref[cb, pl.ds(b, tsteps, stride=bp), :] for cb in range(D // LANES)],
                                        axis=1)

    @pl.when(last & (i >= prompt_tiles))
    def _():
        for t in range(ssteps):
            ys_ref[:, t, :] = jnp.concatenate([yf_ref[cb, t * bs:(t + 1) * bs, :] for cb in range(D // LANES)],
                                              axis=1)


def _moe(t, comb, wgu, wd, x1, fn, prompt_shape, sample_shape, tm):
    n = t.shape[0]
    bp, tp, _ = prompt_shape
    bs, ts, _ = sample_shape
    assert tm % bp == 0 and tp % (tm // bp) == 0 and bs * ts == tm
    prompt_tiles = bp * tp // tm
    kern = functools.partial(_moe_kernel, prompt_tiles=prompt_tiles)
    return pl.pallas_call(
        kern,
        grid=(n // tm, N_EXPERTS),
        in_specs=[
            pl.BlockSpec((tm, D), lambda i, e: (i, 0)),
            pl.BlockSpec((tm, LANES), lambda i, e: (i, 0)),
            pl.BlockSpec((1, D, 2 * D_EXPERT), lambda i, e: (e, 0, 0)),
            pl.BlockSpec((1, D_EXPERT, D), lambda i, e: (e, 0, 0)),
            pl.BlockSpec((tm, D), lambda i, e: (i, 0)),
            pl.BlockSpec((1, D), lambda i, e: (0, 0)),
        ],
        out_specs=[pl.BlockSpec((bp, tm // bp, D), lambda i, e: (0, jnp.minimum(i, prompt_tiles - 1), 0)),
                   pl.BlockSpec((bs, ts, D), lambda i, e: (0, 0, 0))],
        out_shape=[jax.ShapeDtypeStruct(prompt_shape, f32), jax.ShapeDtypeStruct(sample_shape, f32)],
        scratch_shapes=[pltpu.VMEM((tm, D), f32), pltpu.VMEM((D // LANES, tm, LANES), f32)],
        compiler_params=pltpu.CompilerParams(dimension_semantics=("arbitrary", "arbitrary"),
                                             vmem_limit_bytes=58 * MIB),
        name="moe",
    )(t, comb, wgu, wd, x1, fn)


def _chain_tiles_prompt(vec, batch):
    t = vec.reshape(HEADS // 2, 2, HD)
    t = jnp.transpose(t, (2, 1, 0))
    return jnp.broadcast_to(t[..., None], (HD, 2, HEADS // 2, batch)).reshape(HD, LANES)


def _chain_tiles_sample(vec):
    return jnp.broadcast_to(vec.reshape(HEADS, HD)[..., None], (HEADS, HD, LANES))


def kernel(x_prompt, x_sample, state_conv, state_lru, state_shift, state_wkv, norm1, w_in, conv_w, conv_b, lru_wx, lru_bx, lru_wa, lru_ba, lru_a_param, rwkv_mu, rwkv_w0, rwkv_w2, rwkv_a0, rwkv_a2, rwkv_g2, rwkv_k_k, rwkv_k_a, rwkv_r_k, rwkv_ln_w, rwkv_ln_b, w_out, norm2, router_group, router_group_b, router_expert, router_expert_b, exp_gate, exp_up, exp_down, final_norm):
    bp, tp, _ = x_prompt.shape
    bs, ts, _ = x_sample.shape
    assert norm1.shape[0] == 1 and bp * HEADS == LANES and bs == LANES and tp % 64 == 0 and ts % 2 == 0
    n_p, n_s = bp * tp, bs * ts
    c_rw = 2 * D
    c_lora = c_rw + 3 * D
    c_gl = c_lora + LORA

    w = w_in[0]
    w_all = jnp.concatenate([w[:, :c_lora], w[:, c_gl:], w[:, c_lora:c_gl],
                             jnp.zeros((D, LORA_PAD - LORA), f32)], axis=1).astype(bf16)
    mu = rwkv_mu[0]
    mu_all = jnp.concatenate([mu[:3 * D], mu[3 * D:], jnp.zeros((LORA_PAD - LORA,), f32)])[None]
    w2p = jnp.concatenate([rwkv_w2[0], jnp.zeros((LANES - LORA_W, D), f32)]).astype(bf16)
    a2p = jnp.concatenate([jnp.zeros((LORA_W, D), f32), rwkv_a2[0]]).astype(bf16)
    g2p = jnp.concatenate([rwkv_g2[0], jnp.zeros((2 * LANES - LORA_G, D), f32)]).astype(bf16)
    wxa = jnp.concatenate([lru_wx[0], lru_wa[0]], axis=-1).astype(bf16)
    row = lambda v: v.reshape(1, -1)
    rk = rwkv_r_k[0].reshape(D)
    cvecs = (rwkv_k_k[0], rwkv_k_a[0], rk, rwkv_ln_w[0], rwkv_ln_b[0])
    consts_p = jnp.stack([_chain_tiles_prompt(v, bp) for v in cvecs])[None]
    consts_s = jnp.stack([_chain_tiles_sample(v) for v in cvecs], axis=1)
    rw = jnp.zeros((ROUTER_ROWS, D), f32)
    rw = rw.at[0:N_GROUPS].set(router_group[0].T).at[EPG:EPG + N_EXPERTS].set(router_expert[0].T)
    rh = rw.astype(bf16)
    rl = (rw - rh.astype(f32)).astype(bf16)
    rb = jnp.zeros((ROUTER_ROWS, 1), f32)
    rb = rb.at[0:N_GROUPS, 0].set(router_group_b[0]).at[EPG:EPG + N_EXPERTS, 0].set(router_expert_b[0])
    wgu = jnp.concatenate([exp_gate[0], exp_up[0]], axis=-1).astype(bf16)
    wd = exp_down[0].astype(bf16)
    wo = w_out[0].astype(bf16)

    n = n_p + n_s
    proj, x = _inproj(x_prompt, x_sample, row(norm1[0]), w_all, tm=TOKEN_TILE)

    lru_args = (conv_w[0], row(conv_b[0]), wxa, row(lru_bx[0]), row(lru_ba[0]), row(lru_a_param[0]))
    conv_init_s = jnp.transpose(state_conv[0], (1, 0, 2)).reshape(3 * bs, D)
    ml, hl_p = _lru(proj, 0, n_p, bp, 512, True, jnp.zeros((3 * bp, D), f32), jnp.zeros((bp, D), f32), *lru_args)
    ml, hl_s = _lru(proj, n_p, n_s, bs, 512, False, conv_init_s, state_lru[0], *lru_args, dst=ml)

    sh = state_shift[0]
    shift_init_s = jnp.concatenate([sh, jnp.zeros((bs, LORA_PAD - LORA), f32)], axis=1)
    pre_args = (mu_all, row(rwkv_w0[0]), row(rwkv_a0[0]), w2p, a2p, g2p)
    tiles_p, rows_p, g = _rwkv_pre(proj, 0, n_p, bp, 256, 1, jnp.zeros((bp, 3 * D + LORA_PAD), f32),
                                   consts_p, *pre_args)
    tiles_s, rows_s, g = _rwkv_pre(proj, n_p, n_s, bs, 256, HEADS, shift_init_s, consts_s, *pre_args, dst=g)

    y_p, sfin_p = _wkv_scan(tiles_p, rows_p, jnp.zeros((LANES, STATE_ROWS), f32), 16)
    y_s, sfin_s = _wkv_scan(tiles_s, rows_s, state_wkv[0].reshape(bs, HEADS * STATE_ROWS), ts)
    op = _wkv_post(y_p, tiles_p, rows_p, consts_p, bp, 64, 0, n)
    op = _wkv_post(y_s, tiles_s, rows_s, consts_s, bs, 2, n_p, n, dst=op)

    x1, t, comb = _outproj(ml, op, g, proj, x, wo, row(norm2[0]), rh, rl, rb)
    y_prompt, y_sample = _moe(t, comb, wgu, wd, x1, row(final_norm), x_prompt.shape, x_sample.shape, tm=TOKEN_TILE)

    def last_rows(lo, hi, n_end, batch, steps):
        return proj[n_end - steps * batch:n_end, lo:hi].reshape(steps, batch, hi - lo)

    conv_p = jnp.transpose(last_rows(0, D, n_p, bp, 3), (1, 0, 2))[None]
    conv_s = jnp.transpose(last_rows(0, D, n_p + n_s, bs, 3), (1, 0, 2))[None]

    def shift_rows(n_end, batch):
        return jnp.concatenate([last_rows(2 * D, 5 * D, n_end, batch, 1)[0],
                                last_rows(7 * D, 7 * D + LORA, n_end, batch, 1)[0]], axis=1)[None]

    wkv_p = jnp.transpose(sfin_p.reshape(2, HEADS // 2, bp, HD, HD), (2, 1, 0, 3, 4)).reshape(1, bp, HEADS, HD, HD)
    wkv_s = sfin_s.reshape(1, bs, HEADS, HD, HD)
    return (y_prompt, y_sample, conv_p, hl_p[None], shift_rows(n_p, bp), wkv_p,
            conv_s, hl_s[None], shift_rows(n_p + n_s, bs), wkv_s)
```

```python
import functools

import jax
import jax.numpy as jnp
from jax import lax
from jax.experimental import pallas as pl
from jax.experimental.pallas import tpu as pltpu

f32 = jnp.float32
bf16 = jnp.bfloat16

D = 1024
HEADS = 16
HD = 64
LANES = 128
SUBLANES = 8
LORA_W = 64
LORA_A = 64
LORA_G = 160
LORA = LORA_W + LORA_A + LORA_G
LORA_PAD = 512
N_GROUPS = 4
EPG = 8
N_EXPERTS = N_GROUPS * EPG
D_EXPERT = 256
LRU_C = 8.0
GN_EPS = 64e-5
NORM_EPS = 1e-6
PROJ_COLS = 7 * D + LORA_PAD
COL_LRU_X, COL_LRU_Y, COL_R, COL_K, COL_V, COL_GL, COL_GR = range(7)
COL_LORA = 7 * D // LORA_PAD
ROUTER_ROWS = 48
MIB = 1024 * 1024
TOKEN_TILE = 1024


def _full(shape, grid_rank=1):
    zeros = tuple(0 for _ in shape)
    if grid_rank == 1:
        return pl.BlockSpec(shape, lambda i: zeros)
    return pl.BlockSpec(shape, lambda i, j: zeros)


def _inproj_kernel(xp_ref, xs_ref, g_ref, w_ref, o_ref, xo_ref, xf_ref, xn_ref, *, prompt_tiles):
    i = pl.program_id(0)
    first_col = pl.program_id(1) == 0
    bp, tsteps = xp_ref.shape[0], xp_ref.shape[1]
    bs, ssteps = xs_ref.shape[0], xs_ref.shape[1]

    @pl.when(first_col & (i < prompt_tiles))
    def _():
        for b in range(bp):
            v = xp_ref[b]
            for cb in range(D // LANES):
                xf_ref[cb, pl.ds(b, tsteps, stride=bp), :] = v[:, cb * LANES:(cb + 1) * LANES]

    @pl.when(first_col & (i >= prompt_tiles))
    def _():
        for t in range(ssteps):
            v = xs_ref[:, t, :]
            for cb in range(D // LANES):
                xf_ref[cb, t * bs:(t + 1) * bs, :] = v[:, cb * LANES:(cb + 1) * LANES]

    @pl.when(first_col)
    def _():
        x = jnp.concatenate([xf_ref[cb] for cb in range(D // LANES)], axis=1)
        xo_ref[...] = x
        ms = jnp.mean(x * x, axis=-1, keepdims=True)
        xn_ref[...] = (x * lax.rsqrt(ms + NORM_EPS) * g_ref[...]).astype(bf16)

    o_ref[...] = jnp.dot(xn_ref[...], w_ref[...], preferred_element_type=f32)


def _inproj(x_prompt, x_sample, gain, w, tm, tn=1280):
    bp, tp, _ = x_prompt.shape
    bs, ts, _ = x_sample.shape
    assert tm % bp == 0 and tp % (tm // bp) == 0 and bs * ts == tm
    prompt_tiles = bp * tp // tm
    n = bp * tp + bs * ts
    kern = functools.partial(_inproj_kernel, prompt_tiles=prompt_tiles)
    return pl.pallas_call(
        kern,
        grid=(n // tm, PROJ_COLS // tn),
        in_specs=[
            pl.BlockSpec((bp, tm // bp, D), lambda i, j: (0, jnp.minimum(i, prompt_tiles - 1), 0)),
            pl.BlockSpec((bs, ts, D), lambda i, j: (0, 0, 0)),
            pl.BlockSpec((1, D), lambda i, j: (0, 0)),
            pl.BlockSpec((D, tn), lambda i, j: (0, j)),
        ],
        out_specs=[pl.BlockSpec((tm, tn), lambda i, j: (i, j)), pl.BlockSpec((tm, D), lambda i, j: (i, 0))],
        out_shape=[jax.ShapeDtypeStruct((n, PROJ_COLS), f32), jax.ShapeDtypeStruct((n, D), f32)],
        scratch_shapes=[pltpu.VMEM((D // LANES, tm, LANES), f32), pltpu.VMEM((tm, D), bf16)],
        compiler_params=pltpu.CompilerParams(dimension_semantics=("arbitrary", "arbitrary"),
                                             vmem_limit_bytes=56 * MIB),
        name="inproj",
    )(x_prompt, x_sample, gain, w)


def _lru_kernel(x_ref, y_ref, gl_ref, cinit_ref, h0_ref, cw_ref, cb_ref, wxa_ref, bx_ref, ba_ref, ap_ref,
                o_ref, hl_ref, xs_ref, a_ref, b_ref, h_ref, *, batch, reset_first, chunk):
    rows = x_ref.shape[0]
    nt = rows // batch
    hist = 3 * batch
    pid = pl.program_id(0)

    @pl.when(pid == 0)
    def _():
        xs_ref[0:hist, :] = cinit_ref[...]
        h_ref[...] = h0_ref[...]

    xs_ref[hist:hist + rows, :] = x_ref[...]
    logsig = -jax.nn.softplus(-ap_ref[...])

    def gates(c, _):
        r0 = pl.multiple_of(c * chunk, chunk)
        xc = (cb_ref[...]
              + cw_ref[3:4, :] * xs_ref[pl.ds(pl.multiple_of(r0 + hist, SUBLANES), chunk), :]
              + cw_ref[2:3, :] * xs_ref[pl.ds(pl.multiple_of(r0 + 2 * batch, SUBLANES), chunk), :]
              + cw_ref[1:2, :] * xs_ref[pl.ds(pl.multiple_of(r0 + batch, SUBLANES), chunk), :]
              + cw_ref[0:1, :] * xs_ref[pl.ds(r0, chunk), :])
        if reset_first:
            grow = lax.broadcasted_iota(jnp.int32, (chunk, LANES), 0) + (r0 + pid * rows)
            first = grow < batch
        for n in range(D // LANES):
            sl = slice(n * LANES, (n + 1) * LANES)
            xn = xc[:, sl]
            g2 = jnp.dot(xn.astype(bf16), wxa_ref[n], preferred_element_type=f32)
            gate_x = jax.nn.sigmoid(g2[:, :LANES] + bx_ref[:, sl])
            gate_a = jax.nn.sigmoid(g2[:, LANES:] + ba_ref[:, sl])
            log_a = LRU_C * gate_a * logsig[:, sl]
            a = jnp.exp(log_a)
            mult = jnp.sqrt(-jnp.tanh(log_a) * (a * a + 1.0))
            if reset_first:
                mult = jnp.where(first, 1.0, mult)
            a_ref[pl.ds(r0, chunk), sl] = a
            b_ref[pl.ds(r0, chunk), sl] = xn * gate_x * mult
        return 0

    lax.fori_loop(0, rows // chunk, gates, 0)

    def scan(t, h):
        r0 = pl.multiple_of(t * batch, batch)
        h = a_ref[pl.ds(r0, batch), :] * h + b_ref[pl.ds(r0, batch), :]
        b_ref[pl.ds(r0, batch), :] = h
        return h

    h = lax.fori_loop(0, nt, scan, h_ref[...], unroll=(8 if nt >= 8 and batch == SUBLANES else 1))
    h_ref[...] = h
    hl_ref[...] = h
    xs_ref[0:hist, :] = xs_ref[rows:rows + hist, :]

    def outp(c, _):
        r0 = pl.multiple_of(c * chunk, chunk)
        o_ref[pl.ds(r0, chunk), :] = (b_ref[pl.ds(r0, chunk), :] * jax.nn.gelu(y_ref[pl.ds(r0, chunk), :])
                                      * jax.nn.sigmoid(gl_ref[pl.ds(r0, chunk), :]))
        return 0

    lax.fori_loop(0, rows // chunk, outp, 0)


def _with_dst(kern, n_in):
    def wrapped(*refs):
        return kern(*refs[:n_in], *refs[n_in + 1:])
    return wrapped


def _shared_rows(dst, n_total):
    if dst is None:
        return [], [], jax.ShapeDtypeStruct((n_total, D), f32)
    return [pl.BlockSpec(memory_space=pl.ANY)], [dst], jax.ShapeDtypeStruct(dst.shape, dst.dtype)


def _lru(proj, row_off, n_rows, batch, rows_blk, reset_first, conv_init, h0, cw, cb, wxa, bx, ba, ap, dst=None):
    ob = row_off // rows_blk
    kern = functools.partial(_lru_kernel, batch=batch, reset_first=reset_first, chunk=128)
    n_in = 11
    dst_spec, dst_arg, out0 = _shared_rows(dst, proj.shape[0])
    return pl.pallas_call(
        kern if dst is None else _with_dst(kern, n_in),
        grid=(n_rows // rows_blk,),
        in_specs=[
            pl.BlockSpec((rows_blk, D), lambda i: (ob + i, COL_LRU_X)),
            pl.BlockSpec((rows_blk, D), lambda i: (ob + i, COL_LRU_Y)),
            pl.BlockSpec((rows_blk, D), lambda i: (ob + i, COL_GL)),
            _full((3 * batch, D)), _full((batch, D)), _full((4, D)), _full((1, D)),
            _full((D // LANES, LANES, 2 * LANES)), _full((1, D)), _full((1, D)), _full((1, D)),
        ] + dst_spec,
        out_specs=[pl.BlockSpec((rows_blk, D), lambda i: (ob + i, 0)), _full((batch, D))],
        out_shape=[out0, jax.ShapeDtypeStruct((batch, D), f32)],
        input_output_aliases={} if dst is None else {n_in: 0},
        scratch_shapes=[pltpu.VMEM((rows_blk + 3 * batch, D), f32), pltpu.VMEM((rows_blk, D), f32),
                        pltpu.VMEM((rows_blk, D), f32), pltpu.VMEM((batch, D), f32)],
        compiler_params=pltpu.CompilerParams(dimension_semantics=("arbitrary",), vmem_limit_bytes=48 * MIB),
        name="lru",
    )(proj, proj, proj, conv_init, h0, cw, cb, wxa, bx, ba, ap, *dst_arg)


TILE_ALPHA, TILE_BETA, TILE_W, TILE_KP, TILE_WR, TILE_V = range(6)
ROW_BR, ROW_KR, ROW_BONUS = range(3)
CONST_KK, CONST_KA, CONST_RK, CONST_LNW, CONST_LNB = range(5)
NAT_R, NAT_W, NAT_K, NAT_V, NAT_A = range(5)


def _wkv_prep(tiles_ref, rows_ref, t, g, r, w, k, v, a, const):
    kk = k * const(CONST_KK)
    nrm = jnp.sqrt(jnp.sum(kk * kk, axis=0, keepdims=True))
    kk = kk * (1.0 / jnp.maximum(nrm, 1e-12))
    beta = kk * a
    kp = k * (1.0 + (a - 1.0) * const(CONST_KA))
    tiles_ref[t, g, TILE_ALPHA] = -kk
    tiles_ref[t, g, TILE_BETA] = beta
    tiles_ref[t, g, TILE_W] = w
    tiles_ref[t, g, TILE_KP] = kp
    tiles_ref[t, g, TILE_WR] = w * r
    tiles_ref[t, g, TILE_V] = v
    rows_ref[t, g, ROW_BR:ROW_BR + 1, :] = jnp.sum(beta * r, axis=0, keepdims=True)
    rows_ref[t, g, ROW_KR:ROW_KR + 1, :] = jnp.sum(kp * r, axis=0, keepdims=True)
    rows_ref[t, g, ROW_BONUS:ROW_BONUS + 1, :] = jnp.sum(r * kp * const(CONST_RK), axis=0, keepdims=True)


def _rwkv_pre_kernel(r_ref, k_ref, v_ref, l_ref, sinit_ref, mu_ref, w0_ref, a0_ref, w2_ref, a2_ref, g2_ref, c_ref,
                     tiles_ref, rows_ref, go_ref, ps_ref, nat_ref, *, batch, chunk, chains_are_heads):
    rows = r_ref.shape[0]

    @pl.when(pl.program_id(0) == 0)
    def _():
        ps_ref[0:batch, :] = sinit_ref[...]

    ps_ref[batch:batch + rows, 0:D] = r_ref[...]
    ps_ref[batch:batch + rows, D:2 * D] = k_ref[...]
    ps_ref[batch:batch + rows, 2 * D:3 * D] = v_ref[...]
    ps_ref[batch:batch + rows, 3 * D:3 * D + LORA_PAD] = l_ref[...]

    def body(c, _):
        r0 = pl.multiple_of(c * chunk, chunk)

        def mixed(lo, hi):
            cur = ps_ref[pl.ds(pl.multiple_of(r0 + batch, SUBLANES), chunk), lo:hi]
            prev = ps_ref[pl.ds(r0, chunk), lo:hi]
            return cur + (prev - cur) * mu_ref[:, lo:hi]

        nat_ref[NAT_R, pl.ds(r0, chunk), :] = mixed(0, D)
        nat_ref[NAT_K, pl.ds(r0, chunk), :] = mixed(D, 2 * D)
        nat_ref[NAT_V, pl.ds(r0, chunk), :] = mixed(2 * D, 3 * D)
        lm = mixed(3 * D, 3 * D + LORA_PAD)
        xwa = lm[:, 0:LANES]
        xg = lm[:, LANES:3 * LANES]
        lw = jnp.dot(jnp.tanh(xwa).astype(bf16), w2_ref[...], preferred_element_type=f32)
        w_log = -jax.nn.softplus(-(w0_ref[...] + lw)) - 0.5
        nat_ref[NAT_W, pl.ds(r0, chunk), :] = jnp.exp(-jnp.exp(w_log))
        la = jnp.dot(xwa.astype(bf16), a2_ref[...], preferred_element_type=f32)
        nat_ref[NAT_A, pl.ds(r0, chunk), :] = jax.nn.sigmoid(a0_ref[...] + la)
        go_ref[pl.ds(r0, chunk), :] = jnp.dot(jax.nn.sigmoid(xg).astype(bf16), g2_ref[...],
                                              preferred_element_type=f32)
        return 0

    lax.fori_loop(0, rows // chunk, body, 0)
    ps_ref[0:batch, :] = ps_ref[rows:rows + batch, :]

    if chains_are_heads:
        def tstep(t, _):
            r0 = pl.multiple_of(t * batch, batch)
            for hp in range(D // LANES):
                trs = [nat_ref[q, pl.ds(r0, batch), hp * LANES:(hp + 1) * LANES].T for q in range(5)]
                for h2 in range(2):
                    h = 2 * hp + h2
                    _wkv_prep(tiles_ref, rows_ref, t, h, *[tr[h2 * HD:(h2 + 1) * HD] for tr in trs],
                              lambda i, h=h: c_ref[h, i])
            return 0

        lax.fori_loop(0, rows // batch, tstep, 0)
    else:
        half = LANES // 2
        lane = lax.broadcasted_iota(jnp.int32, (HD, LANES), 1)

        def load_pair(q, tp):
            zz = jnp.concatenate(
                [nat_ref[q, pl.ds(pl.multiple_of(tp * 2 * SUBLANES + tt * SUBLANES, SUBLANES), SUBLANES),
                         hp * LANES:(hp + 1) * LANES] for tt in range(2) for hp in range(D // LANES)], axis=0)
            tr = zz.T
            top, bot = tr[0:HD], tr[HD:2 * HD]
            d0 = jnp.where(lane < half, top, pltpu.roll(bot, half, axis=1))
            d1 = jnp.where(lane < half, pltpu.roll(top, half, axis=1), bot)
            return d0, d1

        def pair(tp, _):
            pairs = [load_pair(q, tp) for q in range(5)]
            for tt in range(2):
                _wkv_prep(tiles_ref, rows_ref, 2 * tp + tt, 0, *[p[tt] for p in pairs], lambda i: c_ref[0, i])
            return 0

        lax.fori_loop(0, rows // (2 * SUBLANES), pair, 0, unroll=4)


def _rwkv_pre(proj, row_off, n_rows, batch, rows_blk, groups, shift_init, consts, mu, w0, a0, w2p, a2p, g2p,
              dst=None):
    ob = row_off // rows_blk
    width = 3 * D + LORA_PAD
    steps_blk = rows_blk // batch
    kern = functools.partial(_rwkv_pre_kernel, batch=batch, chunk=128, chains_are_heads=groups > 1)
    n_in = 12
    dst_spec, dst_arg, out_g = _shared_rows(dst, proj.shape[0])
    return pl.pallas_call(
        kern if dst is None else _with_dst(kern, n_in),
        grid=(n_rows // rows_blk,),
        in_specs=[
            pl.BlockSpec((rows_blk, D), lambda i: (ob + i, COL_R)),
            pl.BlockSpec((rows_blk, D), lambda i: (ob + i, COL_K)),
            pl.BlockSpec((rows_blk, D), lambda i: (ob + i, COL_V)),
            pl.BlockSpec((rows_blk, LORA_PAD), lambda i: (ob + i, COL_LORA)),
            _full((batch, width)), _full((1, width)), _full((1, D)), _full((1, D)),
            _full((LANES, D)), _full((LANES, D)), _full((2 * LANES, D)), _full((groups, 5, HD, LANES)),
        ] + dst_spec,
        out_specs=[pl.BlockSpec((steps_blk, groups, 6, HD, LANES), lambda i: (i, 0, 0, 0, 0)),
                   pl.BlockSpec((steps_blk, groups, SUBLANES, LANES), lambda i: (i, 0, 0, 0)),
                   pl.BlockSpec((rows_blk, D), lambda i: (ob + i, 0))],
        out_shape=[jax.ShapeDtypeStruct((n_rows // batch, groups, 6, HD, LANES), f32),
                   jax.ShapeDtypeStruct((n_rows // batch, groups, SUBLANES, LANES), f32),
                   out_g],
        input_output_aliases={} if dst is None else {n_in: 2},
        scratch_shapes=[pltpu.VMEM((rows_blk + batch, width), f32), pltpu.VMEM((5, rows_blk, D), f32)],
        compiler_params=pltpu.CompilerParams(dimension_semantics=("arbitrary",), vmem_limit_bytes=56 * MIB),
        name="rwkv_pre",
    )(proj, proj, proj, proj, shift_init, mu, w0, a0, w2p, a2p, g2p, consts, *dst_arg)


STATE_ROWS = HD * HD
KEY_UNROLL = 32


def _wkv_step(s_ref, tiles_ref, rows_ref, y_ref, t):
    def key_row(tile, j):
        return tiles_ref[t, 0, tile, pl.ds(j, 1), :]

    def reduce_keys(j, acc):
        sa, y0 = acc
        s = s_ref[pl.ds(pl.multiple_of(j * HD, HD), HD), :]
        return sa + s * key_row(TILE_ALPHA, j), y0 + s * key_row(TILE_WR, j)

    zero = jnp.zeros((HD, LANES), f32)
    sa, y0 = lax.fori_loop(0, HD, reduce_keys, (zero, zero), unroll=KEY_UNROLL)
    v = tiles_ref[t, 0, TILE_V]

    def update_keys(j, _):
        rows = pl.ds(pl.multiple_of(j * HD, HD), HD)
        s_ref[rows, :] = (s_ref[rows, :] * key_row(TILE_W, j) + sa * key_row(TILE_BETA, j)
                          + v * key_row(TILE_KP, j))
        return 0

    lax.fori_loop(0, HD, update_keys, 0, unroll=KEY_UNROLL)
    y_ref[t, 0] = y0 + sa * rows_ref[t, 0, ROW_BR:ROW_BR + 1, :] + v * rows_ref[t, 0, ROW_KR:ROW_KR + 1, :]


def _wkv_scan_kernel(tiles_ref, rows_ref, st_ref, y_ref, sfin_ref, s_ref, *, steps):
    tb = pl.program_id(1)

    def value_rows(i):
        return pl.ds(i, HD, stride=HD)

    @pl.when(tb == 0)
    def _():
        def init(c, _):
            tr = st_ref[:, pl.ds(pl.multiple_of(c * LANES, LANES), LANES)].T
            for i2 in range(2):
                s_ref[value_rows(2 * c + i2), :] = tr[i2 * HD:(i2 + 1) * HD]
            return 0

        lax.fori_loop(0, STATE_ROWS // LANES, init, 0, unroll=4)

    def step(t, _):
        _wkv_step(s_ref, tiles_ref, rows_ref, y_ref, t)
        return 0

    lax.fori_loop(0, steps, step, 0)

    @pl.when(tb == pl.num_programs(1) - 1)
    def _():
        def fin(c, _):
            pair = jnp.concatenate([s_ref[value_rows(2 * c + i2), :] for i2 in range(2)], axis=0)
            sfin_ref[:, pl.ds(pl.multiple_of(c * LANES, LANES), LANES)] = pair.T
            return 0

        lax.fori_loop(0, STATE_ROWS // LANES, fin, 0, unroll=4)


def _wkv_scan(tiles, rows, state, steps_blk):
    n_steps, groups = tiles.shape[0], tiles.shape[1]
    kern = functools.partial(_wkv_scan_kernel, steps=steps_blk)
    sblk = pl.BlockSpec((LANES, STATE_ROWS), lambda g, i: (0, g))
    return pl.pallas_call(
        kern,
        grid=(groups, n_steps // steps_blk),
        in_specs=[pl.BlockSpec((steps_blk, 1, 6, HD, LANES), lambda g, i: (i, g, 0, 0, 0)),
                  pl.BlockSpec((steps_blk, 1, SUBLANES, LANES), lambda g, i: (i, g, 0, 0)),
                  sblk],
        out_specs=[pl.BlockSpec((steps_blk, 1, HD, LANES), lambda g, i: (i, g, 0, 0)), sblk],
        out_shape=[jax.ShapeDtypeStruct((n_steps, groups, HD, LANES), f32),
                   jax.ShapeDtypeStruct((LANES, groups * STATE_ROWS), f32)],
        scratch_shapes=[pltpu.VMEM((STATE_ROWS, LANES), f32)],
        compiler_params=pltpu.CompilerParams(dimension_semantics=("arbitrary", "arbitrary"),
                                             vmem_limit_bytes=40 * MIB),
        name="wkv_scan",
    )(tiles, rows, state)


def _wkv_norm(y, v, bonus, lnw, lnb):
    mean = jnp.sum(y, axis=0, keepdims=True) * (1.0 / HD)
    d = y - mean
    var = jnp.sum(d * d, axis=0, keepdims=True) * (1.0 / HD)
    return d * lax.rsqrt(var + GN_EPS) * lnw + lnb + bonus * v


def _wkv_post_kernel(y_ref, v_ref, rows_ref, c_ref, o_ref, *, batch, chains_are_heads):
    steps = y_ref.shape[0]

    def normed(t, g):
        return _wkv_norm(y_ref[t, g], v_ref[t, g, 0], rows_ref[t, g, ROW_BONUS:ROW_BONUS + 1, :],
                         c_ref[g, CONST_LNW], c_ref[g, CONST_LNB])

    if chains_are_heads:
        def tstep(t, _):
            r0 = pl.multiple_of(t * batch, batch)
            for hp in range(D // LANES):
                two = jnp.concatenate([normed(t, 2 * hp), normed(t, 2 * hp + 1)], axis=0)
                o_ref[pl.ds(r0, batch), hp * LANES:(hp + 1) * LANES] = two.T
            return 0

        lax.fori_loop(0, steps, tstep, 0)
    else:
        half = LANES // 2
        lane = lax.broadcasted_iota(jnp.int32, (HD, LANES), 1)

        def pair(tp, _):
            o0 = normed(2 * tp, 0)
            o1 = normed(2 * tp + 1, 0)
            top = jnp.where(lane < half, o0, pltpu.roll(o1, half, axis=1))
            bot = jnp.where(lane < half, pltpu.roll(o0, half, axis=1), o1)
            zz = jnp.concatenate([top, bot], axis=0).T
            for tt in range(2):
                row = pl.multiple_of(tp * 2 * SUBLANES + tt * SUBLANES, SUBLANES)
                for hp in range(D // LANES):
                    src = tt * HD + hp * SUBLANES
                    o_ref[pl.ds(row, SUBLANES), hp * LANES:(hp + 1) * LANES] = zz[src:src + SUBLANES, :]
            return 0

        lax.fori_loop(0, steps // 2, pair, 0, unroll=2)


def _wkv_post(y, tiles, rows, consts, batch, steps_blk, row_off, n_total, dst=None):
    n_steps, groups = y.shape[0], y.shape[1]
    ob = row_off // (steps_blk * batch)
    kern = functools.partial(_wkv_post_kernel, batch=batch, chains_are_heads=groups > 1)
    n_in = 4
    dst_spec, dst_arg, out0 = _shared_rows(dst, n_total)
    return pl.pallas_call(
        kern if dst is None else _with_dst(kern, n_in),
        grid=(n_steps // steps_blk,),
        in_specs=[pl.BlockSpec((steps_blk, groups, HD, LANES), lambda i: (i, 0, 0, 0)),
                  pl.BlockSpec((steps_blk, groups, 1, HD, LANES), lambda i: (i, 0, TILE_V, 0, 0)),
                  pl.BlockSpec((steps_blk, groups, SUBLANES, LANES), lambda i: (i, 0, 0, 0)),
                  _full((groups, 5, HD, LANES))] + dst_spec,
        out_specs=pl.BlockSpec((steps_blk * batch, D), lambda i: (ob + i, 0)),
        out_shape=out0,
        input_output_aliases={} if dst is None else {n_in: 0},
        compiler_params=pltpu.CompilerParams(dimension_semantics=("arbitrary",), vmem_limit_bytes=40 * MIB),
        name="wkv_post",
    )(y, tiles, rows, consts, *dst_arg)


def _outproj_kernel(ml_ref, op_ref, g_ref, gr_ref, x_ref, wo_ref, n2_ref, rh_ref, rl_ref, rb_ref,
                    x1_ref, t_ref, selt_ref, seln_ref):
    tm = x_ref.shape[0]
    merged = ml_ref[...] + jax.nn.sigmoid(gr_ref[...]) * (op_ref[...] * g_ref[...])
    x1 = x_ref[...] + jnp.dot(merged.astype(bf16), wo_ref[...], preferred_element_type=f32)
    x1_ref[...] = x1
    t = x1 * lax.rsqrt(jnp.mean(x1 * x1, axis=-1, keepdims=True) + NORM_EPS) * n2_ref[...]
    th = t.astype(bf16)
    t_ref[...] = th
    tl = (t - th.astype(f32)).astype(bf16)
    nt_dims = (((1,), (1,)), ((), ()))
    lg = (lax.dot_general(rh_ref[...], th, nt_dims, preferred_element_type=f32)
          + lax.dot_general(rh_ref[...], tl, nt_dims, preferred_element_type=f32)
          + lax.dot_general(rl_ref[...], th, nt_dims, preferred_element_type=f32)) + rb_ref[...]
    row = lax.broadcasted_iota(jnp.int32, (EPG, tm), 0).astype(f32)
    neg = jnp.float32(-jnp.inf)
    glog = jnp.where(row < N_GROUPS, lg[0:EPG], neg)
    ge = jnp.exp(glog - jnp.max(glog, axis=0, keepdims=True))
    pg = ge / jnp.sum(ge, axis=0, keepdims=True)
    p_top = jnp.max(pg, axis=0, keepdims=True)
    g_idx = jnp.min(jnp.where(pg == p_top, row, EPG), axis=0, keepdims=True)
    le = jnp.zeros((EPG, tm), f32)
    for g in range(N_GROUPS):
        le = jnp.where(g_idx == g, lg[EPG * (g + 1):EPG * (g + 2)], le)
    qe = jnp.exp(le - jnp.max(le, axis=0, keepdims=True))
    q = qe / jnp.sum(qe, axis=0, keepdims=True)
    q1 = jnp.max(q, axis=0, keepdims=True)
    i1 = jnp.min(jnp.where(q == q1, row, EPG), axis=0, keepdims=True)
    qm = jnp.where(row == i1, -1.0, q)
    q2 = jnp.max(qm, axis=0, keepdims=True)
    i2 = jnp.min(jnp.where(qm == q2, row, EPG), axis=0, keepdims=True)
    qs = q1 + q2
    sel = jnp.concatenate([g_idx * EPG + i1, g_idx * EPG + i2, q1 / qs * p_top, q2 / qs * p_top,
                           jnp.zeros((SUBLANES - 4, tm), f32)], axis=0)
    selt_ref[...] = sel
    sel_pad = jnp.concatenate([sel, jnp.zeros((LANES - SUBLANES, tm), f32)], axis=0)
    for c in range(tm // LANES):
        seln_ref[c * LANES:(c + 1) * LANES, :] = sel_pad[:, c * LANES:(c + 1) * LANES].T


def _outproj(ml, op, g, proj, x, wo, n2, rh, rl, rb, tm=512):
    n = x.shape[0]
    blk = pl.BlockSpec((tm, D), lambda i: (i, 0))
    return pl.pallas_call(
        _outproj_kernel,
        grid=(n // tm,),
        in_specs=[blk, blk, blk, pl.BlockSpec((tm, D), lambda i: (i, COL_GR)), blk,
                  _full((D, D)), _full((1, D)), _full((ROUTER_ROWS, D)), _full((ROUTER_ROWS, D)),
                  _full((ROUTER_ROWS, 1))],
        out_specs=[blk, blk, pl.BlockSpec((SUBLANES, tm), lambda i: (0, i)),
                   pl.BlockSpec((tm, LANES), lambda i: (i, 0))],
        out_shape=[jax.ShapeDtypeStruct((n, D), f32), jax.ShapeDtypeStruct((n, D), bf16),
                   jax.ShapeDtypeStruct((SUBLANES, n), f32), jax.ShapeDtypeStruct((n, LANES), f32)],
        compiler_params=pltpu.CompilerParams(dimension_semantics=("arbitrary",), vmem_limit_bytes=48 * MIB),
        name="outproj_router",
    )(ml, op, g, proj, x, wo, n2, rh, rl, rb)


MOE_CHUNK = 64
MOE_CAP = 2 * TOKEN_TILE + N_EXPERTS * MOE_CHUNK
MOE_GATHER = 512
SEL_E1, SEL_E2, SEL_C1, SEL_C2 = range(4)


def _route_meta_kernel(seln_ref, o_ref):
    tm = seln_ref.shape[0]
    lane = lax.broadcasted_iota(jnp.int32, (tm, LANES), 1).astype(f32)
    sel = seln_ref[...]
    hit = (lane == sel[:, SEL_E1:SEL_E1 + 1]) | (lane == sel[:, SEL_E2:SEL_E2 + 1])
    cnt = jnp.sum(jnp.where(hit, 1.0, 0.0), axis=0, keepdims=True)
    nchunk = jnp.floor((cnt + (MOE_CHUNK - 1)) * (1.0 / MOE_CHUNK))
    upper = (lax.broadcasted_iota(jnp.int32, (LANES, LANES), 0)
             < lax.broadcasted_iota(jnp.int32, (LANES, LANES), 1))
    offs = jnp.dot(jnp.broadcast_to(nchunk, (SUBLANES, LANES)).astype(bf16), jnp.where(upper, 1.0, 0.0).astype(bf16),
                   preferred_element_type=f32)
    row = lax.broadcasted_iota(jnp.int32, (SUBLANES, LANES), 0)
    o_ref[0] = jnp.where(row == 0, nchunk, jnp.where(row == 1, offs, 0.0)).astype(jnp.int32)


def _route_meta(seln, tm):
    tiles = seln.shape[0] // tm
    return pl.pallas_call(
        _route_meta_kernel,
        grid=(tiles,),
        in_specs=[pl.BlockSpec((tm, LANES), lambda i: (i, 0))],
        out_specs=pl.BlockSpec((1, SUBLANES, LANES), lambda i: (i, 0, 0)),
        out_shape=jax.ShapeDtypeStruct((tiles, SUBLANES, LANES), jnp.int32),
        compiler_params=pltpu.CompilerParams(dimension_semantics=("arbitrary",)),
        name="route_meta",
    )(seln)


def _moe_experts_kernel(nch_ref, off_ref, t_ref, selt_ref, tri_ref, wgu_ref, wd_ref, ys_ref, pos_ref,
                        xs_ref, cw_ref):
    i = pl.program_id(0)
    e = pl.program_id(1)
    tm = t_ref.shape[0]
    used_rows = (off_ref[i * N_EXPERTS + N_EXPERTS - 1] + nch_ref[i * N_EXPERTS + N_EXPERTS - 1]) * MOE_CHUNK

    @pl.when(e == 0)
    def _():
        ys_ref[...] = jnp.zeros_like(ys_ref)
        e1 = selt_ref[SEL_E1:SEL_E1 + 1, :]
        e2 = selt_ref[SEL_E2:SEL_E2 + 1, :]
        erow = lax.broadcasted_iota(jnp.int32, (N_EXPERTS, tm), 0).astype(f32)
        oh1 = jnp.where(erow == e1, 1.0, 0.0)
        oh2 = jnp.where(erow == e2, 1.0, 0.0)
        before1 = jnp.dot(oh1.astype(bf16), tri_ref[...], preferred_element_type=f32)
        before2 = jnp.dot(oh2.astype(bf16), tri_ref[...], preferred_element_type=f32)
        cnt1 = jnp.sum(oh1, axis=1, keepdims=True)
        cnt2 = jnp.sum(oh2, axis=1, keepdims=True)
        nchunk = jnp.floor((cnt1 + cnt2 + (MOE_CHUNK - 1)) * (1.0 / MOE_CHUNK))
        lower = (lax.broadcasted_iota(jnp.int32, (N_EXPERTS, LANES), 1)
                 < lax.broadcasted_iota(jnp.int32, (N_EXPERTS, LANES), 0))
        nchunk_rows = jnp.concatenate([jnp.broadcast_to(nchunk, (N_EXPERTS, LANES)),
                                       jnp.zeros((LANES - N_EXPERTS, LANES), f32)], axis=0)
        start = jnp.dot(jnp.where(lower, 1.0, 0.0).astype(bf16), nchunk_rows.astype(bf16),
                        preferred_element_type=f32)[:, 0:1] * MOE_CHUNK
        pos1 = jnp.sum(oh1 * (start + before1), axis=0, keepdims=True)
        pos2 = jnp.sum(oh2 * (start + cnt1 + before2), axis=0, keepdims=True)
        pos_ref[...] = jnp.concatenate([pos1, pos2, jnp.zeros((SUBLANES - 2, tm), f32)], axis=0)
        c1 = selt_ref[SEL_C1:SEL_C1 + 1, :]
        c2 = selt_ref[SEL_C2:SEL_C2 + 1, :]
        for k in range(MOE_CAP // MOE_GATHER):
            @pl.when(k * MOE_GATHER < used_rows)
            def _(k=k):
                ridx = (lax.broadcasted_iota(jnp.int32, (MOE_GATHER, tm), 0) + k * MOE_GATHER).astype(f32)
                p1 = ridx == pos1
                p2 = ridx == pos2
                onehot = jnp.where(p1 | p2, 1.0, 0.0).astype(bf16)
                xs_ref[k * MOE_GATHER:(k + 1) * MOE_GATHER, :] = jnp.dot(
                    onehot, t_ref[...], preferred_element_type=f32).astype(bf16)
                w = jnp.sum(jnp.where(p1, c1, 0.0) + jnp.where(p2, c2, 0.0), axis=1, keepdims=True)
                cw_ref[k * MOE_GATHER:(k + 1) * MOE_GATHER, :] = jnp.broadcast_to(w, (MOE_GATHER, LANES))

    base = off_ref[i * N_EXPERTS + e] * MOE_CHUNK

    def chunk(c, _):
        rows = pl.ds(pl.multiple_of(base + c * MOE_CHUNK, MOE_CHUNK), MOE_CHUNK)
        gu = jnp.dot(xs_ref[rows, :], wgu_ref[0], preferred_element_type=f32)
        w = cw_ref[rows, :]
        h = jax.nn.silu(gu[:, :D_EXPERT]) * gu[:, D_EXPERT:] * jnp.concatenate([w, w], axis=1)
        ys_ref[0, rows, :] = jnp.dot(h.astype(bf16), wd_ref[0], preferred_element_type=f32).astype(bf16)
        return 0

    lax.fori_loop(0, nch_ref[i * N_EXPERTS + e], chunk, 0)


def _moe_experts(nch, off, t, selt, tri, wgu, wd, tm):
    n = t.shape[0]
    tiles = n // tm
    grid_spec = pltpu.PrefetchScalarGridSpec(
        num_scalar_prefetch=2,
        grid=(tiles, N_EXPERTS),
        in_specs=[
            pl.BlockSpec((tm, D), lambda i, e, nch, off: (i, 0)),
            pl.BlockSpec((SUBLANES, tm), lambda i, e, nch, off: (0, i)),
            pl.BlockSpec((tm, tm), lambda i, e, nch, off: (0, 0)),
            pl.BlockSpec((1, D, 2 * D_EXPERT), lambda i, e, nch, off: (e, 0, 0)),
            pl.BlockSpec((1, D_EXPERT, D), lambda i, e, nch, off: (e, 0, 0)),
        ],
        out_specs=[pl.BlockSpec((1, MOE_CAP, D), lambda i, e, nch, off: (i, 0, 0)),
                   pl.BlockSpec((SUBLANES, tm), lambda i, e, nch, off: (0, i))],
        scratch_shapes=[pltpu.VMEM((MOE_CAP, D), bf16), pltpu.VMEM((MOE_CAP, LANES), f32)],
    )
    return pl.pallas_call(
        _moe_experts_kernel,
        grid_spec=grid_spec,
        out_shape=[jax.ShapeDtypeStruct((tiles, MOE_CAP, D), bf16), jax.ShapeDtypeStruct((SUBLANES, n), f32)],
        compiler_params=pltpu.CompilerParams(dimension_semantics=("arbitrary", "arbitrary"),
                                             vmem_limit_bytes=56 * MIB),
        name="moe_experts",
    )(nch, off, t, selt, tri, wgu, wd)


def _moe_combine_kernel(nch_ref, off_ref, ysort_ref, pos_ref, x1_ref, fn_ref, yp_ref, ys_ref, acc_ref, yf_ref,
                        *, prompt_tiles, parts):
    i = pl.program_id(0)
    part = pl.program_id(1)
    rows = x1_ref.shape[0]
    bp, tsteps = yp_ref.shape[0], yp_ref.shape[1]
    bs = ys_ref.shape[0]
    ssteps = ys_ref.shape[1] // parts
    used_rows = (off_ref[i * N_EXPERTS + N_EXPERTS - 1] + nch_ref[i * N_EXPERTS + N_EXPERTS - 1]) * MOE_CHUNK

    pos_t = jnp.concatenate([pos_ref[...], jnp.zeros((LANES - SUBLANES, rows), f32)], axis=0)
    pos_n = jnp.concatenate([pos_t[:, c * LANES:(c + 1) * LANES].T for c in range(rows // LANES)], axis=0)
    pos1 = pos_n[:, 0:1]
    pos2 = pos_n[:, 1:2]
    acc_ref[...] = x1_ref[...]
    for k in range(MOE_CAP // MOE_GATHER):
        @pl.when(k * MOE_GATHER < used_rows)
        def _(k=k):
            cidx = (lax.broadcasted_iota(jnp.int32, (rows, MOE_GATHER), 1) + k * MOE_GATHER).astype(f32)
            onehot = jnp.where((cidx == pos1) | (cidx == pos2), 1.0, 0.0).astype(bf16)
            acc_ref[...] += jnp.dot(onehot, ysort_ref[0, k * MOE_GATHER:(k + 1) * MOE_GATHER, :],
                                    preferred_element_type=f32)

    xo = acc_ref[...]
    y = xo * lax.rsqrt(jnp.mean(xo * xo, axis=-1, keepdims=True) + NORM_EPS) * fn_ref[...]
    for cb in range(D // LANES):
        yf_ref[cb] = y[:, cb * LANES:(cb + 1) * LANES]

    @pl.when(i < prompt_tiles)
    def _():
        for b in range(bp):
            yp_ref[b] = jnp.concatenate([yf_ref[cb, pl.ds(b, tsteps, stride=bp), :] for cb in range(D // LANES)],
                                        axis=1)

    for p in range(parts):
        @pl.when((i >= prompt_tiles) & (part == p))
        def _(p=p):
            for t in range(ssteps):
                ys_ref[:, p * ssteps + t, :] = jnp.concatenate(
                    [yf_ref[cb, t * bs:(t + 1) * bs, :] for cb in range(D // LANES)], axis=1)


def _moe_combine(nch, off, ysort, pos, x1, fn, prompt_shape, sample_shape, tm, parts=2):
    n = x1.shape[0]
    bp, tp, _ = prompt_shape
    bs, ts, _ = sample_shape
    rows = tm // parts
    assert rows % bp == 0 and tp % (rows // bp) == 0 and bs * ts == tm and ts % parts == 0
    prompt_tiles = bp * tp // tm
    last_prompt_blk = prompt_tiles * parts - 1
    kern = functools.partial(_moe_combine_kernel, prompt_tiles=prompt_tiles, parts=parts)
    grid_spec = pltpu.PrefetchScalarGridSpec(
        num_scalar_prefetch=2,
        grid=(n // tm, parts),
        in_specs=[
            pl.BlockSpec((1, MOE_CAP, D), lambda i, p, nch, off: (i, 0, 0)),
            pl.BlockSpec((SUBLANES, rows), lambda i, p, nch, off: (0, i * parts + p)),
            pl.BlockSpec((rows, D), lambda i, p, nch, off: (i * parts + p, 0)),
            pl.BlockSpec((1, D), lambda i, p, nch, off: (0, 0)),
        ],
        out_specs=[pl.BlockSpec((bp, rows // bp, D),
                                lambda i, p, nch, off: (0, jnp.minimum(i * parts + p, last_prompt_blk), 0)),
                   pl.BlockSpec((bs, ts, D), lambda i, p, nch, off: (0, 0, 0))],
        scratch_shapes=[pltpu.VMEM((rows, D), f32), pltpu.VMEM((D // LANES, rows, LANES), f32)],
    )
    return pl.pallas_call(
        kern,
        grid_spec=grid_spec,
        out_shape=[jax.ShapeDtypeStruct(prompt_shape, f32), jax.ShapeDtypeStruct(sample_shape, f32)],
        compiler_params=pltpu.CompilerParams(dimension_semantics=("arbitrary", "arbitrary"),
                                             vmem_limit_bytes=56 * MIB),
        name="moe_combine",
    )(nch, off, ysort, pos, x1, fn)


def _chain_tiles_prompt(vec, batch):
    t = vec.reshape(HEADS // 2, 2, HD)
    t = jnp.transpose(t, (2, 1, 0))
    return jnp.broadcast_to(t[..., None], (HD, 2, HEADS // 2, batch)).reshape(HD, LANES)


def _chain_tiles_sample(vec):
    return jnp.broadcast_to(vec.reshape(HEADS, HD)[..., None], (HEADS, HD, LANES))


def kernel(x_prompt, x_sample, state_conv, state_lru, state_shift, state_wkv, norm1, w_in, conv_w, conv_b, lru_wx, lru_bx, lru_wa, lru_ba, lru_a_param, rwkv_mu, rwkv_w0, rwkv_w2, rwkv_a0, rwkv_a2, rwkv_g2, rwkv_k_k, rwkv_k_a, rwkv_r_k, rwkv_ln_w, rwkv_ln_b, w_out, norm2, router_group, router_group_b, router_expert, router_expert_b, exp_gate, exp_up, exp_down, final_norm):
    bp, tp, _ = x_prompt.shape
    bs, ts, _ = x_sample.shape
    assert norm1.shape[0] == 1 and bp * HEADS == LANES and bs == LANES and tp % 64 == 0 and ts % 2 == 0
    n_p, n_s = bp * tp, bs * ts
    c_rw = 2 * D
    c_lora = c_rw + 3 * D
    c_gl = c_lora + LORA

    w = w_in[0]
    w_all = jnp.concatenate([w[:, :c_lora], w[:, c_gl:], w[:, c_lora:c_gl],
                             jnp.zeros((D, LORA_PAD - LORA), f32)], axis=1).astype(bf16)
    mu = rwkv_mu[0]
    mu_all = jnp.concatenate([mu[:3 * D], mu[3 * D:], jnp.zeros((LORA_PAD - LORA,), f32)])[None]
    w2p = jnp.concatenate([rwkv_w2[0], jnp.zeros((LANES - LORA_W, D), f32)]).astype(bf16)
    a2p = jnp.concatenate([jnp.zeros((LORA_W, D), f32), rwkv_a2[0]]).astype(bf16)
    g2p = jnp.concatenate([rwkv_g2[0], jnp.zeros((2 * LANES - LORA_G, D), f32)]).astype(bf16)
    wxa = jnp.concatenate([lru_wx[0], lru_wa[0]], axis=-1).astype(bf16)
    row = lambda v: v.reshape(1, -1)
    rk = rwkv_r_k[0].reshape(D)
    cvecs = (rwkv_k_k[0], rwkv_k_a[0], rk, rwkv_ln_w[0], rwkv_ln_b[0])
    consts_p = jnp.stack([_chain_tiles_prompt(v, bp) for v in cvecs])[None]
    consts_s = jnp.stack([_chain_tiles_sample(v) for v in cvecs], axis=1)
    rw = jnp.zeros((ROUTER_ROWS, D), f32)
    rw = rw.at[0:N_GROUPS].set(router_group[0].T).at[EPG:EPG + N_EXPERTS].set(router_expert[0].T)
    rh = rw.astype(bf16)
    rl = (rw - rh.astype(f32)).astype(bf16)
    rb = jnp.zeros((ROUTER_ROWS, 1), f32)
    rb = rb.at[0:N_GROUPS, 0].set(router_group_b[0]).at[EPG:EPG + N_EXPERTS, 0].set(router_expert_b[0])
    wgu = jnp.concatenate([exp_gate[0], exp_up[0]], axis=-1).astype(bf16)
    wd = exp_down[0].astype(bf16)
    wo = w_out[0].astype(bf16)

    n = n_p + n_s
    proj, x = _inproj(x_prompt, x_sample, row(norm1[0]), w_all, tm=TOKEN_TILE)

    lru_args = (conv_w[0], row(conv_b[0]), wxa, row(lru_bx[0]), row(lru_ba[0]), row(lru_a_param[0]))
    conv_init_s = jnp.transpose(state_conv[0], (1, 0, 2)).reshape(3 * bs, D)
    ml, hl_p = _lru(proj, 0, n_p, bp, 512, True, jnp.zeros((3 * bp, D), f32), jnp.zeros((bp, D), f32), *lru_args)
    ml, hl_s = _lru(proj, n_p, n_s, bs, 512, False, conv_init_s, state_lru[0], *lru_args, dst=ml)

    sh = state_shift[0]
    shift_init_s = jnp.concatenate([sh, jnp.zeros((bs, LORA_PAD - LORA), f32)], axis=1)
    pre_args = (mu_all, row(rwkv_w0[0]), row(rwkv_a0[0]), w2p, a2p, g2p)
    tiles_p, rows_p, g = _rwkv_pre(proj, 0, n_p, bp, 256, 1, jnp.zeros((bp, 3 * D + LORA_PAD), f32),
                                   consts_p, *pre_args)
    tiles_s, rows_s, g = _rwkv_pre(proj, n_p, n_s, bs, 256, HEADS, shift_init_s, consts_s, *pre_args, dst=g)

    y_p, sfin_p = _wkv_scan(tiles_p, rows_p, jnp.zeros((LANES, STATE_ROWS), f32), 16)
    y_s, sfin_s = _wkv_scan(tiles_s, rows_s, state_wkv[0].reshape(bs, HEADS * STATE_ROWS), ts)
    op = _wkv_post(y_p, tiles_p, rows_p, consts_p, bp, 64, 0, n)
    op = _wkv_post(y_s, tiles_s, rows_s, consts_s, bs, 2, n_p, n, dst=op)

    x1, t, selt, seln = _outproj(ml, op, g, proj, x, wo, row(norm2[0]), rh, rl, rb)
    meta = _route_meta(seln, TOKEN_TILE)
    nch = meta[:, 0, :N_EXPERTS].reshape(-1)
    off = meta[:, 1, :N_EXPERTS].reshape(-1)
    tri = jnp.triu(jnp.ones((TOKEN_TILE, TOKEN_TILE), bf16), k=1)
    ysort, pos = _moe_experts(nch, off, t, selt, tri, wgu, wd, TOKEN_TILE)
    y_prompt, y_sample = _moe_combine(nch, off, ysort, pos, x1, row(final_norm), x_prompt.shape, x_sample.shape,
                                      TOKEN_TILE)

    def last_rows(lo, hi, n_end, batch, steps):
        return proj[n_end - steps * batch:n_end, lo:hi].reshape(steps, batch, hi - lo)

    conv_p = jnp.transpose(last_rows(0, D, n_p, bp, 3), (1, 0, 2))[None]
    conv_s = jnp.transpose(last_rows(0, D, n_p + n_s, bs, 3), (1, 0, 2))[None]

    def shift_rows(n_end, batch):
        return jnp.concatenate([last_rows(2 * D, 5 * D, n_end, batch, 1)[0],
                                last_rows(7 * D, 7 * D + LORA, n_end, batch, 1)[0]], axis=1)[None]

    wkv_p = jnp.transpose(sfin_p.reshape(2, HEADS // 2, bp, HD, HD), (2, 1, 0, 3, 4)).reshape(1, bp, HEADS, HD, HD)
    wkv_s = sfin_s.reshape(1, bs, HEADS, HD, HD)
    return (y_prompt, y_sample, conv_p, hl_p[None], shift_rows(n_p, bp), wkv_p,
            conv_s, hl_s[None], shift_rows(n_p + n_s, bs), wkv_s)
```

```python
import functools

import jax
import jax.numpy as jnp
from jax import lax
from jax.experimental import pallas as pl
from jax.experimental.pallas import tpu as pltpu

f32 = jnp.float32
bf16 = jnp.bfloat16

D = 1024
HEADS = 16
HD = 64
LANES = 128
SUBLANES = 8
LORA_W = 64
LORA_A = 64
LORA_G = 160
LORA = LORA_W + LORA_A + LORA_G
LORA_PAD = 512
N_GROUPS = 4
EPG = 8
N_EXPERTS = N_GROUPS * EPG
D_EXPERT = 256
LRU_C = 8.0
GN_EPS = 64e-5
NORM_EPS = 1e-6
PROJ_COLS = 7 * D + LORA_PAD
COL_LRU_X, COL_LRU_Y, COL_R, COL_K, COL_V, COL_GL, COL_GR = range(7)
COL_LORA = 7 * D // LORA_PAD
ROUTER_ROWS = 48
MIB = 1024 * 1024
TOKEN_TILE = 1024


def _full(shape, grid_rank=1):
    zeros = tuple(0 for _ in shape)
    if grid_rank == 1:
        return pl.BlockSpec(shape, lambda i: zeros)
    return pl.BlockSpec(shape, lambda i, j: zeros)


def _inproj_kernel(xp_ref, xs_ref, g_ref, w_ref, o_ref, xo_ref, xf_ref, xn_ref, *, prompt_tiles):
    i = pl.program_id(0)
    first_col = pl.program_id(1) == 0
    bp, tsteps = xp_ref.shape[0], xp_ref.shape[1]
    bs, ssteps = xs_ref.shape[0], xs_ref.shape[1]

    @pl.when(first_col & (i < prompt_tiles))
    def _():
        for b in range(bp):
            v = xp_ref[b]
            for cb in range(D // LANES):
                xf_ref[cb, pl.ds(b, tsteps, stride=bp), :] = v[:, cb * LANES:(cb + 1) * LANES]

    @pl.when(first_col & (i >= prompt_tiles))
    def _():
        for t in range(ssteps):
            v = xs_ref[:, t, :]
            for cb in range(D // LANES):
                xf_ref[cb, t * bs:(t + 1) * bs, :] = v[:, cb * LANES:(cb + 1) * LANES]

    @pl.when(first_col)
    def _():
        x = jnp.concatenate([xf_ref[cb] for cb in range(D // LANES)], axis=1)
        xo_ref[...] = x
        ms = jnp.mean(x * x, axis=-1, keepdims=True)
        xn_ref[...] = (x * lax.rsqrt(ms + NORM_EPS) * g_ref[...]).astype(bf16)

    o_ref[...] = jnp.dot(xn_ref[...], w_ref[...], preferred_element_type=f32)


def _inproj(x_prompt, x_sample, gain, w, tm, tn=1280):
    bp, tp, _ = x_prompt.shape
    bs, ts, _ = x_sample.shape
    assert tm % bp == 0 and tp % (tm // bp) == 0 and bs * ts == tm
    prompt_tiles = bp * tp // tm
    n = bp * tp + bs * ts
    kern = functools.partial(_inproj_kernel, prompt_tiles=prompt_tiles)
    return pl.pallas_call(
        kern,
        grid=(n // tm, PROJ_COLS // tn),
        in_specs=[
            pl.BlockSpec((bp, tm // bp, D), lambda i, j: (0, jnp.minimum(i, prompt_tiles - 1), 0)),
            pl.BlockSpec((bs, ts, D), lambda i, j: (0, 0, 0)),
            pl.BlockSpec((1, D), lambda i, j: (0, 0)),
            pl.BlockSpec((D, tn), lambda i, j: (0, j)),
        ],
        out_specs=[pl.BlockSpec((tm, tn), lambda i, j: (i, j)), pl.BlockSpec((tm, D), lambda i, j: (i, 0))],
        out_shape=[jax.ShapeDtypeStruct((n, PROJ_COLS), f32), jax.ShapeDtypeStruct((n, D), f32)],
        scratch_shapes=[pltpu.VMEM((D // LANES, tm, LANES), f32), pltpu.VMEM((tm, D), bf16)],
        compiler_params=pltpu.CompilerParams(dimension_semantics=("arbitrary", "arbitrary"),
                                             vmem_limit_bytes=56 * MIB),
        name="inproj",
    )(x_prompt, x_sample, gain, w)


def _lru_kernel(x_ref, y_ref, gl_ref, cinit_ref, h0_ref, cw_ref, cb_ref, wxa_ref, bx_ref, ba_ref, ap_ref,
                o_ref, hl_ref, xs_ref, a_ref, b_ref, h_ref, *, batch, reset_first, chunk):
    rows = x_ref.shape[0]
    nt = rows // batch
    hist = 3 * batch
    pid = pl.program_id(0)

    @pl.when(pid == 0)
    def _():
        xs_ref[0:hist, :] = cinit_ref[...]
        h_ref[...] = h0_ref[...]

    xs_ref[hist:hist + rows, :] = x_ref[...]
    logsig = -jax.nn.softplus(-ap_ref[...])

    def gates(c, _):
        r0 = pl.multiple_of(c * chunk, chunk)
        xc = (cb_ref[...]
              + cw_ref[3:4, :] * xs_ref[pl.ds(pl.multiple_of(r0 + hist, SUBLANES), chunk), :]
              + cw_ref[2:3, :] * xs_ref[pl.ds(pl.multiple_of(r0 + 2 * batch, SUBLANES), chunk), :]
              + cw_ref[1:2, :] * xs_ref[pl.ds(pl.multiple_of(r0 + batch, SUBLANES), chunk), :]
              + cw_ref[0:1, :] * xs_ref[pl.ds(r0, chunk), :])
        if reset_first:
            grow = lax.broadcasted_iota(jnp.int32, (chunk, LANES), 0) + (r0 + pid * rows)
            first = grow < batch
        for n in range(D // LANES):
            sl = slice(n * LANES, (n + 1) * LANES)
            xn = xc[:, sl]
            g2 = jnp.dot(xn.astype(bf16), wxa_ref[n], preferred_element_type=f32)
            gate_x = jax.nn.sigmoid(g2[:, :LANES] + bx_ref[:, sl])
            gate_a = jax.nn.sigmoid(g2[:, LANES:] + ba_ref[:, sl])
            log_a = LRU_C * gate_a * logsig[:, sl]
            a = jnp.exp(log_a)
            mult = jnp.sqrt(-jnp.tanh(log_a) * (a * a + 1.0))
            if reset_first:
                mult = jnp.where(first, 1.0, mult)
            a_ref[pl.ds(r0, chunk), sl] = a
            b_ref[pl.ds(r0, chunk), sl] = xn * gate_x * mult
        return 0

    lax.fori_loop(0, rows // chunk, gates, 0)

    def scan(t, h):
        r0 = pl.multiple_of(t * batch, batch)
        h = a_ref[pl.ds(r0, batch), :] * h + b_ref[pl.ds(r0, batch), :]
        b_ref[pl.ds(r0, batch), :] = h
        return h

    h = lax.fori_loop(0, nt, scan, h_ref[...], unroll=(8 if nt >= 8 and batch == SUBLANES else 1))
    h_ref[...] = h
    hl_ref[...] = h
    xs_ref[0:hist, :] = xs_ref[rows:rows + hist, :]

    def outp(c, _):
        r0 = pl.multiple_of(c * chunk, chunk)
        o_ref[pl.ds(r0, chunk), :] = (b_ref[pl.ds(r0, chunk), :] * jax.nn.gelu(y_ref[pl.ds(r0, chunk), :])
                                      * jax.nn.sigmoid(gl_ref[pl.ds(r0, chunk), :]))
        return 0

    lax.fori_loop(0, rows // chunk, outp, 0)


def _with_dst(kern, n_in):
    def wrapped(*refs):
        return kern(*refs[:n_in], *refs[n_in + 1:])
    return wrapped


def _shared_rows(dst, n_total):
    if dst is None:
        return [], [], jax.ShapeDtypeStruct((n_total, D), f32)
    return [pl.BlockSpec(memory_space=pl.ANY)], [dst], jax.ShapeDtypeStruct(dst.shape, dst.dtype)


def _lru(proj, row_off, n_rows, batch, rows_blk, reset_first, conv_init, h0, cw, cb, wxa, bx, ba, ap, dst=None):
    ob = row_off // rows_blk
    kern = functools.partial(_lru_kernel, batch=batch, reset_first=reset_first, chunk=128)
    n_in = 11
    dst_spec, dst_arg, out0 = _shared_rows(dst, proj.shape[0])
    return pl.pallas_call(
        kern if dst is None else _with_dst(kern, n_in),
        grid=(n_rows // rows_blk,),
        in_specs=[
            pl.BlockSpec((rows_blk, D), lambda i: (ob + i, COL_LRU_X)),
            pl.BlockSpec((rows_blk, D), lambda i: (ob + i, COL_LRU_Y)),
            pl.BlockSpec((rows_blk, D), lambda i: (ob + i, COL_GL)),
            _full((3 * batch, D)), _full((batch, D)), _full((4, D)), _full((1, D)),
            _full((D // LANES, LANES, 2 * LANES)), _full((1, D)), _full((1, D)), _full((1, D)),
        ] + dst_spec,
        out_specs=[pl.BlockSpec((rows_blk, D), lambda i: (ob + i, 0)), _full((batch, D))],
        out_shape=[out0, jax.ShapeDtypeStruct((batch, D), f32)],
        input_output_aliases={} if dst is None else {n_in: 0},
        scratch_shapes=[pltpu.VMEM((rows_blk + 3 * batch, D), f32), pltpu.VMEM((rows_blk, D), f32),
                        pltpu.VMEM((rows_blk, D), f32), pltpu.VMEM((batch, D), f32)],
        compiler_params=pltpu.CompilerParams(dimension_semantics=("arbitrary",), vmem_limit_bytes=48 * MIB),
        name="lru",
    )(proj, proj, proj, conv_init, h0, cw, cb, wxa, bx, ba, ap, *dst_arg)


TILE_ALPHA, TILE_BETA, TILE_W, TILE_KP, TILE_WR, TILE_V = range(6)
ROW_BR, ROW_KR, ROW_BONUS = range(3)
CONST_KK, CONST_KA, CONST_RK, CONST_LNW, CONST_LNB = range(5)
NAT_R, NAT_W, NAT_K, NAT_V, NAT_A = range(5)


def _wkv_prep(tiles_ref, rows_ref, t, g, r, w, k, v, a, const):
    kk = k * const(CONST_KK)
    nrm = jnp.sqrt(jnp.sum(kk * kk, axis=0, keepdims=True))
    kk = kk * (1.0 / jnp.maximum(nrm, 1e-12))
    beta = kk * a
    kp = k * (1.0 + (a - 1.0) * const(CONST_KA))
    tiles_ref[t, g, TILE_ALPHA] = -kk
    tiles_ref[t, g, TILE_BETA] = beta
    tiles_ref[t, g, TILE_W] = w
    tiles_ref[t, g, TILE_KP] = kp
    tiles_ref[t, g, TILE_WR] = w * r
    tiles_ref[t, g, TILE_V] = v
    rows_ref[t, g, ROW_BR:ROW_BR + 1, :] = jnp.sum(beta * r, axis=0, keepdims=True)
    rows_ref[t, g, ROW_KR:ROW_KR + 1, :] = jnp.sum(kp * r, axis=0, keepdims=True)
    rows_ref[t, g, ROW_BONUS:ROW_BONUS + 1, :] = jnp.sum(r * kp * const(CONST_RK), axis=0, keepdims=True)


def _rwkv_pre_kernel(r_ref, k_ref, v_ref, l_ref, sinit_ref, mu_ref, w0_ref, a0_ref, w2_ref, a2_ref, g2_ref, c_ref,
                     tiles_ref, rows_ref, go_ref, ps_ref, nat_ref, *, batch, chunk, chains_are_heads):
    rows = r_ref.shape[0]

    @pl.when(pl.program_id(0) == 0)
    def _():
        ps_ref[0:batch, :] = sinit_ref[...]

    ps_ref[batch:batch + rows, 0:D] = r_ref[...]
    ps_ref[batch:batch + rows, D:2 * D] = k_ref[...]
    ps_ref[batch:batch + rows, 2 * D:3 * D] = v_ref[...]
    ps_ref[batch:batch + rows, 3 * D:3 * D + LORA_PAD] = l_ref[...]

    def body(c, _):
        r0 = pl.multiple_of(c * chunk, chunk)

        def mixed(lo, hi):
            cur = ps_ref[pl.ds(pl.multiple_of(r0 + batch, SUBLANES), chunk), lo:hi]
            prev = ps_ref[pl.ds(r0, chunk), lo:hi]
            return cur + (prev - cur) * mu_ref[:, lo:hi]

        nat_ref[NAT_R, pl.ds(r0, chunk), :] = mixed(0, D)
        nat_ref[NAT_K, pl.ds(r0, chunk), :] = mixed(D, 2 * D)
        nat_ref[NAT_V, pl.ds(r0, chunk), :] = mixed(2 * D, 3 * D)
        lm = mixed(3 * D, 3 * D + LORA_PAD)
        xwa = lm[:, 0:LANES]
        xg = lm[:, LANES:3 * LANES]
        lw = jnp.dot(jnp.tanh(xwa).astype(bf16), w2_ref[...], preferred_element_type=f32)
        w_log = -jax.nn.softplus(-(w0_ref[...] + lw)) - 0.5
        nat_ref[NAT_W, pl.ds(r0, chunk), :] = jnp.exp(-jnp.exp(w_log))
        la = jnp.dot(xwa.astype(bf16), a2_ref[...], preferred_element_type=f32)
        nat_ref[NAT_A, pl.ds(r0, chunk), :] = jax.nn.sigmoid(a0_ref[...] + la)
        go_ref[pl.ds(r0, chunk), :] = jnp.dot(jax.nn.sigmoid(xg).astype(bf16), g2_ref[...],
                                              preferred_element_type=f32)
        return 0

    lax.fori_loop(0, rows // chunk, body, 0)
    ps_ref[0:batch, :] = ps_ref[rows:rows + batch, :]

    if chains_are_heads:
        def tstep(t, _):
            r0 = pl.multiple_of(t * batch, batch)
            for hp in range(D // LANES):
                trs = [nat_ref[q, pl.ds(r0, batch), hp * LANES:(hp + 1) * LANES].T for q in range(5)]
                for h2 in range(2):
                    h = 2 * hp + h2
                    _wkv_prep(tiles_ref, rows_ref, t, h, *[tr[h2 * HD:(h2 + 1) * HD] for tr in trs],
                              lambda i, h=h: c_ref[h, i])
            return 0

        lax.fori_loop(0, rows // batch, tstep, 0)
    else:
        half = LANES // 2
        lane = lax.broadcasted_iota(jnp.int32, (HD, LANES), 1)

        def load_pair(q, tp):
            zz = jnp.concatenate(
                [nat_ref[q, pl.ds(pl.multiple_of(tp * 2 * SUBLANES + tt * SUBLANES, SUBLANES), SUBLANES),
                         hp * LANES:(hp + 1) * LANES] for tt in range(2) for hp in range(D // LANES)], axis=0)
            tr = zz.T
            top, bot = tr[0:HD], tr[HD:2 * HD]
            d0 = jnp.where(lane < half, top, pltpu.roll(bot, half, axis=1))
            d1 = jnp.where(lane < half, pltpu.roll(top, half, axis=1), bot)
            return d0, d1

        def pair(tp, _):
            pairs = [load_pair(q, tp) for q in range(5)]
            for tt in range(2):
                _wkv_prep(tiles_ref, rows_ref, 2 * tp + tt, 0, *[p[tt] for p in pairs], lambda i: c_ref[0, i])
            return 0

        lax.fori_loop(0, rows // (2 * SUBLANES), pair, 0, unroll=4)


def _rwkv_pre(proj, row_off, n_rows, batch, rows_blk, groups, shift_init, consts, mu, w0, a0, w2p, a2p, g2p,
              dst=None):
    ob = row_off // rows_blk
    width = 3 * D + LORA_PAD
    steps_blk = rows_blk // batch
    kern = functools.partial(_rwkv_pre_kernel, batch=batch, chunk=128, chains_are_heads=groups > 1)
    n_in = 12
    dst_spec, dst_arg, out_g = _shared_rows(dst, proj.shape[0])
    return pl.pallas_call(
        kern if dst is None else _with_dst(kern, n_in),
        grid=(n_rows // rows_blk,),
        in_specs=[
            pl.BlockSpec((rows_blk, D), lambda i: (ob + i, COL_R)),
            pl.BlockSpec((rows_blk, D), lambda i: (ob + i, COL_K)),
            pl.BlockSpec((rows_blk, D), lambda i: (ob + i, COL_V)),
            pl.BlockSpec((rows_blk, LORA_PAD), lambda i: (ob + i, COL_LORA)),
            _full((batch, width)), _full((1, width)), _full((1, D)), _full((1, D)),
            _full((LANES, D)), _full((LANES, D)), _full((2 * LANES, D)), _full((groups, 5, HD, LANES)),
        ] + dst_spec,
        out_specs=[pl.BlockSpec((steps_blk, groups, 6, HD, LANES), lambda i: (i, 0, 0, 0, 0)),
                   pl.BlockSpec((steps_blk, groups, SUBLANES, LANES), lambda i: (i, 0, 0, 0)),
                   pl.BlockSpec((rows_blk, D), lambda i: (ob + i, 0))],
        out_shape=[jax.ShapeDtypeStruct((n_rows // batch, groups, 6, HD, LANES), f32),
                   jax.ShapeDtypeStruct((n_rows // batch, groups, SUBLANES, LANES), f32),
                   out_g],
        input_output_aliases={} if dst is None else {n_in: 2},
        scratch_shapes=[pltpu.VMEM((rows_blk + batch, width), f32), pltpu.VMEM((5, rows_blk, D), f32)],
        compiler_params=pltpu.CompilerParams(dimension_semantics=("arbitrary",), vmem_limit_bytes=56 * MIB),
        name="rwkv_pre",
    )(proj, proj, proj, proj, shift_init, mu, w0, a0, w2p, a2p, g2p, consts, *dst_arg)


STATE_ROWS = HD * HD
KEY_UNROLL = 32


def _wkv_step(s_ref, tiles_ref, rows_ref, y_ref, t):
    def key_row(tile, j):
        return tiles_ref[t, 0, tile, pl.ds(j, 1), :]

    def reduce_keys(j, acc):
        sa, y0 = acc
        s = s_ref[pl.ds(pl.multiple_of(j * HD, HD), HD), :]
        return sa + s * key_row(TILE_ALPHA, j), y0 + s * key_row(TILE_WR, j)

    zero = jnp.zeros((HD, LANES), f32)
    sa, y0 = lax.fori_loop(0, HD, reduce_keys, (zero, zero), unroll=KEY_UNROLL)
    v = tiles_ref[t, 0, TILE_V]

    def update_keys(j, _):
        rows = pl.ds(pl.multiple_of(j * HD, HD), HD)
        s_ref[rows, :] = (s_ref[rows, :] * key_row(TILE_W, j) + sa * key_row(TILE_BETA, j)
                          + v * key_row(TILE_KP, j))
        return 0

    lax.fori_loop(0, HD, update_keys, 0, unroll=KEY_UNROLL)
    y_ref[t, 0] = y0 + sa * rows_ref[t, 0, ROW_BR:ROW_BR + 1, :] + v * rows_ref[t, 0, ROW_KR:ROW_KR + 1, :]


def _wkv_scan_kernel(tiles_ref, rows_ref, st_ref, y_ref, sfin_ref, s_ref, *, steps):
    tb = pl.program_id(1)

    def value_rows(i):
        return pl.ds(i, HD, stride=HD)

    @pl.when(tb == 0)
    def _():
        def init(c, _):
            tr = st_ref[:, pl.ds(pl.multiple_of(c * LANES, LANES), LANES)].T
            for i2 in range(2):
                s_ref[value_rows(2 * c + i2), :] = tr[i2 * HD:(i2 + 1) * HD]
            return 0

        lax.fori_loop(0, STATE_ROWS // LANES, init, 0, unroll=4)

    def step(t, _):
        _wkv_step(s_ref, tiles_ref, rows_ref, y_ref, t)
        return 0

    lax.fori_loop(0, steps, step, 0)

    @pl.when(tb == pl.num_programs(1) - 1)
    def _():
        def fin(c, _):
            pair = jnp.concatenate([s_ref[value_rows(2 * c + i2), :] for i2 in range(2)], axis=0)
            sfin_ref[:, pl.ds(pl.multiple_of(c * LANES, LANES), LANES)] = pair.T
            return 0

        lax.fori_loop(0, STATE_ROWS // LANES, fin, 0, unroll=4)


def _wkv_scan(tiles, rows, state, steps_blk):
    n_steps, groups = tiles.shape[0], tiles.shape[1]
    kern = functools.partial(_wkv_scan_kernel, steps=steps_blk)
    sblk = pl.BlockSpec((LANES, STATE_ROWS), lambda g, i: (0, g))
    return pl.pallas_call(
        kern,
        grid=(groups, n_steps // steps_blk),
        in_specs=[pl.BlockSpec((steps_blk, 1, 6, HD, LANES), lambda g, i: (i, g, 0, 0, 0)),
                  pl.BlockSpec((steps_blk, 1, SUBLANES, LANES), lambda g, i: (i, g, 0, 0)),
                  sblk],
        out_specs=[pl.BlockSpec((steps_blk, 1, HD, LANES), lambda g, i: (i, g, 0, 0)), sblk],
        out_shape=[jax.ShapeDtypeStruct((n_steps, groups, HD, LANES), f32),
                   jax.ShapeDtypeStruct((LANES, groups * STATE_ROWS), f32)],
        scratch_shapes=[pltpu.VMEM((STATE_ROWS, LANES), f32)],
        compiler_params=pltpu.CompilerParams(dimension_semantics=("arbitrary", "arbitrary"),
                                             vmem_limit_bytes=40 * MIB),
        name="wkv_scan",
    )(tiles, rows, state)


def _wkv_norm(y, v, bonus, lnw, lnb):
    mean = jnp.sum(y, axis=0, keepdims=True) * (1.0 / HD)
    d = y - mean
    var = jnp.sum(d * d, axis=0, keepdims=True) * (1.0 / HD)
    return d * lax.rsqrt(var + GN_EPS) * lnw + lnb + bonus * v


def _wkv_post_kernel(y_ref, v_ref, rows_ref, c_ref, o_ref, *, batch, chains_are_heads):
    steps = y_ref.shape[0]

    def normed(t, g):
        return _wkv_norm(y_ref[t, g], v_ref[t, g, 0], rows_ref[t, g, ROW_BONUS:ROW_BONUS + 1, :],
                         c_ref[g, CONST_LNW], c_ref[g, CONST_LNB])

    if chains_are_heads:
        def tstep(t, _):
            r0 = pl.multiple_of(t * batch, batch)
            for hp in range(D // LANES):
                two = jnp.concatenate([normed(t, 2 * hp), normed(t, 2 * hp + 1)], axis=0)
                o_ref[pl.ds(r0, batch), hp * LANES:(hp + 1) * LANES] = two.T
            return 0

        lax.fori_loop(0, steps, tstep, 0)
    else:
        half = LANES // 2
        lane = lax.broadcasted_iota(jnp.int32, (HD, LANES), 1)

        def pair(tp, _):
            o0 = normed(2 * tp, 0)
            o1 = normed(2 * tp + 1, 0)
            top = jnp.where(lane < half, o0, pltpu.roll(o1, half, axis=1))
            bot = jnp.where(lane < half, pltpu.roll(o0, half, axis=1), o1)
            zz = jnp.concatenate([top, bot], axis=0).T
            for tt in range(2):
                row = pl.multiple_of(tp * 2 * SUBLANES + tt * SUBLANES, SUBLANES)
                for hp in range(D // LANES):
                    src = tt * HD + hp * SUBLANES
                    o_ref[pl.ds(row, SUBLANES), hp * LANES:(hp + 1) * LANES] = zz[src:src + SUBLANES, :]
            return 0

        lax.fori_loop(0, steps // 2, pair, 0, unroll=2)


def _wkv_post(y, tiles, rows, consts, batch, steps_blk, row_off, n_total, dst=None):
    n_steps, groups = y.shape[0], y.shape[1]
    ob = row_off // (steps_blk * batch)
    kern = functools.partial(_wkv_post_kernel, batch=batch, chains_are_heads=groups > 1)
    n_in = 4
    dst_spec, dst_arg, out0 = _shared_rows(dst, n_total)
    return pl.pallas_call(
        kern if dst is None else _with_dst(kern, n_in),
        grid=(n_steps // steps_blk,),
        in_specs=[pl.BlockSpec((steps_blk, groups, HD, LANES), lambda i: (i, 0, 0, 0)),
                  pl.BlockSpec((steps_blk, groups, 1, HD, LANES), lambda i: (i, 0, TILE_V, 0, 0)),
                  pl.BlockSpec((steps_blk, groups, SUBLANES, LANES), lambda i: (i, 0, 0, 0)),
                  _full((groups, 5, HD, LANES))] + dst_spec,
        out_specs=pl.BlockSpec((steps_blk * batch, D), lambda i: (ob + i, 0)),
        out_shape=out0,
        input_output_aliases={} if dst is None else {n_in: 0},
        compiler_params=pltpu.CompilerParams(dimension_semantics=("arbitrary",), vmem_limit_bytes=40 * MIB),
        name="wkv_post",
    )(y, tiles, rows, consts, *dst_arg)


def _outproj_kernel(ml_ref, op_ref, g_ref, gr_ref, x_ref, wo_ref, n2_ref, rh_ref, rl_ref, rb_ref,
                    x1_ref, t_ref, selt_ref, seln_ref):
    tm = x_ref.shape[0]
    merged = ml_ref[...] + jax.nn.sigmoid(gr_ref[...]) * (op_ref[...] * g_ref[...])
    x1 = x_ref[...] + jnp.dot(merged.astype(bf16), wo_ref[...], preferred_element_type=f32)
    x1_ref[...] = x1
    t = x1 * lax.rsqrt(jnp.mean(x1 * x1, axis=-1, keepdims=True) + NORM_EPS) * n2_ref[...]
    th = t.astype(bf16)
    t_ref[...] = th
    tl = (t - th.astype(f32)).astype(bf16)
    nt_dims = (((1,), (1,)), ((), ()))
    lg = (lax.dot_general(rh_ref[...], th, nt_dims, preferred_element_type=f32)
          + lax.dot_general(rh_ref[...], tl, nt_dims, preferred_element_type=f32)
          + lax.dot_general(rl_ref[...], th, nt_dims, preferred_element_type=f32)) + rb_ref[...]
    row = lax.broadcasted_iota(jnp.int32, (EPG, tm), 0).astype(f32)
    neg = jnp.float32(-jnp.inf)
    glog = jnp.where(row < N_GROUPS, lg[0:EPG], neg)
    ge = jnp.exp(glog - jnp.max(glog, axis=0, keepdims=True))
    pg = ge / jnp.sum(ge, axis=0, keepdims=True)
    p_top = jnp.max(pg, axis=0, keepdims=True)
    g_idx = jnp.min(jnp.where(pg == p_top, row, EPG), axis=0, keepdims=True)
    le = jnp.zeros((EPG, tm), f32)
    for g in range(N_GROUPS):
        le = jnp.where(g_idx == g, lg[EPG * (g + 1):EPG * (g + 2)], le)
    qe = jnp.exp(le - jnp.max(le, axis=0, keepdims=True))
    q = qe / jnp.sum(qe, axis=0, keepdims=True)
    q1 = jnp.max(q, axis=0, keepdims=True)
    i1 = jnp.min(jnp.where(q == q1, row, EPG), axis=0, keepdims=True)
    qm = jnp.where(row == i1, -1.0, q)
    q2 = jnp.max(qm, axis=0, keepdims=True)
    i2 = jnp.min(jnp.where(qm == q2, row, EPG), axis=0, keepdims=True)
    qs = q1 + q2
    sel = jnp.concatenate([g_idx * EPG + i1, g_idx * EPG + i2, q1 / qs * p_top, q2 / qs * p_top,
                           jnp.zeros((SUBLANES - 4, tm), f32)], axis=0)
    selt_ref[...] = sel
    sel_pad = jnp.concatenate([sel, jnp.zeros((LANES - SUBLANES, tm), f32)], axis=0)
    for c in range(tm // LANES):
        seln_ref[c * LANES:(c + 1) * LANES, :] = sel_pad[:, c * LANES:(c + 1) * LANES].T


def _outproj(ml, op, g, proj, x, wo, n2, rh, rl, rb, tm=512):
    n = x.shape[0]
    blk = pl.BlockSpec((tm, D), lambda i: (i, 0))
    return pl.pallas_call(
        _outproj_kernel,
        grid=(n // tm,),
        in_specs=[blk, blk, blk, pl.BlockSpec((tm, D), lambda i: (i, COL_GR)), blk,
                  _full((D, D)), _full((1, D)), _full((ROUTER_ROWS, D)), _full((ROUTER_ROWS, D)),
                  _full((ROUTER_ROWS, 1))],
        out_specs=[blk, blk, pl.BlockSpec((SUBLANES, tm), lambda i: (0, i)),
                   pl.BlockSpec((tm, LANES), lambda i: (i, 0))],
        out_shape=[jax.ShapeDtypeStruct((n, D), f32), jax.ShapeDtypeStruct((n, D), bf16),
                   jax.ShapeDtypeStruct((SUBLANES, n), f32), jax.ShapeDtypeStruct((n, LANES), f32)],
        compiler_params=pltpu.CompilerParams(dimension_semantics=("arbitrary",), vmem_limit_bytes=48 * MIB),
        name="outproj_router",
    )(ml, op, g, proj, x, wo, n2, rh, rl, rb)


MOE_CHUNK = 32
MOE_ROWS = 128
MOE_GATHER = 512
MOE_OVERRUN = MOE_ROWS
MOE_CAP = -(-(2 * TOKEN_TILE + N_EXPERTS * (MOE_CHUNK - 1) + MOE_OVERRUN) // MOE_GATHER) * MOE_GATHER
SEL_E1, SEL_E2, SEL_C1, SEL_C2 = range(4)


def _route_meta_kernel(seln_ref, o_ref):
    tm = seln_ref.shape[0]
    lane = lax.broadcasted_iota(jnp.int32, (tm, LANES), 1).astype(f32)
    sel = seln_ref[...]
    hit = (lane == sel[:, SEL_E1:SEL_E1 + 1]) | (lane == sel[:, SEL_E2:SEL_E2 + 1])
    cnt = jnp.sum(jnp.where(hit, 1.0, 0.0), axis=0, keepdims=True)
    nchunk = jnp.floor((cnt + (MOE_CHUNK - 1)) * (1.0 / MOE_CHUNK))
    upper = (lax.broadcasted_iota(jnp.int32, (LANES, LANES), 0)
             < lax.broadcasted_iota(jnp.int32, (LANES, LANES), 1))
    offs = jnp.dot(jnp.broadcast_to(nchunk, (SUBLANES, LANES)).astype(bf16), jnp.where(upper, 1.0, 0.0).astype(bf16),
                   preferred_element_type=f32)
    row = lax.broadcasted_iota(jnp.int32, (SUBLANES, LANES), 0)
    o_ref[0] = jnp.where(row == 0, nchunk, jnp.where(row == 1, offs, 0.0)).astype(jnp.int32)


def _route_meta(seln, tm):
    tiles = seln.shape[0] // tm
    return pl.pallas_call(
        _route_meta_kernel,
        grid=(tiles,),
        in_specs=[pl.BlockSpec((tm, LANES), lambda i: (i, 0))],
        out_specs=pl.BlockSpec((1, SUBLANES, LANES), lambda i: (i, 0, 0)),
        out_shape=jax.ShapeDtypeStruct((tiles, SUBLANES, LANES), jnp.int32),
        compiler_params=pltpu.CompilerParams(dimension_semantics=("arbitrary",)),
        name="route_meta",
    )(seln)


def _moe_experts_kernel(nch_ref, off_ref, t_ref, selt_ref, tri_ref, wgu_ref, wd_ref, ys_ref, pos_ref,
                        xs_ref, cw_ref):
    i = pl.program_id(0)
    e = pl.program_id(1)
    tm = t_ref.shape[0]
    used_rows = (off_ref[i * N_EXPERTS + N_EXPERTS - 1] + nch_ref[i * N_EXPERTS + N_EXPERTS - 1]) * MOE_CHUNK

    @pl.when(e == 0)
    def _():
        ys_ref[...] = jnp.zeros_like(ys_ref)
        e1 = selt_ref[SEL_E1:SEL_E1 + 1, :]
        e2 = selt_ref[SEL_E2:SEL_E2 + 1, :]
        erow = lax.broadcasted_iota(jnp.int32, (N_EXPERTS, tm), 0).astype(f32)
        oh1 = jnp.where(erow == e1, 1.0, 0.0)
        oh2 = jnp.where(erow == e2, 1.0, 0.0)
        before1 = jnp.dot(oh1.astype(bf16), tri_ref[...], preferred_element_type=f32)
        before2 = jnp.dot(oh2.astype(bf16), tri_ref[...], preferred_element_type=f32)
        cnt1 = jnp.sum(oh1, axis=1, keepdims=True)
        cnt2 = jnp.sum(oh2, axis=1, keepdims=True)
        nchunk = jnp.floor((cnt1 + cnt2 + (MOE_CHUNK - 1)) * (1.0 / MOE_CHUNK))
        lower = (lax.broadcasted_iota(jnp.int32, (N_EXPERTS, LANES), 1)
                 < lax.broadcasted_iota(jnp.int32, (N_EXPERTS, LANES), 0))
        nchunk_rows = jnp.concatenate([jnp.broadcast_to(nchunk, (N_EXPERTS, LANES)),
                                       jnp.zeros((LANES - N_EXPERTS, LANES), f32)], axis=0)
        start = jnp.dot(jnp.where(lower, 1.0, 0.0).astype(bf16), nchunk_rows.astype(bf16),
                        preferred_element_type=f32)[:, 0:1] * MOE_CHUNK
        pos1 = jnp.sum(oh1 * (start + before1), axis=0, keepdims=True)
        pos2 = jnp.sum(oh2 * (start + cnt1 + before2), axis=0, keepdims=True)
        pos_ref[...] = jnp.concatenate([pos1, pos2, jnp.zeros((SUBLANES - 2, tm), f32)], axis=0)
        c1 = selt_ref[SEL_C1:SEL_C1 + 1, :]
        c2 = selt_ref[SEL_C2:SEL_C2 + 1, :]
        for k in range(MOE_CAP // MOE_GATHER):
            @pl.when(k * MOE_GATHER < used_rows + MOE_OVERRUN)
            def _(k=k):
                ridx = (lax.broadcasted_iota(jnp.int32, (MOE_GATHER, tm), 0) + k * MOE_GATHER).astype(f32)
                p1 = ridx == pos1
                p2 = ridx == pos2
                onehot = jnp.where(p1 | p2, 1.0, 0.0).astype(bf16)
                xs_ref[k * MOE_GATHER:(k + 1) * MOE_GATHER, :] = jnp.dot(
                    onehot, t_ref[...], preferred_element_type=f32).astype(bf16)
                w = jnp.sum(jnp.where(p1, c1, 0.0) + jnp.where(p2, c2, 0.0), axis=1, keepdims=True)
                cw_ref[k * MOE_GATHER:(k + 1) * MOE_GATHER, :] = jnp.broadcast_to(w, (MOE_GATHER, LANES))

    base = off_ref[i * N_EXPERTS + e] * MOE_CHUNK

    def expert_rows(start):
        rows = pl.ds(pl.multiple_of(start, MOE_CHUNK), MOE_ROWS)
        gu = jnp.dot(xs_ref[rows, :], wgu_ref[0], preferred_element_type=f32)
        w = cw_ref[rows, :]
        h = jax.nn.silu(gu[:, :D_EXPERT]) * gu[:, D_EXPERT:] * jnp.concatenate([w, w], axis=1)
        ys_ref[0, rows, :] = jnp.dot(h.astype(bf16), wd_ref[0], preferred_element_type=f32).astype(bf16)

    expert_rows(base)

    def more(c, _):
        expert_rows(base + (c + 1) * MOE_ROWS)
        return 0

    per_block = MOE_ROWS // MOE_CHUNK
    lax.fori_loop(0, jnp.maximum(nch_ref[i * N_EXPERTS + e] - 1, 0) // per_block, more, 0)


def _moe_experts(nch, off, t, selt, tri, wgu, wd, tm):
    n = t.shape[0]
    tiles = n // tm
    grid_spec = pltpu.PrefetchScalarGridSpec(
        num_scalar_prefetch=2,
        grid=(tiles, N_EXPERTS),
        in_specs=[
            pl.BlockSpec((tm, D), lambda i, e, nch, off: (i, 0)),
            pl.BlockSpec((SUBLANES, tm), lambda i, e, nch, off: (0, i)),
            pl.BlockSpec((tm, tm), lambda i, e, nch, off: (0, 0)),
            pl.BlockSpec((1, D, 2 * D_EXPERT), lambda i, e, nch, off: (e, 0, 0)),
            pl.BlockSpec((1, D_EXPERT, D), lambda i, e, nch, off: (e, 0, 0)),
        ],
        out_specs=[pl.BlockSpec((1, MOE_CAP, D), lambda i, e, nch, off: (i, 0, 0)),
                   pl.BlockSpec((SUBLANES, tm), lambda i, e, nch, off: (0, i))],
        scratch_shapes=[pltpu.VMEM((MOE_CAP, D), bf16), pltpu.VMEM((MOE_CAP, LANES), f32)],
    )
    return pl.pallas_call(
        _moe_experts_kernel,
        grid_spec=grid_spec,
        out_shape=[jax.ShapeDtypeStruct((tiles, MOE_CAP, D), bf16), jax.ShapeDtypeStruct((SUBLANES, n), f32)],
        compiler_params=pltpu.CompilerParams(dimension_semantics=("arbitrary", "arbitrary"),
                                             vmem_limit_bytes=56 * MIB),
        name="moe_experts",
    )(nch, off, t, selt, tri, wgu, wd)


def _moe_combine_kernel(nch_ref, off_ref, ysort_ref, pos_ref, x1_ref, fn_ref, yp_ref, ys_ref, acc_ref, yf_ref,
                        *, prompt_tiles, parts):
    i = pl.program_id(0)
    part = pl.program_id(1)
    rows = x1_ref.shape[0]
    bp, tsteps = yp_ref.shape[0], yp_ref.shape[1]
    bs = ys_ref.shape[0]
    ssteps = ys_ref.shape[1] // parts
    used_rows = (off_ref[i * N_EXPERTS + N_EXPERTS - 1] + nch_ref[i * N_EXPERTS + N_EXPERTS - 1]) * MOE_CHUNK

    pos_t = jnp.concatenate([pos_ref[...], jnp.zeros((LANES - SUBLANES, rows), f32)], axis=0)
    pos_n = jnp.concatenate([pos_t[:, c * LANES:(c + 1) * LANES].T for c in range(rows // LANES)], axis=0)
    pos1 = pos_n[:, 0:1]
    pos2 = pos_n[:, 1:2]
    acc_ref[...] = x1_ref[...]
    for k in range(MOE_CAP // MOE_GATHER):
        @pl.when(k * MOE_GATHER < used_rows)
        def _(k=k):
            cidx = (lax.broadcasted_iota(jnp.int32, (rows, MOE_GATHER), 1) + k * MOE_GATHER).astype(f32)
            onehot = jnp.where((cidx == pos1) | (cidx == pos2), 1.0, 0.0).astype(bf16)
            acc_ref[...] += jnp.dot(onehot, ysort_ref[0, k * MOE_GATHER:(k + 1) * MOE_GATHER, :],
                                    preferred_element_type=f32)

    xo = acc_ref[...]
    y = xo * lax.rsqrt(jnp.mean(xo * xo, axis=-1, keepdims=True) + NORM_EPS) * fn_ref[...]
    for cb in range(D // LANES):
        yf_ref[cb] = y[:, cb * LANES:(cb + 1) * LANES]

    @pl.when(i < prompt_tiles)
    def _():
        for b in range(bp):
            yp_ref[b] = jnp.concatenate([yf_ref[cb, pl.ds(b, tsteps, stride=bp), :] for cb in range(D // LANES)],
                                        axis=1)

    for p in range(parts):
        @pl.when((i >= prompt_tiles) & (part == p))
        def _(p=p):
            for t in range(ssteps):
                ys_ref[:, p * ssteps + t, :] = jnp.concatenate(
                    [yf_ref[cb, t * bs:(t + 1) * bs, :] for cb in range(D // LANES)], axis=1)


def _moe_combine(nch, off, ysort, pos, x1, fn, prompt_shape, sample_shape, tm, parts=2):
    n = x1.shape[0]
    bp, tp, _ = prompt_shape
    bs, ts, _ = sample_shape
    rows = tm // parts
    assert rows % bp == 0 and tp % (rows // bp) == 0 and bs * ts == tm and ts % parts == 0
    prompt_tiles = bp * tp // tm
    last_prompt_blk = prompt_tiles * parts - 1
    kern = functools.partial(_moe_combine_kernel, prompt_tiles=prompt_tiles, parts=parts)
    grid_spec = pltpu.PrefetchScalarGridSpec(
        num_scalar_prefetch=2,
        grid=(n // tm, parts),
        in_specs=[
            pl.BlockSpec((1, MOE_CAP, D), lambda i, p, nch, off: (i, 0, 0)),
            pl.BlockSpec((SUBLANES, rows), lambda i, p, nch, off: (0, i * parts + p)),
            pl.BlockSpec((rows, D), lambda i, p, nch, off: (i * parts + p, 0)),
            pl.BlockSpec((1, D), lambda i, p, nch, off: (0, 0)),
        ],
        out_specs=[pl.BlockSpec((bp, rows // bp, D),
                                lambda i, p, nch, off: (0, jnp.minimum(i * parts + p, last_prompt_blk), 0)),
                   pl.BlockSpec((bs, ts, D), lambda i, p, nch, off: (0, 0, 0))],
        scratch_shapes=[pltpu.VMEM((rows, D), f32), pltpu.VMEM((D // LANES, rows, LANES), f32)],
    )
    return pl.pallas_call(
        kern,
        grid_spec=grid_spec,
        out_shape=[jax.ShapeDtypeStruct(prompt_shape, f32), jax.ShapeDtypeStruct(sample_shape, f32)],
        compiler_params=pltpu.CompilerParams(dimension_semantics=("arbitrary", "arbitrary"),
                                             vmem_limit_bytes=56 * MIB),
        name="moe_combine",
    )(nch, off, ysort, pos, x1, fn)


def _chain_tiles_prompt(vec, batch):
    t = vec.reshape(HEADS // 2, 2, HD)
    t = jnp.transpose(t, (2, 1, 0))
    return jnp.broadcast_to(t[..., None], (HD, 2, HEADS // 2, batch)).reshape(HD, LANES)


def _chain_tiles_sample(vec):
    return jnp.broadcast_to(vec.reshape(HEADS, HD)[..., None], (HEADS, HD, LANES))


def kernel(x_prompt, x_sample, state_conv, state_lru, state_shift, state_wkv, norm1, w_in, conv_w, conv_b, lru_wx, lru_bx, lru_wa, lru_ba, lru_a_param, rwkv_mu, rwkv_w0, rwkv_w2, rwkv_a0, rwkv_a2, rwkv_g2, rwkv_k_k, rwkv_k_a, rwkv_r_k, rwkv_ln_w, rwkv_ln_b, w_out, norm2, router_group, router_group_b, router_expert, router_expert_b, exp_gate, exp_up, exp_down, final_norm):
    bp, tp, _ = x_prompt.shape
    bs, ts, _ = x_sample.shape
    assert norm1.shape[0] == 1 and bp * HEADS == LANES and bs == LANES and tp % 64 == 0 and ts % 2 == 0
    n_p, n_s = bp * tp, bs * ts
    c_rw = 2 * D
    c_lora = c_rw + 3 * D
    c_gl = c_lora + LORA

    w = w_in[0]
    w_all = jnp.concatenate([w[:, :c_lora], w[:, c_gl:], w[:, c_lora:c_gl],
                             jnp.zeros((D, LORA_PAD - LORA), f32)], axis=1).astype(bf16)
    mu = rwkv_mu[0]
    mu_all = jnp.concatenate([mu[:3 * D], mu[3 * D:], jnp.zeros((LORA_PAD - LORA,), f32)])[None]
    w2p = jnp.concatenate([rwkv_w2[0], jnp.zeros((LANES - LORA_W, D), f32)]).astype(bf16)
    a2p = jnp.concatenate([jnp.zeros((LORA_W, D), f32), rwkv_a2[0]]).astype(bf16)
    g2p = jnp.concatenate([rwkv_g2[0], jnp.zeros((2 * LANES - LORA_G, D), f32)]).astype(bf16)
    wxa = jnp.concatenate([lru_wx[0], lru_wa[0]], axis=-1).astype(bf16)
    row = lambda v: v.reshape(1, -1)
    rk = rwkv_r_k[0].reshape(D)
    cvecs = (rwkv_k_k[0], rwkv_k_a[0], rk, rwkv_ln_w[0], rwkv_ln_b[0])
    consts_p = jnp.stack([_chain_tiles_prompt(v, bp) for v in cvecs])[None]
    consts_s = jnp.stack([_chain_tiles_sample(v) for v in cvecs], axis=1)
    rw = jnp.zeros((ROUTER_ROWS, D), f32)
    rw = rw.at[0:N_GROUPS].set(router_group[0].T).at[EPG:EPG + N_EXPERTS].set(router_expert[0].T)
    rh = rw.astype(bf16)
    rl = (rw - rh.astype(f32)).astype(bf16)
    rb = jnp.zeros((ROUTER_ROWS, 1), f32)
    rb = rb.at[0:N_GROUPS, 0].set(router_group_b[0]).at[EPG:EPG + N_EXPERTS, 0].set(router_expert_b[0])
    wgu = jnp.concatenate([exp_gate[0], exp_up[0]], axis=-1).astype(bf16)
    wd = exp_down[0].astype(bf16)
    wo = w_out[0].astype(bf16)

    n = n_p + n_s
    proj, x = _inproj(x_prompt, x_sample, row(norm1[0]), w_all, tm=TOKEN_TILE)

    lru_args = (conv_w[0], row(conv_b[0]), wxa, row(lru_bx[0]), row(lru_ba[0]), row(lru_a_param[0]))
    conv_init_s = jnp.transpose(state_conv[0], (1, 0, 2)).reshape(3 * bs, D)
    ml, hl_p = _lru(proj, 0, n_p, bp, 512, True, jnp.zeros((3 * bp, D), f32), jnp.zeros((bp, D), f32), *lru_args)
    ml, hl_s = _lru(proj, n_p, n_s, bs, 512, False, conv_init_s, state_lru[0], *lru_args, dst=ml)

    sh = state_shift[0]
    shift_init_s = jnp.concatenate([sh, jnp.zeros((bs, LORA_PAD - LORA), f32)], axis=1)
    pre_args = (mu_all, row(rwkv_w0[0]), row(rwkv_a0[0]), w2p, a2p, g2p)
    tiles_p, rows_p, g = _rwkv_pre(proj, 0, n_p, bp, 256, 1, jnp.zeros((bp, 3 * D + LORA_PAD), f32),
                                   consts_p, *pre_args)
    tiles_s, rows_s, g = _rwkv_pre(proj, n_p, n_s, bs, 256, HEADS, shift_init_s, consts_s, *pre_args, dst=g)

    y_p, sfin_p = _wkv_scan(tiles_p, rows_p, jnp.zeros((LANES, STATE_ROWS), f32), 16)
    y_s, sfin_s = _wkv_scan(tiles_s, rows_s, state_wkv[0].reshape(bs, HEADS * STATE_ROWS), ts)
    op = _wkv_post(y_p, tiles_p, rows_p, consts_p, bp, 64, 0, n)
    op = _wkv_post(y_s, tiles_s, rows_s, consts_s, bs, 2, n_p, n, dst=op)

    x1, t, selt, seln = _outproj(ml, op, g, proj, x, wo, row(norm2[0]), rh, rl, rb)
    meta = _route_meta(seln, TOKEN_TILE)
    nch = meta[:, 0, :N_EXPERTS].reshape(-1)
    off = meta[:, 1, :N_EXPERTS].reshape(-1)
    tri = jnp.triu(jnp.ones((TOKEN_TILE, TOKEN_TILE), bf16), k=1)
    ysort, pos = _moe_experts(nch, off, t, selt, tri, wgu, wd, TOKEN_TILE)
    y_prompt, y_sample = _moe_combine(nch, off, ysort, pos, x1, row(final_norm), x_prompt.shape, x_sample.shape,
                                      TOKEN_TILE)

    def last_rows(lo, hi, n_end, batch, steps):
        return proj[n_end - steps * batch:n_end, lo:hi].reshape(steps, batch, hi - lo)

    conv_p = jnp.transpose(last_rows(0, D, n_p, bp, 3), (1, 0, 2))[None]
    conv_s = jnp.transpose(last_rows(0, D, n_p + n_s, bs, 3), (1, 0, 2))[None]

    def shift_rows(n_end, batch):
        return jnp.concatenate([last_rows(2 * D, 5 * D, n_end, batch, 1)[0],
                                last_rows(7 * D, 7 * D + LORA, n_end, batch, 1)[0]], axis=1)[None]

    wkv_p = jnp.transpose(sfin_p.reshape(2, HEADS // 2, bp, HD, HD), (2, 1, 0, 3, 4)).reshape(1, bp, HEADS, HD, HD)
    wkv_s = sfin_s.reshape(1, bs, HEADS, HD, HD)
    return (y_prompt, y_sample, conv_p, hl_p[None], shift_rows(n_p, bp), wkv_p,
            conv_s, hl_s[None], shift_rows(n_p + n_s, bs), wkv_s)
```

```python
import functools

import jax
import jax.numpy as jnp
from jax import lax
from jax.experimental import pallas as pl
from jax.experimental.pallas import tpu as pltpu

f32 = jnp.float32
bf16 = jnp.bfloat16

D = 1024
HEADS = 16
HD = 64
LANES = 128
SUBLANES = 8
LORA_W = 64
LORA_A = 64
LORA_G = 160
LORA = LORA_W + LORA_A + LORA_G
LORA_PAD = 512
N_GROUPS = 4
EPG = 8
N_EXPERTS = N_GROUPS * EPG
D_EXPERT = 256
LRU_C = 8.0
GN_EPS = 64e-5
NORM_EPS = 1e-6
PROJ_COLS = 7 * D + LORA_PAD
COL_LRU_X, COL_LRU_Y, COL_R, COL_K, COL_V, COL_GL, COL_GR = range(7)
COL_LORA = 7 * D // LORA_PAD
ROUTER_ROWS = 48
MIB = 1024 * 1024
TOKEN_TILE = 1024


def _full(shape, grid_rank=1):
    zeros = tuple(0 for _ in shape)
    if grid_rank == 1:
        return pl.BlockSpec(shape, lambda i: zeros)
    return pl.BlockSpec(shape, lambda i, j: zeros)


def _inproj_kernel(xp_ref, xs_ref, g_ref, w_ref, o_ref, xo_ref, xf_ref, xn_ref, *, prompt_tiles):
    i = pl.program_id(0)
    first_col = pl.program_id(1) == 0
    bp, tsteps = xp_ref.shape[0], xp_ref.shape[1]
    bs, ssteps = xs_ref.shape[0], xs_ref.shape[1]

    @pl.when(first_col & (i < prompt_tiles))
    def _():
        for b in range(bp):
            v = xp_ref[b]
            for cb in range(D // LANES):
                xf_ref[cb, pl.ds(b, tsteps, stride=bp), :] = v[:, cb * LANES:(cb + 1) * LANES]

    @pl.when(first_col & (i >= prompt_tiles))
    def _():
        for t in range(ssteps):
            v = xs_ref[:, t, :]
            for cb in range(D // LANES):
                xf_ref[cb, t * bs:(t + 1) * bs, :] = v[:, cb * LANES:(cb + 1) * LANES]

    @pl.when(first_col)
    def _():
        x = jnp.concatenate([xf_ref[cb] for cb in range(D // LANES)], axis=1)
        xo_ref[...] = x
        ms = jnp.mean(x * x, axis=-1, keepdims=True)
        xn_ref[...] = (x * lax.rsqrt(ms + NORM_EPS) * g_ref[...]).astype(bf16)

    o_ref[...] = jnp.dot(xn_ref[...], w_ref[...], preferred_element_type=f32)


def _inproj(x_prompt, x_sample, gain, w, tm, tn=1280):
    bp, tp, _ = x_prompt.shape
    bs, ts, _ = x_sample.shape
    assert tm % bp == 0 and tp % (tm // bp) == 0 and bs * ts == tm
    prompt_tiles = bp * tp // tm
    n = bp * tp + bs * ts
    kern = functools.partial(_inproj_kernel, prompt_tiles=prompt_tiles)
    return pl.pallas_call(
        kern,
        grid=(n // tm, PROJ_COLS // tn),
        in_specs=[
            pl.BlockSpec((bp, tm // bp, D), lambda i, j: (0, jnp.minimum(i, prompt_tiles - 1), 0)),
            pl.BlockSpec((bs, ts, D), lambda i, j: (0, 0, 0)),
            pl.BlockSpec((1, D), lambda i, j: (0, 0)),
            pl.BlockSpec((D, tn), lambda i, j: (0, j)),
        ],
        out_specs=[pl.BlockSpec((tm, tn), lambda i, j: (i, j)), pl.BlockSpec((tm, D), lambda i, j: (i, 0))],
        out_shape=[jax.ShapeDtypeStruct((n, PROJ_COLS), f32), jax.ShapeDtypeStruct((n, D), f32)],
        scratch_shapes=[pltpu.VMEM((D // LANES, tm, LANES), f32), pltpu.VMEM((tm, D), bf16)],
        compiler_params=pltpu.CompilerParams(dimension_semantics=("arbitrary", "arbitrary"),
                                             vmem_limit_bytes=56 * MIB),
        name="inproj",
    )(x_prompt, x_sample, gain, w)


def _lru_kernel(x_ref, y_ref, gl_ref, cinit_ref, h0_ref, cw_ref, cb_ref, wxa_ref, bx_ref, ba_ref, ap_ref,
                o_ref, hl_ref, xs_ref, a_ref, b_ref, h_ref, *, batch, reset_first, chunk):
    rows = x_ref.shape[0]
    nt = rows // batch
    hist = 3 * batch
    pid = pl.program_id(0)

    @pl.when(pid == 0)
    def _():
        xs_ref[0:hist, :] = cinit_ref[...]
        h_ref[...] = h0_ref[...]

    xs_ref[hist:hist + rows, :] = x_ref[...]
    logsig = -jax.nn.softplus(-ap_ref[...])

    def gates(c, _):
        r0 = pl.multiple_of(c * chunk, chunk)
        xc = (cb_ref[...]
              + cw_ref[3:4, :] * xs_ref[pl.ds(pl.multiple_of(r0 + hist, SUBLANES), chunk), :]
              + cw_ref[2:3, :] * xs_ref[pl.ds(pl.multiple_of(r0 + 2 * batch, SUBLANES), chunk), :]
              + cw_ref[1:2, :] * xs_ref[pl.ds(pl.multiple_of(r0 + batch, SUBLANES), chunk), :]
              + cw_ref[0:1, :] * xs_ref[pl.ds(r0, chunk), :])
        if reset_first:
            grow = lax.broadcasted_iota(jnp.int32, (chunk, LANES), 0) + (r0 + pid * rows)
            first = grow < batch
        for n in range(D // LANES):
            sl = slice(n * LANES, (n + 1) * LANES)
            xn = xc[:, sl]
            g2 = jnp.dot(xn.astype(bf16), wxa_ref[n], preferred_element_type=f32)
            gate_x = jax.nn.sigmoid(g2[:, :LANES] + bx_ref[:, sl])
            gate_a = jax.nn.sigmoid(g2[:, LANES:] + ba_ref[:, sl])
            log_a = LRU_C * gate_a * logsig[:, sl]
            a = jnp.exp(log_a)
            mult = jnp.sqrt(-jnp.tanh(log_a) * (a * a + 1.0))
            if reset_first:
                mult = jnp.where(first, 1.0, mult)
            a_ref[pl.ds(r0, chunk), sl] = a
            b_ref[pl.ds(r0, chunk), sl] = xn * gate_x * mult
        return 0

    lax.fori_loop(0, rows // chunk, gates, 0)

    def scan(t, h):
        r0 = pl.multiple_of(t * batch, batch)
        h = a_ref[pl.ds(r0, batch), :] * h + b_ref[pl.ds(r0, batch), :]
        b_ref[pl.ds(r0, batch), :] = h
        return h

    h = lax.fori_loop(0, nt, scan, h_ref[...], unroll=(8 if nt >= 8 and batch == SUBLANES else 1))
    h_ref[...] = h
    hl_ref[...] = h
    xs_ref[0:hist, :] = xs_ref[rows:rows + hist, :]

    def outp(c, _):
        r0 = pl.multiple_of(c * chunk, chunk)
        o_ref[pl.ds(r0, chunk), :] = (b_ref[pl.ds(r0, chunk), :] * jax.nn.gelu(y_ref[pl.ds(r0, chunk), :])
                                      * jax.nn.sigmoid(gl_ref[pl.ds(r0, chunk), :]))
        return 0

    lax.fori_loop(0, rows // chunk, outp, 0)


def _with_dst(kern, n_in):
    def wrapped(*refs):
        return kern(*refs[:n_in], *refs[n_in + 1:])
    return wrapped


def _shared_rows(dst, n_total):
    if dst is None:
        return [], [], jax.ShapeDtypeStruct((n_total, D), f32)
    return [pl.BlockSpec(memory_space=pl.ANY)], [dst], jax.ShapeDtypeStruct(dst.shape, dst.dtype)


def _lru(proj, row_off, n_rows, batch, rows_blk, reset_first, conv_init, h0, cw, cb, wxa, bx, ba, ap, dst=None):
    ob = row_off // rows_blk
    kern = functools.partial(_lru_kernel, batch=batch, reset_first=reset_first, chunk=128)
    n_in = 11
    dst_spec, dst_arg, out0 = _shared_rows(dst, proj.shape[0])
    return pl.pallas_call(
        kern if dst is None else _with_dst(kern, n_in),
        grid=(n_rows // rows_blk,),
        in_specs=[
            pl.BlockSpec((rows_blk, D), lambda i: (ob + i, COL_LRU_X)),
            pl.BlockSpec((rows_blk, D), lambda i: (ob + i, COL_LRU_Y)),
            pl.BlockSpec((rows_blk, D), lambda i: (ob + i, COL_GL)),
            _full((3 * batch, D)), _full((batch, D)), _full((4, D)), _full((1, D)),
            _full((D // LANES, LANES, 2 * LANES)), _full((1, D)), _full((1, D)), _full((1, D)),
        ] + dst_spec,
        out_specs=[pl.BlockSpec((rows_blk, D), lambda i: (ob + i, 0)), _full((batch, D))],
        out_shape=[out0, jax.ShapeDtypeStruct((batch, D), f32)],
        input_output_aliases={} if dst is None else {n_in: 0},
        scratch_shapes=[pltpu.VMEM((rows_blk + 3 * batch, D), f32), pltpu.VMEM((rows_blk, D), f32),
                        pltpu.VMEM((rows_blk, D), f32), pltpu.VMEM((batch, D), f32)],
        compiler_params=pltpu.CompilerParams(dimension_semantics=("arbitrary",), vmem_limit_bytes=48 * MIB),
        name="lru",
    )(proj, proj, proj, conv_init, h0, cw, cb, wxa, bx, ba, ap, *dst_arg)


TILE_ALPHA, TILE_BETA, TILE_W, TILE_KP, TILE_WR, TILE_V = range(6)
ROW_BR, ROW_KR, ROW_BONUS = range(3)
CONST_KK, CONST_KA, CONST_RK, CONST_LNW, CONST_LNB = range(5)
NAT_R, NAT_W, NAT_K, NAT_V, NAT_A = range(5)


def _wkv_prep(tiles_ref, rows_ref, t, g, r, w, k, v, a, const):
    kk = k * const(CONST_KK)
    nrm = jnp.sqrt(jnp.sum(kk * kk, axis=0, keepdims=True))
    kk = kk * (1.0 / jnp.maximum(nrm, 1e-12))
    beta = kk * a
    kp = k * (1.0 + (a - 1.0) * const(CONST_KA))
    tiles_ref[t, g, TILE_ALPHA] = -kk
    tiles_ref[t, g, TILE_BETA] = beta
    tiles_ref[t, g, TILE_W] = w
    tiles_ref[t, g, TILE_KP] = kp
    tiles_ref[t, g, TILE_WR] = w * r
    tiles_ref[t, g, TILE_V] = v
    rows_ref[t, g, ROW_BR:ROW_BR + 1, :] = jnp.sum(beta * r, axis=0, keepdims=True)
    rows_ref[t, g, ROW_KR:ROW_KR + 1, :] = jnp.sum(kp * r, axis=0, keepdims=True)
    rows_ref[t, g, ROW_BONUS:ROW_BONUS + 1, :] = jnp.sum(r * kp * const(CONST_RK), axis=0, keepdims=True)


def _rwkv_pre_kernel(r_ref, k_ref, v_ref, l_ref, sinit_ref, mu_ref, w0_ref, a0_ref, w2_ref, a2_ref, g2_ref, c_ref,
                     tiles_ref, rows_ref, go_ref, ps_ref, nat_ref, *, batch, chunk, chains_are_heads):
    rows = r_ref.shape[0]

    @pl.when(pl.program_id(0) == 0)
    def _():
        ps_ref[0:batch, :] = sinit_ref[...]

    ps_ref[batch:batch + rows, 0:D] = r_ref[...]
    ps_ref[batch:batch + rows, D:2 * D] = k_ref[...]
    ps_ref[batch:batch + rows, 2 * D:3 * D] = v_ref[...]
    ps_ref[batch:batch + rows, 3 * D:3 * D + LORA_PAD] = l_ref[...]

    def body(c, _):
        r0 = pl.multiple_of(c * chunk, chunk)

        def mixed(lo, hi):
            cur = ps_ref[pl.ds(pl.multiple_of(r0 + batch, SUBLANES), chunk), lo:hi]
            prev = ps_ref[pl.ds(r0, chunk), lo:hi]
            return cur + (prev - cur) * mu_ref[:, lo:hi]

        nat_ref[NAT_R, pl.ds(r0, chunk), :] = mixed(0, D)
        nat_ref[NAT_K, pl.ds(r0, chunk), :] = mixed(D, 2 * D)
        nat_ref[NAT_V, pl.ds(r0, chunk), :] = mixed(2 * D, 3 * D)
        lm = mixed(3 * D, 3 * D + LORA_PAD)
        xwa = lm[:, 0:LANES]
        xg = lm[:, LANES:3 * LANES]
        lw = jnp.dot(jnp.tanh(xwa).astype(bf16), w2_ref[...], preferred_element_type=f32)
        w_log = -jax.nn.softplus(-(w0_ref[...] + lw)) - 0.5
        nat_ref[NAT_W, pl.ds(r0, chunk), :] = jnp.exp(-jnp.exp(w_log))
        la = jnp.dot(xwa.astype(bf16), a2_ref[...], preferred_element_type=f32)
        nat_ref[NAT_A, pl.ds(r0, chunk), :] = jax.nn.sigmoid(a0_ref[...] + la)
        go_ref[pl.ds(r0, chunk), :] = jnp.dot(jax.nn.sigmoid(xg).astype(bf16), g2_ref[...],
                                              preferred_element_type=f32)
        return 0

    lax.fori_loop(0, rows // chunk, body, 0)
    ps_ref[0:batch, :] = ps_ref[rows:rows + batch, :]

    if chains_are_heads:
        def tstep(t, _):
            r0 = pl.multiple_of(t * batch, batch)
            for hp in range(D // LANES):
                trs = [nat_ref[q, pl.ds(r0, batch), hp * LANES:(hp + 1) * LANES].T for q in range(5)]
                for h2 in range(2):
                    h = 2 * hp + h2
                    _wkv_prep(tiles_ref, rows_ref, t, h, *[tr[h2 * HD:(h2 + 1) * HD] for tr in trs],
                              lambda i, h=h: c_ref[h, i])
            return 0

        lax.fori_loop(0, rows // batch, tstep, 0)
    else:
        half = LANES // 2
        lane = lax.broadcasted_iota(jnp.int32, (HD, LANES), 1)

        def load_pair(q, tp):
            zz = jnp.concatenate(
                [nat_ref[q, pl.ds(pl.multiple_of(tp * 2 * SUBLANES + tt * SUBLANES, SUBLANES), SUBLANES),
                         hp * LANES:(hp + 1) * LANES] for tt in range(2) for hp in range(D // LANES)], axis=0)
            tr = zz.T
            top, bot = tr[0:HD], tr[HD:2 * HD]
            d0 = jnp.where(lane < half, top, pltpu.roll(bot, half, axis=1))
            d1 = jnp.where(lane < half, pltpu.roll(top, half, axis=1), bot)
            return d0, d1

        def pair(tp, _):
            pairs = [load_pair(q, tp) for q in range(5)]
            for tt in range(2):
                _wkv_prep(tiles_ref, rows_ref, 2 * tp + tt, 0, *[p[tt] for p in pairs], lambda i: c_ref[0, i])
            return 0

        lax.fori_loop(0, rows // (2 * SUBLANES), pair, 0, unroll=4)


def _rwkv_pre(proj, row_off, n_rows, batch, rows_blk, groups, shift_init, consts, mu, w0, a0, w2p, a2p, g2p,
              dst=None):
    ob = row_off // rows_blk
    width = 3 * D + LORA_PAD
    steps_blk = rows_blk // batch
    kern = functools.partial(_rwkv_pre_kernel, batch=batch, chunk=128, chains_are_heads=groups > 1)
    n_in = 12
    dst_spec, dst_arg, out_g = _shared_rows(dst, proj.shape[0])
    return pl.pallas_call(
        kern if dst is None else _with_dst(kern, n_in),
        grid=(n_rows // rows_blk,),
        in_specs=[
            pl.BlockSpec((rows_blk, D), lambda i: (ob + i, COL_R)),
            pl.BlockSpec((rows_blk, D), lambda i: (ob + i, COL_K)),
            pl.BlockSpec((rows_blk, D), lambda i: (ob + i, COL_V)),
            pl.BlockSpec((rows_blk, LORA_PAD), lambda i: (ob + i, COL_LORA)),
            _full((batch, width)), _full((1, width)), _full((1, D)), _full((1, D)),
            _full((LANES, D)), _full((LANES, D)), _full((2 * LANES, D)), _full((groups, 5, HD, LANES)),
        ] + dst_spec,
        out_specs=[pl.BlockSpec((steps_blk, groups, 6, HD, LANES), lambda i: (i, 0, 0, 0, 0)),
                   pl.BlockSpec((steps_blk, groups, SUBLANES, LANES), lambda i: (i, 0, 0, 0)),
                   pl.BlockSpec((rows_blk, D), lambda i: (ob + i, 0))],
        out_shape=[jax.ShapeDtypeStruct((n_rows // batch, groups, 6, HD, LANES), f32),
                   jax.ShapeDtypeStruct((n_rows // batch, groups, SUBLANES, LANES), f32),
                   out_g],
        input_output_aliases={} if dst is None else {n_in: 2},
        scratch_shapes=[pltpu.VMEM((rows_blk + batch, width), f32), pltpu.VMEM((5, rows_blk, D), f32)],
        compiler_params=pltpu.CompilerParams(dimension_semantics=("arbitrary",), vmem_limit_bytes=56 * MIB),
        name="rwkv_pre",
    )(proj, proj, proj, proj, shift_init, mu, w0, a0, w2p, a2p, g2p, consts, *dst_arg)


STATE_ROWS = HD * HD
KEY_UNROLL = 32


def _wkv_step(s_ref, tiles_ref, rows_ref, y_ref, t):
    def key_row(tile, j):
        return tiles_ref[t, 0, tile, pl.ds(j, 1), :]

    def reduce_keys(j, acc):
        sa, y0 = acc
        s = s_ref[pl.ds(pl.multiple_of(j * HD, HD), HD), :]
        return sa + s * key_row(TILE_ALPHA, j), y0 + s * key_row(TILE_WR, j)

    zero = jnp.zeros((HD, LANES), f32)
    sa, y0 = lax.fori_loop(0, HD, reduce_keys, (zero, zero), unroll=KEY_UNROLL)
    v = tiles_ref[t, 0, TILE_V]

    def update_keys(j, _):
        rows = pl.ds(pl.multiple_of(j * HD, HD), HD)
        s_ref[rows, :] = (s_ref[rows, :] * key_row(TILE_W, j) + sa * key_row(TILE_BETA, j)
                          + v * key_row(TILE_KP, j))
        return 0

    lax.fori_loop(0, HD, update_keys, 0, unroll=KEY_UNROLL)
    y_ref[t, 0] = y0 + sa * rows_ref[t, 0, ROW_BR:ROW_BR + 1, :] + v * rows_ref[t, 0, ROW_KR:ROW_KR + 1, :]


def _wkv_scan_kernel(tiles_ref, rows_ref, st_ref, y_ref, sfin_ref, s_ref, *, steps):
    tb = pl.program_id(1)

    def value_rows(i):
        return pl.ds(i, HD, stride=HD)

    @pl.when(tb == 0)
    def _():
        def init(c, _):
            tr = st_ref[:, pl.ds(pl.multiple_of(c * LANES, LANES), LANES)].T
            for i2 in range(2):
                s_ref[value_rows(2 * c + i2), :] = tr[i2 * HD:(i2 + 1) * HD]
            return 0

        lax.fori_loop(0, STATE_ROWS // LANES, init, 0, unroll=4)

    def step(t, _):
        _wkv_step(s_ref, tiles_ref, rows_ref, y_ref, t)
        return 0

    lax.fori_loop(0, steps, step, 0)

    @pl.when(tb == pl.num_programs(1) - 1)
    def _():
        def fin(c, _):
            pair = jnp.concatenate([s_ref[value_rows(2 * c + i2), :] for i2 in range(2)], axis=0)
            sfin_ref[:, pl.ds(pl.multiple_of(c * LANES, LANES), LANES)] = pair.T
            return 0

        lax.fori_loop(0, STATE_ROWS // LANES, fin, 0, unroll=4)


def _wkv_scan(tiles, rows, state, steps_blk):
    n_steps, groups = tiles.shape[0], tiles.shape[1]
    kern = functools.partial(_wkv_scan_kernel, steps=steps_blk)
    sblk = pl.BlockSpec((LANES, STATE_ROWS), lambda g, i: (0, g))
    return pl.pallas_call(
        kern,
        grid=(groups, n_steps // steps_blk),
        in_specs=[pl.BlockSpec((steps_blk, 1, 6, HD, LANES), lambda g, i: (i, g, 0, 0, 0)),
                  pl.BlockSpec((steps_blk, 1, SUBLANES, LANES), lambda g, i: (i, g, 0, 0)),
                  sblk],
        out_specs=[pl.BlockSpec((steps_blk, 1, HD, LANES), lambda g, i: (i, g, 0, 0)), sblk],
        out_shape=[jax.ShapeDtypeStruct((n_steps, groups, HD, LANES), f32),
                   jax.ShapeDtypeStruct((LANES, groups * STATE_ROWS), f32)],
        scratch_shapes=[pltpu.VMEM((STATE_ROWS, LANES), f32)],
        compiler_params=pltpu.CompilerParams(dimension_semantics=("arbitrary", "arbitrary"),
                                             vmem_limit_bytes=40 * MIB),
        name="wkv_scan",
    )(tiles, rows, state)


def _wkv_norm(y, v, bonus, lnw, lnb):
    mean = jnp.sum(y, axis=0, keepdims=True) * (1.0 / HD)
    d = y - mean
    var = jnp.sum(d * d, axis=0, keepdims=True) * (1.0 / HD)
    return d * lax.rsqrt(var + GN_EPS) * lnw + lnb + bonus * v


def _wkv_post_kernel(y_ref, v_ref, rows_ref, c_ref, o_ref, *, batch, chains_are_heads):
    steps = y_ref.shape[0]

    def normed(t, g):
        return _wkv_norm(y_ref[t, g], v_ref[t, g, 0], rows_ref[t, g, ROW_BONUS:ROW_BONUS + 1, :],
                         c_ref[g, CONST_LNW], c_ref[g, CONST_LNB])

    if chains_are_heads:
        def tstep(t, _):
            r0 = pl.multiple_of(t * batch, batch)
            for hp in range(D // LANES):
                two = jnp.concatenate([normed(t, 2 * hp), normed(t, 2 * hp + 1)], axis=0)
                o_ref[pl.ds(r0, batch), hp * LANES:(hp + 1) * LANES] = two.T
            return 0

        lax.fori_loop(0, steps, tstep, 0)
    else:
        half = LANES // 2
        lane = lax.broadcasted_iota(jnp.int32, (HD, LANES), 1)

        def pair(tp, _):
            o0 = normed(2 * tp, 0)
            o1 = normed(2 * tp + 1, 0)
            top = jnp.where(lane < half, o0, pltpu.roll(o1, half, axis=1))
            bot = jnp.where(lane < half, pltpu.roll(o0, half, axis=1), o1)
            zz = jnp.concatenate([top, bot], axis=0).T
            for tt in range(2):
                row = pl.multiple_of(tp * 2 * SUBLANES + tt * SUBLANES, SUBLANES)
                for hp in range(D // LANES):
                    src = tt * HD + hp * SUBLANES
                    o_ref[pl.ds(row, SUBLANES), hp * LANES:(hp + 1) * LANES] = zz[src:src + SUBLANES, :]
            return 0

        lax.fori_loop(0, steps // 2, pair, 0, unroll=2)


def _wkv_post(y, tiles, rows, consts, batch, steps_blk, row_off, n_total, dst=None):
    n_steps, groups = y.shape[0], y.shape[1]
    ob = row_off // (steps_blk * batch)
    kern = functools.partial(_wkv_post_kernel, batch=batch, chains_are_heads=groups > 1)
    n_in = 4
    dst_spec, dst_arg, out0 = _shared_rows(dst, n_total)
    return pl.pallas_call(
        kern if dst is None else _with_dst(kern, n_in),
        grid=(n_steps // steps_blk,),
        in_specs=[pl.BlockSpec((steps_blk, groups, HD, LANES), lambda i: (i, 0, 0, 0)),
                  pl.BlockSpec((steps_blk, groups, 1, HD, LANES), lambda i: (i, 0, TILE_V, 0, 0)),
                  pl.BlockSpec((steps_blk, groups, SUBLANES, LANES), lambda i: (i, 0, 0, 0)),
                  _full((groups, 5, HD, LANES))] + dst_spec,
        out_specs=pl.BlockSpec((steps_blk * batch, D), lambda i: (ob + i, 0)),
        out_shape=out0,
        input_output_aliases={} if dst is None else {n_in: 0},
        compiler_params=pltpu.CompilerParams(dimension_semantics=("arbitrary",), vmem_limit_bytes=40 * MIB),
        name="wkv_post",
    )(y, tiles, rows, consts, *dst_arg)


def _outproj_kernel(ml_ref, op_ref, g_ref, gr_ref, x_ref, wo_ref, n2_ref, rh_ref, rl_ref, rb_ref,
                    x1_ref, t_ref, selt_ref, seln_ref):
    tm = x_ref.shape[0]
    merged = ml_ref[...] + jax.nn.sigmoid(gr_ref[...]) * (op_ref[...] * g_ref[...])
    x1 = x_ref[...] + jnp.dot(merged.astype(bf16), wo_ref[...], preferred_element_type=f32)
    x1_ref[...] = x1
    t = x1 * lax.rsqrt(jnp.mean(x1 * x1, axis=-1, keepdims=True) + NORM_EPS) * n2_ref[...]
    th = t.astype(bf16)
    t_ref[...] = th
    tl = (t - th.astype(f32)).astype(bf16)
    nt_dims = (((1,), (1,)), ((), ()))
    lg = (lax.dot_general(rh_ref[...], th, nt_dims, preferred_element_type=f32)
          + lax.dot_general(rh_ref[...], tl, nt_dims, preferred_element_type=f32)
          + lax.dot_general(rl_ref[...], th, nt_dims, preferred_element_type=f32)) + rb_ref[...]
    row = lax.broadcasted_iota(jnp.int32, (EPG, tm), 0).astype(f32)
    neg = jnp.float32(-jnp.inf)
    glog = jnp.where(row < N_GROUPS, lg[0:EPG], neg)
    ge = jnp.exp(glog - jnp.max(glog, axis=0, keepdims=True))
    pg = ge / jnp.sum(ge, axis=0, keepdims=True)
    p_top = jnp.max(pg, axis=0, keepdims=True)
    g_idx = jnp.min(jnp.where(pg == p_top, row, EPG), axis=0, keepdims=True)
    le = jnp.zeros((EPG, tm), f32)
    for g in range(N_GROUPS):
        le = jnp.where(g_idx == g, lg[EPG * (g + 1):EPG * (g + 2)], le)
    qe = jnp.exp(le - jnp.max(le, axis=0, keepdims=True))
    q = qe / jnp.sum(qe, axis=0, keepdims=True)
    q1 = jnp.max(q, axis=0, keepdims=True)
    i1 = jnp.min(jnp.where(q == q1, row, EPG), axis=0, keepdims=True)
    qm = jnp.where(row == i1, -1.0, q)
    q2 = jnp.max(qm, axis=0, keepdims=True)
    i2 = jnp.min(jnp.where(qm == q2, row, EPG), axis=0, keepdims=True)
    qs = q1 + q2
    sel = jnp.concatenate([g_idx * EPG + i1, g_idx * EPG + i2, q1 / qs * p_top, q2 / qs * p_top,
                           jnp.zeros((SUBLANES - 4, tm), f32)], axis=0)
    selt_ref[...] = sel
    sel_pad = jnp.concatenate([sel, jnp.zeros((LANES - SUBLANES, tm), f32)], axis=0)
    for c in range(tm // LANES):
        seln_ref[c * LANES:(c + 1) * LANES, :] = sel_pad[:, c * LANES:(c + 1) * LANES].T


def _outproj(ml, op, g, proj, x, wo, n2, rh, rl, rb, tm=512):
    n = x.shape[0]
    blk = pl.BlockSpec((tm, D), lambda i: (i, 0))
    return pl.pallas_call(
        _outproj_kernel,
        grid=(n // tm,),
        in_specs=[blk, blk, blk, pl.BlockSpec((tm, D), lambda i: (i, COL_GR)), blk,
                  _full((D, D)), _full((1, D)), _full((ROUTER_ROWS, D)), _full((ROUTER_ROWS, D)),
                  _full((ROUTER_ROWS, 1))],
        out_specs=[blk, blk, pl.BlockSpec((SUBLANES, tm), lambda i: (0, i)),
                   pl.BlockSpec((tm, LANES), lambda i: (i, 0))],
        out_shape=[jax.ShapeDtypeStruct((n, D), f32), jax.ShapeDtypeStruct((n, D), bf16),
                   jax.ShapeDtypeStruct((SUBLANES, n), f32), jax.ShapeDtypeStruct((n, LANES), f32)],
        compiler_params=pltpu.CompilerParams(dimension_semantics=("arbitrary",), vmem_limit_bytes=48 * MIB),
        name="outproj_router",
    )(ml, op, g, proj, x, wo, n2, rh, rl, rb)


MOE_CHUNK = 32
MOE_ROWS = 128
MOE_GATHER = 512
MOE_EXPERTS_PER_STEP = 4
MOE_OVERRUN = MOE_ROWS
MOE_CAP = -(-(2 * TOKEN_TILE + N_EXPERTS * (MOE_CHUNK - 1) + MOE_OVERRUN) // MOE_GATHER) * MOE_GATHER
SEL_E1, SEL_E2, SEL_C1, SEL_C2 = range(4)


def _route_meta_kernel(seln_ref, o_ref):
    tm = seln_ref.shape[0]
    lane = lax.broadcasted_iota(jnp.int32, (tm, LANES), 1).astype(f32)
    sel = seln_ref[...]
    hit = (lane == sel[:, SEL_E1:SEL_E1 + 1]) | (lane == sel[:, SEL_E2:SEL_E2 + 1])
    cnt = jnp.sum(jnp.where(hit, 1.0, 0.0), axis=0, keepdims=True)
    nchunk = jnp.floor((cnt + (MOE_CHUNK - 1)) * (1.0 / MOE_CHUNK))
    upper = (lax.broadcasted_iota(jnp.int32, (LANES, LANES), 0)
             < lax.broadcasted_iota(jnp.int32, (LANES, LANES), 1))
    offs = jnp.dot(jnp.broadcast_to(nchunk, (SUBLANES, LANES)).astype(bf16), jnp.where(upper, 1.0, 0.0).astype(bf16),
                   preferred_element_type=f32)
    row = lax.broadcasted_iota(jnp.int32, (SUBLANES, LANES), 0)
    o_ref[0] = jnp.where(row == 0, nchunk, jnp.where(row == 1, offs, 0.0)).astype(jnp.int32)


def _route_meta(seln, tm):
    tiles = seln.shape[0] // tm
    return pl.pallas_call(
        _route_meta_kernel,
        grid=(tiles,),
        in_specs=[pl.BlockSpec((tm, LANES), lambda i: (i, 0))],
        out_specs=pl.BlockSpec((1, SUBLANES, LANES), lambda i: (i, 0, 0)),
        out_shape=jax.ShapeDtypeStruct((tiles, SUBLANES, LANES), jnp.int32),
        compiler_params=pltpu.CompilerParams(dimension_semantics=("arbitrary",)),
        name="route_meta",
    )(seln)


def _moe_experts_kernel(nch_ref, off_ref, t_ref, selt_ref, tri_ref, wgu_ref, wd_ref, ys_ref, pos_ref,
                        xs_ref, cw_ref):
    i = pl.program_id(0)
    e = pl.program_id(1)
    tm = t_ref.shape[0]
    used_rows = (off_ref[i * N_EXPERTS + N_EXPERTS - 1] + nch_ref[i * N_EXPERTS + N_EXPERTS - 1]) * MOE_CHUNK

    @pl.when(e == 0)
    def _():
        ys_ref[...] = jnp.zeros_like(ys_ref)
        e1 = selt_ref[SEL_E1:SEL_E1 + 1, :]
        e2 = selt_ref[SEL_E2:SEL_E2 + 1, :]
        erow = lax.broadcasted_iota(jnp.int32, (N_EXPERTS, tm), 0).astype(f32)
        oh1 = jnp.where(erow == e1, 1.0, 0.0)
        oh2 = jnp.where(erow == e2, 1.0, 0.0)
        before1 = jnp.dot(oh1.astype(bf16), tri_ref[...], preferred_element_type=f32)
        before2 = jnp.dot(oh2.astype(bf16), tri_ref[...], preferred_element_type=f32)
        cnt1 = jnp.sum(oh1, axis=1, keepdims=True)
        cnt2 = jnp.sum(oh2, axis=1, keepdims=True)
        nchunk = jnp.floor((cnt1 + cnt2 + (MOE_CHUNK - 1)) * (1.0 / MOE_CHUNK))
        lower = (lax.broadcasted_iota(jnp.int32, (N_EXPERTS, LANES), 1)
                 < lax.broadcasted_iota(jnp.int32, (N_EXPERTS, LANES), 0))
        nchunk_rows = jnp.concatenate([jnp.broadcast_to(nchunk, (N_EXPERTS, LANES)),
                                       jnp.zeros((LANES - N_EXPERTS, LANES), f32)], axis=0)
        start = jnp.dot(jnp.where(lower, 1.0, 0.0).astype(bf16), nchunk_rows.astype(bf16),
                        preferred_element_type=f32)[:, 0:1] * MOE_CHUNK
        pos1 = jnp.sum(oh1 * (start + before1), axis=0, keepdims=True)
        pos2 = jnp.sum(oh2 * (start + cnt1 + before2), axis=0, keepdims=True)
        pos_ref[...] = jnp.concatenate([pos1, pos2, jnp.zeros((SUBLANES - 2, tm), f32)], axis=0)
        c1 = selt_ref[SEL_C1:SEL_C1 + 1, :]
        c2 = selt_ref[SEL_C2:SEL_C2 + 1, :]
        for k in range(MOE_CAP // MOE_GATHER):
            @pl.when(k * MOE_GATHER < used_rows + MOE_OVERRUN)
            def _(k=k):
                ridx = (lax.broadcasted_iota(jnp.int32, (MOE_GATHER, tm), 0) + k * MOE_GATHER).astype(f32)
                p1 = ridx == pos1
                p2 = ridx == pos2
                onehot = jnp.where(p1 | p2, 1.0, 0.0).astype(bf16)
                xs_ref[k * MOE_GATHER:(k + 1) * MOE_GATHER, :] = jnp.dot(
                    onehot, t_ref[...], preferred_element_type=f32).astype(bf16)
                w = jnp.sum(jnp.where(p1, c1, 0.0) + jnp.where(p2, c2, 0.0), axis=1, keepdims=True)
                cw_ref[k * MOE_GATHER:(k + 1) * MOE_GATHER, :] = jnp.broadcast_to(w, (MOE_GATHER, LANES))

    def expert_rows(u, start):
        rows = pl.ds(pl.multiple_of(start, MOE_CHUNK), MOE_ROWS)
        gu = jnp.dot(xs_ref[rows, :], wgu_ref[u], preferred_element_type=f32)
        w = cw_ref[rows, :]
        h = jax.nn.silu(gu[:, :D_EXPERT]) * gu[:, D_EXPERT:] * jnp.concatenate([w, w], axis=1)
        ys_ref[0, rows, :] = jnp.dot(h.astype(bf16), wd_ref[u], preferred_element_type=f32).astype(bf16)

    per_block = MOE_ROWS // MOE_CHUNK
    for u in range(wgu_ref.shape[0]):
        expert = i * N_EXPERTS + e * wgu_ref.shape[0] + u
        base = off_ref[expert] * MOE_CHUNK
        expert_rows(u, base)

        def more(c, _, u=u, base=base):
            expert_rows(u, base + (c + 1) * MOE_ROWS)
            return 0

        lax.fori_loop(0, jnp.maximum(nch_ref[expert] - 1, 0) // per_block, more, 0)


def _moe_experts(nch, off, t, selt, tri, wgu, wd, tm):
    n = t.shape[0]
    tiles = n // tm
    grid_spec = pltpu.PrefetchScalarGridSpec(
        num_scalar_prefetch=2,
        grid=(tiles, N_EXPERTS // MOE_EXPERTS_PER_STEP),
        in_specs=[
            pl.BlockSpec((tm, D), lambda i, e, nch, off: (i, 0)),
            pl.BlockSpec((SUBLANES, tm), lambda i, e, nch, off: (0, i)),
            pl.BlockSpec((tm, tm), lambda i, e, nch, off: (0, 0)),
            pl.BlockSpec((MOE_EXPERTS_PER_STEP, D, 2 * D_EXPERT), lambda i, e, nch, off: (e, 0, 0)),
            pl.BlockSpec((MOE_EXPERTS_PER_STEP, D_EXPERT, D), lambda i, e, nch, off: (e, 0, 0)),
        ],
        out_specs=[pl.BlockSpec((1, MOE_CAP, D), lambda i, e, nch, off: (i, 0, 0)),
                   pl.BlockSpec((SUBLANES, tm), lambda i, e, nch, off: (0, i))],
        scratch_shapes=[pltpu.VMEM((MOE_CAP, D), bf16), pltpu.VMEM((MOE_CAP, LANES), f32)],
    )
    return pl.pallas_call(
        _moe_experts_kernel,
        grid_spec=grid_spec,
        out_shape=[jax.ShapeDtypeStruct((tiles, MOE_CAP, D), bf16), jax.ShapeDtypeStruct((SUBLANES, n), f32)],
        compiler_params=pltpu.CompilerParams(dimension_semantics=("arbitrary", "arbitrary"),
                                             vmem_limit_bytes=56 * MIB),
        name="moe_experts",
    )(nch, off, t, selt, tri, wgu, wd)


def _moe_combine_kernel(nch_ref, off_ref, ysort_ref, pos_ref, x1_ref, fn_ref, yp_ref, ys_ref, acc_ref, yf_ref,
                        *, prompt_tiles, parts):
    i = pl.program_id(0)
    part = pl.program_id(1)
    rows = x1_ref.shape[0]
    bp, tsteps = yp_ref.shape[0], yp_ref.shape[1]
    bs = ys_ref.shape[0]
    ssteps = ys_ref.shape[1] // parts
    used_rows = (off_ref[i * N_EXPERTS + N_EXPERTS - 1] + nch_ref[i * N_EXPERTS + N_EXPERTS - 1]) * MOE_CHUNK

    pos_t = jnp.concatenate([pos_ref[...], jnp.zeros((LANES - SUBLANES, rows), f32)], axis=0)
    pos_n = jnp.concatenate([pos_t[:, c * LANES:(c + 1) * LANES].T for c in range(rows // LANES)], axis=0)
    pos1 = pos_n[:, 0:1]
    pos2 = pos_n[:, 1:2]
    acc_ref[...] = x1_ref[...]
    for k in range(MOE_CAP // MOE_GATHER):
        @pl.when(k * MOE_GATHER < used_rows)
        def _(k=k):
            cidx = (lax.broadcasted_iota(jnp.int32, (rows, MOE_GATHER), 1) + k * MOE_GATHER).astype(f32)
            onehot = jnp.where((cidx == pos1) | (cidx == pos2), 1.0, 0.0).astype(bf16)
            acc_ref[...] += jnp.dot(onehot, ysort_ref[0, k * MOE_GATHER:(k + 1) * MOE_GATHER, :],
                                    preferred_element_type=f32)

    xo = acc_ref[...]
    y = xo * lax.rsqrt(jnp.mean(xo * xo, axis=-1, keepdims=True) + NORM_EPS) * fn_ref[...]
    for cb in range(D // LANES):
        yf_ref[cb] = y[:, cb * LANES:(cb + 1) * LANES]

    @pl.when(i < prompt_tiles)
    def _():
        for b in range(bp):
            yp_ref[b] = jnp.concatenate([yf_ref[cb, pl.ds(b, tsteps, stride=bp), :] for cb in range(D // LANES)],
                                        axis=1)

    for p in range(parts):
        @pl.when((i >= prompt_tiles) & (part == p))
        def _(p=p):
            for t in range(ssteps):
                ys_ref[:, p * ssteps + t, :] = jnp.concatenate(
                    [yf_ref[cb, t * bs:(t + 1) * bs, :] for cb in range(D // LANES)], axis=1)


def _moe_combine(nch, off, ysort, pos, x1, fn, prompt_shape, sample_shape, tm, parts=2):
    n = x1.shape[0]
    bp, tp, _ = prompt_shape
    bs, ts, _ = sample_shape
    rows = tm // parts
    assert rows % bp == 0 and tp % (rows // bp) == 0 and bs * ts == tm and ts % parts == 0
    prompt_tiles = bp * tp // tm
    last_prompt_blk = prompt_tiles * parts - 1
    kern = functools.partial(_moe_combine_kernel, prompt_tiles=prompt_tiles, parts=parts)
    grid_spec = pltpu.PrefetchScalarGridSpec(
        num_scalar_prefetch=2,
        grid=(n // tm, parts),
        in_specs=[
            pl.BlockSpec((1, MOE_CAP, D), lambda i, p, nch, off: (i, 0, 0)),
            pl.BlockSpec((SUBLANES, rows), lambda i, p, nch, off: (0, i * parts + p)),
            pl.BlockSpec((rows, D), lambda i, p, nch, off: (i * parts + p, 0)),
            pl.BlockSpec((1, D), lambda i, p, nch, off: (0, 0)),
        ],
        out_specs=[pl.BlockSpec((bp, rows // bp, D),
                                lambda i, p, nch, off: (0, jnp.minimum(i * parts + p, last_prompt_blk), 0)),
                   pl.BlockSpec((bs, ts, D), lambda i, p, nch, off: (0, 0, 0))],
        scratch_shapes=[pltpu.VMEM((rows, D), f32), pltpu.VMEM((D // LANES, rows, LANES), f32)],
    )
    return pl.pallas_call(
        kern,
        grid_spec=grid_spec,
        out_shape=[jax.ShapeDtypeStruct(prompt_shape, f32), jax.ShapeDtypeStruct(sample_shape, f32)],
        compiler_params=pltpu.CompilerParams(dimension_semantics=("arbitrary", "arbitrary"),
                                             vmem_limit_bytes=56 * MIB),
        name="moe_combine",
    )(nch, off, ysort, pos, x1, fn)


def _chain_tiles_prompt(vec, batch):
    t = vec.reshape(HEADS // 2, 2, HD)
    t = jnp.transpose(t, (2, 1, 0))
    return jnp.broadcast_to(t[..., None], (HD, 2, HEADS // 2, batch)).reshape(HD, LANES)


def _chain_tiles_sample(vec):
    return jnp.broadcast_to(vec.reshape(HEADS, HD)[..., None], (HEADS, HD, LANES))


def kernel(x_prompt, x_sample, state_conv, state_lru, state_shift, state_wkv, norm1, w_in, conv_w, conv_b, lru_wx, lru_bx, lru_wa, lru_ba, lru_a_param, rwkv_mu, rwkv_w0, rwkv_w2, rwkv_a0, rwkv_a2, rwkv_g2, rwkv_k_k, rwkv_k_a, rwkv_r_k, rwkv_ln_w, rwkv_ln_b, w_out, norm2, router_group, router_group_b, router_expert, router_expert_b, exp_gate, exp_up, exp_down, final_norm):
    bp, tp, _ = x_prompt.shape
    bs, ts, _ = x_sample.shape
    assert norm1.shape[0] == 1 and bp * HEADS == LANES and bs == LANES and tp % 64 == 0 and ts % 2 == 0
    n_p, n_s = bp * tp, bs * ts
    c_rw = 2 * D
    c_lora = c_rw + 3 * D
    c_gl = c_lora + LORA

    w = w_in[0]
    w_all = jnp.concatenate([w[:, :c_lora], w[:, c_gl:], w[:, c_lora:c_gl],
                             jnp.zeros((D, LORA_PAD - LORA), f32)], axis=1).astype(bf16)
    mu = rwkv_mu[0]
    mu_all = jnp.concatenate([mu[:3 * D], mu[3 * D:], jnp.zeros((LORA_PAD - LORA,), f32)])[None]
    w2p = jnp.concatenate([rwkv_w2[0], jnp.zeros((LANES - LORA_W, D), f32)]).astype(bf16)
    a2p = jnp.concatenate([jnp.zeros((LORA_W, D), f32), rwkv_a2[0]]).astype(bf16)
    g2p = jnp.concatenate([rwkv_g2[0], jnp.zeros((2 * LANES - LORA_G, D), f32)]).astype(bf16)
    wxa = jnp.concatenate([lru_wx[0], lru_wa[0]], axis=-1).astype(bf16)
    row = lambda v: v.reshape(1, -1)
    rk = rwkv_r_k[0].reshape(D)
    cvecs = (rwkv_k_k[0], rwkv_k_a[0], rk, rwkv_ln_w[0], rwkv_ln_b[0])
    consts_p = jnp.stack([_chain_tiles_prompt(v, bp) for v in cvecs])[None]
    consts_s = jnp.stack([_chain_tiles_sample(v) for v in cvecs], axis=1)
    rw = jnp.zeros((ROUTER_ROWS, D), f32)
    rw = rw.at[0:N_GROUPS].set(router_group[0].T).at[EPG:EPG + N_EXPERTS].set(router_expert[0].T)
    rh = rw.astype(bf16)
    rl = (rw - rh.astype(f32)).astype(bf16)
    rb = jnp.zeros((ROUTER_ROWS, 1), f32)
    rb = rb.at[0:N_GROUPS, 0].set(router_group_b[0]).at[EPG:EPG + N_EXPERTS, 0].set(router_expert_b[0])
    wgu = jnp.concatenate([exp_gate[0], exp_up[0]], axis=-1).astype(bf16)
    wd = exp_down[0].astype(bf16)
    wo = w_out[0].astype(bf16)

    n = n_p + n_s
    proj, x = _inproj(x_prompt, x_sample, row(norm1[0]), w_all, tm=TOKEN_TILE)

    lru_args = (conv_w[0], row(conv_b[0]), wxa, row(lru_bx[0]), row(lru_ba[0]), row(lru_a_param[0]))
    conv_init_s = jnp.transpose(state_conv[0], (1, 0, 2)).reshape(3 * bs, D)
    ml, hl_p = _lru(proj, 0, n_p, bp, 512, True, jnp.zeros((3 * bp, D), f32), jnp.zeros((bp, D), f32), *lru_args)
    ml, hl_s = _lru(proj, n_p, n_s, bs, 512, False, conv_init_s, state_lru[0], *lru_args, dst=ml)

    sh = state_shift[0]
    shift_init_s = jnp.concatenate([sh, jnp.zeros((bs, LORA_PAD - LORA), f32)], axis=1)
    pre_args = (mu_all, row(rwkv_w0[0]), row(rwkv_a0[0]), w2p, a2p, g2p)
    tiles_p, rows_p, g = _rwkv_pre(proj, 0, n_p, bp, 256, 1, jnp.zeros((bp, 3 * D + LORA_PAD), f32),
                                   consts_p, *pre_args)
    tiles_s, rows_s, g = _rwkv_pre(proj, n_p, n_s, bs, 256, HEADS, shift_init_s, consts_s, *pre_args, dst=g)

    y_p, sfin_p = _wkv_scan(tiles_p, rows_p, jnp.zeros((LANES, STATE_ROWS), f32), 16)
    y_s, sfin_s = _wkv_scan(tiles_s, rows_s, state_wkv[0].reshape(bs, HEADS * STATE_ROWS), ts)
    op = _wkv_post(y_p, tiles_p, rows_p, consts_p, bp, 64, 0, n)
    op = _wkv_post(y_s, tiles_s, rows_s, consts_s, bs, 2, n_p, n, dst=op)

    x1, t, selt, seln = _outproj(ml, op, g, proj, x, wo, row(norm2[0]), rh, rl, rb)
    meta = _route_meta(seln, TOKEN_TILE)
    nch = meta[:, 0, :N_EXPERTS].reshape(-1)
    off = meta[:, 1, :N_EXPERTS].reshape(-1)
    tri = jnp.triu(jnp.ones((TOKEN_TILE, TOKEN_TILE), bf16), k=1)
    ysort, pos = _moe_experts(nch, off, t, selt, tri, wgu, wd, TOKEN_TILE)
    y_prompt, y_sample = _moe_combine(nch, off, ysort, pos, x1, row(final_norm), x_prompt.shape, x_sample.shape,
                                      TOKEN_TILE)

    def last_rows(lo, hi, n_end, batch, steps):
        return proj[n_end - steps * batch:n_end, lo:hi].reshape(steps, batch, hi - lo)

    conv_p = jnp.transpose(last_rows(0, D, n_p, bp, 3), (1, 0, 2))[None]
    conv_s = jnp.transpose(last_rows(0, D, n_p + n_s, bs, 3), (1, 0, 2))[None]

    def shift_rows(n_end, batch):
        return jnp.concatenate([last_rows(2 * D, 5 * D, n_end, batch, 1)[0],
                                last_rows(7 * D, 7 * D + LORA, n_end, batch, 1)[0]], axis=1)[None]

    wkv_p = jnp.transpose(sfin_p.reshape(2, HEADS // 2, bp, HD, HD), (2, 1, 0, 3, 4)).reshape(1, bp, HEADS, HD, HD)
    wkv_s = sfin_s.reshape(1, bs, HEADS, HD, HD)
    return (y_prompt, y_sample, conv_p, hl_p[None], shift_rows(n_p, bp), wkv_p,
            conv_s, hl_s[None], shift_rows(n_p + n_s, bs), wkv_s)
```

```python
import functools

import jax
import jax.numpy as jnp
from jax import lax
from jax.experimental import pallas as pl
from jax.experimental.pallas import tpu as pltpu

f32 = jnp.float32
bf16 = jnp.bfloat16

D = 1024
HEADS = 16
HD = 64
LANES = 128
SUBLANES = 8
LORA_W = 64
LORA_A = 64
LORA_G = 160
LORA = LORA_W + LORA_A + LORA_G
LORA_PAD = 512
N_GROUPS = 4
EPG = 8
N_EXPERTS = N_GROUPS * EPG
D_EXPERT = 256
LRU_C = 8.0
GN_EPS = 64e-5
NORM_EPS = 1e-6
PROJ_COLS = 7 * D + LORA_PAD
COL_LRU_X, COL_LRU_Y, COL_R, COL_K, COL_V, COL_GL, COL_GR = range(7)
COL_LORA = 7 * D // LORA_PAD
ROUTER_ROWS = 48
MIB = 1024 * 1024
TOKEN_TILE = 1024


def _full(shape, grid_rank=1):
    zeros = tuple(0 for _ in shape)
    if grid_rank == 1:
        return pl.BlockSpec(shape, lambda i: zeros)
    return pl.BlockSpec(shape, lambda i, j: zeros)


def _inproj_kernel(xp_ref, xs_ref, g_ref, w_ref, o_ref, xo_ref, xf_ref, xn_ref, *, prompt_tiles):
    i = pl.program_id(0)
    first_col = pl.program_id(1) == 0
    bp, tsteps = xp_ref.shape[0], xp_ref.shape[1]
    bs, ssteps = xs_ref.shape[0], xs_ref.shape[1]

    @pl.when(first_col & (i < prompt_tiles))
    def _():
        for b in range(bp):
            v = xp_ref[b]
            for cb in range(D // LANES):
                xf_ref[cb, pl.ds(b, tsteps, stride=bp), :] = v[:, cb * LANES:(cb + 1) * LANES]

    @pl.when(first_col & (i >= prompt_tiles))
    def _():
        for t in range(ssteps):
            v = xs_ref[:, t, :]
            for cb in range(D // LANES):
                xf_ref[cb, t * bs:(t + 1) * bs, :] = v[:, cb * LANES:(cb + 1) * LANES]

    @pl.when(first_col)
    def _():
        x = jnp.concatenate([xf_ref[cb] for cb in range(D // LANES)], axis=1)
        xo_ref[...] = x
        ms = jnp.mean(x * x, axis=-1, keepdims=True)
        xn_ref[...] = (x * lax.rsqrt(ms + NORM_EPS) * g_ref[...]).astype(bf16)

    o_ref[...] = jnp.dot(xn_ref[...], w_ref[...], preferred_element_type=f32)


def _inproj(x_prompt, x_sample, gain, w, tm, tn=1280):
    bp, tp, _ = x_prompt.shape
    bs, ts, _ = x_sample.shape
    assert tm % bp == 0 and tp % (tm // bp) == 0 and bs * ts == tm
    prompt_tiles = bp * tp // tm
    n = bp * tp + bs * ts
    kern = functools.partial(_inproj_kernel, prompt_tiles=prompt_tiles)
    return pl.pallas_call(
        kern,
        grid=(n // tm, PROJ_COLS // tn),
        in_specs=[
            pl.BlockSpec((bp, tm // bp, D), lambda i, j: (0, jnp.minimum(i, prompt_tiles - 1), 0)),
            pl.BlockSpec((bs, ts, D), lambda i, j: (0, 0, 0)),
            pl.BlockSpec((1, D), lambda i, j: (0, 0)),
            pl.BlockSpec((D, tn), lambda i, j: (0, j)),
        ],
        out_specs=[pl.BlockSpec((tm, tn), lambda i, j: (i, j)), pl.BlockSpec((tm, D), lambda i, j: (i, 0))],
        out_shape=[jax.ShapeDtypeStruct((n, PROJ_COLS), f32), jax.ShapeDtypeStruct((n, D), f32)],
        scratch_shapes=[pltpu.VMEM((D // LANES, tm, LANES), f32), pltpu.VMEM((tm, D), bf16)],
        compiler_params=pltpu.CompilerParams(dimension_semantics=("arbitrary", "arbitrary"),
                                             vmem_limit_bytes=56 * MIB),
        name="inproj",
    )(x_prompt, x_sample, gain, w)


def _lru_kernel(x_ref, y_ref, gl_ref, cinit_ref, h0_ref, cw_ref, cb_ref, wxa_ref, bx_ref, ba_ref, ap_ref,
                o_ref, hl_ref, xs_ref, a_ref, b_ref, h_ref, *, batch, reset_first, chunk):
    rows = x_ref.shape[0]
    nt = rows // batch
    hist = 3 * batch
    pid = pl.program_id(0)

    @pl.when(pid == 0)
    def _():
        xs_ref[0:hist, :] = cinit_ref[...]
        h_ref[...] = h0_ref[...]

    xs_ref[hist:hist + rows, :] = x_ref[...]
    logsig = -jax.nn.softplus(-ap_ref[...])

    def gates(c, _):
        r0 = pl.multiple_of(c * chunk, chunk)
        xc = (cb_ref[...]
              + cw_ref[3:4, :] * xs_ref[pl.ds(pl.multiple_of(r0 + hist, SUBLANES), chunk), :]
              + cw_ref[2:3, :] * xs_ref[pl.ds(pl.multiple_of(r0 + 2 * batch, SUBLANES), chunk), :]
              + cw_ref[1:2, :] * xs_ref[pl.ds(pl.multiple_of(r0 + batch, SUBLANES), chunk), :]
              + cw_ref[0:1, :] * xs_ref[pl.ds(r0, chunk), :])
        if reset_first:
            grow = lax.broadcasted_iota(jnp.int32, (chunk, LANES), 0) + (r0 + pid * rows)
            first = grow < batch
        for n in range(D // LANES):
            sl = slice(n * LANES, (n + 1) * LANES)
            xn = xc[:, sl]
            g2 = jnp.dot(xn.astype(bf16), wxa_ref[n], preferred_element_type=f32)
            gate_x = jax.nn.sigmoid(g2[:, :LANES] + bx_ref[:, sl])
            gate_a = jax.nn.sigmoid(g2[:, LANES:] + ba_ref[:, sl])
            log_a = LRU_C * gate_a * logsig[:, sl]
            a = jnp.exp(log_a)
            mult = jnp.sqrt(-jnp.tanh(log_a) * (a * a + 1.0))
            if reset_first:
                mult = jnp.where(first, 1.0, mult)
            a_ref[pl.ds(r0, chunk), sl] = a
            b_ref[pl.ds(r0, chunk), sl] = xn * gate_x * mult
        return 0

    lax.fori_loop(0, rows // chunk, gates, 0)

    def scan(t, h):
        r0 = pl.multiple_of(t * batch, batch)
        h = a_ref[pl.ds(r0, batch), :] * h + b_ref[pl.ds(r0, batch), :]
        b_ref[pl.ds(r0, batch), :] = h
        return h

    h = lax.fori_loop(0, nt, scan, h_ref[...], unroll=(8 if nt >= 8 and batch == SUBLANES else 1))
    h_ref[...] = h
    hl_ref[...] = h
    xs_ref[0:hist, :] = xs_ref[rows:rows + hist, :]

    def outp(c, _):
        r0 = pl.multiple_of(c * chunk, chunk)
        o_ref[pl.ds(r0, chunk), :] = (b_ref[pl.ds(r0, chunk), :] * jax.nn.gelu(y_ref[pl.ds(r0, chunk), :])
                                      * jax.nn.sigmoid(gl_ref[pl.ds(r0, chunk), :]))
        return 0

    lax.fori_loop(0, rows // chunk, outp, 0)


def _with_dst(kern, n_in):
    def wrapped(*refs):
        return kern(*refs[:n_in], *refs[n_in + 1:])
    return wrapped


def _shared_rows(dst, n_total):
    if dst is None:
        return [], [], jax.ShapeDtypeStruct((n_total, D), f32)
    return [pl.BlockSpec(memory_space=pl.ANY)], [dst], jax.ShapeDtypeStruct(dst.shape, dst.dtype)


def _lru(proj, row_off, n_rows, batch, rows_blk, reset_first, conv_init, h0, cw, cb, wxa, bx, ba, ap, dst=None):
    ob = row_off // rows_blk
    kern = functools.partial(_lru_kernel, batch=batch, reset_first=reset_first, chunk=128)
    n_in = 11
    dst_spec, dst_arg, out0 = _shared_rows(dst, proj.shape[0])
    return pl.pallas_call(
        kern if dst is None else _with_dst(kern, n_in),
        grid=(n_rows // rows_blk,),
        in_specs=[
            pl.BlockSpec((rows_blk, D), lambda i: (ob + i, COL_LRU_X)),
            pl.BlockSpec((rows_blk, D), lambda i: (ob + i, COL_LRU_Y)),
            pl.BlockSpec((rows_blk, D), lambda i: (ob + i, COL_GL)),
            _full((3 * batch, D)), _full((batch, D)), _full((4, D)), _full((1, D)),
            _full((D // LANES, LANES, 2 * LANES)), _full((1, D)), _full((1, D)), _full((1, D)),
        ] + dst_spec,
        out_specs=[pl.BlockSpec((rows_blk, D), lambda i: (ob + i, 0)), _full((batch, D))],
        out_shape=[out0, jax.ShapeDtypeStruct((batch, D), f32)],
        input_output_aliases={} if dst is None else {n_in: 0},
        scratch_shapes=[pltpu.VMEM((rows_blk + 3 * batch, D), f32), pltpu.VMEM((rows_blk, D), f32),
                        pltpu.VMEM((rows_blk, D), f32), pltpu.VMEM((batch, D), f32)],
        compiler_params=pltpu.CompilerParams(dimension_semantics=("arbitrary",), vmem_limit_bytes=48 * MIB),
        name="lru",
    )(proj, proj, proj, conv_init, h0, cw, cb, wxa, bx, ba, ap, *dst_arg)


TILE_ALPHA, TILE_BETA, TILE_W, TILE_KP, TILE_WR, TILE_V = range(6)
ROW_BR, ROW_KR, ROW_BONUS = range(3)
CONST_KK, CONST_KA, CONST_RK, CONST_LNW, CONST_LNB = range(5)
NAT_R, NAT_W, NAT_K, NAT_V, NAT_A = range(5)


def _wkv_prep(tiles_ref, rows_ref, t, g, r, w, k, v, a, const):
    kk = k * const(CONST_KK)
    nrm = jnp.sqrt(jnp.sum(kk * kk, axis=0, keepdims=True))
    kk = kk * (1.0 / jnp.maximum(nrm, 1e-12))
    beta = kk * a
    kp = k * (1.0 + (a - 1.0) * const(CONST_KA))
    tiles_ref[t, g, TILE_ALPHA] = -kk
    tiles_ref[t, g, TILE_BETA] = beta
    tiles_ref[t, g, TILE_W] = w
    tiles_ref[t, g, TILE_KP] = kp
    tiles_ref[t, g, TILE_WR] = w * r
    tiles_ref[t, g, TILE_V] = v
    rows_ref[t, g, ROW_BR:ROW_BR + 1, :] = jnp.sum(beta * r, axis=0, keepdims=True)
    rows_ref[t, g, ROW_KR:ROW_KR + 1, :] = jnp.sum(kp * r, axis=0, keepdims=True)
    rows_ref[t, g, ROW_BONUS:ROW_BONUS + 1, :] = jnp.sum(r * kp * const(CONST_RK), axis=0, keepdims=True)


def _rwkv_pre_kernel(r_ref, k_ref, v_ref, l_ref, sinit_ref, mu_ref, w0_ref, a0_ref, w2_ref, a2_ref, g2_ref, c_ref,
                     tiles_ref, rows_ref, go_ref, ps_ref, nat_ref, *, batch, chunk, chains_are_heads):
    rows = r_ref.shape[0]

    @pl.when(pl.program_id(0) == 0)
    def _():
        ps_ref[0:batch, :] = sinit_ref[...]

    ps_ref[batch:batch + rows, 0:D] = r_ref[...]
    ps_ref[batch:batch + rows, D:2 * D] = k_ref[...]
    ps_ref[batch:batch + rows, 2 * D:3 * D] = v_ref[...]
    ps_ref[batch:batch + rows, 3 * D:3 * D + LORA_PAD] = l_ref[...]

    def body(c, _):
        r0 = pl.multiple_of(c * chunk, chunk)

        def mixed(lo, hi):
            cur = ps_ref[pl.ds(pl.multiple_of(r0 + batch, SUBLANES), chunk), lo:hi]
            prev = ps_ref[pl.ds(r0, chunk), lo:hi]
            return cur + (prev - cur) * mu_ref[:, lo:hi]

        nat_ref[NAT_R, pl.ds(r0, chunk), :] = mixed(0, D)
        nat_ref[NAT_K, pl.ds(r0, chunk), :] = mixed(D, 2 * D)
        nat_ref[NAT_V, pl.ds(r0, chunk), :] = mixed(2 * D, 3 * D)
        lm = mixed(3 * D, 3 * D + LORA_PAD)
        xwa = lm[:, 0:LANES]
        xg = lm[:, LANES:3 * LANES]
        lw = jnp.dot(jnp.tanh(xwa).astype(bf16), w2_ref[...], preferred_element_type=f32)
        w_log = -jax.nn.softplus(-(w0_ref[...] + lw)) - 0.5
        nat_ref[NAT_W, pl.ds(r0, chunk), :] = jnp.exp(-jnp.exp(w_log))
        la = jnp.dot(xwa.astype(bf16), a2_ref[...], preferred_element_type=f32)
        nat_ref[NAT_A, pl.ds(r0, chunk), :] = jax.nn.sigmoid(a0_ref[...] + la)
        go_ref[pl.ds(r0, chunk), :] = jnp.dot(jax.nn.sigmoid(xg).astype(bf16), g2_ref[...],
                                              preferred_element_type=f32)
        return 0

    lax.fori_loop(0, rows // chunk, body, 0)
    ps_ref[0:batch, :] = ps_ref[rows:rows + batch, :]

    if chains_are_heads:
        def tstep(t, _):
            r0 = pl.multiple_of(t * batch, batch)
            for hp in range(D // LANES):
                trs = [nat_ref[q, pl.ds(r0, batch), hp * LANES:(hp + 1) * LANES].T for q in range(5)]
                for h2 in range(2):
                    h = 2 * hp + h2
                    _wkv_prep(tiles_ref, rows_ref, t, h, *[tr[h2 * HD:(h2 + 1) * HD] for tr in trs],
                              lambda i, h=h: c_ref[h, i])
            return 0

        lax.fori_loop(0, rows // batch, tstep, 0)
    else:
        half = LANES // 2
        lane = lax.broadcasted_iota(jnp.int32, (HD, LANES), 1)

        def load_pair(q, tp):
            zz = jnp.concatenate(
                [nat_ref[q, pl.ds(pl.multiple_of(tp * 2 * SUBLANES + tt * SUBLANES, SUBLANES), SUBLANES),
                         hp * LANES:(hp + 1) * LANES] for tt in range(2) for hp in range(D // LANES)], axis=0)
            tr = zz.T
            top, bot = tr[0:HD], tr[HD:2 * HD]
            d0 = jnp.where(lane < half, top, pltpu.roll(bot, half, axis=1))
            d1 = jnp.where(lane < half, pltpu.roll(top, half, axis=1), bot)
            return d0, d1

        def pair(tp, _):
            pairs = [load_pair(q, tp) for q in range(5)]
            for tt in range(2):
                _wkv_prep(tiles_ref, rows_ref, 2 * tp + tt, 0, *[p[tt] for p in pairs], lambda i: c_ref[0, i])
            return 0

        lax.fori_loop(0, rows // (2 * SUBLANES), pair, 0, unroll=4)


def _rwkv_pre(proj, row_off, n_rows, batch, rows_blk, groups, shift_init, consts, mu, w0, a0, w2p, a2p, g2p,
              dst=None):
    ob = row_off // rows_blk
    width = 3 * D + LORA_PAD
    steps_blk = rows_blk // batch
    kern = functools.partial(_rwkv_pre_kernel, batch=batch, chunk=128, chains_are_heads=groups > 1)
    n_in = 12
    dst_spec, dst_arg, out_g = _shared_rows(dst, proj.shape[0])
    return pl.pallas_call(
        kern if dst is None else _with_dst(kern, n_in),
        grid=(n_rows // rows_blk,),
        in_specs=[
            pl.BlockSpec((rows_blk, D), lambda i: (ob + i, COL_R)),
            pl.BlockSpec((rows_blk, D), lambda i: (ob + i, COL_K)),
            pl.BlockSpec((rows_blk, D), lambda i: (ob + i, COL_V)),
            pl.BlockSpec((rows_blk, LORA_PAD), lambda i: (ob + i, COL_LORA)),
            _full((batch, width)), _full((1, width)), _full((1, D)), _full((1, D)),
            _full((LANES, D)), _full((LANES, D)), _full((2 * LANES, D)), _full((groups, 5, HD, LANES)),
        ] + dst_spec,
        out_specs=[pl.BlockSpec((steps_blk, groups, 6, HD, LANES), lambda i: (i, 0, 0, 0, 0)),
                   pl.BlockSpec((steps_blk, groups, SUBLANES, LANES), lambda i: (i, 0, 0, 0)),
                   pl.BlockSpec((rows_blk, D), lambda i: (ob + i, 0))],
        out_shape=[jax.ShapeDtypeStruct((n_rows // batch, groups, 6, HD, LANES), f32),
                   jax.ShapeDtypeStruct((n_rows // batch, groups, SUBLANES, LANES), f32),
                   out_g],
        input_output_aliases={} if dst is None else {n_in: 2},
        scratch_shapes=[pltpu.VMEM((rows_blk + batch, width), f32), pltpu.VMEM((5, rows_blk, D), f32)],
        compiler_params=pltpu.CompilerParams(dimension_semantics=("arbitrary",), vmem_limit_bytes=56 * MIB),
        name="rwkv_pre",
    )(proj, proj, proj, proj, shift_init, mu, w0, a0, w2p, a2p, g2p, consts, *dst_arg)


STATE_ROWS = HD * HD
KEY_UNROLL = 32


def _wkv_step(s_ref, tiles_ref, rows_ref, y_ref, t):
    def key_row(tile, j):
        return tiles_ref[t, 0, tile, pl.ds(j, 1), :]

    def reduce_keys(j, acc):
        sa, y0 = acc
        s = s_ref[pl.ds(pl.multiple_of(j * HD, HD), HD), :]
        return sa + s * key_row(TILE_ALPHA, j), y0 + s * key_row(TILE_WR, j)

    zero = jnp.zeros((HD, LANES), f32)
    sa, y0 = lax.fori_loop(0, HD, reduce_keys, (zero, zero), unroll=KEY_UNROLL)
    v = tiles_ref[t, 0, TILE_V]

    def update_keys(j, _):
        rows = pl.ds(pl.multiple_of(j * HD, HD), HD)
        s_ref[rows, :] = (s_ref[rows, :] * key_row(TILE_W, j) + sa * key_row(TILE_BETA, j)
                          + v * key_row(TILE_KP, j))
        return 0

    lax.fori_loop(0, HD, update_keys, 0, unroll=KEY_UNROLL)
    y_ref[t, 0] = y0 + sa * rows_ref[t, 0, ROW_BR:ROW_BR + 1, :] + v * rows_ref[t, 0, ROW_KR:ROW_KR + 1, :]


def _wkv_scan_kernel(tiles_ref, rows_ref, st_ref, y_ref, sfin_ref, s_ref, *, steps):
    tb = pl.program_id(1)

    def value_rows(i):
        return pl.ds(i, HD, stride=HD)

    @pl.when(tb == 0)
    def _():
        def init(c, _):
            tr = jnp.concatenate([st_ref[:, 2 * c, :], st_ref[:, 2 * c + 1, :]], axis=1).T
            for i2 in range(2):
                s_ref[value_rows(2 * c + i2), :] = tr[i2 * HD:(i2 + 1) * HD]
            return 0

        lax.fori_loop(0, STATE_ROWS // LANES, init, 0, unroll=4)

    def step(t, _):
        _wkv_step(s_ref, tiles_ref, rows_ref, y_ref, t)
        return 0

    lax.fori_loop(0, steps, step, 0)

    @pl.when(tb == pl.num_programs(1) - 1)
    def _():
        def fin(c, _):
            pair = jnp.concatenate([s_ref[value_rows(2 * c + i2), :] for i2 in range(2)], axis=0).T
            sfin_ref[:, 2 * c, :] = pair[:, 0:HD]
            sfin_ref[:, 2 * c + 1, :] = pair[:, HD:2 * HD]
            return 0

        lax.fori_loop(0, STATE_ROWS // LANES, fin, 0, unroll=4)


def _wkv_scan(tiles, rows, state, steps_blk):
    n_steps, groups = tiles.shape[0], tiles.shape[1]
    kern = functools.partial(_wkv_scan_kernel, steps=steps_blk)
    sblk = pl.BlockSpec((LANES, HD, HD), lambda g, i: (0, g, 0))
    return pl.pallas_call(
        kern,
        grid=(groups, n_steps // steps_blk),
        in_specs=[pl.BlockSpec((steps_blk, 1, 6, HD, LANES), lambda g, i: (i, g, 0, 0, 0)),
                  pl.BlockSpec((steps_blk, 1, SUBLANES, LANES), lambda g, i: (i, g, 0, 0)),
                  sblk],
        out_specs=[pl.BlockSpec((steps_blk, 1, HD, LANES), lambda g, i: (i, g, 0, 0)), sblk],
        out_shape=[jax.ShapeDtypeStruct((n_steps, groups, HD, LANES), f32),
                   jax.ShapeDtypeStruct((LANES, groups * HD, HD), f32)],
        scratch_shapes=[pltpu.VMEM((STATE_ROWS, LANES), f32)],
        compiler_params=pltpu.CompilerParams(dimension_semantics=("arbitrary", "arbitrary"),
                                             vmem_limit_bytes=40 * MIB),
        name="wkv_scan",
    )(tiles, rows, state)


def _wkv_norm(y, v, bonus, lnw, lnb):
    mean = jnp.sum(y, axis=0, keepdims=True) * (1.0 / HD)
    d = y - mean
    var = jnp.sum(d * d, axis=0, keepdims=True) * (1.0 / HD)
    return d * lax.rsqrt(var + GN_EPS) * lnw + lnb + bonus * v


def _wkv_post_kernel(y_ref, v_ref, rows_ref, c_ref, o_ref, *, batch, chains_are_heads):
    steps = y_ref.shape[0]

    def normed(t, g):
        return _wkv_norm(y_ref[t, g], v_ref[t, g, 0], rows_ref[t, g, ROW_BONUS:ROW_BONUS + 1, :],
                         c_ref[g, CONST_LNW], c_ref[g, CONST_LNB])

    if chains_are_heads:
        def tstep(t, _):
            r0 = pl.multiple_of(t * batch, batch)
            for hp in range(D // LANES):
                two = jnp.concatenate([normed(t, 2 * hp), normed(t, 2 * hp + 1)], axis=0)
                o_ref[pl.ds(r0, batch), hp * LANES:(hp + 1) * LANES] = two.T
            return 0

        lax.fori_loop(0, steps, tstep, 0)
    else:
        half = LANES // 2
        lane = lax.broadcasted_iota(jnp.int32, (HD, LANES), 1)

        def pair(tp, _):
            o0 = normed(2 * tp, 0)
            o1 = normed(2 * tp + 1, 0)
            top = jnp.where(lane < half, o0, pltpu.roll(o1, half, axis=1))
            bot = jnp.where(lane < half, pltpu.roll(o0, half, axis=1), o1)
            zz = jnp.concatenate([top, bot], axis=0).T
            for tt in range(2):
                row = pl.multiple_of(tp * 2 * SUBLANES + tt * SUBLANES, SUBLANES)
                for hp in range(D // LANES):
                    src = tt * HD + hp * SUBLANES
                    o_ref[pl.ds(row, SUBLANES), hp * LANES:(hp + 1) * LANES] = zz[src:src + SUBLANES, :]
            return 0

        lax.fori_loop(0, steps // 2, pair, 0, unroll=2)


def _outproj_kernel(ml_ref, g_ref, gr_ref, x_ref, yp_ref, vp_ref, rp_ref, cp_ref, ysm_ref, vs_ref, rs_ref, cs_ref,
                    wo_ref, n2_ref, rh_ref, rl_ref, rb_ref,
                    x1_ref, t_ref, selt_ref, seln_ref, op_ref, *, prompt_tiles, prompt_batch, sample_batch):
    tm = x_ref.shape[0]
    i = pl.program_id(0)

    @pl.when(i < prompt_tiles)
    def _():
        _wkv_post_kernel(yp_ref, vp_ref, rp_ref, cp_ref, op_ref, batch=prompt_batch, chains_are_heads=False)

    @pl.when(i >= prompt_tiles)
    def _():
        _wkv_post_kernel(ysm_ref, vs_ref, rs_ref, cs_ref, op_ref, batch=sample_batch, chains_are_heads=True)

    merged = ml_ref[...] + jax.nn.sigmoid(gr_ref[...]) * (op_ref[...] * g_ref[...])
    x1 = x_ref[...] + jnp.dot(merged.astype(bf16), wo_ref[...], preferred_element_type=f32)
    x1_ref[...] = x1
    t = x1 * lax.rsqrt(jnp.mean(x1 * x1, axis=-1, keepdims=True) + NORM_EPS) * n2_ref[...]
    th = t.astype(bf16)
    t_ref[...] = th
    tl = (t - th.astype(f32)).astype(bf16)
    nt_dims = (((1,), (1,)), ((), ()))
    lg = (lax.dot_general(rh_ref[...], th, nt_dims, preferred_element_type=f32)
          + lax.dot_general(rh_ref[...], tl, nt_dims, preferred_element_type=f32)
          + lax.dot_general(rl_ref[...], th, nt_dims, preferred_element_type=f32)) + rb_ref[...]
    row = lax.broadcasted_iota(jnp.int32, (EPG, tm), 0).astype(f32)
    neg = jnp.float32(-jnp.inf)
    glog = jnp.where(row < N_GROUPS, lg[0:EPG], neg)
    ge = jnp.exp(glog - jnp.max(glog, axis=0, keepdims=True))
    pg = ge / jnp.sum(ge, axis=0, keepdims=True)
    p_top = jnp.max(pg, axis=0, keepdims=True)
    g_idx = jnp.min(jnp.where(pg == p_top, row, EPG), axis=0, keepdims=True)
    le = jnp.zeros((EPG, tm), f32)
    for g in range(N_GROUPS):
        le = jnp.where(g_idx == g, lg[EPG * (g + 1):EPG * (g + 2)], le)
    qe = jnp.exp(le - jnp.max(le, axis=0, keepdims=True))
    q = qe / jnp.sum(qe, axis=0, keepdims=True)
    q1 = jnp.max(q, axis=0, keepdims=True)
    i1 = jnp.min(jnp.where(q == q1, row, EPG), axis=0, keepdims=True)
    qm = jnp.where(row == i1, -1.0, q)
    q2 = jnp.max(qm, axis=0, keepdims=True)
    i2 = jnp.min(jnp.where(qm == q2, row, EPG), axis=0, keepdims=True)
    qs = q1 + q2
    sel = jnp.concatenate([g_idx * EPG + i1, g_idx * EPG + i2, q1 / qs * p_top, q2 / qs * p_top,
                           jnp.zeros((SUBLANES - 4, tm), f32)], axis=0)
    selt_ref[...] = sel
    sel_pad = jnp.concatenate([sel, jnp.zeros((LANES - SUBLANES, tm), f32)], axis=0)
    for c in range(tm // LANES):
        seln_ref[c * LANES:(c + 1) * LANES, :] = sel_pad[:, c * LANES:(c + 1) * LANES].T


def _outproj(ml, g, proj, x, scan_p, scan_s, wo, n2, rh, rl, rb, prompt_batch, sample_batch, tm=512):
    n = x.shape[0]
    y_p, tiles_p, rows_p, consts_p = scan_p
    y_s, tiles_s, rows_s, consts_s = scan_s
    n_p = y_p.shape[0] * prompt_batch
    sp, ss = tm // prompt_batch, tm // sample_batch
    prompt_tiles = n_p // tm
    sample_tiles = (n - n_p) // tm
    assert n_p % tm == 0 and sp % 2 == 0 and y_s.shape[0] == sample_tiles * ss
    pidx = lambda i: jnp.minimum(i, prompt_tiles - 1)
    sidx = lambda i: jnp.clip(i - prompt_tiles, 0, sample_tiles - 1)
    groups = y_s.shape[1]
    blk = pl.BlockSpec((tm, D), lambda i: (i, 0))
    kern = functools.partial(_outproj_kernel, prompt_tiles=prompt_tiles, prompt_batch=prompt_batch,
                             sample_batch=sample_batch)
    return pl.pallas_call(
        kern,
        grid=(n // tm,),
        in_specs=[blk, blk, pl.BlockSpec((tm, D), lambda i: (i, COL_GR)), blk,
                  pl.BlockSpec((sp, 1, HD, LANES), lambda i: (pidx(i), 0, 0, 0)),
                  pl.BlockSpec((sp, 1, 1, HD, LANES), lambda i: (pidx(i), 0, TILE_V, 0, 0)),
                  pl.BlockSpec((sp, 1, SUBLANES, LANES), lambda i: (pidx(i), 0, 0, 0)),
                  _full((1, 5, HD, LANES)),
                  pl.BlockSpec((ss, groups, HD, LANES), lambda i: (sidx(i), 0, 0, 0)),
                  pl.BlockSpec((ss, groups, 1, HD, LANES), lambda i: (sidx(i), 0, TILE_V, 0, 0)),
                  pl.BlockSpec((ss, groups, SUBLANES, LANES), lambda i: (sidx(i), 0, 0, 0)),
                  _full((groups, 5, HD, LANES)),
                  _full((D, D)), _full((1, D)), _full((ROUTER_ROWS, D)), _full((ROUTER_ROWS, D)),
                  _full((ROUTER_ROWS, 1))],
        out_specs=[blk, blk, pl.BlockSpec((SUBLANES, tm), lambda i: (0, i)),
                   pl.BlockSpec((tm, LANES), lambda i: (i, 0))],
        out_shape=[jax.ShapeDtypeStruct((n, D), f32), jax.ShapeDtypeStruct((n, D), bf16),
                   jax.ShapeDtypeStruct((SUBLANES, n), f32), jax.ShapeDtypeStruct((n, LANES), f32)],
        scratch_shapes=[pltpu.VMEM((tm, D), f32)],
        compiler_params=pltpu.CompilerParams(dimension_semantics=("arbitrary",), vmem_limit_bytes=56 * MIB),
        name="outproj_router",
    )(ml, g, proj, x, y_p, tiles_p, rows_p, consts_p, y_s, tiles_s, rows_s, consts_s, wo, n2, rh, rl, rb)


MOE_CHUNK = 32
MOE_ROWS = 128
MOE_GATHER = 512
MOE_EXPERTS_PER_STEP = 4
MOE_OVERRUN = MOE_ROWS
MOE_CAP = -(-(2 * TOKEN_TILE + N_EXPERTS * (MOE_CHUNK - 1) + MOE_OVERRUN) // MOE_GATHER) * MOE_GATHER
SEL_E1, SEL_E2, SEL_C1, SEL_C2 = range(4)


def _route_meta_kernel(seln_ref, o_ref):
    tm = seln_ref.shape[0]
    lane = lax.broadcasted_iota(jnp.int32, (tm, LANES), 1).astype(f32)
    sel = seln_ref[...]
    hit = (lane == sel[:, SEL_E1:SEL_E1 + 1]) | (lane == sel[:, SEL_E2:SEL_E2 + 1])
    cnt = jnp.sum(jnp.where(hit, 1.0, 0.0), axis=0, keepdims=True)
    nchunk = jnp.floor((cnt + (MOE_CHUNK - 1)) * (1.0 / MOE_CHUNK))
    upper = (lax.broadcasted_iota(jnp.int32, (LANES, LANES), 0)
             < lax.broadcasted_iota(jnp.int32, (LANES, LANES), 1))
    offs = jnp.dot(jnp.broadcast_to(nchunk, (SUBLANES, LANES)).astype(bf16), jnp.where(upper, 1.0, 0.0).astype(bf16),
                   preferred_element_type=f32)
    row = lax.broadcasted_iota(jnp.int32, (SUBLANES, LANES), 0)
    o_ref[0] = jnp.where(row == 0, nchunk, jnp.where(row == 1, offs, 0.0)).astype(jnp.int32)


def _route_meta(seln, tm):
    tiles = seln.shape[0] // tm
    return pl.pallas_call(
        _route_meta_kernel,
        grid=(tiles,),
        in_specs=[pl.BlockSpec((tm, LANES), lambda i: (i, 0))],
        out_specs=pl.BlockSpec((1, SUBLANES, LANES), lambda i: (i, 0, 0)),
        out_shape=jax.ShapeDtypeStruct((tiles, SUBLANES, LANES), jnp.int32),
        compiler_params=pltpu.CompilerParams(dimension_semantics=("arbitrary",)),
        name="route_meta",
    )(seln)


def _moe_experts_kernel(nch_ref, off_ref, t_ref, selt_ref, tri_ref, wgu_ref, wd_ref, ys_ref, pos_ref,
                        xs_ref, cw_ref):
    i = pl.program_id(0)
    e = pl.program_id(1)
    tm = t_ref.shape[0]
    used_rows = (off_ref[i * N_EXPERTS + N_EXPERTS - 1] + nch_ref[i * N_EXPERTS + N_EXPERTS - 1]) * MOE_CHUNK

    @pl.when(e == 0)
    def _():
        ys_ref[...] = jnp.zeros_like(ys_ref)
        e1 = selt_ref[SEL_E1:SEL_E1 + 1, :]
        e2 = selt_ref[SEL_E2:SEL_E2 + 1, :]
        erow = lax.broadcasted_iota(jnp.int32, (N_EXPERTS, tm), 0).astype(f32)
        oh1 = jnp.where(erow == e1, 1.0, 0.0)
        oh2 = jnp.where(erow == e2, 1.0, 0.0)
        before1 = jnp.dot(oh1.astype(bf16), tri_ref[...], preferred_element_type=f32)
        before2 = jnp.dot(oh2.astype(bf16), tri_ref[...], preferred_element_type=f32)
        cnt1 = jnp.sum(oh1, axis=1, keepdims=True)
        cnt2 = jnp.sum(oh2, axis=1, keepdims=True)
        nchunk = jnp.floor((cnt1 + cnt2 + (MOE_CHUNK - 1)) * (1.0 / MOE_CHUNK))
        lower = (lax.broadcasted_iota(jnp.int32, (N_EXPERTS, LANES), 1)
                 < lax.broadcasted_iota(jnp.int32, (N_EXPERTS, LANES), 0))
        nchunk_rows = jnp.concatenate([jnp.broadcast_to(nchunk, (N_EXPERTS, LANES)),
                                       jnp.zeros((LANES - N_EXPERTS, LANES), f32)], axis=0)
        start = jnp.dot(jnp.where(lower, 1.0, 0.0).astype(bf16), nchunk_rows.astype(bf16),
                        preferred_element_type=f32)[:, 0:1] * MOE_CHUNK
        pos1 = jnp.sum(oh1 * (start + before1), axis=0, keepdims=True)
        pos2 = jnp.sum(oh2 * (start + cnt1 + before2), axis=0, keepdims=True)
        pos_ref[...] = jnp.concatenate([pos1, pos2, jnp.zeros((SUBLANES - 2, tm), f32)], axis=0)
        c1 = selt_ref[SEL_C1:SEL_C1 + 1, :]
        c2 = selt_ref[SEL_C2:SEL_C2 + 1, :]
        for k in range(MOE_CAP // MOE_GATHER):
            @pl.when(k * MOE_GATHER < used_rows + MOE_OVERRUN)
            def _(k=k):
                ridx = (lax.broadcasted_iota(jnp.int32, (MOE_GATHER, tm), 0) + k * MOE_GATHER).astype(f32)
                p1 = ridx == pos1
                p2 = ridx == pos2
                onehot = jnp.where(p1 | p2, 1.0, 0.0).astype(bf16)
                xs_ref[k * MOE_GATHER:(k + 1) * MOE_GATHER, :] = jnp.dot(
                    onehot, t_ref[...], preferred_element_type=f32).astype(bf16)
                w = jnp.sum(jnp.where(p1, c1, 0.0) + jnp.where(p2, c2, 0.0), axis=1, keepdims=True)
                cw_ref[k * MOE_GATHER:(k + 1) * MOE_GATHER, :] = jnp.broadcast_to(w, (MOE_GATHER, LANES))

    def expert_rows(u, start):
        rows = pl.ds(pl.multiple_of(start, MOE_CHUNK), MOE_ROWS)
        gu = jnp.dot(xs_ref[rows, :], wgu_ref[u], preferred_element_type=f32)
        w = cw_ref[rows, :]
        h = jax.nn.silu(gu[:, :D_EXPERT]) * gu[:, D_EXPERT:] * jnp.concatenate([w, w], axis=1)
        ys_ref[0, rows, :] = jnp.dot(h.astype(bf16), wd_ref[u], preferred_element_type=f32).astype(bf16)

    per_block = MOE_ROWS // MOE_CHUNK
    for u in range(wgu_ref.shape[0]):
        expert = i * N_EXPERTS + e * wgu_ref.shape[0] + u
        base = off_ref[expert] * MOE_CHUNK
        expert_rows(u, base)

        def more(c, _, u=u, base=base):
            expert_rows(u, base + (c + 1) * MOE_ROWS)
            return 0

        lax.fori_loop(0, jnp.maximum(nch_ref[expert] - 1, 0) // per_block, more, 0)


def _moe_experts(nch, off, t, selt, tri, wgu, wd, tm):
    n = t.shape[0]
    tiles = n // tm
    grid_spec = pltpu.PrefetchScalarGridSpec(
        num_scalar_prefetch=2,
        grid=(tiles, N_EXPERTS // MOE_EXPERTS_PER_STEP),
        in_specs=[
            pl.BlockSpec((tm, D), lambda i, e, nch, off: (i, 0)),
            pl.BlockSpec((SUBLANES, tm), lambda i, e, nch, off: (0, i)),
            pl.BlockSpec((tm, tm), lambda i, e, nch, off: (0, 0)),
            pl.BlockSpec((MOE_EXPERTS_PER_STEP, D, 2 * D_EXPERT), lambda i, e, nch, off: (e, 0, 0)),
            pl.BlockSpec((MOE_EXPERTS_PER_STEP, D_EXPERT, D), lambda i, e, nch, off: (e, 0, 0)),
        ],
        out_specs=[pl.BlockSpec((1, MOE_CAP, D), lambda i, e, nch, off: (i, 0, 0)),
                   pl.BlockSpec((SUBLANES, tm), lambda i, e, nch, off: (0, i))],
        scratch_shapes=[pltpu.VMEM((MOE_CAP, D), bf16), pltpu.VMEM((MOE_CAP, LANES), f32)],
    )
    return pl.pallas_call(
        _moe_experts_kernel,
        grid_spec=grid_spec,
        out_shape=[jax.ShapeDtypeStruct((tiles, MOE_CAP, D), bf16), jax.ShapeDtypeStruct((SUBLANES, n), f32)],
        compiler_params=pltpu.CompilerParams(dimension_semantics=("arbitrary", "arbitrary"),
                                             vmem_limit_bytes=56 * MIB),
        name="moe_experts",
    )(nch, off, t, selt, tri, wgu, wd)


def _moe_combine_kernel(nch_ref, off_ref, ysort_ref, pos_ref, x1_ref, fn_ref, yp_ref, ys_ref, acc_ref, yf_ref,
                        *, prompt_tiles, parts):
    i = pl.program_id(0)
    part = pl.program_id(1)
    rows = x1_ref.shape[0]
    bp, tsteps = yp_ref.shape[0], yp_ref.shape[1]
    bs = ys_ref.shape[0]
    ssteps = ys_ref.shape[1] // parts
    used_rows = (off_ref[i * N_EXPERTS + N_EXPERTS - 1] + nch_ref[i * N_EXPERTS + N_EXPERTS - 1]) * MOE_CHUNK

    pos_t = jnp.concatenate([pos_ref[...], jnp.zeros((LANES - SUBLANES, rows), f32)], axis=0)
    pos_n = jnp.concatenate([pos_t[:, c * LANES:(c + 1) * LANES].T for c in range(rows // LANES)], axis=0)
    pos1 = pos_n[:, 0:1]
    pos2 = pos_n[:, 1:2]
    acc_ref[...] = x1_ref[...]
    for k in range(MOE_CAP // MOE_GATHER):
        @pl.when(k * MOE_GATHER < used_rows)
        def _(k=k):
            cidx = (lax.broadcasted_iota(jnp.int32, (rows, MOE_GATHER), 1) + k * MOE_GATHER).astype(f32)
            onehot = jnp.where((cidx == pos1) | (cidx == pos2), 1.0, 0.0).astype(bf16)
            acc_ref[...] += jnp.dot(onehot, ysort_ref[0, k * MOE_GATHER:(k + 1) * MOE_GATHER, :],
                                    preferred_element_type=f32)

    xo = acc_ref[...]
    y = xo * lax.rsqrt(jnp.mean(xo * xo, axis=-1, keepdims=True) + NORM_EPS) * fn_ref[...]
    for cb in range(D // LANES):
        yf_ref[cb] = y[:, cb * LANES:(cb + 1) * LANES]

    @pl.when(i < prompt_tiles)
    def _():
        for b in range(bp):
            yp_ref[b] = jnp.concatenate([yf_ref[cb, pl.ds(b, tsteps, stride=bp), :] for cb in range(D // LANES)],
                                        axis=1)

    for p in range(parts):
        @pl.when((i >= prompt_tiles) & (part == p))
        def _(p=p):
            for t in range(ssteps):
                ys_ref[:, p * ssteps + t, :] = jnp.concatenate(
                    [yf_ref[cb, t * bs:(t + 1) * bs, :] for cb in range(D // LANES)], axis=1)


def _moe_combine(nch, off, ysort, pos, x1, fn, prompt_shape, sample_shape, tm, parts=2):
    n = x1.shape[0]
    bp, tp, _ = prompt_shape
    bs, ts, _ = sample_shape
    rows = tm // parts
    assert rows % bp == 0 and tp % (rows // bp) == 0 and bs * ts == tm and ts % parts == 0
    prompt_tiles = bp * tp // tm
    last_prompt_blk = prompt_tiles * parts - 1
    kern = functools.partial(_moe_combine_kernel, prompt_tiles=prompt_tiles, parts=parts)
    grid_spec = pltpu.PrefetchScalarGridSpec(
        num_scalar_prefetch=2,
        grid=(n // tm, parts),
        in_specs=[
            pl.BlockSpec((1, MOE_CAP, D), lambda i, p, nch, off: (i, 0, 0)),
            pl.BlockSpec((SUBLANES, rows), lambda i, p, nch, off: (0, i * parts + p)),
            pl.BlockSpec((rows, D), lambda i, p, nch, off: (i * parts + p, 0)),
            pl.BlockSpec((1, D), lambda i, p, nch, off: (0, 0)),
        ],
        out_specs=[pl.BlockSpec((bp, rows // bp, D),
                                lambda i, p, nch, off: (0, jnp.minimum(i * parts + p, last_prompt_blk), 0)),
                   pl.BlockSpec((bs, ts, D), lambda i, p, nch, off: (0, 0, 0))],
        scratch_shapes=[pltpu.VMEM((rows, D), f32), pltpu.VMEM((D // LANES, rows, LANES), f32)],
    )
    return pl.pallas_call(
        kern,
        grid_spec=grid_spec,
        out_shape=[jax.ShapeDtypeStruct(prompt_shape, f32), jax.ShapeDtypeStruct(sample_shape, f32)],
        compiler_params=pltpu.CompilerParams(dimension_semantics=("arbitrary", "arbitrary"),
                                             vmem_limit_bytes=56 * MIB),
        name="moe_combine",
    )(nch, off, ysort, pos, x1, fn)


def _chain_tiles_prompt(vec, batch):
    t = vec.reshape(HEADS // 2, 2, HD)
    t = jnp.transpose(t, (2, 1, 0))
    return jnp.broadcast_to(t[..., None], (HD, 2, HEADS // 2, batch)).reshape(HD, LANES)


def _chain_tiles_sample(vec):
    return jnp.broadcast_to(vec.reshape(HEADS, HD)[..., None], (HEADS, HD, LANES))


def kernel(x_prompt, x_sample, state_conv, state_lru, state_shift, state_wkv, norm1, w_in, conv_w, conv_b, lru_wx, lru_bx, lru_wa, lru_ba, lru_a_param, rwkv_mu, rwkv_w0, rwkv_w2, rwkv_a0, rwkv_a2, rwkv_g2, rwkv_k_k, rwkv_k_a, rwkv_r_k, rwkv_ln_w, rwkv_ln_b, w_out, norm2, router_group, router_group_b, router_expert, router_expert_b, exp_gate, exp_up, exp_down, final_norm):
    bp, tp, _ = x_prompt.shape
    bs, ts, _ = x_sample.shape
    assert norm1.shape[0] == 1 and bp * HEADS == LANES and bs == LANES and tp % 64 == 0 and ts % 2 == 0
    n_p, n_s = bp * tp, bs * ts
    c_rw = 2 * D
    c_lora = c_rw + 3 * D
    c_gl = c_lora + LORA

    w = w_in[0]
    w_all = jnp.concatenate([w[:, :c_lora], w[:, c_gl:], w[:, c_lora:c_gl],
                             jnp.zeros((D, LORA_PAD - LORA), f32)], axis=1).astype(bf16)
    mu = rwkv_mu[0]
    mu_all = jnp.concatenate([mu[:3 * D], mu[3 * D:], jnp.zeros((LORA_PAD - LORA,), f32)])[None]
    w2p = jnp.concatenate([rwkv_w2[0], jnp.zeros((LANES - LORA_W, D), f32)]).astype(bf16)
    a2p = jnp.concatenate([jnp.zeros((LORA_W, D), f32), rwkv_a2[0]]).astype(bf16)
    g2p = jnp.concatenate([rwkv_g2[0], jnp.zeros((2 * LANES - LORA_G, D), f32)]).astype(bf16)
    wxa = jnp.concatenate([lru_wx[0], lru_wa[0]], axis=-1).astype(bf16)
    row = lambda v: v.reshape(1, -1)
    rk = rwkv_r_k[0].reshape(D)
    cvecs = (rwkv_k_k[0], rwkv_k_a[0], rk, rwkv_ln_w[0], rwkv_ln_b[0])
    consts_p = jnp.stack([_chain_tiles_prompt(v, bp) for v in cvecs])[None]
    consts_s = jnp.stack([_chain_tiles_sample(v) for v in cvecs], axis=1)
    rw = jnp.zeros((ROUTER_ROWS, D), f32)
    rw = rw.at[0:N_GROUPS].set(router_group[0].T).at[EPG:EPG + N_EXPERTS].set(router_expert[0].T)
    rh = rw.astype(bf16)
    rl = (rw - rh.astype(f32)).astype(bf16)
    rb = jnp.zeros((ROUTER_ROWS, 1), f32)
    rb = rb.at[0:N_GROUPS, 0].set(router_group_b[0]).at[EPG:EPG + N_EXPERTS, 0].set(router_expert_b[0])
    wgu = jnp.concatenate([exp_gate[0], exp_up[0]], axis=-1).astype(bf16)
    wd = exp_down[0].astype(bf16)
    wo = w_out[0].astype(bf16)

    proj, x = _inproj(x_prompt, x_sample, row(norm1[0]), w_all, tm=TOKEN_TILE)

    lru_args = (conv_w[0], row(conv_b[0]), wxa, row(lru_bx[0]), row(lru_ba[0]), row(lru_a_param[0]))
    conv_init_s = jnp.transpose(state_conv[0], (1, 0, 2)).reshape(3 * bs, D)
    ml, hl_p = _lru(proj, 0, n_p, bp, 512, True, jnp.zeros((3 * bp, D), f32), jnp.zeros((bp, D), f32), *lru_args)
    ml, hl_s = _lru(proj, n_p, n_s, bs, 512, False, conv_init_s, state_lru[0], *lru_args, dst=ml)

    sh = state_shift[0]
    shift_init_s = jnp.concatenate([sh, jnp.zeros((bs, LORA_PAD - LORA), f32)], axis=1)
    pre_args = (mu_all, row(rwkv_w0[0]), row(rwkv_a0[0]), w2p, a2p, g2p)
    tiles_p, rows_p, g = _rwkv_pre(proj, 0, n_p, bp, 256, 1, jnp.zeros((bp, 3 * D + LORA_PAD), f32),
                                   consts_p, *pre_args)
    tiles_s, rows_s, g = _rwkv_pre(proj, n_p, n_s, bs, 256, HEADS, shift_init_s, consts_s, *pre_args, dst=g)

    y_p, sfin_p = _wkv_scan(tiles_p, rows_p, jnp.zeros((LANES, HD, HD), f32), 16)
    y_s, sfin_s = _wkv_scan(tiles_s, rows_s, state_wkv[0].reshape(bs, HEADS * HD, HD), ts)

    x1, t, selt, seln = _outproj(ml, g, proj, x, (y_p, tiles_p, rows_p, consts_p), (y_s, tiles_s, rows_s, consts_s),
                                 wo, row(norm2[0]), rh, rl, rb, bp, bs)
    meta = _route_meta(seln, TOKEN_TILE)
    nch = meta[:, 0, :N_EXPERTS].reshape(-1)
    off = meta[:, 1, :N_EXPERTS].reshape(-1)
    tri = jnp.triu(jnp.ones((TOKEN_TILE, TOKEN_TILE), bf16), k=1)
    ysort, pos = _moe_experts(nch, off, t, selt, tri, wgu, wd, TOKEN_TILE)
    y_prompt, y_sample = _moe_combine(nch, off, ysort, pos, x1, row(final_norm), x_prompt.shape, x_sample.shape,
                                      TOKEN_TILE)

    def last_rows(lo, hi, n_end, batch, steps):
        return proj[n_end - steps * batch:n_end, lo:hi].reshape(steps, batch, hi - lo)

    conv_p = jnp.transpose(last_rows(0, D, n_p, bp, 3), (1, 0, 2))[None]
    conv_s = jnp.transpose(last_rows(0, D, n_p + n_s, bs, 3), (1, 0, 2))[None]

    def shift_rows(n_end, batch):
        return jnp.concatenate([last_rows(2 * D, 5 * D, n_end, batch, 1)[0],
                                last_rows(7 * D, 7 * D + LORA, n_end, batch, 1)[0]], axis=1)[None]

    wkv_p = jnp.transpose(sfin_p.reshape(2, HEADS // 2, bp, HD, HD), (2, 1, 0, 3, 4)).reshape(1, bp, HEADS, HD, HD)
    wkv_s = sfin_s.reshape(1, bs, HEADS, HD, HD)
    return (y_prompt, y_sample, conv_p, hl_p[None], shift_rows(n_p, bp), wkv_p,
            conv_s, hl_s[None], shift_rows(n_p + n_s, bs), wkv_s)
```

```python
import functools

import jax
import jax.numpy as jnp
from jax import lax
from jax.experimental import pallas as pl
from jax.experimental.pallas import tpu as pltpu

f32 = jnp.float32
bf16 = jnp.bfloat16

D = 1024
HEADS = 16
HD = 64
LANES = 128
SUBLANES = 8
LORA_W = 64
LORA_A = 64
LORA_G = 160
LORA = LORA_W + LORA_A + LORA_G
LORA_PAD = 512
N_GROUPS = 4
EPG = 8
N_EXPERTS = N_GROUPS * EPG
D_EXPERT = 256
LRU_C = 8.0
GN_EPS = 64e-5
NORM_EPS = 1e-6
PROJ_COLS = 7 * D + LORA_PAD
COL_LRU_X, COL_LRU_Y, COL_R, COL_K, COL_V, COL_GL, COL_GR = range(7)
COL_LORA = 7 * D // LORA_PAD
ROUTER_ROWS = 48
MIB = 1024 * 1024
TOKEN_TILE = 1024


def _aligned(start, multiple):
    return start if isinstance(start, int) else pl.multiple_of(start, multiple)


def _full(shape, grid_rank=1):
    zeros = tuple(0 for _ in shape)
    if grid_rank == 1:
        return pl.BlockSpec(shape, lambda i: zeros)
    return pl.BlockSpec(shape, lambda i, j: zeros)


def _inproj_kernel(xp_ref, xs_ref, g_ref, w_ref, o_ref, xo_ref, xf_ref, xn_ref, *, prompt_tiles):
    i = pl.program_id(0)
    first_col = pl.program_id(1) == 0
    bp, tsteps = xp_ref.shape[0], xp_ref.shape[1]
    bs, ssteps = xs_ref.shape[0], xs_ref.shape[1]

    @pl.when(first_col & (i < prompt_tiles))
    def _():
        for b in range(bp):
            v = xp_ref[b]
            for cb in range(D // LANES):
                xf_ref[cb, pl.ds(b, tsteps, stride=bp), :] = v[:, cb * LANES:(cb + 1) * LANES]

    @pl.when(first_col & (i >= prompt_tiles))
    def _():
        for t in range(ssteps):
            v = xs_ref[:, t, :]
            for cb in range(D // LANES):
                xf_ref[cb, t * bs:(t + 1) * bs, :] = v[:, cb * LANES:(cb + 1) * LANES]

    @pl.when(first_col)
    def _():
        x = jnp.concatenate([xf_ref[cb] for cb in range(D // LANES)], axis=1)
        xo_ref[...] = x
        ms = jnp.mean(x * x, axis=-1, keepdims=True)
        xn_ref[...] = (x * lax.rsqrt(ms + NORM_EPS) * g_ref[...]).astype(bf16)

    o_ref[...] = jnp.dot(xn_ref[...], w_ref[...], preferred_element_type=f32)


def _inproj(x_prompt, x_sample, gain, w, tm, tn=1280):
    bp, tp, _ = x_prompt.shape
    bs, ts, _ = x_sample.shape
    assert tm % bp == 0 and tp % (tm // bp) == 0 and bs * ts == tm
    prompt_tiles = bp * tp // tm
    n = bp * tp + bs * ts
    kern = functools.partial(_inproj_kernel, prompt_tiles=prompt_tiles)
    return pl.pallas_call(
        kern,
        grid=(n // tm, PROJ_COLS // tn),
        in_specs=[
            pl.BlockSpec((bp, tm // bp, D), lambda i, j: (0, jnp.minimum(i, prompt_tiles - 1), 0)),
            pl.BlockSpec((bs, ts, D), lambda i, j: (0, 0, 0)),
            pl.BlockSpec((1, D), lambda i, j: (0, 0)),
            pl.BlockSpec((D, tn), lambda i, j: (0, j)),
        ],
        out_specs=[pl.BlockSpec((tm, tn), lambda i, j: (i, j)), pl.BlockSpec((tm, D), lambda i, j: (i, 0))],
        out_shape=[jax.ShapeDtypeStruct((n, PROJ_COLS), f32), jax.ShapeDtypeStruct((n, D), f32)],
        scratch_shapes=[pltpu.VMEM((D // LANES, tm, LANES), f32), pltpu.VMEM((tm, D), bf16)],
        compiler_params=pltpu.CompilerParams(dimension_semantics=("arbitrary", "arbitrary"),
                                             vmem_limit_bytes=56 * MIB),
        name="inproj",
    )(x_prompt, x_sample, gain, w)


def _lru_kernel(x_ref, y_ref, gl_ref, cinit_ref, h0_ref, cw_ref, cb_ref, wxa_ref, bx_ref, ba_ref, ap_ref,
                o_ref, hl_ref, xs_ref, a_ref, b_ref, h_ref, *, batch, reset_first, chunk):
    rows = x_ref.shape[0]
    nt = rows // batch
    hist = 3 * batch
    pid = pl.program_id(0)

    @pl.when(pid == 0)
    def _():
        xs_ref[0:hist, :] = cinit_ref[...]
        h_ref[...] = h0_ref[...]

    xs_ref[hist:hist + rows, :] = x_ref[...]
    logsig = -jax.nn.softplus(-ap_ref[...])

    def gates(c, _):
        r0 = pl.multiple_of(c * chunk, chunk)
        xc = (cb_ref[...]
              + cw_ref[3:4, :] * xs_ref[pl.ds(pl.multiple_of(r0 + hist, SUBLANES), chunk), :]
              + cw_ref[2:3, :] * xs_ref[pl.ds(pl.multiple_of(r0 + 2 * batch, SUBLANES), chunk), :]
              + cw_ref[1:2, :] * xs_ref[pl.ds(pl.multiple_of(r0 + batch, SUBLANES), chunk), :]
              + cw_ref[0:1, :] * xs_ref[pl.ds(r0, chunk), :])
        if reset_first:
            grow = lax.broadcasted_iota(jnp.int32, (chunk, LANES), 0) + (r0 + pid * rows)
            first = grow < batch
        for n in range(D // LANES):
            sl = slice(n * LANES, (n + 1) * LANES)
            xn = xc[:, sl]
            g2 = jnp.dot(xn.astype(bf16), wxa_ref[n], preferred_element_type=f32)
            gate_x = jax.nn.sigmoid(g2[:, :LANES] + bx_ref[:, sl])
            gate_a = jax.nn.sigmoid(g2[:, LANES:] + ba_ref[:, sl])
            log_a = LRU_C * gate_a * logsig[:, sl]
            a = jnp.exp(log_a)
            mult = jnp.sqrt(-jnp.tanh(log_a) * (a * a + 1.0))
            if reset_first:
                mult = jnp.where(first, 1.0, mult)
            a_ref[pl.ds(r0, chunk), sl] = a
            b_ref[pl.ds(r0, chunk), sl] = xn * gate_x * mult
        return 0

    lax.fori_loop(0, rows // chunk, gates, 0)

    def scan(t, h):
        r0 = pl.multiple_of(t * batch, batch)
        h = a_ref[pl.ds(r0, batch), :] * h + b_ref[pl.ds(r0, batch), :]
        b_ref[pl.ds(r0, batch), :] = h
        return h

    h = lax.fori_loop(0, nt, scan, h_ref[...], unroll=(8 if nt >= 8 and batch == SUBLANES else 1))
    h_ref[...] = h
    hl_ref[...] = h
    xs_ref[0:hist, :] = xs_ref[rows:rows + hist, :]

    def outp(c, _):
        r0 = pl.multiple_of(c * chunk, chunk)
        o_ref[pl.ds(r0, chunk), :] = (b_ref[pl.ds(r0, chunk), :] * jax.nn.gelu(y_ref[pl.ds(r0, chunk), :])
                                      * jax.nn.sigmoid(gl_ref[pl.ds(r0, chunk), :]))
        return 0

    lax.fori_loop(0, rows // chunk, outp, 0)


def _with_dst(kern, n_in):
    def wrapped(*refs):
        return kern(*refs[:n_in], *refs[n_in + 1:])
    return wrapped


def _shared_rows(dst, n_total):
    if dst is None:
        return [], [], jax.ShapeDtypeStruct((n_total, D), f32)
    return [pl.BlockSpec(memory_space=pl.ANY)], [dst], jax.ShapeDtypeStruct(dst.shape, dst.dtype)


def _lru(proj, row_off, n_rows, batch, rows_blk, reset_first, conv_init, h0, cw, cb, wxa, bx, ba, ap, dst=None):
    ob = row_off // rows_blk
    kern = functools.partial(_lru_kernel, batch=batch, reset_first=reset_first, chunk=128)
    n_in = 11
    dst_spec, dst_arg, out0 = _shared_rows(dst, proj.shape[0])
    return pl.pallas_call(
        kern if dst is None else _with_dst(kern, n_in),
        grid=(n_rows // rows_blk,),
        in_specs=[
            pl.BlockSpec((rows_blk, D), lambda i: (ob + i, COL_LRU_X)),
            pl.BlockSpec((rows_blk, D), lambda i: (ob + i, COL_LRU_Y)),
            pl.BlockSpec((rows_blk, D), lambda i: (ob + i, COL_GL)),
            _full((3 * batch, D)), _full((batch, D)), _full((4, D)), _full((1, D)),
            _full((D // LANES, LANES, 2 * LANES)), _full((1, D)), _full((1, D)), _full((1, D)),
        ] + dst_spec,
        out_specs=[pl.BlockSpec((rows_blk, D), lambda i: (ob + i, 0)), _full((batch, D))],
        out_shape=[out0, jax.ShapeDtypeStruct((batch, D), f32)],
        input_output_aliases={} if dst is None else {n_in: 0},
        scratch_shapes=[pltpu.VMEM((rows_blk + 3 * batch, D), f32), pltpu.VMEM((rows_blk, D), f32),
                        pltpu.VMEM((rows_blk, D), f32), pltpu.VMEM((batch, D), f32)],
        compiler_params=pltpu.CompilerParams(dimension_semantics=("arbitrary",), vmem_limit_bytes=48 * MIB),
        name="lru",
    )(proj, proj, proj, conv_init, h0, cw, cb, wxa, bx, ba, ap, *dst_arg)


TILE_ALPHA, TILE_BETA, TILE_W, TILE_KP, TILE_WR, TILE_V = range(6)
ROW_BR, ROW_KR, ROW_BONUS = range(3)
CONST_KK, CONST_KA, CONST_RK, CONST_LNW, CONST_LNB = range(5)
NAT_R, NAT_W, NAT_K, NAT_V, NAT_A = range(5)


def _wkv_prep(tiles_ref, rows_ref, t, g, r, w, k, v, a, const):
    kk = k * const(CONST_KK)
    nrm = jnp.sqrt(jnp.sum(kk * kk, axis=0, keepdims=True))
    kk = kk * (1.0 / jnp.maximum(nrm, 1e-12))
    beta = kk * a
    kp = k * (1.0 + (a - 1.0) * const(CONST_KA))
    tiles_ref[t, g, TILE_ALPHA] = -kk
    tiles_ref[t, g, TILE_BETA] = beta
    tiles_ref[t, g, TILE_W] = w
    tiles_ref[t, g, TILE_KP] = kp
    tiles_ref[t, g, TILE_WR] = w * r
    tiles_ref[t, g, TILE_V] = v
    rows_ref[t, g, ROW_BR:ROW_BR + 1, :] = jnp.sum(beta * r, axis=0, keepdims=True)
    rows_ref[t, g, ROW_KR:ROW_KR + 1, :] = jnp.sum(kp * r, axis=0, keepdims=True)
    rows_ref[t, g, ROW_BONUS:ROW_BONUS + 1, :] = jnp.sum(r * kp * const(CONST_RK), axis=0, keepdims=True)


def _rwkv_pre_kernel(r_ref, k_ref, v_ref, l_ref, sinit_ref, mu_ref, w0_ref, a0_ref, w2_ref, a2_ref, g2_ref, c_ref,
                     tiles_ref, rows_ref, go_ref, ps_ref, nat_ref, *, batch, chunk, chains_are_heads):
    rows = r_ref.shape[0]

    @pl.when(pl.program_id(0) == 0)
    def _():
        ps_ref[0:batch, :] = sinit_ref[...]

    ps_ref[batch:batch + rows, 0:D] = r_ref[...]
    ps_ref[batch:batch + rows, D:2 * D] = k_ref[...]
    ps_ref[batch:batch + rows, 2 * D:3 * D] = v_ref[...]
    ps_ref[batch:batch + rows, 3 * D:3 * D + LORA_PAD] = l_ref[...]

    def body(c, _):
        r0 = _aligned(c * chunk, chunk)

        def mixed(lo, hi):
            cur = ps_ref[pl.ds(_aligned(r0 + batch, SUBLANES), chunk), lo:hi]
            prev = ps_ref[pl.ds(r0, chunk), lo:hi]
            return cur + (prev - cur) * mu_ref[:, lo:hi]

        nat_ref[NAT_R, pl.ds(r0, chunk), :] = mixed(0, D)
        nat_ref[NAT_K, pl.ds(r0, chunk), :] = mixed(D, 2 * D)
        nat_ref[NAT_V, pl.ds(r0, chunk), :] = mixed(2 * D, 3 * D)
        lm = mixed(3 * D, 3 * D + LORA_PAD)
        xwa = lm[:, 0:LANES]
        xg = lm[:, LANES:3 * LANES]
        lw = jnp.dot(jnp.tanh(xwa).astype(bf16), w2_ref[...], preferred_element_type=f32)
        w_log = -jax.nn.softplus(-(w0_ref[...] + lw)) - 0.5
        nat_ref[NAT_W, pl.ds(r0, chunk), :] = jnp.exp(-jnp.exp(w_log))
        la = jnp.dot(xwa.astype(bf16), a2_ref[...], preferred_element_type=f32)
        nat_ref[NAT_A, pl.ds(r0, chunk), :] = jax.nn.sigmoid(a0_ref[...] + la)
        go_ref[pl.ds(r0, chunk), :] = jnp.dot(jax.nn.sigmoid(xg).astype(bf16), g2_ref[...],
                                              preferred_element_type=f32)
        return 0

    if chains_are_heads:
        lax.fori_loop(0, rows // chunk, body, 0)
        def tstep(t, _):
            r0 = pl.multiple_of(t * batch, batch)
            for hp in range(D // LANES):
                trs = [nat_ref[q, pl.ds(r0, batch), hp * LANES:(hp + 1) * LANES].T for q in range(5)]
                for h2 in range(2):
                    h = 2 * hp + h2
                    _wkv_prep(tiles_ref, rows_ref, t, h, *[tr[h2 * HD:(h2 + 1) * HD] for tr in trs],
                              lambda i, h=h: c_ref[h, i])
            return 0

        lax.fori_loop(0, rows // batch, tstep, 0)
    else:
        half = LANES // 2
        lane = lax.broadcasted_iota(jnp.int32, (HD, LANES), 1)

        def load_pair(q, tp):
            zz = jnp.concatenate(
                [nat_ref[q, pl.ds(_aligned(tp * 2 * SUBLANES + tt * SUBLANES, SUBLANES), SUBLANES),
                         hp * LANES:(hp + 1) * LANES] for tt in range(2) for hp in range(D // LANES)], axis=0)
            tr = zz.T
            top, bot = tr[0:HD], tr[HD:2 * HD]
            d0 = jnp.where(lane < half, top, pltpu.roll(bot, half, axis=1))
            d1 = jnp.where(lane < half, pltpu.roll(top, half, axis=1), bot)
            return d0, d1

        def pair(tp, _):
            pairs = [load_pair(q, tp) for q in range(5)]
            for tt in range(2):
                _wkv_prep(tiles_ref, rows_ref, 2 * tp + tt, 0, *[p[tt] for p in pairs], lambda i: c_ref[0, i])
            return 0

        n_chunks = rows // chunk
        pairs_per_chunk = chunk // (2 * SUBLANES)
        body(0, 0)
        for c in range(1, n_chunks):
            body(c, 0)
            for tp in range((c - 1) * pairs_per_chunk, c * pairs_per_chunk):
                pair(tp, 0)
        for tp in range((n_chunks - 1) * pairs_per_chunk, n_chunks * pairs_per_chunk):
            pair(tp, 0)

    ps_ref[0:batch, :] = ps_ref[rows:rows + batch, :]


def _rwkv_pre(proj, row_off, n_rows, batch, rows_blk, groups, shift_init, consts, mu, w0, a0, w2p, a2p, g2p,
              dst=None):
    ob = row_off // rows_blk
    width = 3 * D + LORA_PAD
    steps_blk = rows_blk // batch
    kern = functools.partial(_rwkv_pre_kernel, batch=batch, chunk=128, chains_are_heads=groups > 1)
    n_in = 12
    dst_spec, dst_arg, out_g = _shared_rows(dst, proj.shape[0])
    return pl.pallas_call(
        kern if dst is None else _with_dst(kern, n_in),
        grid=(n_rows // rows_blk,),
        in_specs=[
            pl.BlockSpec((rows_blk, D), lambda i: (ob + i, COL_R)),
            pl.BlockSpec((rows_blk, D), lambda i: (ob + i, COL_K)),
            pl.BlockSpec((rows_blk, D), lambda i: (ob + i, COL_V)),
            pl.BlockSpec((rows_blk, LORA_PAD), lambda i: (ob + i, COL_LORA)),
            _full((batch, width)), _full((1, width)), _full((1, D)), _full((1, D)),
            _full((LANES, D)), _full((LANES, D)), _full((2 * LANES, D)), _full((groups, 5, HD, LANES)),
        ] + dst_spec,
        out_specs=[pl.BlockSpec((steps_blk, groups, 6, HD, LANES), lambda i: (i, 0, 0, 0, 0)),
                   pl.BlockSpec((steps_blk, groups, SUBLANES, LANES), lambda i: (i, 0, 0, 0)),
                   pl.BlockSpec((rows_blk, D), lambda i: (ob + i, 0))],
        out_shape=[jax.ShapeDtypeStruct((n_rows // batch, groups, 6, HD, LANES), f32),
                   jax.ShapeDtypeStruct((n_rows // batch, groups, SUBLANES, LANES), f32),
                   out_g],
        input_output_aliases={} if dst is None else {n_in: 2},
        scratch_shapes=[pltpu.VMEM((rows_blk + batch, width), f32), pltpu.VMEM((5, rows_blk, D), f32)],
        compiler_params=pltpu.CompilerParams(dimension_semantics=("arbitrary",), vmem_limit_bytes=56 * MIB),
        name="rwkv_pre",
    )(proj, proj, proj, proj, shift_init, mu, w0, a0, w2p, a2p, g2p, consts, *dst_arg)


STATE_ROWS = HD * HD
KEY_UNROLL = 32


def _wkv_step(s_ref, tiles_ref, rows_ref, y_ref, t):
    def key_row(tile, j):
        return tiles_ref[t, 0, tile, pl.ds(j, 1), :]

    def reduce_keys(j, acc):
        sa, y0 = acc
        s = s_ref[pl.ds(pl.multiple_of(j * HD, HD), HD), :]
        return sa + s * key_row(TILE_ALPHA, j), y0 + s * key_row(TILE_WR, j)

    zero = jnp.zeros((HD, LANES), f32)
    sa, y0 = lax.fori_loop(0, HD, reduce_keys, (zero, zero), unroll=KEY_UNROLL)
    v = tiles_ref[t, 0, TILE_V]

    def update_keys(j, _):
        rows = pl.ds(pl.multiple_of(j * HD, HD), HD)
        s_ref[rows, :] = (s_ref[rows, :] * key_row(TILE_W, j) + sa * key_row(TILE_BETA, j)
                          + v * key_row(TILE_KP, j))
        return 0

    lax.fori_loop(0, HD, update_keys, 0, unroll=KEY_UNROLL)
    y_ref[t, 0] = y0 + sa * rows_ref[t, 0, ROW_BR:ROW_BR + 1, :] + v * rows_ref[t, 0, ROW_KR:ROW_KR + 1, :]


def _wkv_scan_kernel(tiles_ref, rows_ref, st_ref, y_ref, sfin_ref, s_ref, *, steps):
    tb = pl.program_id(1)

    def value_rows(i):
        return pl.ds(i, HD, stride=HD)

    @pl.when(tb == 0)
    def _():
        def init(c, _):
            tr = st_ref[:, pl.ds(pl.multiple_of(c * LANES, LANES), LANES)].T
            for i2 in range(2):
                s_ref[value_rows(2 * c + i2), :] = tr[i2 * HD:(i2 + 1) * HD]
            return 0

        lax.fori_loop(0, STATE_ROWS // LANES, init, 0, unroll=8)

    def step(t, _):
        _wkv_step(s_ref, tiles_ref, rows_ref, y_ref, t)
        return 0

    lax.fori_loop(0, steps, step, 0)

    @pl.when(tb == pl.num_programs(1) - 1)
    def _():
        def fin(c, _):
            pair = jnp.concatenate([s_ref[value_rows(2 * c + i2), :] for i2 in range(2)], axis=0)
            sfin_ref[:, pl.ds(pl.multiple_of(c * LANES, LANES), LANES)] = pair.T
            return 0

        lax.fori_loop(0, STATE_ROWS // LANES, fin, 0, unroll=8)


def _wkv_scan(tiles, rows, state, steps_blk):
    n_steps, groups = tiles.shape[0], tiles.shape[1]
    kern = functools.partial(_wkv_scan_kernel, steps=steps_blk)
    sblk = pl.BlockSpec((LANES, STATE_ROWS), lambda g, i: (0, g))
    return pl.pallas_call(
        kern,
        grid=(groups, n_steps // steps_blk),
        in_specs=[pl.BlockSpec((steps_blk, 1, 6, HD, LANES), lambda g, i: (i, g, 0, 0, 0)),
                  pl.BlockSpec((steps_blk, 1, SUBLANES, LANES), lambda g, i: (i, g, 0, 0)),
                  sblk],
        out_specs=[pl.BlockSpec((steps_blk, 1, HD, LANES), lambda g, i: (i, g, 0, 0)), sblk],
        out_shape=[jax.ShapeDtypeStruct((n_steps, groups, HD, LANES), f32),
                   jax.ShapeDtypeStruct((LANES, groups * STATE_ROWS), f32)],
        scratch_shapes=[pltpu.VMEM((STATE_ROWS, LANES), f32)],
        compiler_params=pltpu.CompilerParams(dimension_semantics=("arbitrary", "arbitrary"),
                                             vmem_limit_bytes=48 * MIB),
        name="wkv_scan",
    )(tiles, rows, state)


def _wkv_norm(y, v, bonus, lnw, lnb):
    mean = jnp.sum(y, axis=0, keepdims=True) * (1.0 / HD)
    d = y - mean
    var = jnp.sum(d * d, axis=0, keepdims=True) * (1.0 / HD)
    return d * lax.rsqrt(var + GN_EPS) * lnw + lnb + bonus * v


def _wkv_post_kernel(y_ref, v_ref, rows_ref, c_ref, o_ref, *, batch, chains_are_heads):
    steps = y_ref.shape[0]

    def normed(t, g):
        return _wkv_norm(y_ref[t, g], v_ref[t, g, 0], rows_ref[t, g, ROW_BONUS:ROW_BONUS + 1, :],
                         c_ref[g, CONST_LNW], c_ref[g, CONST_LNB])

    if chains_are_heads:
        def tstep(t, _):
            r0 = pl.multiple_of(t * batch, batch)
            for hp in range(D // LANES):
                two = jnp.concatenate([normed(t, 2 * hp), normed(t, 2 * hp + 1)], axis=0)
                o_ref[pl.ds(r0, batch), hp * LANES:(hp + 1) * LANES] = two.T
            return 0

        lax.fori_loop(0, steps, tstep, 0)
    else:
        half = LANES // 2
        lane = lax.broadcasted_iota(jnp.int32, (HD, LANES), 1)

        def pair(tp, _):
            o0 = normed(2 * tp, 0)
            o1 = normed(2 * tp + 1, 0)
            top = jnp.where(lane < half, o0, pltpu.roll(o1, half, axis=1))
            bot = jnp.where(lane < half, pltpu.roll(o0, half, axis=1), o1)
            zz = jnp.concatenate([top, bot], axis=0).T
            for tt in range(2):
                row = pl.multiple_of(tp * 2 * SUBLANES + tt * SUBLANES, SUBLANES)
                for hp in range(D // LANES):
                    src = tt * HD + hp * SUBLANES
                    o_ref[pl.ds(row, SUBLANES), hp * LANES:(hp + 1) * LANES] = zz[src:src + SUBLANES, :]
            return 0

        lax.fori_loop(0, steps // 2, pair, 0, unroll=2)


def _outproj_kernel(ml_ref, g_ref, gr_ref, x_ref, yp_ref, vp_ref, rp_ref, cp_ref, ysm_ref, vs_ref, rs_ref, cs_ref,
                    wo_ref, n2_ref, rh_ref, rl_ref, rb_ref,
                    x1_ref, t_ref, selt_ref, seln_ref, op_ref, *, prompt_tiles, prompt_batch, sample_batch):
    tm = x_ref.shape[0]
    i = pl.program_id(0)

    @pl.when(i < prompt_tiles)
    def _():
        _wkv_post_kernel(yp_ref, vp_ref, rp_ref, cp_ref, op_ref, batch=prompt_batch, chains_are_heads=False)

    @pl.when(i >= prompt_tiles)
    def _():
        _wkv_post_kernel(ysm_ref, vs_ref, rs_ref, cs_ref, op_ref, batch=sample_batch, chains_are_heads=True)

    merged = ml_ref[...] + jax.nn.sigmoid(gr_ref[...]) * (op_ref[...] * g_ref[...])
    x1 = x_ref[...] + jnp.dot(merged.astype(bf16), wo_ref[...], preferred_element_type=f32)
    x1_ref[...] = x1
    t = x1 * lax.rsqrt(jnp.mean(x1 * x1, axis=-1, keepdims=True) + NORM_EPS) * n2_ref[...]
    th = t.astype(bf16)
    t_ref[...] = th
    tl = (t - th.astype(f32)).astype(bf16)
    nt_dims = (((1,), (1,)), ((), ()))
    lg = (lax.dot_general(rh_ref[...], th, nt_dims, preferred_element_type=f32)
          + lax.dot_general(rh_ref[...], tl, nt_dims, preferred_element_type=f32)
          + lax.dot_general(rl_ref[...], th, nt_dims, preferred_element_type=f32)) + rb_ref[...]
    row = lax.broadcasted_iota(jnp.int32, (EPG, tm), 0).astype(f32)
    neg = jnp.float32(-jnp.inf)
    glog = jnp.where(row < N_GROUPS, lg[0:EPG], neg)
    ge = jnp.exp(glog - jnp.max(glog, axis=0, keepdims=True))
    pg = ge / jnp.sum(ge, axis=0, keepdims=True)
    p_top = jnp.max(pg, axis=0, keepdims=True)
    g_idx = jnp.min(jnp.where(pg == p_top, row, EPG), axis=0, keepdims=True)
    le = jnp.zeros((EPG, tm), f32)
    for g in range(N_GROUPS):
        le = jnp.where(g_idx == g, lg[EPG * (g + 1):EPG * (g + 2)], le)
    qe = jnp.exp(le - jnp.max(le, axis=0, keepdims=True))
    q = qe / jnp.sum(qe, axis=0, keepdims=True)
    q1 = jnp.max(q, axis=0, keepdims=True)
    i1 = jnp.min(jnp.where(q == q1, row, EPG), axis=0, keepdims=True)
    qm = jnp.where(row == i1, -1.0, q)
    q2 = jnp.max(qm, axis=0, keepdims=True)
    i2 = jnp.min(jnp.where(qm == q2, row, EPG), axis=0, keepdims=True)
    qs = q1 + q2
    sel = jnp.concatenate([g_idx * EPG + i1, g_idx * EPG + i2, q1 / qs * p_top, q2 / qs * p_top,
                           jnp.zeros((SUBLANES - 4, tm), f32)], axis=0)
    selt_ref[...] = sel
    sel_pad = jnp.concatenate([sel, jnp.zeros((LANES - SUBLANES, tm), f32)], axis=0)
    for c in range(tm // LANES):
        seln_ref[c * LANES:(c + 1) * LANES, :] = sel_pad[:, c * LANES:(c + 1) * LANES].T


def _outproj(ml, g, proj, x, scan_p, scan_s, wo, n2, rh, rl, rb, prompt_batch, sample_batch, tm=512):
    n = x.shape[0]
    y_p, tiles_p, rows_p, consts_p = scan_p
    y_s, tiles_s, rows_s, consts_s = scan_s
    n_p = y_p.shape[0] * prompt_batch
    sp, ss = tm // prompt_batch, tm // sample_batch
    prompt_tiles = n_p // tm
    sample_tiles = (n - n_p) // tm
    assert n_p % tm == 0 and sp % 2 == 0 and y_s.shape[0] == sample_tiles * ss
    pidx = lambda i: jnp.minimum(i, prompt_tiles - 1)
    sidx = lambda i: jnp.clip(i - prompt_tiles, 0, sample_tiles - 1)
    groups = y_s.shape[1]
    blk = pl.BlockSpec((tm, D), lambda i: (i, 0))
    kern = functools.partial(_outproj_kernel, prompt_tiles=prompt_tiles, prompt_batch=prompt_batch,
                             sample_batch=sample_batch)
    return pl.pallas_call(
        kern,
        grid=(n // tm,),
        in_specs=[blk, blk, pl.BlockSpec((tm, D), lambda i: (i, COL_GR)), blk,
                  pl.BlockSpec((sp, 1, HD, LANES), lambda i: (pidx(i), 0, 0, 0)),
                  pl.BlockSpec((sp, 1, 1, HD, LANES), lambda i: (pidx(i), 0, TILE_V, 0, 0)),
                  pl.BlockSpec((sp, 1, SUBLANES, LANES), lambda i: (pidx(i), 0, 0, 0)),
                  _full((1, 5, HD, LANES)),
                  pl.BlockSpec((ss, groups, HD, LANES), lambda i: (sidx(i), 0, 0, 0)),
                  pl.BlockSpec((ss, groups, 1, HD, LANES), lambda i: (sidx(i), 0, TILE_V, 0, 0)),
                  pl.BlockSpec((ss, groups, SUBLANES, LANES), lambda i: (sidx(i), 0, 0, 0)),
                  _full((groups, 5, HD, LANES)),
                  _full((D, D)), _full((1, D)), _full((ROUTER_ROWS, D)), _full((ROUTER_ROWS, D)),
                  _full((ROUTER_ROWS, 1))],
        out_specs=[blk, blk, pl.BlockSpec((SUBLANES, tm), lambda i: (0, i)),
                   pl.BlockSpec((tm, LANES), lambda i: (i, 0))],
        out_shape=[jax.ShapeDtypeStruct((n, D), f32), jax.ShapeDtypeStruct((n, D), bf16),
                   jax.ShapeDtypeStruct((SUBLANES, n), f32), jax.ShapeDtypeStruct((n, LANES), f32)],
        scratch_shapes=[pltpu.VMEM((tm, D), f32)],
        compiler_params=pltpu.CompilerParams(dimension_semantics=("arbitrary",), vmem_limit_bytes=56 * MIB),
        name="outproj_router",
    )(ml, g, proj, x, y_p, tiles_p, rows_p, consts_p, y_s, tiles_s, rows_s, consts_s, wo, n2, rh, rl, rb)


MOE_CHUNK = 32
MOE_ROWS = 128
MOE_GATHER = 512
MOE_EXPERTS_PER_STEP = 4
MOE_OVERRUN = MOE_ROWS
MOE_CAP = -(-(2 * TOKEN_TILE + N_EXPERTS * (MOE_CHUNK - 1) + MOE_OVERRUN) // MOE_GATHER) * MOE_GATHER
SEL_E1, SEL_E2, SEL_C1, SEL_C2 = range(4)


def _route_meta_kernel(seln_ref, o_ref):
    tm = seln_ref.shape[0]
    lane = lax.broadcasted_iota(jnp.int32, (tm, LANES), 1).astype(f32)
    sel = seln_ref[...]
    hit = (lane == sel[:, SEL_E1:SEL_E1 + 1]) | (lane == sel[:, SEL_E2:SEL_E2 + 1])
    cnt = jnp.sum(jnp.where(hit, 1.0, 0.0), axis=0, keepdims=True)
    nchunk = jnp.floor((cnt + (MOE_CHUNK - 1)) * (1.0 / MOE_CHUNK))
    upper = (lax.broadcasted_iota(jnp.int32, (LANES, LANES), 0)
             < lax.broadcasted_iota(jnp.int32, (LANES, LANES), 1))
    offs = jnp.dot(jnp.broadcast_to(nchunk, (SUBLANES, LANES)).astype(bf16), jnp.where(upper, 1.0, 0.0).astype(bf16),
                   preferred_element_type=f32)
    row = lax.broadcasted_iota(jnp.int32, (SUBLANES, LANES), 0)
    o_ref[0] = jnp.where(row == 0, nchunk, jnp.where(row == 1, offs, 0.0)).astype(jnp.int32)


def _route_meta(seln, tm):
    tiles = seln.shape[0] // tm
    return pl.pallas_call(
        _route_meta_kernel,
        grid=(tiles,),
        in_specs=[pl.BlockSpec((tm, LANES), lambda i: (i, 0))],
        out_specs=pl.BlockSpec((1, SUBLANES, LANES), lambda i: (i, 0, 0)),
        out_shape=jax.ShapeDtypeStruct((tiles, SUBLANES, LANES), jnp.int32),
        compiler_params=pltpu.CompilerParams(dimension_semantics=("arbitrary",)),
        name="route_meta",
    )(seln)


def _moe_experts_kernel(nch_ref, off_ref, t_ref, selt_ref, tri_ref, wgu_ref, wd_ref, ys_ref, pos_ref,
                        xs_ref, cw_ref):
    i = pl.program_id(0)
    e = pl.program_id(1)
    tm = t_ref.shape[0]
    used_rows = (off_ref[i * N_EXPERTS + N_EXPERTS - 1] + nch_ref[i * N_EXPERTS + N_EXPERTS - 1]) * MOE_CHUNK

    @pl.when(e == 0)
    def _():
        ys_ref[...] = jnp.zeros_like(ys_ref)
        e1 = selt_ref[SEL_E1:SEL_E1 + 1, :]
        e2 = selt_ref[SEL_E2:SEL_E2 + 1, :]
        erow = lax.broadcasted_iota(jnp.int32, (N_EXPERTS, tm), 0).astype(f32)
        oh1 = jnp.where(erow == e1, 1.0, 0.0)
        oh2 = jnp.where(erow == e2, 1.0, 0.0)
        before1 = jnp.dot(oh1.astype(bf16), tri_ref[...], preferred_element_type=f32)
        before2 = jnp.dot(oh2.astype(bf16), tri_ref[...], preferred_element_type=f32)
        cnt1 = jnp.sum(oh1, axis=1, keepdims=True)
        cnt2 = jnp.sum(oh2, axis=1, keepdims=True)
        nchunk = jnp.floor((cnt1 + cnt2 + (MOE_CHUNK - 1)) * (1.0 / MOE_CHUNK))
        lower = (lax.broadcasted_iota(jnp.int32, (N_EXPERTS, LANES), 1)
                 < lax.broadcasted_iota(jnp.int32, (N_EXPERTS, LANES), 0))
        nchunk_rows = jnp.concatenate([jnp.broadcast_to(nchunk, (N_EXPERTS, LANES)),
                                       jnp.zeros((LANES - N_EXPERTS, LANES), f32)], axis=0)
        start = jnp.dot(jnp.where(lower, 1.0, 0.0).astype(bf16), nchunk_rows.astype(bf16),
                        preferred_element_type=f32)[:, 0:1] * MOE_CHUNK
        pos1 = jnp.sum(oh1 * (start + before1), axis=0, keepdims=True)
        pos2 = jnp.sum(oh2 * (start + cnt1 + before2), axis=0, keepdims=True)
        pos_ref[...] = jnp.concatenate([pos1, pos2, jnp.zeros((SUBLANES - 2, tm), f32)], axis=0)
        c1 = selt_ref[SEL_C1:SEL_C1 + 1, :]
        c2 = selt_ref[SEL_C2:SEL_C2 + 1, :]
        for k in range(MOE_CAP // MOE_GATHER):
            @pl.when(k * MOE_GATHER < used_rows + MOE_OVERRUN)
            def _(k=k):
                ridx = (lax.broadcasted_iota(jnp.int32, (MOE_GATHER, tm), 0) + k * MOE_GATHER).astype(f32)
                p1 = ridx == pos1
                p2 = ridx == pos2
                onehot = jnp.where(p1 | p2, 1.0, 0.0).astype(bf16)
                xs_ref[k * MOE_GATHER:(k + 1) * MOE_GATHER, :] = jnp.dot(
                    onehot, t_ref[...], preferred_element_type=f32).astype(bf16)
                w = jnp.sum(jnp.where(p1, c1, 0.0) + jnp.where(p2, c2, 0.0), axis=1, keepdims=True)
                cw_ref[k * MOE_GATHER:(k + 1) * MOE_GATHER, :] = jnp.broadcast_to(w, (MOE_GATHER, LANES))

    def expert_rows(u, start):
        rows = pl.ds(pl.multiple_of(start, MOE_CHUNK), MOE_ROWS)
        gu = jnp.dot(xs_ref[rows, :], wgu_ref[u], preferred_element_type=f32)
        w = cw_ref[rows, :]
        h = jax.nn.silu(gu[:, :D_EXPERT]) * gu[:, D_EXPERT:] * jnp.concatenate([w, w], axis=1)
        ys_ref[0, rows, :] = jnp.dot(h.astype(bf16), wd_ref[u], preferred_element_type=f32).astype(bf16)

    per_block = MOE_ROWS // MOE_CHUNK
    for u in range(wgu_ref.shape[0]):
        expert = i * N_EXPERTS + e * wgu_ref.shape[0] + u
        base = off_ref[expert] * MOE_CHUNK
        expert_rows(u, base)

        def more(c, _, u=u, base=base):
            expert_rows(u, base + (c + 1) * MOE_ROWS)
            return 0

        lax.fori_loop(0, jnp.maximum(nch_ref[expert] - 1, 0) // per_block, more, 0)


def _moe_experts(nch, off, t, selt, tri, wgu, wd, tm):
    n = t.shape[0]
    tiles = n // tm
    grid_spec = pltpu.PrefetchScalarGridSpec(
        num_scalar_prefetch=2,
        grid=(tiles, N_EXPERTS // MOE_EXPERTS_PER_STEP),
        in_specs=[
            pl.BlockSpec((tm, D), lambda i, e, nch, off: (i, 0)),
            pl.BlockSpec((SUBLANES, tm), lambda i, e, nch, off: (0, i)),
            pl.BlockSpec((tm, tm), lambda i, e, nch, off: (0, 0)),
            pl.BlockSpec((MOE_EXPERTS_PER_STEP, D, 2 * D_EXPERT), lambda i, e, nch, off: (e, 0, 0)),
            pl.BlockSpec((MOE_EXPERTS_PER_STEP, D_EXPERT, D), lambda i, e, nch, off: (e, 0, 0)),
        ],
        out_specs=[pl.BlockSpec((1, MOE_CAP, D), lambda i, e, nch, off: (i, 0, 0)),
                   pl.BlockSpec((SUBLANES, tm), lambda i, e, nch, off: (0, i))],
        scratch_shapes=[pltpu.VMEM((MOE_CAP, D), bf16), pltpu.VMEM((MOE_CAP, LANES), f32)],
    )
    return pl.pallas_call(
        _moe_experts_kernel,
        grid_spec=grid_spec,
        out_shape=[jax.ShapeDtypeStruct((tiles, MOE_CAP, D), bf16), jax.ShapeDtypeStruct((SUBLANES, n), f32)],
        compiler_params=pltpu.CompilerParams(dimension_semantics=("arbitrary", "arbitrary"),
                                             vmem_limit_bytes=56 * MIB),
        name="moe_experts",
    )(nch, off, t, selt, tri, wgu, wd)


def _moe_combine_kernel(nch_ref, off_ref, ysort_ref, pos_ref, x1_ref, fn_ref, yp_ref, ys_ref, acc_ref, yf_ref,
                        *, prompt_tiles, parts):
    i = pl.program_id(0)
    part = pl.program_id(1)
    rows = x1_ref.shape[0]
    bp, tsteps = yp_ref.shape[0], yp_ref.shape[1]
    bs = ys_ref.shape[0]
    ssteps = ys_ref.shape[1] // parts
    used_rows = (off_ref[i * N_EXPERTS + N_EXPERTS - 1] + nch_ref[i * N_EXPERTS + N_EXPERTS - 1]) * MOE_CHUNK

    pos_t = jnp.concatenate([pos_ref[...], jnp.zeros((LANES - SUBLANES, rows), f32)], axis=0)
    pos_n = jnp.concatenate([pos_t[:, c * LANES:(c + 1) * LANES].T for c in range(rows // LANES)], axis=0)
    pos1 = pos_n[:, 0:1]
    pos2 = pos_n[:, 1:2]
    acc_ref[...] = x1_ref[...]
    for k in range(MOE_CAP // MOE_GATHER):
        @pl.when(k * MOE_GATHER < used_rows)
        def _(k=k):
            cidx = (lax.broadcasted_iota(jnp.int32, (rows, MOE_GATHER), 1) + k * MOE_GATHER).astype(f32)
            onehot = jnp.where((cidx == pos1) | (cidx == pos2), 1.0, 0.0).astype(bf16)
            acc_ref[...] += jnp.dot(onehot, ysort_ref[0, k * MOE_GATHER:(k + 1) * MOE_GATHER, :],
                                    preferred_element_type=f32)

    xo = acc_ref[...]
    y = xo * lax.rsqrt(jnp.mean(xo * xo, axis=-1, keepdims=True) + NORM_EPS) * fn_ref[...]
    for cb in range(D // LANES):
        yf_ref[cb] = y[:, cb * LANES:(cb + 1) * LANES]

    @pl.when(i < prompt_tiles)
    def _():
        for b in range(bp):
            yp_ref[b] = jnp.concatenate([yf_ref[cb, pl.ds(b, tsteps, stride=bp), :] for cb in range(D // LANES)],
                                        axis=1)

    for p in range(parts):
        @pl.when((i >= prompt_tiles) & (part == p))
        def _(p=p):
            for t in range(ssteps):
                ys_ref[:, p * ssteps + t, :] = jnp.concatenate(
                    [yf_ref[cb, t * bs:(t + 1) * bs, :] for cb in range(D // LANES)], axis=1)


def _moe_combine(nch, off, ysort, pos, x1, fn, prompt_shape, sample_shape, tm, parts=2):
    n = x1.shape[0]
    bp, tp, _ = prompt_shape
    bs, ts, _ = sample_shape
    rows = tm // parts
    assert rows % bp == 0 and tp % (rows // bp) == 0 and bs * ts == tm and ts % parts == 0
    prompt_tiles = bp * tp // tm
    last_prompt_blk = prompt_tiles * parts - 1
    kern = functools.partial(_moe_combine_kernel, prompt_tiles=prompt_tiles, parts=parts)
    grid_spec = pltpu.PrefetchScalarGridSpec(
        num_scalar_prefetch=2,
        grid=(n // tm, parts),
        in_specs=[
            pl.BlockSpec((1, MOE_CAP, D), lambda i, p, nch, off: (i, 0, 0)),
            pl.BlockSpec((SUBLANES, rows), lambda i, p, nch, off: (0, i * parts + p)),
            pl.BlockSpec((rows, D), lambda i, p, nch, off: (i * parts + p, 0)),
            pl.BlockSpec((1, D), lambda i, p, nch, off: (0, 0)),
        ],
        out_specs=[pl.BlockSpec((bp, rows // bp, D),
                                lambda i, p, nch, off: (0, jnp.minimum(i * parts + p, last_prompt_blk), 0)),
                   pl.BlockSpec((bs, ts, D), lambda i, p, nch, off: (0, 0, 0))],
        scratch_shapes=[pltpu.VMEM((rows, D), f32), pltpu.VMEM((D // LANES, rows, LANES), f32)],
    )
    return pl.pallas_call(
        kern,
        grid_spec=grid_spec,
        out_shape=[jax.ShapeDtypeStruct(prompt_shape, f32), jax.ShapeDtypeStruct(sample_shape, f32)],
        compiler_params=pltpu.CompilerParams(dimension_semantics=("arbitrary", "arbitrary"),
                                             vmem_limit_bytes=56 * MIB),
        name="moe_combine",
    )(nch, off, ysort, pos, x1, fn)


def _chain_tiles_prompt(vec, batch):
    t = vec.reshape(HEADS // 2, 2, HD)
    t = jnp.transpose(t, (2, 1, 0))
    return jnp.broadcast_to(t[..., None], (HD, 2, HEADS // 2, batch)).reshape(HD, LANES)


def _chain_tiles_sample(vec):
    return jnp.broadcast_to(vec.reshape(HEADS, HD)[..., None], (HEADS, HD, LANES))


def kernel(x_prompt, x_sample, state_conv, state_lru, state_shift, state_wkv, norm1, w_in, conv_w, conv_b, lru_wx, lru_bx, lru_wa, lru_ba, lru_a_param, rwkv_mu, rwkv_w0, rwkv_w2, rwkv_a0, rwkv_a2, rwkv_g2, rwkv_k_k, rwkv_k_a, rwkv_r_k, rwkv_ln_w, rwkv_ln_b, w_out, norm2, router_group, router_group_b, router_expert, router_expert_b, exp_gate, exp_up, exp_down, final_norm):
    bp, tp, _ = x_prompt.shape
    bs, ts, _ = x_sample.shape
    assert norm1.shape[0] == 1 and bp * HEADS == LANES and bs == LANES and tp % 64 == 0 and ts % 2 == 0
    n_p, n_s = bp * tp, bs * ts
    c_rw = 2 * D
    c_lora = c_rw + 3 * D
    c_gl = c_lora + LORA

    w = w_in[0]
    w_all = jnp.concatenate([w[:, :c_lora], w[:, c_gl:], w[:, c_lora:c_gl],
                             jnp.zeros((D, LORA_PAD - LORA), f32)], axis=1).astype(bf16)
    mu = rwkv_mu[0]
    mu_all = jnp.concatenate([mu[:3 * D], mu[3 * D:], jnp.zeros((LORA_PAD - LORA,), f32)])[None]
    w2p = jnp.concatenate([rwkv_w2[0], jnp.zeros((LANES - LORA_W, D), f32)]).astype(bf16)
    a2p = jnp.concatenate([jnp.zeros((LORA_W, D), f32), rwkv_a2[0]]).astype(bf16)
    g2p = jnp.concatenate([rwkv_g2[0], jnp.zeros((2 * LANES - LORA_G, D), f32)]).astype(bf16)
    wxa = jnp.concatenate([lru_wx[0], lru_wa[0]], axis=-1).astype(bf16)
    row = lambda v: v.reshape(1, -1)
    rk = rwkv_r_k[0].reshape(D)
    cvecs = (rwkv_k_k[0], rwkv_k_a[0], rk, rwkv_ln_w[0], rwkv_ln_b[0])
    consts_p = jnp.stack([_chain_tiles_prompt(v, bp) for v in cvecs])[None]
    consts_s = jnp.stack([_chain_tiles_sample(v) for v in cvecs], axis=1)
    rw = jnp.zeros((ROUTER_ROWS, D), f32)
    rw = rw.at[0:N_GROUPS].set(router_group[0].T).at[EPG:EPG + N_EXPERTS].set(router_expert[0].T)
    rh = rw.astype(bf16)
    rl = (rw - rh.astype(f32)).astype(bf16)
    rb = jnp.zeros((ROUTER_ROWS, 1), f32)
    rb = rb.at[0:N_GROUPS, 0].set(router_group_b[0]).at[EPG:EPG + N_EXPERTS, 0].set(router_expert_b[0])
    wgu = jnp.concatenate([exp_gate[0], exp_up[0]], axis=-1).astype(bf16)
    wd = exp_down[0].astype(bf16)
    wo = w_out[0].astype(bf16)

    proj, x = _inproj(x_prompt, x_sample, row(norm1[0]), w_all, tm=TOKEN_TILE)

    lru_args = (conv_w[0], row(conv_b[0]), wxa, row(lru_bx[0]), row(lru_ba[0]), row(lru_a_param[0]))
    conv_init_s = jnp.transpose(state_conv[0], (1, 0, 2)).reshape(3 * bs, D)
    ml, hl_p = _lru(proj, 0, n_p, bp, 512, True, jnp.zeros((3 * bp, D), f32), jnp.zeros((bp, D), f32), *lru_args)
    ml, hl_s = _lru(proj, n_p, n_s, bs, 512, False, conv_init_s, state_lru[0], *lru_args, dst=ml)

    sh = state_shift[0]
    shift_init_s = jnp.concatenate([sh, jnp.zeros((bs, LORA_PAD - LORA), f32)], axis=1)
    pre_args = (mu_all, row(rwkv_w0[0]), row(rwkv_a0[0]), w2p, a2p, g2p)
    tiles_p, rows_p, g = _rwkv_pre(proj, 0, n_p, bp, 256, 1, jnp.zeros((bp, 3 * D + LORA_PAD), f32),
                                   consts_p, *pre_args)
    tiles_s, rows_s, g = _rwkv_pre(proj, n_p, n_s, bs, 256, HEADS, shift_init_s, consts_s, *pre_args, dst=g)

    y_p, sfin_p = _wkv_scan(tiles_p, rows_p, jnp.zeros((LANES, STATE_ROWS), f32), 64)
    y_s, sfin_s = _wkv_scan(tiles_s, rows_s, state_wkv[0].reshape(bs, HEADS * STATE_ROWS), ts)

    x1, t, selt, seln = _outproj(ml, g, proj, x, (y_p, tiles_p, rows_p, consts_p), (y_s, tiles_s, rows_s, consts_s),
                                 wo, row(norm2[0]), rh, rl, rb, bp, bs)
    meta = _route_meta(seln, TOKEN_TILE)
    nch = meta[:, 0, :N_EXPERTS].reshape(-1)
    off = meta[:, 1, :N_EXPERTS].reshape(-1)
    tri = jnp.triu(jnp.ones((TOKEN_TILE, TOKEN_TILE), bf16), k=1)
    ysort, pos = _moe_experts(nch, off, t, selt, tri, wgu, wd, TOKEN_TILE)
    y_prompt, y_sample = _moe_combine(nch, off, ysort, pos, x1, row(final_norm), x_prompt.shape, x_sample.shape,
                                      TOKEN_TILE)

    def last_rows(lo, hi, n_end, batch, steps):
        return proj[n_end - steps * batch:n_end, lo:hi].reshape(steps, batch, hi - lo)

    conv_p = jnp.transpose(last_rows(0, D, n_p, bp, 3), (1, 0, 2))[None]
    conv_s = jnp.transpose(last_rows(0, D, n_p + n_s, bs, 3), (1, 0, 2))[None]

    def shift_rows(n_end, batch):
        return jnp.concatenate([last_rows(2 * D, 5 * D, n_end, batch, 1)[0],
                                last_rows(7 * D, 7 * D + LORA, n_end, batch, 1)[0]], axis=1)[None]

    wkv_p = jnp.transpose(sfin_p.reshape(2, HEADS // 2, bp, HD, HD), (2, 1, 0, 3, 4)).reshape(1, bp, HEADS, HD, HD)
    wkv_s = sfin_s.reshape(1, bs, HEADS, HD, HD)
    return (y_prompt, y_sample, conv_p, hl_p[None], shift_rows(n_p, bp), wkv_p,
            conv_s, hl_s[None], shift_rows(n_p + n_s, bs), wkv_s)
```

```python
import functools

import jax
import jax.numpy as jnp
from jax import lax
from jax.experimental import pallas as pl
from jax.experimental.pallas import tpu as pltpu

f32 = jnp.float32
bf16 = jnp.bfloat16

D = 1024
HEADS = 16
HD = 64
LANES = 128
SUBLANES = 8
LORA_W = 64
LORA_A = 64
LORA_G = 160
LORA = LORA_W + LORA_A + LORA_G
LORA_PAD = 512
N_GROUPS = 4
EPG = 8
N_EXPERTS = N_GROUPS * EPG
D_EXPERT = 256
LRU_C = 8.0
GN_EPS = 64e-5
NORM_EPS = 1e-6
PROJ_COLS = 7 * D + LORA_PAD
COL_LRU_X, COL_LRU_Y, COL_R, COL_K, COL_V, COL_GL, COL_GR = range(7)
COL_LORA = 7 * D // LORA_PAD
ROUTER_ROWS = 48
MIB = 1024 * 1024
TOKEN_TILE = 1024


def _aligned(start, multiple):
    return start if isinstance(start, int) else pl.multiple_of(start, multiple)


def _full(shape, grid_rank=1):
    zeros = tuple(0 for _ in shape)
    if grid_rank == 1:
        return pl.BlockSpec(shape, lambda i: zeros)
    return pl.BlockSpec(shape, lambda i, j: zeros)


def _inproj_kernel(xp_ref, xs_ref, g_ref, w_ref, o_ref, xo_ref, xf_ref, xn_ref, *, prompt_tiles):
    i = pl.program_id(0)
    first_col = pl.program_id(1) == 0
    bp, tsteps = xp_ref.shape[0], xp_ref.shape[1]
    bs, ssteps = xs_ref.shape[0], xs_ref.shape[1]

    @pl.when(first_col & (i < prompt_tiles))
    def _():
        for b in range(bp):
            v = xp_ref[b]
            for cb in range(D // LANES):
                xf_ref[cb, pl.ds(b, tsteps, stride=bp), :] = v[:, cb * LANES:(cb + 1) * LANES]

    @pl.when(first_col & (i >= prompt_tiles))
    def _():
        for t in range(ssteps):
            v = xs_ref[:, t, :]
            for cb in range(D // LANES):
                xf_ref[cb, t * bs:(t + 1) * bs, :] = v[:, cb * LANES:(cb + 1) * LANES]

    @pl.when(first_col)
    def _():
        x = jnp.concatenate([xf_ref[cb] for cb in range(D // LANES)], axis=1)
        xo_ref[...] = x
        ms = jnp.mean(x * x, axis=-1, keepdims=True)
        xn_ref[...] = (x * lax.rsqrt(ms + NORM_EPS) * g_ref[...]).astype(bf16)

    o_ref[...] = jnp.dot(xn_ref[...], w_ref[...], preferred_element_type=f32)


def _inproj(x_prompt, x_sample, gain, w, tm, tn=1280):
    bp, tp, _ = x_prompt.shape
    bs, ts, _ = x_sample.shape
    assert tm % bp == 0 and tp % (tm // bp) == 0 and bs * ts == tm
    prompt_tiles = bp * tp // tm
    n = bp * tp + bs * ts
    kern = functools.partial(_inproj_kernel, prompt_tiles=prompt_tiles)
    return pl.pallas_call(
        kern,
        grid=(n // tm, PROJ_COLS // tn),
        in_specs=[
            pl.BlockSpec((bp, tm // bp, D), lambda i, j: (0, jnp.minimum(i, prompt_tiles - 1), 0)),
            pl.BlockSpec((bs, ts, D), lambda i, j: (0, 0, 0)),
            pl.BlockSpec((1, D), lambda i, j: (0, 0)),
            pl.BlockSpec((D, tn), lambda i, j: (0, j)),
        ],
        out_specs=[pl.BlockSpec((tm, tn), lambda i, j: (i, j)), pl.BlockSpec((tm, D), lambda i, j: (i, 0))],
        out_shape=[jax.ShapeDtypeStruct((n, PROJ_COLS), f32), jax.ShapeDtypeStruct((n, D), f32)],
        scratch_shapes=[pltpu.VMEM((D // LANES, tm, LANES), f32), pltpu.VMEM((tm, D), bf16)],
        compiler_params=pltpu.CompilerParams(dimension_semantics=("arbitrary", "arbitrary"),
                                             vmem_limit_bytes=56 * MIB),
        name="inproj",
    )(x_prompt, x_sample, gain, w)


def _lru_kernel(x_ref, y_ref, gl_ref, cinit_ref, h0_ref, cw_ref, cb_ref, wxa_ref, bx_ref, ba_ref, ap_ref,
                o_ref, hl_ref, xs_ref, a_ref, b_ref, h_ref, *, batch, reset_first, chunk):
    rows = x_ref.shape[0]
    nt = rows // batch
    hist = 3 * batch
    pid = pl.program_id(0)

    @pl.when(pid == 0)
    def _():
        xs_ref[0:hist, :] = cinit_ref[...]
        h_ref[...] = h0_ref[...]

    xs_ref[hist:hist + rows, :] = x_ref[...]
    logsig = -jax.nn.softplus(-ap_ref[...])

    def gates(c, _):
        r0 = pl.multiple_of(c * chunk, chunk)
        xc = (cb_ref[...]
              + cw_ref[3:4, :] * xs_ref[pl.ds(pl.multiple_of(r0 + hist, SUBLANES), chunk), :]
              + cw_ref[2:3, :] * xs_ref[pl.ds(pl.multiple_of(r0 + 2 * batch, SUBLANES), chunk), :]
              + cw_ref[1:2, :] * xs_ref[pl.ds(pl.multiple_of(r0 + batch, SUBLANES), chunk), :]
              + cw_ref[0:1, :] * xs_ref[pl.ds(r0, chunk), :])
        if reset_first:
            grow = lax.broadcasted_iota(jnp.int32, (chunk, LANES), 0) + (r0 + pid * rows)
            first = grow < batch
        for n in range(D // LANES):
            sl = slice(n * LANES, (n + 1) * LANES)
            xn = xc[:, sl]
            g2 = jnp.dot(xn.astype(bf16), wxa_ref[n], preferred_element_type=f32)
            gate_x = jax.nn.sigmoid(g2[:, :LANES] + bx_ref[:, sl])
            gate_a = jax.nn.sigmoid(g2[:, LANES:] + ba_ref[:, sl])
            log_a = LRU_C * gate_a * logsig[:, sl]
            a = jnp.exp(log_a)
            mult = jnp.sqrt(-jnp.tanh(log_a) * (a * a + 1.0))
            if reset_first:
                mult = jnp.where(first, 1.0, mult)
            a_ref[pl.ds(r0, chunk), sl] = a
            b_ref[pl.ds(r0, chunk), sl] = xn * gate_x * mult
        return 0

    lax.fori_loop(0, rows // chunk, gates, 0)

    def scan(t, h):
        r0 = pl.multiple_of(t * batch, batch)
        h = a_ref[pl.ds(r0, batch), :] * h + b_ref[pl.ds(r0, batch), :]
        b_ref[pl.ds(r0, batch), :] = h
        return h

    h = lax.fori_loop(0, nt, scan, h_ref[...], unroll=(8 if nt >= 8 and batch == SUBLANES else 1))
    h_ref[...] = h
    hl_ref[...] = h
    xs_ref[0:hist, :] = xs_ref[rows:rows + hist, :]

    def outp(c, _):
        r0 = pl.multiple_of(c * chunk, chunk)
        o_ref[pl.ds(r0, chunk), :] = (b_ref[pl.ds(r0, chunk), :] * jax.nn.gelu(y_ref[pl.ds(r0, chunk), :])
                                      * jax.nn.sigmoid(gl_ref[pl.ds(r0, chunk), :]))
        return 0

    lax.fori_loop(0, rows // chunk, outp, 0)


def _with_dst(kern, n_in):
    def wrapped(*refs):
        return kern(*refs[:n_in], *refs[n_in + 1:])
    return wrapped


def _shared_rows(dst, n_total):
    if dst is None:
        return [], [], jax.ShapeDtypeStruct((n_total, D), f32)
    return [pl.BlockSpec(memory_space=pl.ANY)], [dst], jax.ShapeDtypeStruct(dst.shape, dst.dtype)


def _lru(proj, row_off, n_rows, batch, rows_blk, reset_first, conv_init, h0, cw, cb, wxa, bx, ba, ap, dst=None):
    ob = row_off // rows_blk
    kern = functools.partial(_lru_kernel, batch=batch, reset_first=reset_first, chunk=128)
    n_in = 11
    dst_spec, dst_arg, out0 = _shared_rows(dst, proj.shape[0])
    return pl.pallas_call(
        kern if dst is None else _with_dst(kern, n_in),
        grid=(n_rows // rows_blk,),
        in_specs=[
            pl.BlockSpec((rows_blk, D), lambda i: (ob + i, COL_LRU_X)),
            pl.BlockSpec((rows_blk, D), lambda i: (ob + i, COL_LRU_Y)),
            pl.BlockSpec((rows_blk, D), lambda i: (ob + i, COL_GL)),
            _full((3 * batch, D)), _full((batch, D)), _full((4, D)), _full((1, D)),
            _full((D // LANES, LANES, 2 * LANES)), _full((1, D)), _full((1, D)), _full((1, D)),
        ] + dst_spec,
        out_specs=[pl.BlockSpec((rows_blk, D), lambda i: (ob + i, 0)), _full((batch, D))],
        out_shape=[out0, jax.ShapeDtypeStruct((batch, D), f32)],
        input_output_aliases={} if dst is None else {n_in: 0},
        scratch_shapes=[pltpu.VMEM((rows_blk + 3 * batch, D), f32), pltpu.VMEM((rows_blk, D), f32),
                        pltpu.VMEM((rows_blk, D), f32), pltpu.VMEM((batch, D), f32)],
        compiler_params=pltpu.CompilerParams(dimension_semantics=("arbitrary",), vmem_limit_bytes=48 * MIB),
        name="lru",
    )(proj, proj, proj, conv_init, h0, cw, cb, wxa, bx, ba, ap, *dst_arg)


TILE_ALPHA, TILE_BETA, TILE_W, TILE_KP, TILE_WR, TILE_V = range(6)
ROW_BR, ROW_KR, ROW_BONUS = range(3)
CONST_KK, CONST_KA, CONST_RK, CONST_LNW, CONST_LNB = range(5)
NAT_R, NAT_W, NAT_K, NAT_V, NAT_A = range(5)


def _wkv_prep(tiles_ref, rows_ref, gam_ref, t, g, r, w, k, v, a, const):
    kk = k * const(CONST_KK)
    nrm = jnp.sqrt(jnp.sum(kk * kk, axis=0, keepdims=True))
    kk = kk * (1.0 / jnp.maximum(nrm, 1e-12))
    beta = kk * a
    kp = k * (1.0 + (a - 1.0) * const(CONST_KA))
    gam_prev = gam_ref[g]
    gam = gam_prev * w
    gam_ref[g] = gam
    inv = 1.0 / gam
    tiles_ref[t, g, TILE_ALPHA] = -kk * gam_prev
    tiles_ref[t, g, TILE_BETA] = beta * inv
    tiles_ref[t, g, TILE_W] = gam
    tiles_ref[t, g, TILE_KP] = kp * inv
    tiles_ref[t, g, TILE_WR] = gam * r
    tiles_ref[t, g, TILE_V] = v
    rows_ref[t, g, ROW_BR:ROW_BR + 1, :] = jnp.sum(beta * r, axis=0, keepdims=True)
    rows_ref[t, g, ROW_KR:ROW_KR + 1, :] = jnp.sum(kp * r, axis=0, keepdims=True)
    rows_ref[t, g, ROW_BONUS:ROW_BONUS + 1, :] = jnp.sum(r * kp * const(CONST_RK), axis=0, keepdims=True)


def _rwkv_pre_kernel(r_ref, k_ref, v_ref, l_ref, sinit_ref, mu_ref, w0_ref, a0_ref, w2_ref, a2_ref, g2_ref, c_ref,
                     tiles_ref, rows_ref, go_ref, ps_ref, nat_ref, gam_ref, *, batch, chunk, chains_are_heads):
    rows = r_ref.shape[0]
    gam_ref[...] = jnp.ones_like(gam_ref)

    @pl.when(pl.program_id(0) == 0)
    def _():
        ps_ref[0:batch, :] = sinit_ref[...]

    ps_ref[batch:batch + rows, 0:D] = r_ref[...]
    ps_ref[batch:batch + rows, D:2 * D] = k_ref[...]
    ps_ref[batch:batch + rows, 2 * D:3 * D] = v_ref[...]
    ps_ref[batch:batch + rows, 3 * D:3 * D + LORA_PAD] = l_ref[...]

    def body(c, _):
        r0 = _aligned(c * chunk, chunk)

        def mixed(lo, hi):
            cur = ps_ref[pl.ds(_aligned(r0 + batch, SUBLANES), chunk), lo:hi]
            prev = ps_ref[pl.ds(r0, chunk), lo:hi]
            return cur + (prev - cur) * mu_ref[:, lo:hi]

        nat_ref[NAT_R, pl.ds(r0, chunk), :] = mixed(0, D)
        nat_ref[NAT_K, pl.ds(r0, chunk), :] = mixed(D, 2 * D)
        nat_ref[NAT_V, pl.ds(r0, chunk), :] = mixed(2 * D, 3 * D)
        lm = mixed(3 * D, 3 * D + LORA_PAD)
        xwa = lm[:, 0:LANES]
        xg = lm[:, LANES:3 * LANES]
        lw = jnp.dot(jnp.tanh(xwa).astype(bf16), w2_ref[...], preferred_element_type=f32)
        w_log = -jax.nn.softplus(-(w0_ref[...] + lw)) - 0.5
        nat_ref[NAT_W, pl.ds(r0, chunk), :] = jnp.exp(-jnp.exp(w_log))
        la = jnp.dot(xwa.astype(bf16), a2_ref[...], preferred_element_type=f32)
        nat_ref[NAT_A, pl.ds(r0, chunk), :] = jax.nn.sigmoid(a0_ref[...] + la)
        go_ref[pl.ds(r0, chunk), :] = jnp.dot(jax.nn.sigmoid(xg).astype(bf16), g2_ref[...],
                                              preferred_element_type=f32)
        return 0

    if chains_are_heads:
        lax.fori_loop(0, rows // chunk, body, 0)
        def tstep(t, _):
            r0 = pl.multiple_of(t * batch, batch)
            for hp in range(D // LANES):
                trs = [nat_ref[q, pl.ds(r0, batch), hp * LANES:(hp + 1) * LANES].T for q in range(5)]
                for h2 in range(2):
                    h = 2 * hp + h2
                    _wkv_prep(tiles_ref, rows_ref, gam_ref, t, h, *[tr[h2 * HD:(h2 + 1) * HD] for tr in trs],
                              lambda i, h=h: c_ref[h, i])
            return 0

        lax.fori_loop(0, rows // batch, tstep, 0)
    else:
        half = LANES // 2
        lane = lax.broadcasted_iota(jnp.int32, (HD, LANES), 1)

        def load_pair(q, tp):
            zz = jnp.concatenate(
                [nat_ref[q, pl.ds(_aligned(tp * 2 * SUBLANES + tt * SUBLANES, SUBLANES), SUBLANES),
                         hp * LANES:(hp + 1) * LANES] for tt in range(2) for hp in range(D // LANES)], axis=0)
            tr = zz.T
            top, bot = tr[0:HD], tr[HD:2 * HD]
            d0 = jnp.where(lane < half, top, pltpu.roll(bot, half, axis=1))
            d1 = jnp.where(lane < half, pltpu.roll(top, half, axis=1), bot)
            return d0, d1

        def pair(tp, _):
            pairs = [load_pair(q, tp) for q in range(5)]
            for tt in range(2):
                _wkv_prep(tiles_ref, rows_ref, gam_ref, 2 * tp + tt, 0, *[p[tt] for p in pairs], lambda i: c_ref[0, i])
            return 0

        n_chunks = rows // chunk
        pairs_per_chunk = chunk // (2 * SUBLANES)
        body(0, 0)
        for c in range(1, n_chunks):
            body(c, 0)
            for tp in range((c - 1) * pairs_per_chunk, c * pairs_per_chunk):
                pair(tp, 0)
        for tp in range((n_chunks - 1) * pairs_per_chunk, n_chunks * pairs_per_chunk):
            pair(tp, 0)

    ps_ref[0:batch, :] = ps_ref[rows:rows + batch, :]


def _rwkv_pre(proj, row_off, n_rows, batch, rows_blk, groups, shift_init, consts, mu, w0, a0, w2p, a2p, g2p,
              dst=None):
    ob = row_off // rows_blk
    width = 3 * D + LORA_PAD
    steps_blk = rows_blk // batch
    kern = functools.partial(_rwkv_pre_kernel, batch=batch, chunk=128, chains_are_heads=groups > 1)
    n_in = 12
    dst_spec, dst_arg, out_g = _shared_rows(dst, proj.shape[0])
    return pl.pallas_call(
        kern if dst is None else _with_dst(kern, n_in),
        grid=(n_rows // rows_blk,),
        in_specs=[
            pl.BlockSpec((rows_blk, D), lambda i: (ob + i, COL_R)),
            pl.BlockSpec((rows_blk, D), lambda i: (ob + i, COL_K)),
            pl.BlockSpec((rows_blk, D), lambda i: (ob + i, COL_V)),
            pl.BlockSpec((rows_blk, LORA_PAD), lambda i: (ob + i, COL_LORA)),
            _full((batch, width)), _full((1, width)), _full((1, D)), _full((1, D)),
            _full((LANES, D)), _full((LANES, D)), _full((2 * LANES, D)), _full((groups, 5, HD, LANES)),
        ] + dst_spec,
        out_specs=[pl.BlockSpec((steps_blk, groups, 6, HD, LANES), lambda i: (i, 0, 0, 0, 0)),
                   pl.BlockSpec((steps_blk, groups, SUBLANES, LANES), lambda i: (i, 0, 0, 0)),
                   pl.BlockSpec((rows_blk, D), lambda i: (ob + i, 0))],
        out_shape=[jax.ShapeDtypeStruct((n_rows // batch, groups, 6, HD, LANES), f32),
                   jax.ShapeDtypeStruct((n_rows // batch, groups, SUBLANES, LANES), f32),
                   out_g],
        input_output_aliases={} if dst is None else {n_in: 2},
        scratch_shapes=[pltpu.VMEM((rows_blk + batch, width), f32), pltpu.VMEM((5, rows_blk, D), f32),
                        pltpu.VMEM((groups, HD, LANES), f32)],
        compiler_params=pltpu.CompilerParams(dimension_semantics=("arbitrary",), vmem_limit_bytes=56 * MIB),
        name="rwkv_pre",
    )(proj, proj, proj, proj, shift_init, mu, w0, a0, w2p, a2p, g2p, consts, *dst_arg)


STATE_ROWS = HD * HD
KEY_UNROLL = 32


def _wkv_step(s_ref, tiles_ref, rows_ref, y_ref, t, window):
    def key_row(tile, j):
        return tiles_ref[t, 0, tile, pl.ds(j, 1), :]

    def reduce_keys(j, acc):
        sa, y0 = acc
        s = s_ref[pl.ds(pl.multiple_of(j * HD, HD), HD), :]
        return sa + s * key_row(TILE_ALPHA, j), y0 + s * key_row(TILE_WR, j)

    zero = jnp.zeros((HD, LANES), f32)
    sa, y0 = lax.fori_loop(0, HD, reduce_keys, (zero, zero), unroll=KEY_UNROLL)
    v = tiles_ref[t, 0, TILE_V]

    def update_keys(j, _):
        rows = pl.ds(pl.multiple_of(j * HD, HD), HD)
        s_ref[rows, :] = s_ref[rows, :] + sa * key_row(TILE_BETA, j) + v * key_row(TILE_KP, j)
        return 0

    lax.fori_loop(0, HD, update_keys, 0, unroll=KEY_UNROLL)
    y_ref[t, 0] = y0 + sa * rows_ref[t, 0, ROW_BR:ROW_BR + 1, :] + v * rows_ref[t, 0, ROW_KR:ROW_KR + 1, :]

    @pl.when((t + 1) % window == 0)
    def _():
        def rescale(j, _):
            rows = pl.ds(pl.multiple_of(j * HD, HD), HD)
            s_ref[rows, :] = s_ref[rows, :] * key_row(TILE_W, j)
            return 0

        lax.fori_loop(0, HD, rescale, 0, unroll=KEY_UNROLL)


def _wkv_scan_kernel(tiles_ref, rows_ref, st_ref, y_ref, sfin_ref, s_ref, *, steps, window):
    tb = pl.program_id(1)

    def value_rows(i):
        return pl.ds(i, HD, stride=HD)

    @pl.when(tb == 0)
    def _():
        def init(c, _):
            tr = st_ref[:, pl.ds(pl.multiple_of(c * LANES, LANES), LANES)].T
            for i2 in range(2):
                s_ref[value_rows(2 * c + i2), :] = tr[i2 * HD:(i2 + 1) * HD]
            return 0

        lax.fori_loop(0, STATE_ROWS // LANES, init, 0, unroll=8)

    def step(t, _):
        _wkv_step(s_ref, tiles_ref, rows_ref, y_ref, t, window)
        return 0

    lax.fori_loop(0, steps, step, 0)

    @pl.when(tb == pl.num_programs(1) - 1)
    def _():
        def fin(c, _):
            pair = jnp.concatenate([s_ref[value_rows(2 * c + i2), :] for i2 in range(2)], axis=0)
            sfin_ref[:, pl.ds(pl.multiple_of(c * LANES, LANES), LANES)] = pair.T
            return 0

        lax.fori_loop(0, STATE_ROWS // LANES, fin, 0, unroll=8)


def _wkv_scan(tiles, rows, state, steps_blk, window):
    n_steps, groups = tiles.shape[0], tiles.shape[1]
    assert steps_blk % window == 0
    kern = functools.partial(_wkv_scan_kernel, steps=steps_blk, window=window)
    sblk = pl.BlockSpec((LANES, STATE_ROWS), lambda g, i: (0, g))
    return pl.pallas_call(
        kern,
        grid=(groups, n_steps // steps_blk),
        in_specs=[pl.BlockSpec((steps_blk, 1, 6, HD, LANES), lambda g, i: (i, g, 0, 0, 0)),
                  pl.BlockSpec((steps_blk, 1, SUBLANES, LANES), lambda g, i: (i, g, 0, 0)),
                  sblk],
        out_specs=[pl.BlockSpec((steps_blk, 1, HD, LANES), lambda g, i: (i, g, 0, 0)), sblk],
        out_shape=[jax.ShapeDtypeStruct((n_steps, groups, HD, LANES), f32),
                   jax.ShapeDtypeStruct((LANES, groups * STATE_ROWS), f32)],
        scratch_shapes=[pltpu.VMEM((STATE_ROWS, LANES), f32)],
        compiler_params=pltpu.CompilerParams(dimension_semantics=("arbitrary", "arbitrary"),
                                             vmem_limit_bytes=48 * MIB),
        name="wkv_scan",
    )(tiles, rows, state)


def _wkv_norm(y, v, bonus, lnw, lnb):
    mean = jnp.sum(y, axis=0, keepdims=True) * (1.0 / HD)
    d = y - mean
    var = jnp.sum(d * d, axis=0, keepdims=True) * (1.0 / HD)
    return d * lax.rsqrt(var + GN_EPS) * lnw + lnb + bonus * v


def _wkv_post_kernel(y_ref, v_ref, rows_ref, c_ref, o_ref, *, batch, chains_are_heads):
    steps = y_ref.shape[0]

    def normed(t, g):
        return _wkv_norm(y_ref[t, g], v_ref[t, g, 0], rows_ref[t, g, ROW_BONUS:ROW_BONUS + 1, :],
                         c_ref[g, CONST_LNW], c_ref[g, CONST_LNB])

    if chains_are_heads:
        def tstep(t, _):
            r0 = pl.multiple_of(t * batch, batch)
            for hp in range(D // LANES):
                two = jnp.concatenate([normed(t, 2 * hp), normed(t, 2 * hp + 1)], axis=0)
                o_ref[pl.ds(r0, batch), hp * LANES:(hp + 1) * LANES] = two.T
            return 0

        lax.fori_loop(0, steps, tstep, 0)
    else:
        half = LANES // 2
        lane = lax.broadcasted_iota(jnp.int32, (HD, LANES), 1)

        def pair(tp, _):
            o0 = normed(2 * tp, 0)
            o1 = normed(2 * tp + 1, 0)
            top = jnp.where(lane < half, o0, pltpu.roll(o1, half, axis=1))
            bot = jnp.where(lane < half, pltpu.roll(o0, half, axis=1), o1)
            zz = jnp.concatenate([top, bot], axis=0).T
            for tt in range(2):
                row = pl.multiple_of(tp * 2 * SUBLANES + tt * SUBLANES, SUBLANES)
                for hp in range(D // LANES):
                    src = tt * HD + hp * SUBLANES
                    o_ref[pl.ds(row, SUBLANES), hp * LANES:(hp + 1) * LANES] = zz[src:src + SUBLANES, :]
            return 0

        lax.fori_loop(0, steps // 2, pair, 0, unroll=2)


def _outproj_kernel(ml_ref, g_ref, gr_ref, x_ref, yp_ref, vp_ref, rp_ref, cp_ref, ysm_ref, vs_ref, rs_ref, cs_ref,
                    wo_ref, n2_ref, rh_ref, rl_ref, rb_ref,
                    x1_ref, t_ref, selt_ref, seln_ref, op_ref, *, prompt_tiles, prompt_batch, sample_batch):
    tm = x_ref.shape[0]
    i = pl.program_id(0)

    @pl.when(i < prompt_tiles)
    def _():
        _wkv_post_kernel(yp_ref, vp_ref, rp_ref, cp_ref, op_ref, batch=prompt_batch, chains_are_heads=False)

    @pl.when(i >= prompt_tiles)
    def _():
        _wkv_post_kernel(ysm_ref, vs_ref, rs_ref, cs_ref, op_ref, batch=sample_batch, chains_are_heads=True)

    merged = ml_ref[...] + jax.nn.sigmoid(gr_ref[...]) * (op_ref[...] * g_ref[...])
    x1 = x_ref[...] + jnp.dot(merged.astype(bf16), wo_ref[...], preferred_element_type=f32)
    x1_ref[...] = x1
    t = x1 * lax.rsqrt(jnp.mean(x1 * x1, axis=-1, keepdims=True) + NORM_EPS) * n2_ref[...]
    th = t.astype(bf16)
    t_ref[...] = th
    tl = (t - th.astype(f32)).astype(bf16)
    nt_dims = (((1,), (1,)), ((), ()))
    lg = (lax.dot_general(rh_ref[...], th, nt_dims, preferred_element_type=f32)
          + lax.dot_general(rh_ref[...], tl, nt_dims, preferred_element_type=f32)
          + lax.dot_general(rl_ref[...], th, nt_dims, preferred_element_type=f32)) + rb_ref[...]
    row = lax.broadcasted_iota(jnp.int32, (EPG, tm), 0).astype(f32)
    neg = jnp.float32(-jnp.inf)
    glog = jnp.where(row < N_GROUPS, lg[0:EPG], neg)
    ge = jnp.exp(glog - jnp.max(glog, axis=0, keepdims=True))
    pg = ge / jnp.sum(ge, axis=0, keepdims=True)
    p_top = jnp.max(pg, axis=0, keepdims=True)
    g_idx = jnp.min(jnp.where(pg == p_top, row, EPG), axis=0, keepdims=True)
    le = jnp.zeros((EPG, tm), f32)
    for g in range(N_GROUPS):
        le = jnp.where(g_idx == g, lg[EPG * (g + 1):EPG * (g + 2)], le)
    qe = jnp.exp(le - jnp.max(le, axis=0, keepdims=True))
    q = qe / jnp.sum(qe, axis=0, keepdims=True)
    q1 = jnp.max(q, axis=0, keepdims=True)
    i1 = jnp.min(jnp.where(q == q1, row, EPG), axis=0, keepdims=True)
    qm = jnp.where(row == i1, -1.0, q)
    q2 = jnp.max(qm, axis=0, keepdims=True)
    i2 = jnp.min(jnp.where(qm == q2, row, EPG), axis=0, keepdims=True)
    qs = q1 + q2
    sel = jnp.concatenate([g_idx * EPG + i1, g_idx * EPG + i2, q1 / qs * p_top, q2 / qs * p_top,
                           jnp.zeros((SUBLANES - 4, tm), f32)], axis=0)
    selt_ref[...] = sel
    sel_pad = jnp.concatenate([sel, jnp.zeros((LANES - SUBLANES, tm), f32)], axis=0)
    for c in range(tm // LANES):
        seln_ref[c * LANES:(c + 1) * LANES, :] = sel_pad[:, c * LANES:(c + 1) * LANES].T


def _outproj(ml, g, proj, x, scan_p, scan_s, wo, n2, rh, rl, rb, prompt_batch, sample_batch, tm=512):
    n = x.shape[0]
    y_p, tiles_p, rows_p, consts_p = scan_p
    y_s, tiles_s, rows_s, consts_s = scan_s
    n_p = y_p.shape[0] * prompt_batch
    sp, ss = tm // prompt_batch, tm // sample_batch
    prompt_tiles = n_p // tm
    sample_tiles = (n - n_p) // tm
    assert n_p % tm == 0 and sp % 2 == 0 and y_s.shape[0] == sample_tiles * ss
    pidx = lambda i: jnp.minimum(i, prompt_tiles - 1)
    sidx = lambda i: jnp.clip(i - prompt_tiles, 0, sample_tiles - 1)
    groups = y_s.shape[1]
    blk = pl.BlockSpec((tm, D), lambda i: (i, 0))
    kern = functools.partial(_outproj_kernel, prompt_tiles=prompt_tiles, prompt_batch=prompt_batch,
                             sample_batch=sample_batch)
    return pl.pallas_call(
        kern,
        grid=(n // tm,),
        in_specs=[blk, blk, pl.BlockSpec((tm, D), lambda i: (i, COL_GR)), blk,
                  pl.BlockSpec((sp, 1, HD, LANES), lambda i: (pidx(i), 0, 0, 0)),
                  pl.BlockSpec((sp, 1, 1, HD, LANES), lambda i: (pidx(i), 0, TILE_V, 0, 0)),
                  pl.BlockSpec((sp, 1, SUBLANES, LANES), lambda i: (pidx(i), 0, 0, 0)),
                  _full((1, 5, HD, LANES)),
                  pl.BlockSpec((ss, groups, HD, LANES), lambda i: (sidx(i), 0, 0, 0)),
                  pl.BlockSpec((ss, groups, 1, HD, LANES), lambda i: (sidx(i), 0, TILE_V, 0, 0)),
                  pl.BlockSpec((ss, groups, SUBLANES, LANES), lambda i: (sidx(i), 0, 0, 0)),
                  _full((groups, 5, HD, LANES)),
                  _full((D, D)), _full((1, D)), _full((ROUTER_ROWS, D)), _full((ROUTER_ROWS, D)),
                  _full((ROUTER_ROWS, 1))],
        out_specs=[blk, blk, pl.BlockSpec((SUBLANES, tm), lambda i: (0, i)),
                   pl.BlockSpec((tm, LANES), lambda i: (i, 0))],
        out_shape=[jax.ShapeDtypeStruct((n, D), f32), jax.ShapeDtypeStruct((n, D), bf16),
                   jax.ShapeDtypeStruct((SUBLANES, n), f32), jax.ShapeDtypeStruct((n, LANES), f32)],
        scratch_shapes=[pltpu.VMEM((tm, D), f32)],
        compiler_params=pltpu.CompilerParams(dimension_semantics=("arbitrary",), vmem_limit_bytes=56 * MIB),
        name="outproj_router",
    )(ml, g, proj, x, y_p, tiles_p, rows_p, consts_p, y_s, tiles_s, rows_s, consts_s, wo, n2, rh, rl, rb)


MOE_CHUNK = 32
MOE_ROWS = 128
MOE_GATHER = 512
MOE_EXPERTS_PER_STEP = 4
MOE_OVERRUN = MOE_ROWS
MOE_CAP = -(-(2 * TOKEN_TILE + N_EXPERTS * (MOE_CHUNK - 1) + MOE_OVERRUN) // MOE_GATHER) * MOE_GATHER
SEL_E1, SEL_E2, SEL_C1, SEL_C2 = range(4)


def _route_meta_kernel(seln_ref, o_ref):
    tm = seln_ref.shape[0]
    lane = lax.broadcasted_iota(jnp.int32, (tm, LANES), 1).astype(f32)
    sel = seln_ref[...]
    hit = (lane == sel[:, SEL_E1:SEL_E1 + 1]) | (lane == sel[:, SEL_E2:SEL_E2 + 1])
    cnt = jnp.sum(jnp.where(hit, 1.0, 0.0), axis=0, keepdims=True)
    nchunk = jnp.floor((cnt + (MOE_CHUNK - 1)) * (1.0 / MOE_CHUNK))
    upper = (lax.broadcasted_iota(jnp.int32, (LANES, LANES), 0)
             < lax.broadcasted_iota(jnp.int32, (LANES, LANES), 1))
    offs = jnp.dot(jnp.broadcast_to(nchunk, (SUBLANES, LANES)).astype(bf16), jnp.where(upper, 1.0, 0.0).astype(bf16),
                   preferred_element_type=f32)
    row = lax.broadcasted_iota(jnp.int32, (SUBLANES, LANES), 0)
    o_ref[0] = jnp.where(row == 0, nchunk, jnp.where(row == 1, offs, 0.0)).astype(jnp.int32)


def _route_meta(seln, tm):
    tiles = seln.shape[0] // tm
    return pl.pallas_call(
        _route_meta_kernel,
        grid=(tiles,),
        in_specs=[pl.BlockSpec((tm, LANES), lambda i: (i, 0))],
        out_specs=pl.BlockSpec((1, SUBLANES, LANES), lambda i: (i, 0, 0)),
        out_shape=jax.ShapeDtypeStruct((tiles, SUBLANES, LANES), jnp.int32),
        compiler_params=pltpu.CompilerParams(dimension_semantics=("arbitrary",)),
        name="route_meta",
    )(seln)


def _moe_experts_kernel(nch_ref, off_ref, t_ref, selt_ref, tri_ref, wgu_ref, wd_ref, ys_ref, pos_ref,
                        xs_ref, cw_ref):
    i = pl.program_id(0)
    e = pl.program_id(1)
    tm = t_ref.shape[0]
    used_rows = (off_ref[i * N_EXPERTS + N_EXPERTS - 1] + nch_ref[i * N_EXPERTS + N_EXPERTS - 1]) * MOE_CHUNK

    @pl.when(e == 0)
    def _():
        ys_ref[...] = jnp.zeros_like(ys_ref)
        e1 = selt_ref[SEL_E1:SEL_E1 + 1, :]
        e2 = selt_ref[SEL_E2:SEL_E2 + 1, :]
        erow = lax.broadcasted_iota(jnp.int32, (N_EXPERTS, tm), 0).astype(f32)
        oh1 = jnp.where(erow == e1, 1.0, 0.0)
        oh2 = jnp.where(erow == e2, 1.0, 0.0)
        before1 = jnp.dot(oh1.astype(bf16), tri_ref[...], preferred_element_type=f32)
        before2 = jnp.dot(oh2.astype(bf16), tri_ref[...], preferred_element_type=f32)
        cnt1 = jnp.sum(oh1, axis=1, keepdims=True)
        cnt2 = jnp.sum(oh2, axis=1, keepdims=True)
        nchunk = jnp.floor((cnt1 + cnt2 + (MOE_CHUNK - 1)) * (1.0 / MOE_CHUNK))
        lower = (lax.broadcasted_iota(jnp.int32, (N_EXPERTS, LANES), 1)
                 < lax.broadcasted_iota(jnp.int32, (N_EXPERTS, LANES), 0))
        nchunk_rows = jnp.concatenate([jnp.broadcast_to(nchunk, (N_EXPERTS, LANES)),
                                       jnp.zeros((LANES - N_EXPERTS, LANES), f32)], axis=0)
        start = jnp.dot(jnp.where(lower, 1.0, 0.0).astype(bf16), nchunk_rows.astype(bf16),
                        preferred_element_type=f32)[:, 0:1] * MOE_CHUNK
        pos1 = jnp.sum(oh1 * (start + before1), axis=0, keepdims=True)
        pos2 = jnp.sum(oh2 * (start + cnt1 + before2), axis=0, keepdims=True)
        pos_ref[...] = jnp.concatenate([pos1, pos2, jnp.zeros((SUBLANES - 2, tm), f32)], axis=0)
        c1 = selt_ref[SEL_C1:SEL_C1 + 1, :]
        c2 = selt_ref[SEL_C2:SEL_C2 + 1, :]
        for k in range(MOE_CAP // MOE_GATHER):
            @pl.when(k * MOE_GATHER < used_rows + MOE_OVERRUN)
            def _(k=k):
                ridx = (lax.broadcasted_iota(jnp.int32, (MOE_GATHER, tm), 0) + k * MOE_GATHER).astype(f32)
                p1 = ridx == pos1
                p2 = ridx == pos2
                onehot = jnp.where(p1 | p2, 1.0, 0.0).astype(bf16)
                xs_ref[k * MOE_GATHER:(k + 1) * MOE_GATHER, :] = jnp.dot(
                    onehot, t_ref[...], preferred_element_type=f32).astype(bf16)
                w = jnp.sum(jnp.where(p1, c1, 0.0) + jnp.where(p2, c2, 0.0), axis=1, keepdims=True)
                cw_ref[k * MOE_GATHER:(k + 1) * MOE_GATHER, :] = jnp.broadcast_to(w, (MOE_GATHER, LANES))

    def expert_rows(u, start):
        rows = pl.ds(pl.multiple_of(start, MOE_CHUNK), MOE_ROWS)
        gu = jnp.dot(xs_ref[rows, :], wgu_ref[u], preferred_element_type=f32)
        w = cw_ref[rows, :]
        h = jax.nn.silu(gu[:, :D_EXPERT]) * gu[:, D_EXPERT:] * jnp.concatenate([w, w], axis=1)
        ys_ref[0, rows, :] = jnp.dot(h.astype(bf16), wd_ref[u], preferred_element_type=f32).astype(bf16)

    per_block = MOE_ROWS // MOE_CHUNK
    for u in range(wgu_ref.shape[0]):
        expert = i * N_EXPERTS + e * wgu_ref.shape[0] + u
        base = off_ref[expert] * MOE_CHUNK
        expert_rows(u, base)

        def more(c, _, u=u, base=base):
            expert_rows(u, base + (c + 1) * MOE_ROWS)
            return 0

        lax.fori_loop(0, jnp.maximum(nch_ref[expert] - 1, 0) // per_block, more, 0)


def _moe_experts(nch, off, t, selt, tri, wgu, wd, tm):
    n = t.shape[0]
    tiles = n // tm
    grid_spec = pltpu.PrefetchScalarGridSpec(
        num_scalar_prefetch=2,
        grid=(tiles, N_EXPERTS // MOE_EXPERTS_PER_STEP),
        in_specs=[
            pl.BlockSpec((tm, D), lambda i, e, nch, off: (i, 0)),
            pl.BlockSpec((SUBLANES, tm), lambda i, e, nch, off: (0, i)),
            pl.BlockSpec((tm, tm), lambda i, e, nch, off: (0, 0)),
            pl.BlockSpec((MOE_EXPERTS_PER_STEP, D, 2 * D_EXPERT), lambda i, e, nch, off: (e, 0, 0)),
            pl.BlockSpec((MOE_EXPERTS_PER_STEP, D_EXPERT, D), lambda i, e, nch, off: (e, 0, 0)),
        ],
        out_specs=[pl.BlockSpec((1, MOE_CAP, D), lambda i, e, nch, off: (i, 0, 0)),
                   pl.BlockSpec((SUBLANES, tm), lambda i, e, nch, off: (0, i))],
        scratch_shapes=[pltpu.VMEM((MOE_CAP, D), bf16), pltpu.VMEM((MOE_CAP, LANES), f32)],
    )
    return pl.pallas_call(
        _moe_experts_kernel,
        grid_spec=grid_spec,
        out_shape=[jax.ShapeDtypeStruct((tiles, MOE_CAP, D), bf16), jax.ShapeDtypeStruct((SUBLANES, n), f32)],
        compiler_params=pltpu.CompilerParams(dimension_semantics=("arbitrary", "arbitrary"),
                                             vmem_limit_bytes=56 * MIB),
        name="moe_experts",
    )(nch, off, t, selt, tri, wgu, wd)


def _moe_combine_kernel(nch_ref, off_ref, ysort_ref, pos_ref, x1_ref, fn_ref, yp_ref, ys_ref, acc_ref, yf_ref,
                        *, prompt_tiles, parts):
    i = pl.program_id(0)
    part = pl.program_id(1)
    rows = x1_ref.shape[0]
    bp, tsteps = yp_ref.shape[0], yp_ref.shape[1]
    bs = ys_ref.shape[0]
    ssteps = ys_ref.shape[1] // parts
    used_rows = (off_ref[i * N_EXPERTS + N_EXPERTS - 1] + nch_ref[i * N_EXPERTS + N_EXPERTS - 1]) * MOE_CHUNK

    pos_t = jnp.concatenate([pos_ref[...], jnp.zeros((LANES - SUBLANES, rows), f32)], axis=0)
    pos_n = jnp.concatenate([pos_t[:, c * LANES:(c + 1) * LANES].T for c in range(rows // LANES)], axis=0)
    pos1 = pos_n[:, 0:1]
    pos2 = pos_n[:, 1:2]
    acc_ref[...] = x1_ref[...]
    for k in range(MOE_CAP // MOE_GATHER):
        @pl.when(k * MOE_GATHER < used_rows)
        def _(k=k):
            cidx = (lax.broadcasted_iota(jnp.int32, (rows, MOE_GATHER), 1) + k * MOE_GATHER).astype(f32)
            onehot = jnp.where((cidx == pos1) | (cidx == pos2), 1.0, 0.0).astype(bf16)
            acc_ref[...] += jnp.dot(onehot, ysort_ref[0, k * MOE_GATHER:(k + 1) * MOE_GATHER, :],
                                    preferred_element_type=f32)

    xo = acc_ref[...]
    y = xo * lax.rsqrt(jnp.mean(xo * xo, axis=-1, keepdims=True) + NORM_EPS) * fn_ref[...]
    for cb in range(D // LANES):
        yf_ref[cb] = y[:, cb * LANES:(cb + 1) * LANES]

    @pl.when(i < prompt_tiles)
    def _():
        for b in range(bp):
            yp_ref[b] = jnp.concatenate([yf_ref[cb, pl.ds(b, tsteps, stride=bp), :] for cb in range(D // LANES)],
                                        axis=1)

    for p in range(parts):
        @pl.when((i >= prompt_tiles) & (part == p))
        def _(p=p):
            for t in range(ssteps):
                ys_ref[:, p * ssteps + t, :] = jnp.concatenate(
                    [yf_ref[cb, t * bs:(t + 1) * bs, :] for cb in range(D // LANES)], axis=1)


def _moe_combine(nch, off, ysort, pos, x1, fn, prompt_shape, sample_shape, tm, parts=2):
    n = x1.shape[0]
    bp, tp, _ = prompt_shape
    bs, ts, _ = sample_shape
    rows = tm // parts
    assert rows % bp == 0 and tp % (rows // bp) == 0 and bs * ts == tm and ts % parts == 0
    prompt_tiles = bp * tp // tm
    last_prompt_blk = prompt_tiles * parts - 1
    kern = functools.partial(_moe_combine_kernel, prompt_tiles=prompt_tiles, parts=parts)
    grid_spec = pltpu.PrefetchScalarGridSpec(
        num_scalar_prefetch=2,
        grid=(n // tm, parts),
        in_specs=[
            pl.BlockSpec((1, MOE_CAP, D), lambda i, p, nch, off: (i, 0, 0)),
            pl.BlockSpec((SUBLANES, rows), lambda i, p, nch, off: (0, i * parts + p)),
            pl.BlockSpec((rows, D), lambda i, p, nch, off: (i * parts + p, 0)),
            pl.BlockSpec((1, D), lambda i, p, nch, off: (0, 0)),
        ],
        out_specs=[pl.BlockSpec((bp, rows // bp, D),
                                lambda i, p, nch, off: (0, jnp.minimum(i * parts + p, last_prompt_blk), 0)),
                   pl.BlockSpec((bs, ts, D), lambda i, p, nch, off: (0, 0, 0))],
        scratch_shapes=[pltpu.VMEM((rows, D), f32), pltpu.VMEM((D // LANES, rows, LANES), f32)],
    )
    return pl.pallas_call(
        kern,
        grid_spec=grid_spec,
        out_shape=[jax.ShapeDtypeStruct(prompt_shape, f32), jax.ShapeDtypeStruct(sample_shape, f32)],
        compiler_params=pltpu.CompilerParams(dimension_semantics=("arbitrary", "arbitrary"),
                                             vmem_limit_bytes=56 * MIB),
        name="moe_combine",
    )(nch, off, ysort, pos, x1, fn)


def _chain_tiles_prompt(vec, batch):
    t = vec.reshape(HEADS // 2, 2, HD)
    t = jnp.transpose(t, (2, 1, 0))
    return jnp.broadcast_to(t[..., None], (HD, 2, HEADS // 2, batch)).reshape(HD, LANES)


def _chain_tiles_sample(vec):
    return jnp.broadcast_to(vec.reshape(HEADS, HD)[..., None], (HEADS, HD, LANES))


def kernel(x_prompt, x_sample, state_conv, state_lru, state_shift, state_wkv, norm1, w_in, conv_w, conv_b, lru_wx, lru_bx, lru_wa, lru_ba, lru_a_param, rwkv_mu, rwkv_w0, rwkv_w2, rwkv_a0, rwkv_a2, rwkv_g2, rwkv_k_k, rwkv_k_a, rwkv_r_k, rwkv_ln_w, rwkv_ln_b, w_out, norm2, router_group, router_group_b, router_expert, router_expert_b, exp_gate, exp_up, exp_down, final_norm):
    bp, tp, _ = x_prompt.shape
    bs, ts, _ = x_sample.shape
    assert norm1.shape[0] == 1 and bp * HEADS == LANES and bs == LANES and tp % 64 == 0 and ts % 2 == 0
    n_p, n_s = bp * tp, bs * ts
    c_rw = 2 * D
    c_lora = c_rw + 3 * D
    c_gl = c_lora + LORA

    w = w_in[0]
    w_all = jnp.concatenate([w[:, :c_lora], w[:, c_gl:], w[:, c_lora:c_gl],
                             jnp.zeros((D, LORA_PAD - LORA), f32)], axis=1).astype(bf16)
    mu = rwkv_mu[0]
    mu_all = jnp.concatenate([mu[:3 * D], mu[3 * D:], jnp.zeros((LORA_PAD - LORA,), f32)])[None]
    w2p = jnp.concatenate([rwkv_w2[0], jnp.zeros((LANES - LORA_W, D), f32)]).astype(bf16)
    a2p = jnp.concatenate([jnp.zeros((LORA_W, D), f32), rwkv_a2[0]]).astype(bf16)
    g2p = jnp.concatenate([rwkv_g2[0], jnp.zeros((2 * LANES - LORA_G, D), f32)]).astype(bf16)
    wxa = jnp.concatenate([lru_wx[0], lru_wa[0]], axis=-1).astype(bf16)
    row = lambda v: v.reshape(1, -1)
    rk = rwkv_r_k[0].reshape(D)
    cvecs = (rwkv_k_k[0], rwkv_k_a[0], rk, rwkv_ln_w[0], rwkv_ln_b[0])
    consts_p = jnp.stack([_chain_tiles_prompt(v, bp) for v in cvecs])[None]
    consts_s = jnp.stack([_chain_tiles_sample(v) for v in cvecs], axis=1)
    rw = jnp.zeros((ROUTER_ROWS, D), f32)
    rw = rw.at[0:N_GROUPS].set(router_group[0].T).at[EPG:EPG + N_EXPERTS].set(router_expert[0].T)
    rh = rw.astype(bf16)
    rl = (rw - rh.astype(f32)).astype(bf16)
    rb = jnp.zeros((ROUTER_ROWS, 1), f32)
    rb = rb.at[0:N_GROUPS, 0].set(router_group_b[0]).at[EPG:EPG + N_EXPERTS, 0].set(router_expert_b[0])
    wgu = jnp.concatenate([exp_gate[0], exp_up[0]], axis=-1).astype(bf16)
    wd = exp_down[0].astype(bf16)
    wo = w_out[0].astype(bf16)

    proj, x = _inproj(x_prompt, x_sample, row(norm1[0]), w_all, tm=TOKEN_TILE)

    lru_args = (conv_w[0], row(conv_b[0]), wxa, row(lru_bx[0]), row(lru_ba[0]), row(lru_a_param[0]))
    conv_init_s = jnp.transpose(state_conv[0], (1, 0, 2)).reshape(3 * bs, D)
    ml, hl_p = _lru(proj, 0, n_p, bp, 512, True, jnp.zeros((3 * bp, D), f32), jnp.zeros((bp, D), f32), *lru_args)
    ml, hl_s = _lru(proj, n_p, n_s, bs, 512, False, conv_init_s, state_lru[0], *lru_args, dst=ml)

    sh = state_shift[0]
    shift_init_s = jnp.concatenate([sh, jnp.zeros((bs, LORA_PAD - LORA), f32)], axis=1)
    pre_args = (mu_all, row(rwkv_w0[0]), row(rwkv_a0[0]), w2p, a2p, g2p)
    pre_rows = 256
    tiles_p, rows_p, g = _rwkv_pre(proj, 0, n_p, bp, pre_rows, 1, jnp.zeros((bp, 3 * D + LORA_PAD), f32),
                                   consts_p, *pre_args)
    tiles_s, rows_s, g = _rwkv_pre(proj, n_p, n_s, bs, pre_rows, HEADS, shift_init_s, consts_s, *pre_args, dst=g)

    y_p, sfin_p = _wkv_scan(tiles_p, rows_p, jnp.zeros((LANES, STATE_ROWS), f32), 64, pre_rows // bp)
    y_s, sfin_s = _wkv_scan(tiles_s, rows_s, state_wkv[0].reshape(bs, HEADS * STATE_ROWS), ts, pre_rows // bs)

    x1, t, selt, seln = _outproj(ml, g, proj, x, (y_p, tiles_p, rows_p, consts_p), (y_s, tiles_s, rows_s, consts_s),
                                 wo, row(norm2[0]), rh, rl, rb, bp, bs)
    meta = _route_meta(seln, TOKEN_TILE)
    nch = meta[:, 0, :N_EXPERTS].reshape(-1)
    off = meta[:, 1, :N_EXPERTS].reshape(-1)
    tri = jnp.triu(jnp.ones((TOKEN_TILE, TOKEN_TILE), bf16), k=1)
    ysort, pos = _moe_experts(nch, off, t, selt, tri, wgu, wd, TOKEN_TILE)
    y_prompt, y_sample = _moe_combine(nch, off, ysort, pos, x1, row(final_norm), x_prompt.shape, x_sample.shape,
                                      TOKEN_TILE)

    def last_rows(lo, hi, n_end, batch, steps):
        return proj[n_end - steps * batch:n_end, lo:hi].reshape(steps, batch, hi - lo)

    conv_p = jnp.transpose(last_rows(0, D, n_p, bp, 3), (1, 0, 2))[None]
    conv_s = jnp.transpose(last_rows(0, D, n_p + n_s, bs, 3), (1, 0, 2))[None]

    def shift_rows(n_end, batch):
        return jnp.concatenate([last_rows(2 * D, 5 * D, n_end, batch, 1)[0],
                                last_rows(7 * D, 7 * D + LORA, n_end, batch, 1)[0]], axis=1)[None]

    wkv_p = jnp.transpose(sfin_p.reshape(2, HEADS // 2, bp, HD, HD), (2, 1, 0, 3, 4)).reshape(1, bp, HEADS, HD, HD)
    wkv_s = sfin_s.reshape(1, bs, HEADS, HD, HD)
    return (y_prompt, y_sample, conv_p, hl_p[None], shift_rows(n_p, bp), wkv_p,
            conv_s, hl_s[None], shift_rows(n_p + n_s, bs), wkv_s)
```

```python
import functools

import jax
import jax.numpy as jnp
from jax import lax
from jax.experimental import pallas as pl
from jax.experimental.pallas import tpu as pltpu

f32 = jnp.float32
bf16 = jnp.bfloat16

D = 1024
HEADS = 16
HD = 64
LANES = 128
SUBLANES = 8
LORA_W = 64
LORA_A = 64
LORA_G = 160
LORA = LORA_W + LORA_A + LORA_G
LORA_PAD = 512
N_GROUPS = 4
EPG = 8
N_EXPERTS = N_GROUPS * EPG
D_EXPERT = 256
LRU_C = 8.0
GN_EPS = 64e-5
NORM_EPS = 1e-6
PROJ_COLS = 7 * D + LORA_PAD
COL_LRU_X, COL_LRU_Y, COL_R, COL_K, COL_V, COL_GL, COL_GR = range(7)
COL_LORA = 7 * D // LORA_PAD
ROUTER_ROWS = 48
MIB = 1024 * 1024
TOKEN_TILE = 1024


def _aligned(start, multiple):
    return start if isinstance(start, int) else pl.multiple_of(start, multiple)


def _full(shape, grid_rank=1):
    zeros = tuple(0 for _ in shape)
    if grid_rank == 1:
        return pl.BlockSpec(shape, lambda i: zeros)
    return pl.BlockSpec(shape, lambda i, j: zeros)


def _inproj_kernel(xp_ref, xs_ref, g_ref, w_ref, o_ref, xo_ref, xf_ref, xn_ref, *, prompt_tiles):
    i = pl.program_id(0)
    first_col = pl.program_id(1) == 0
    bp, tsteps = xp_ref.shape[0], xp_ref.shape[1]
    bs, ssteps = xs_ref.shape[0], xs_ref.shape[1]

    @pl.when(first_col & (i < prompt_tiles))
    def _():
        for b in range(bp):
            v = xp_ref[b]
            for cb in range(D // LANES):
                xf_ref[cb, pl.ds(b, tsteps, stride=bp), :] = v[:, cb * LANES:(cb + 1) * LANES]

    @pl.when(first_col & (i >= prompt_tiles))
    def _():
        for t in range(ssteps):
            v = xs_ref[:, t, :]
            for cb in range(D // LANES):
                xf_ref[cb, t * bs:(t + 1) * bs, :] = v[:, cb * LANES:(cb + 1) * LANES]

    @pl.when(first_col)
    def _():
        x = jnp.concatenate([xf_ref[cb] for cb in range(D // LANES)], axis=1)
        xo_ref[...] = x
        ms = jnp.mean(x * x, axis=-1, keepdims=True)
        xn_ref[...] = (x * lax.rsqrt(ms + NORM_EPS) * g_ref[...]).astype(bf16)

    o_ref[...] = jnp.dot(xn_ref[...], w_ref[...], preferred_element_type=f32)


def _inproj(x_prompt, x_sample, gain, w, tm, tn=1536):
    bp, tp, _ = x_prompt.shape
    bs, ts, _ = x_sample.shape
    assert tm % bp == 0 and tp % (tm // bp) == 0 and bs * ts == tm
    prompt_tiles = bp * tp // tm
    n = bp * tp + bs * ts
    kern = functools.partial(_inproj_kernel, prompt_tiles=prompt_tiles)
    return pl.pallas_call(
        kern,
        grid=(n // tm, PROJ_COLS // tn),
        in_specs=[
            pl.BlockSpec((bp, tm // bp, D), lambda i, j: (0, jnp.minimum(i, prompt_tiles - 1), 0)),
            pl.BlockSpec((bs, ts, D), lambda i, j: (0, 0, 0)),
            pl.BlockSpec((1, D), lambda i, j: (0, 0)),
            pl.BlockSpec((D, tn), lambda i, j: (0, j)),
        ],
        out_specs=[pl.BlockSpec((tm, tn), lambda i, j: (i, j)), pl.BlockSpec((tm, D), lambda i, j: (i, 0))],
        out_shape=[jax.ShapeDtypeStruct((n, PROJ_COLS), f32), jax.ShapeDtypeStruct((n, D), f32)],
        scratch_shapes=[pltpu.VMEM((D // LANES, tm, LANES), f32), pltpu.VMEM((tm, D), bf16)],
        compiler_params=pltpu.CompilerParams(dimension_semantics=("arbitrary", "arbitrary"),
                                             vmem_limit_bytes=56 * MIB),
        name="inproj",
    )(x_prompt, x_sample, gain, w)


def _lru_kernel(x_ref, y_ref, gl_ref, cinit_ref, h0_ref, cw_ref, cb_ref, wxa_ref, bx_ref, ba_ref, ap_ref,
                o_ref, hl_ref, xs_ref, a_ref, b_ref, h_ref, *, batch, reset_first, chunk):
    rows = x_ref.shape[0]
    nt = rows // batch
    hist = 3 * batch
    pid = pl.program_id(0)

    @pl.when(pid == 0)
    def _():
        xs_ref[0:hist, :] = cinit_ref[...]
        h_ref[...] = h0_ref[...]

    xs_ref[hist:hist + rows, :] = x_ref[...]
    logsig = -jax.nn.softplus(-ap_ref[...])

    def gates(c, _):
        r0 = pl.multiple_of(c * chunk, chunk)
        xc = (cb_ref[...]
              + cw_ref[3:4, :] * xs_ref[pl.ds(pl.multiple_of(r0 + hist, SUBLANES), chunk), :]
              + cw_ref[2:3, :] * xs_ref[pl.ds(pl.multiple_of(r0 + 2 * batch, SUBLANES), chunk), :]
              + cw_ref[1:2, :] * xs_ref[pl.ds(pl.multiple_of(r0 + batch, SUBLANES), chunk), :]
              + cw_ref[0:1, :] * xs_ref[pl.ds(r0, chunk), :])
        if reset_first:
            grow = lax.broadcasted_iota(jnp.int32, (chunk, LANES), 0) + (r0 + pid * rows)
            first = grow < batch
        for n in range(D // LANES):
            sl = slice(n * LANES, (n + 1) * LANES)
            xn = xc[:, sl]
            g2 = jnp.dot(xn.astype(bf16), wxa_ref[n], preferred_element_type=f32)
            gate_x = jax.nn.sigmoid(g2[:, :LANES] + bx_ref[:, sl])
            gate_a = jax.nn.sigmoid(g2[:, LANES:] + ba_ref[:, sl])
            log_a = LRU_C * gate_a * logsig[:, sl]
            a = jnp.exp(log_a)
            mult = jnp.sqrt(-jnp.tanh(log_a) * (a * a + 1.0))
            if reset_first:
                mult = jnp.where(first, 1.0, mult)
            a_ref[pl.ds(r0, chunk), sl] = a
            b_ref[pl.ds(r0, chunk), sl] = xn * gate_x * mult
        return 0

    lax.fori_loop(0, rows // chunk, gates, 0)

    def scan(t, h):
        r0 = pl.multiple_of(t * batch, batch)
        h = a_ref[pl.ds(r0, batch), :] * h + b_ref[pl.ds(r0, batch), :]
        b_ref[pl.ds(r0, batch), :] = h
        return h

    h = lax.fori_loop(0, nt, scan, h_ref[...], unroll=(8 if nt >= 8 and batch == SUBLANES else 1))
    h_ref[...] = h
    hl_ref[...] = h
    xs_ref[0:hist, :] = xs_ref[rows:rows + hist, :]

    def outp(c, _):
        r0 = pl.multiple_of(c * chunk, chunk)
        o_ref[pl.ds(r0, chunk), :] = (b_ref[pl.ds(r0, chunk), :] * jax.nn.gelu(y_ref[pl.ds(r0, chunk), :])
                                      * jax.nn.sigmoid(gl_ref[pl.ds(r0, chunk), :]))
        return 0

    lax.fori_loop(0, rows // chunk, outp, 0)


def _with_dst(kern, n_in):
    def wrapped(*refs):
        return kern(*refs[:n_in], *refs[n_in + 1:])
    return wrapped


def _shared_rows(dst, n_total):
    if dst is None:
        return [], [], jax.ShapeDtypeStruct((n_total, D), f32)
    return [pl.BlockSpec(memory_space=pl.ANY)], [dst], jax.ShapeDtypeStruct(dst.shape, dst.dtype)


def _lru(proj, row_off, n_rows, batch, rows_blk, reset_first, conv_init, h0, cw, cb, wxa, bx, ba, ap, dst=None):
    ob = row_off // rows_blk
    kern = functools.partial(_lru_kernel, batch=batch, reset_first=reset_first, chunk=128)
    n_in = 11
    dst_spec, dst_arg, out0 = _shared_rows(dst, proj.shape[0])
    return pl.pallas_call(
        kern if dst is None else _with_dst(kern, n_in),
        grid=(n_rows // rows_blk,),
        in_specs=[
            pl.BlockSpec((rows_blk, D), lambda i: (ob + i, COL_LRU_X)),
            pl.BlockSpec((rows_blk, D), lambda i: (ob + i, COL_LRU_Y)),
            pl.BlockSpec((rows_blk, D), lambda i: (ob + i, COL_GL)),
            _full((3 * batch, D)), _full((batch, D)), _full((4, D)), _full((1, D)),
            _full((D // LANES, LANES, 2 * LANES)), _full((1, D)), _full((1, D)), _full((1, D)),
        ] + dst_spec,
        out_specs=[pl.BlockSpec((rows_blk, D), lambda i: (ob + i, 0)), _full((batch, D))],
        out_shape=[out0, jax.ShapeDtypeStruct((batch, D), f32)],
        input_output_aliases={} if dst is None else {n_in: 0},
        scratch_shapes=[pltpu.VMEM((rows_blk + 3 * batch, D), f32), pltpu.VMEM((rows_blk, D), f32),
                        pltpu.VMEM((rows_blk, D), f32), pltpu.VMEM((batch, D), f32)],
        compiler_params=pltpu.CompilerParams(dimension_semantics=("arbitrary",), vmem_limit_bytes=48 * MIB),
        name="lru",
    )(proj, proj, proj, conv_init, h0, cw, cb, wxa, bx, ba, ap, *dst_arg)


TILE_ALPHA, TILE_BETA, TILE_W, TILE_KP, TILE_WR, TILE_V = range(6)
ROW_BR, ROW_KR, ROW_BONUS = range(3)
CONST_KK, CONST_KA, CONST_RK, CONST_LNW, CONST_LNB = range(5)
NAT_R, NAT_W, NAT_K, NAT_V, NAT_A = range(5)


def _wkv_prep(tiles_ref, rows_ref, gam_ref, t, g, r, w, k, v, a, const):
    kk = k * const(CONST_KK)
    nrm = jnp.sqrt(jnp.sum(kk * kk, axis=0, keepdims=True))
    kk = kk * (1.0 / jnp.maximum(nrm, 1e-12))
    beta = kk * a
    kp = k * (1.0 + (a - 1.0) * const(CONST_KA))
    gam_prev = gam_ref[g]
    gam = gam_prev * w
    gam_ref[g] = gam
    inv = 1.0 / gam
    tiles_ref[t, g, TILE_ALPHA] = -kk * gam_prev
    tiles_ref[t, g, TILE_BETA] = beta * inv
    tiles_ref[t, g, TILE_W] = gam
    tiles_ref[t, g, TILE_KP] = kp * inv
    tiles_ref[t, g, TILE_WR] = gam * r
    tiles_ref[t, g, TILE_V] = v
    rows_ref[t, g, ROW_BR:ROW_BR + 1, :] = jnp.sum(beta * r, axis=0, keepdims=True)
    rows_ref[t, g, ROW_KR:ROW_KR + 1, :] = jnp.sum(kp * r, axis=0, keepdims=True)
    rows_ref[t, g, ROW_BONUS:ROW_BONUS + 1, :] = jnp.sum(r * kp * const(CONST_RK), axis=0, keepdims=True)


def _rwkv_pre_kernel(r_ref, k_ref, v_ref, l_ref, sinit_ref, mu_ref, w0_ref, a0_ref, w2_ref, a2_ref, g2_ref, c_ref,
                     tiles_ref, rows_ref, go_ref, ps_ref, nat_ref, gam_ref, *, batch, chunk, chains_are_heads):
    rows = r_ref.shape[0]
    gam_ref[...] = jnp.ones_like(gam_ref)

    @pl.when(pl.program_id(0) == 0)
    def _():
        ps_ref[0:batch, :] = sinit_ref[...]

    ps_ref[batch:batch + rows, 0:D] = r_ref[...]
    ps_ref[batch:batch + rows, D:2 * D] = k_ref[...]
    ps_ref[batch:batch + rows, 2 * D:3 * D] = v_ref[...]
    ps_ref[batch:batch + rows, 3 * D:3 * D + LORA_PAD] = l_ref[...]

    def body(c, _):
        r0 = _aligned(c * chunk, chunk)

        def mixed(lo, hi):
            cur = ps_ref[pl.ds(_aligned(r0 + batch, SUBLANES), chunk), lo:hi]
            prev = ps_ref[pl.ds(r0, chunk), lo:hi]
            return cur + (prev - cur) * mu_ref[:, lo:hi]

        nat_ref[NAT_R, pl.ds(r0, chunk), :] = mixed(0, D)
        nat_ref[NAT_K, pl.ds(r0, chunk), :] = mixed(D, 2 * D)
        nat_ref[NAT_V, pl.ds(r0, chunk), :] = mixed(2 * D, 3 * D)
        lm = mixed(3 * D, 3 * D + LORA_PAD)
        xwa = lm[:, 0:LANES]
        xg = lm[:, LANES:3 * LANES]
        lw = jnp.dot(jnp.tanh(xwa).astype(bf16), w2_ref[...], preferred_element_type=f32)
        w_log = -jax.nn.softplus(-(w0_ref[...] + lw)) - 0.5
        nat_ref[NAT_W, pl.ds(r0, chunk), :] = jnp.exp(-jnp.exp(w_log))
        la = jnp.dot(xwa.astype(bf16), a2_ref[...], preferred_element_type=f32)
        nat_ref[NAT_A, pl.ds(r0, chunk), :] = jax.nn.sigmoid(a0_ref[...] + la)
        go_ref[pl.ds(r0, chunk), :] = jnp.dot(jax.nn.sigmoid(xg).astype(bf16), g2_ref[...],
                                              preferred_element_type=f32)
        return 0

    if chains_are_heads:
        lax.fori_loop(0, rows // chunk, body, 0)
        def tstep(t, _):
            r0 = pl.multiple_of(t * batch, batch)
            for hp in range(D // LANES):
                trs = [nat_ref[q, pl.ds(r0, batch), hp * LANES:(hp + 1) * LANES].T for q in range(5)]
                for h2 in range(2):
                    h = 2 * hp + h2
                    _wkv_prep(tiles_ref, rows_ref, gam_ref, t, h, *[tr[h2 * HD:(h2 + 1) * HD] for tr in trs],
                              lambda i, h=h: c_ref[h, i])
            return 0

        lax.fori_loop(0, rows // batch, tstep, 0)
    else:
        half = LANES // 2
        lane = lax.broadcasted_iota(jnp.int32, (HD, LANES), 1)

        def load_pair(q, tp):
            zz = jnp.concatenate(
                [nat_ref[q, pl.ds(_aligned(tp * 2 * SUBLANES + tt * SUBLANES, SUBLANES), SUBLANES),
                         hp * LANES:(hp + 1) * LANES] for tt in range(2) for hp in range(D // LANES)], axis=0)
            tr = zz.T
            top, bot = tr[0:HD], tr[HD:2 * HD]
            d0 = jnp.where(lane < half, top, pltpu.roll(bot, half, axis=1))
            d1 = jnp.where(lane < half, pltpu.roll(top, half, axis=1), bot)
            return d0, d1

        def pair(tp, _):
            pairs = [load_pair(q, tp) for q in range(5)]
            for tt in range(2):
                _wkv_prep(tiles_ref, rows_ref, gam_ref, 2 * tp + tt, 0, *[p[tt] for p in pairs], lambda i: c_ref[0, i])
            return 0

        n_chunks = rows // chunk
        pairs_per_chunk = chunk // (2 * SUBLANES)
        body(0, 0)
        for c in range(1, n_chunks):
            body(c, 0)
            for tp in range((c - 1) * pairs_per_chunk, c * pairs_per_chunk):
                pair(tp, 0)
        for tp in range((n_chunks - 1) * pairs_per_chunk, n_chunks * pairs_per_chunk):
            pair(tp, 0)

    ps_ref[0:batch, :] = ps_ref[rows:rows + batch, :]


def _rwkv_pre(proj, row_off, n_rows, batch, rows_blk, groups, shift_init, consts, mu, w0, a0, w2p, a2p, g2p,
              dst=None):
    ob = row_off // rows_blk
    width = 3 * D + LORA_PAD
    steps_blk = rows_blk // batch
    kern = functools.partial(_rwkv_pre_kernel, batch=batch, chunk=128, chains_are_heads=groups > 1)
    n_in = 12
    dst_spec, dst_arg, out_g = _shared_rows(dst, proj.shape[0])
    return pl.pallas_call(
        kern if dst is None else _with_dst(kern, n_in),
        grid=(n_rows // rows_blk,),
        in_specs=[
            pl.BlockSpec((rows_blk, D), lambda i: (ob + i, COL_R)),
            pl.BlockSpec((rows_blk, D), lambda i: (ob + i, COL_K)),
            pl.BlockSpec((rows_blk, D), lambda i: (ob + i, COL_V)),
            pl.BlockSpec((rows_blk, LORA_PAD), lambda i: (ob + i, COL_LORA)),
            _full((batch, width)), _full((1, width)), _full((1, D)), _full((1, D)),
            _full((LANES, D)), _full((LANES, D)), _full((2 * LANES, D)), _full((groups, 5, HD, LANES)),
        ] + dst_spec,
        out_specs=[pl.BlockSpec((steps_blk, groups, 6, HD, LANES), lambda i: (i, 0, 0, 0, 0)),
                   pl.BlockSpec((steps_blk, groups, SUBLANES, LANES), lambda i: (i, 0, 0, 0)),
                   pl.BlockSpec((rows_blk, D), lambda i: (ob + i, 0))],
        out_shape=[jax.ShapeDtypeStruct((n_rows // batch, groups, 6, HD, LANES), f32),
                   jax.ShapeDtypeStruct((n_rows // batch, groups, SUBLANES, LANES), f32),
                   out_g],
        input_output_aliases={} if dst is None else {n_in: 2},
        scratch_shapes=[pltpu.VMEM((rows_blk + batch, width), f32), pltpu.VMEM((5, rows_blk, D), f32),
                        pltpu.VMEM((groups, HD, LANES), f32)],
        compiler_params=pltpu.CompilerParams(dimension_semantics=("arbitrary",), vmem_limit_bytes=56 * MIB),
        name="rwkv_pre",
    )(proj, proj, proj, proj, shift_init, mu, w0, a0, w2p, a2p, g2p, consts, *dst_arg)


STATE_ROWS = HD * HD
KEY_UNROLL = 32


def _wkv_step(s_ref, tiles_ref, rows_ref, y_ref, t, window):
    def key_row(tile, j):
        return tiles_ref[t, 0, tile, pl.ds(j, 1), :]

    def reduce_keys(j, acc):
        sa, y0 = acc
        s = s_ref[pl.ds(pl.multiple_of(j * HD, HD), HD), :]
        return sa + s * key_row(TILE_ALPHA, j), y0 + s * key_row(TILE_WR, j)

    zero = jnp.zeros((HD, LANES), f32)
    sa, y0 = lax.fori_loop(0, HD, reduce_keys, (zero, zero), unroll=KEY_UNROLL)
    v = tiles_ref[t, 0, TILE_V]

    def update_keys(j, _):
        rows = pl.ds(pl.multiple_of(j * HD, HD), HD)
        s_ref[rows, :] = s_ref[rows, :] + sa * key_row(TILE_BETA, j) + v * key_row(TILE_KP, j)
        return 0

    lax.fori_loop(0, HD, update_keys, 0, unroll=KEY_UNROLL)
    y_ref[t, 0] = y0 + sa * rows_ref[t, 0, ROW_BR:ROW_BR + 1, :] + v * rows_ref[t, 0, ROW_KR:ROW_KR + 1, :]

    @pl.when((t + 1) % window == 0)
    def _():
        def rescale(j, _):
            rows = pl.ds(pl.multiple_of(j * HD, HD), HD)
            s_ref[rows, :] = s_ref[rows, :] * key_row(TILE_W, j)
            return 0

        lax.fori_loop(0, HD, rescale, 0, unroll=KEY_UNROLL)


def _wkv_scan_kernel(tiles_ref, rows_ref, st_ref, y_ref, sfin_ref, s_ref, *, steps, window):
    tb = pl.program_id(1)

    def value_rows(i):
        return pl.ds(i, HD, stride=HD)

    @pl.when(tb == 0)
    def _():
        def init(c, _):
            tr = st_ref[:, pl.ds(pl.multiple_of(c * LANES, LANES), LANES)].T
            for i2 in range(2):
                s_ref[value_rows(2 * c + i2), :] = tr[i2 * HD:(i2 + 1) * HD]
            return 0

        lax.fori_loop(0, STATE_ROWS // LANES, init, 0, unroll=8)

    def step(t, _):
        _wkv_step(s_ref, tiles_ref, rows_ref, y_ref, t, window)
        return 0

    lax.fori_loop(0, steps, step, 0)

    @pl.when(tb == pl.num_programs(1) - 1)
    def _():
        def fin(c, _):
            pair = jnp.concatenate([s_ref[value_rows(2 * c + i2), :] for i2 in range(2)], axis=0)
            sfin_ref[:, pl.ds(pl.multiple_of(c * LANES, LANES), LANES)] = pair.T
            return 0

        lax.fori_loop(0, STATE_ROWS // LANES, fin, 0, unroll=8)


def _wkv_scan(tiles, rows, state, steps_blk, window):
    n_steps, groups = tiles.shape[0], tiles.shape[1]
    assert steps_blk % window == 0
    kern = functools.partial(_wkv_scan_kernel, steps=steps_blk, window=window)
    sblk = pl.BlockSpec((LANES, STATE_ROWS), lambda g, i: (0, g))
    return pl.pallas_call(
        kern,
        grid=(groups, n_steps // steps_blk),
        in_specs=[pl.BlockSpec((steps_blk, 1, 6, HD, LANES), lambda g, i: (i, g, 0, 0, 0)),
                  pl.BlockSpec((steps_blk, 1, SUBLANES, LANES), lambda g, i: (i, g, 0, 0)),
                  sblk],
        out_specs=[pl.BlockSpec((steps_blk, 1, HD, LANES), lambda g, i: (i, g, 0, 0)), sblk],
        out_shape=[jax.ShapeDtypeStruct((n_steps, groups, HD, LANES), f32),
                   jax.ShapeDtypeStruct((LANES, groups * STATE_ROWS), f32)],
        scratch_shapes=[pltpu.VMEM((STATE_ROWS, LANES), f32)],
        compiler_params=pltpu.CompilerParams(dimension_semantics=("arbitrary", "arbitrary"),
                                             vmem_limit_bytes=48 * MIB),
        name="wkv_scan",
    )(tiles, rows, state)


def _wkv_norm(y, v, bonus, lnw, lnb):
    mean = jnp.sum(y, axis=0, keepdims=True) * (1.0 / HD)
    d = y - mean
    var = jnp.sum(d * d, axis=0, keepdims=True) * (1.0 / HD)
    return d * lax.rsqrt(var + GN_EPS) * lnw + lnb + bonus * v


def _wkv_post_kernel(y_ref, v_ref, rows_ref, c_ref, o_ref, *, batch, chains_are_heads):
    steps = y_ref.shape[0]

    def normed(t, g):
        return _wkv_norm(y_ref[t, g], v_ref[t, g, 0], rows_ref[t, g, ROW_BONUS:ROW_BONUS + 1, :],
                         c_ref[g, CONST_LNW], c_ref[g, CONST_LNB])

    if chains_are_heads:
        def tstep(t, _):
            r0 = pl.multiple_of(t * batch, batch)
            for hp in range(D // LANES):
                two = jnp.concatenate([normed(t, 2 * hp), normed(t, 2 * hp + 1)], axis=0)
                o_ref[pl.ds(r0, batch), hp * LANES:(hp + 1) * LANES] = two.T
            return 0

        lax.fori_loop(0, steps, tstep, 0)
    else:
        half = LANES // 2
        lane = lax.broadcasted_iota(jnp.int32, (HD, LANES), 1)

        def pair(tp, _):
            o0 = normed(2 * tp, 0)
            o1 = normed(2 * tp + 1, 0)
            top = jnp.where(lane < half, o0, pltpu.roll(o1, half, axis=1))
            bot = jnp.where(lane < half, pltpu.roll(o0, half, axis=1), o1)
            zz = jnp.concatenate([top, bot], axis=0).T
            for tt in range(2):
                row = pl.multiple_of(tp * 2 * SUBLANES + tt * SUBLANES, SUBLANES)
                for hp in range(D // LANES):
                    src = tt * HD + hp * SUBLANES
                    o_ref[pl.ds(row, SUBLANES), hp * LANES:(hp + 1) * LANES] = zz[src:src + SUBLANES, :]
            return 0

        lax.fori_loop(0, steps // 2, pair, 0, unroll=4)


def _outproj_kernel(ml_ref, g_ref, gr_ref, x_ref, yp_ref, vp_ref, rp_ref, cp_ref, ysm_ref, vs_ref, rs_ref, cs_ref,
                    wo_ref, n2_ref, rh_ref, rl_ref, rb_ref,
                    x1_ref, t_ref, selt_ref, seln_ref, op_ref, *, prompt_tiles, prompt_batch, sample_batch):
    tm = x_ref.shape[0]
    i = pl.program_id(0)

    @pl.when(i < prompt_tiles)
    def _():
        _wkv_post_kernel(yp_ref, vp_ref, rp_ref, cp_ref, op_ref, batch=prompt_batch, chains_are_heads=False)

    @pl.when(i >= prompt_tiles)
    def _():
        _wkv_post_kernel(ysm_ref, vs_ref, rs_ref, cs_ref, op_ref, batch=sample_batch, chains_are_heads=True)

    merged = ml_ref[...] + jax.nn.sigmoid(gr_ref[...]) * (op_ref[...] * g_ref[...])
    x1 = x_ref[...] + jnp.dot(merged.astype(bf16), wo_ref[...], preferred_element_type=f32)
    x1_ref[...] = x1
    t = x1 * lax.rsqrt(jnp.mean(x1 * x1, axis=-1, keepdims=True) + NORM_EPS) * n2_ref[...]
    th = t.astype(bf16)
    t_ref[...] = th
    tl = (t - th.astype(f32)).astype(bf16)
    nt_dims = (((1,), (1,)), ((), ()))
    lg = (lax.dot_general(rh_ref[...], th, nt_dims, preferred_element_type=f32)
          + lax.dot_general(rh_ref[...], tl, nt_dims, preferred_element_type=f32)
          + lax.dot_general(rl_ref[...], th, nt_dims, preferred_element_type=f32)) + rb_ref[...]
    row = lax.broadcasted_iota(jnp.int32, (EPG, tm), 0).astype(f32)
    neg = jnp.float32(-jnp.inf)
    glog = jnp.where(row < N_GROUPS, lg[0:EPG], neg)
    ge = jnp.exp(glog - jnp.max(glog, axis=0, keepdims=True))
    pg = ge / jnp.sum(ge, axis=0, keepdims=True)
    p_top = jnp.max(pg, axis=0, keepdims=True)
    g_idx = jnp.min(jnp.where(pg == p_top, row, EPG), axis=0, keepdims=True)
    le = jnp.zeros((EPG, tm), f32)
    for g in range(N_GROUPS):
        le = jnp.where(g_idx == g, lg[EPG * (g + 1):EPG * (g + 2)], le)
    qe = jnp.exp(le - jnp.max(le, axis=0, keepdims=True))
    q = qe / jnp.sum(qe, axis=0, keepdims=True)
    q1 = jnp.max(q, axis=0, keepdims=True)
    i1 = jnp.min(jnp.where(q == q1, row, EPG), axis=0, keepdims=True)
    qm = jnp.where(row == i1, -1.0, q)
    q2 = jnp.max(qm, axis=0, keepdims=True)
    i2 = jnp.min(jnp.where(qm == q2, row, EPG), axis=0, keepdims=True)
    qs = q1 + q2
    sel = jnp.concatenate([g_idx * EPG + i1, g_idx * EPG + i2, q1 / qs * p_top, q2 / qs * p_top,
                           jnp.zeros((SUBLANES - 4, tm), f32)], axis=0)
    selt_ref[...] = sel
    sel_pad = jnp.concatenate([sel, jnp.zeros((LANES - SUBLANES, tm), f32)], axis=0)
    for c in range(tm // LANES):
        seln_ref[c * LANES:(c + 1) * LANES, :] = sel_pad[:, c * LANES:(c + 1) * LANES].T


def _outproj(ml, g, proj, x, scan_p, scan_s, wo, n2, rh, rl, rb, prompt_batch, sample_batch, tm=512):
    n = x.shape[0]
    y_p, tiles_p, rows_p, consts_p = scan_p
    y_s, tiles_s, rows_s, consts_s = scan_s
    n_p = y_p.shape[0] * prompt_batch
    sp, ss = tm // prompt_batch, tm // sample_batch
    prompt_tiles = n_p // tm
    sample_tiles = (n - n_p) // tm
    assert n_p % tm == 0 and sp % 2 == 0 and y_s.shape[0] == sample_tiles * ss
    pidx = lambda i: jnp.minimum(i, prompt_tiles - 1)
    sidx = lambda i: jnp.clip(i - prompt_tiles, 0, sample_tiles - 1)
    groups = y_s.shape[1]
    blk = pl.BlockSpec((tm, D), lambda i: (i, 0))
    kern = functools.partial(_outproj_kernel, prompt_tiles=prompt_tiles, prompt_batch=prompt_batch,
                             sample_batch=sample_batch)
    return pl.pallas_call(
        kern,
        grid=(n // tm,),
        in_specs=[blk, blk, pl.BlockSpec((tm, D), lambda i: (i, COL_GR)), blk,
                  pl.BlockSpec((sp, 1, HD, LANES), lambda i: (pidx(i), 0, 0, 0)),
                  pl.BlockSpec((sp, 1, 1, HD, LANES), lambda i: (pidx(i), 0, TILE_V, 0, 0)),
                  pl.BlockSpec((sp, 1, SUBLANES, LANES), lambda i: (pidx(i), 0, 0, 0)),
                  _full((1, 5, HD, LANES)),
                  pl.BlockSpec((ss, groups, HD, LANES), lambda i: (sidx(i), 0, 0, 0)),
                  pl.BlockSpec((ss, groups, 1, HD, LANES), lambda i: (sidx(i), 0, TILE_V, 0, 0)),
                  pl.BlockSpec((ss, groups, SUBLANES, LANES), lambda i: (sidx(i), 0, 0, 0)),
                  _full((groups, 5, HD, LANES)),
                  _full((D, D)), _full((1, D)), _full((ROUTER_ROWS, D)), _full((ROUTER_ROWS, D)),
                  _full((ROUTER_ROWS, 1))],
        out_specs=[blk, blk, pl.BlockSpec((SUBLANES, tm), lambda i: (0, i)),
                   pl.BlockSpec((tm, LANES), lambda i: (i, 0))],
        out_shape=[jax.ShapeDtypeStruct((n, D), f32), jax.ShapeDtypeStruct((n, D), bf16),
                   jax.ShapeDtypeStruct((SUBLANES, n), f32), jax.ShapeDtypeStruct((n, LANES), f32)],
        scratch_shapes=[pltpu.VMEM((tm, D), f32)],
        compiler_params=pltpu.CompilerParams(dimension_semantics=("arbitrary",), vmem_limit_bytes=56 * MIB),
        name="outproj_router",
    )(ml, g, proj, x, y_p, tiles_p, rows_p, consts_p, y_s, tiles_s, rows_s, consts_s, wo, n2, rh, rl, rb)


MOE_CHUNK = 32
MOE_ROWS = 128
MOE_GATHER = 512
MOE_EXPERTS_PER_STEP = 4
MOE_OVERRUN = MOE_ROWS
MOE_CAP = -(-(2 * TOKEN_TILE + N_EXPERTS * (MOE_CHUNK - 1) + MOE_OVERRUN) // MOE_GATHER) * MOE_GATHER
SEL_E1, SEL_E2, SEL_C1, SEL_C2 = range(4)


def _route_meta_kernel(seln_ref, o_ref):
    tm = seln_ref.shape[0]
    lane = lax.broadcasted_iota(jnp.int32, (tm, LANES), 1).astype(f32)
    sel = seln_ref[...]
    hit = (lane == sel[:, SEL_E1:SEL_E1 + 1]) | (lane == sel[:, SEL_E2:SEL_E2 + 1])
    cnt = jnp.sum(jnp.where(hit, 1.0, 0.0), axis=0, keepdims=True)
    nchunk = jnp.floor((cnt + (MOE_CHUNK - 1)) * (1.0 / MOE_CHUNK))
    upper = (lax.broadcasted_iota(jnp.int32, (LANES, LANES), 0)
             < lax.broadcasted_iota(jnp.int32, (LANES, LANES), 1))
    offs = jnp.dot(jnp.broadcast_to(nchunk, (SUBLANES, LANES)).astype(bf16), jnp.where(upper, 1.0, 0.0).astype(bf16),
                   preferred_element_type=f32)
    row = lax.broadcasted_iota(jnp.int32, (SUBLANES, LANES), 0)
    o_ref[0] = jnp.where(row == 0, nchunk, jnp.where(row == 1, offs, 0.0)).astype(jnp.int32)


def _route_meta(seln, tm):
    tiles = seln.shape[0] // tm
    return pl.pallas_call(
        _route_meta_kernel,
        grid=(tiles,),
        in_specs=[pl.BlockSpec((tm, LANES), lambda i: (i, 0))],
        out_specs=pl.BlockSpec((1, SUBLANES, LANES), lambda i: (i, 0, 0)),
        out_shape=jax.ShapeDtypeStruct((tiles, SUBLANES, LANES), jnp.int32),
        compiler_params=pltpu.CompilerParams(dimension_semantics=("arbitrary",)),
        name="route_meta",
    )(seln)


def _moe_experts_kernel(nch_ref, off_ref, t_ref, selt_ref, tri_ref, wgu_ref, wd_ref, ys_ref, pos_ref,
                        xs_ref, cw_ref):
    i = pl.program_id(0)
    e = pl.program_id(1)
    tm = t_ref.shape[0]
    used_rows = (off_ref[i * N_EXPERTS + N_EXPERTS - 1] + nch_ref[i * N_EXPERTS + N_EXPERTS - 1]) * MOE_CHUNK

    @pl.when(e == 0)
    def _():
        ys_ref[...] = jnp.zeros_like(ys_ref)
        e1 = selt_ref[SEL_E1:SEL_E1 + 1, :]
        e2 = selt_ref[SEL_E2:SEL_E2 + 1, :]
        erow = lax.broadcasted_iota(jnp.int32, (N_EXPERTS, tm), 0).astype(f32)
        oh1 = jnp.where(erow == e1, 1.0, 0.0)
        oh2 = jnp.where(erow == e2, 1.0, 0.0)
        before1 = jnp.dot(oh1.astype(bf16), tri_ref[...], preferred_element_type=f32)
        before2 = jnp.dot(oh2.astype(bf16), tri_ref[...], preferred_element_type=f32)
        cnt1 = jnp.sum(oh1, axis=1, keepdims=True)
        cnt2 = jnp.sum(oh2, axis=1, keepdims=True)
        nchunk = jnp.floor((cnt1 + cnt2 + (MOE_CHUNK - 1)) * (1.0 / MOE_CHUNK))
        lower = (lax.broadcasted_iota(jnp.int32, (N_EXPERTS, LANES), 1)
                 < lax.broadcasted_iota(jnp.int32, (N_EXPERTS, LANES), 0))
        nchunk_rows = jnp.concatenate([jnp.broadcast_to(nchunk, (N_EXPERTS, LANES)),
                                       jnp.zeros((LANES - N_EXPERTS, LANES), f32)], axis=0)
        start = jnp.dot(jnp.where(lower, 1.0, 0.0).astype(bf16), nchunk_rows.astype(bf16),
                        preferred_element_type=f32)[:, 0:1] * MOE_CHUNK
        pos1 = jnp.sum(oh1 * (start + before1), axis=0, keepdims=True)
        pos2 = jnp.sum(oh2 * (start + cnt1 + before2), axis=0, keepdims=True)
        pos_ref[...] = jnp.concatenate([pos1, pos2, jnp.zeros((SUBLANES - 2, tm), f32)], axis=0)
        c1 = selt_ref[SEL_C1:SEL_C1 + 1, :]
        c2 = selt_ref[SEL_C2:SEL_C2 + 1, :]
        for k in range(MOE_CAP // MOE_GATHER):
            @pl.when(k * MOE_GATHER < used_rows + MOE_OVERRUN)
            def _(k=k):
                ridx = (lax.broadcasted_iota(jnp.int32, (MOE_GATHER, tm), 0) + k * MOE_GATHER).astype(f32)
                p1 = ridx == pos1
                p2 = ridx == pos2
                onehot = jnp.where(p1 | p2, 1.0, 0.0).astype(bf16)
                xs_ref[k * MOE_GATHER:(k + 1) * MOE_GATHER, :] = jnp.dot(
                    onehot, t_ref[...], preferred_element_type=f32).astype(bf16)
                w = jnp.sum(jnp.where(p1, c1, 0.0) + jnp.where(p2, c2, 0.0), axis=1, keepdims=True)
                cw_ref[k * MOE_GATHER:(k + 1) * MOE_GATHER, :] = jnp.broadcast_to(w, (MOE_GATHER, LANES))

    def expert_rows(u, start):
        rows = pl.ds(pl.multiple_of(start, MOE_CHUNK), MOE_ROWS)
        gu = jnp.dot(xs_ref[rows, :], wgu_ref[u], preferred_element_type=f32)
        w = cw_ref[rows, :]
        h = jax.nn.silu(gu[:, :D_EXPERT]) * gu[:, D_EXPERT:] * jnp.concatenate([w, w], axis=1)
        ys_ref[0, rows, :] = jnp.dot(h.astype(bf16), wd_ref[u], preferred_element_type=f32).astype(bf16)

    per_block = MOE_ROWS // MOE_CHUNK
    for u in range(wgu_ref.shape[0]):
        expert = i * N_EXPERTS + e * wgu_ref.shape[0] + u
        base = off_ref[expert] * MOE_CHUNK
        expert_rows(u, base)

        def more(c, _, u=u, base=base):
            expert_rows(u, base + (c + 1) * MOE_ROWS)
            return 0

        lax.fori_loop(0, jnp.maximum(nch_ref[expert] - 1, 0) // per_block, more, 0)


def _moe_experts(nch, off, t, selt, tri, wgu, wd, tm):
    n = t.shape[0]
    tiles = n // tm
    grid_spec = pltpu.PrefetchScalarGridSpec(
        num_scalar_prefetch=2,
        grid=(tiles, N_EXPERTS // MOE_EXPERTS_PER_STEP),
        in_specs=[
            pl.BlockSpec((tm, D), lambda i, e, nch, off: (i, 0)),
            pl.BlockSpec((SUBLANES, tm), lambda i, e, nch, off: (0, i)),
            pl.BlockSpec((tm, tm), lambda i, e, nch, off: (0, 0)),
            pl.BlockSpec((MOE_EXPERTS_PER_STEP, D, 2 * D_EXPERT), lambda i, e, nch, off: (e, 0, 0)),
            pl.BlockSpec((MOE_EXPERTS_PER_STEP, D_EXPERT, D), lambda i, e, nch, off: (e, 0, 0)),
        ],
        out_specs=[pl.BlockSpec((1, MOE_CAP, D), lambda i, e, nch, off: (i, 0, 0)),
                   pl.BlockSpec((SUBLANES, tm), lambda i, e, nch, off: (0, i))],
        scratch_shapes=[pltpu.VMEM((MOE_CAP, D), bf16), pltpu.VMEM((MOE_CAP, LANES), f32)],
    )
    return pl.pallas_call(
        _moe_experts_kernel,
        grid_spec=grid_spec,
        out_shape=[jax.ShapeDtypeStruct((tiles, MOE_CAP, D), bf16), jax.ShapeDtypeStruct((SUBLANES, n), f32)],
        compiler_params=pltpu.CompilerParams(dimension_semantics=("arbitrary", "arbitrary"),
                                             vmem_limit_bytes=56 * MIB),
        name="moe_experts",
    )(nch, off, t, selt, tri, wgu, wd)


def _moe_combine_kernel(nch_ref, off_ref, ysort_ref, pos_ref, x1_ref, fn_ref, yp_ref, ys_ref, acc_ref, yf_ref,
                        *, prompt_tiles, parts):
    i = pl.program_id(0)
    part = pl.program_id(1)
    rows = x1_ref.shape[0]
    bp, tsteps = yp_ref.shape[0], yp_ref.shape[1]
    bs = ys_ref.shape[0]
    ssteps = ys_ref.shape[1] // parts
    used_rows = (off_ref[i * N_EXPERTS + N_EXPERTS - 1] + nch_ref[i * N_EXPERTS + N_EXPERTS - 1]) * MOE_CHUNK

    pos_t = jnp.concatenate([pos_ref[...], jnp.zeros((LANES - SUBLANES, rows), f32)], axis=0)
    pos_n = jnp.concatenate([pos_t[:, c * LANES:(c + 1) * LANES].T for c in range(rows // LANES)], axis=0)
    pos1 = pos_n[:, 0:1]
    pos2 = pos_n[:, 1:2]
    acc_ref[...] = x1_ref[...]
    for k in range(MOE_CAP // MOE_GATHER):
        @pl.when(k * MOE_GATHER < used_rows)
        def _(k=k):
            cidx = (lax.broadcasted_iota(jnp.int32, (rows, MOE_GATHER), 1) + k * MOE_GATHER).astype(f32)
            onehot = jnp.where((cidx == pos1) | (cidx == pos2), 1.0, 0.0).astype(bf16)
            acc_ref[...] += jnp.dot(onehot, ysort_ref[0, k * MOE_GATHER:(k + 1) * MOE_GATHER, :],
                                    preferred_element_type=f32)

    xo = acc_ref[...]
    y = xo * lax.rsqrt(jnp.mean(xo * xo, axis=-1, keepdims=True) + NORM_EPS) * fn_ref[...]
    for cb in range(D // LANES):
        yf_ref[cb] = y[:, cb * LANES:(cb + 1) * LANES]

    @pl.when(i < prompt_tiles)
    def _():
        for b in range(bp):
            yp_ref[b] = jnp.concatenate([yf_ref[cb, pl.ds(b, tsteps, stride=bp), :] for cb in range(D // LANES)],
                                        axis=1)

    for p in range(parts):
        @pl.when((i >= prompt_tiles) & (part == p))
        def _(p=p):
            for t in range(ssteps):
                ys_ref[:, p * ssteps + t, :] = jnp.concatenate(
                    [yf_ref[cb, t * bs:(t + 1) * bs, :] for cb in range(D // LANES)], axis=1)


def _moe_combine(nch, off, ysort, pos, x1, fn, prompt_shape, sample_shape, tm, parts=2):
    n = x1.shape[0]
    bp, tp, _ = prompt_shape
    bs, ts, _ = sample_shape
    rows = tm // parts
    assert rows % bp == 0 and tp % (rows // bp) == 0 and bs * ts == tm and ts % parts == 0
    prompt_tiles = bp * tp // tm
    last_prompt_blk = prompt_tiles * parts - 1
    kern = functools.partial(_moe_combine_kernel, prompt_tiles=prompt_tiles, parts=parts)
    grid_spec = pltpu.PrefetchScalarGridSpec(
        num_scalar_prefetch=2,
        grid=(n // tm, parts),
        in_specs=[
            pl.BlockSpec((1, MOE_CAP, D), lambda i, p, nch, off: (i, 0, 0)),
            pl.BlockSpec((SUBLANES, rows), lambda i, p, nch, off: (0, i * parts + p)),
            pl.BlockSpec((rows, D), lambda i, p, nch, off: (i * parts + p, 0)),
            pl.BlockSpec((1, D), lambda i, p, nch, off: (0, 0)),
        ],
        out_specs=[pl.BlockSpec((bp, rows // bp, D),
                                lambda i, p, nch, off: (0, jnp.minimum(i * parts + p, last_prompt_blk), 0)),
                   pl.BlockSpec((bs, ts, D), lambda i, p, nch, off: (0, 0, 0))],
        scratch_shapes=[pltpu.VMEM((rows, D), f32), pltpu.VMEM((D // LANES, rows, LANES), f32)],
    )
    return pl.pallas_call(
        kern,
        grid_spec=grid_spec,
        out_shape=[jax.ShapeDtypeStruct(prompt_shape, f32), jax.ShapeDtypeStruct(sample_shape, f32)],
        compiler_params=pltpu.CompilerParams(dimension_semantics=("arbitrary", "arbitrary"),
                                             vmem_limit_bytes=56 * MIB),
        name="moe_combine",
    )(nch, off, ysort, pos, x1, fn)


def _chain_tiles_prompt(vec, batch):
    t = vec.reshape(HEADS // 2, 2, HD)
    t = jnp.transpose(t, (2, 1, 0))
    return jnp.broadcast_to(t[..., None], (HD, 2, HEADS // 2, batch)).reshape(HD, LANES)


def _chain_tiles_sample(vec):
    return jnp.broadcast_to(vec.reshape(HEADS, HD)[..., None], (HEADS, HD, LANES))


def kernel(x_prompt, x_sample, state_conv, state_lru, state_shift, state_wkv, norm1, w_in, conv_w, conv_b, lru_wx, lru_bx, lru_wa, lru_ba, lru_a_param, rwkv_mu, rwkv_w0, rwkv_w2, rwkv_a0, rwkv_a2, rwkv_g2, rwkv_k_k, rwkv_k_a, rwkv_r_k, rwkv_ln_w, rwkv_ln_b, w_out, norm2, router_group, router_group_b, router_expert, router_expert_b, exp_gate, exp_up, exp_down, final_norm):
    bp, tp, _ = x_prompt.shape
    bs, ts, _ = x_sample.shape
    assert norm1.shape[0] == 1 and bp * HEADS == LANES and bs == LANES and tp % 64 == 0 and ts % 2 == 0
    n_p, n_s = bp * tp, bs * ts
    c_rw = 2 * D
    c_lora = c_rw + 3 * D
    c_gl = c_lora + LORA

    w = w_in[0]
    w_all = jnp.concatenate([w[:, :c_lora], w[:, c_gl:], w[:, c_lora:c_gl],
                             jnp.zeros((D, LORA_PAD - LORA), f32)], axis=1).astype(bf16)
    mu = rwkv_mu[0]
    mu_all = jnp.concatenate([mu[:3 * D], mu[3 * D:], jnp.zeros((LORA_PAD - LORA,), f32)])[None]
    w2p = jnp.concatenate([rwkv_w2[0], jnp.zeros((LANES - LORA_W, D), f32)]).astype(bf16)
    a2p = jnp.concatenate([jnp.zeros((LORA_W, D), f32), rwkv_a2[0]]).astype(bf16)
    g2p = jnp.concatenate([rwkv_g2[0], jnp.zeros((2 * LANES - LORA_G, D), f32)]).astype(bf16)
    wxa = jnp.concatenate([lru_wx[0], lru_wa[0]], axis=-1).astype(bf16)
    row = lambda v: v.reshape(1, -1)
    rk = rwkv_r_k[0].reshape(D)
    cvecs = (rwkv_k_k[0], rwkv_k_a[0], rk, rwkv_ln_w[0], rwkv_ln_b[0])
    consts_p = jnp.stack([_chain_tiles_prompt(v, bp) for v in cvecs])[None]
    consts_s = jnp.stack([_chain_tiles_sample(v) for v in cvecs], axis=1)
    rw = jnp.zeros((ROUTER_ROWS, D), f32)
    rw = rw.at[0:N_GROUPS].set(router_group[0].T).at[EPG:EPG + N_EXPERTS].set(router_expert[0].T)
    rh = rw.astype(bf16)
    rl = (rw - rh.astype(f32)).astype(bf16)
    rb = jnp.zeros((ROUTER_ROWS, 1), f32)
    rb = rb.at[0:N_GROUPS, 0].set(router_group_b[0]).at[EPG:EPG + N_EXPERTS, 0].set(router_expert_b[0])
    wgu = jnp.concatenate([exp_gate[0], exp_up[0]], axis=-1).astype(bf16)
    wd = exp_down[0].astype(bf16)
    wo = w_out[0].astype(bf16)

    proj, x = _inproj(x_prompt, x_sample, row(norm1[0]), w_all, tm=TOKEN_TILE)

    lru_args = (conv_w[0], row(conv_b[0]), wxa, row(lru_bx[0]), row(lru_ba[0]), row(lru_a_param[0]))
    conv_init_s = jnp.transpose(state_conv[0], (1, 0, 2)).reshape(3 * bs, D)
    ml, hl_p = _lru(proj, 0, n_p, bp, 512, True, jnp.zeros((3 * bp, D), f32), jnp.zeros((bp, D), f32), *lru_args)
    ml, hl_s = _lru(proj, n_p, n_s, bs, 512, False, conv_init_s, state_lru[0], *lru_args, dst=ml)

    sh = state_shift[0]
    shift_init_s = jnp.concatenate([sh, jnp.zeros((bs, LORA_PAD - LORA), f32)], axis=1)
    pre_args = (mu_all, row(rwkv_w0[0]), row(rwkv_a0[0]), w2p, a2p, g2p)
    pre_rows = 256
    tiles_p, rows_p, g = _rwkv_pre(proj, 0, n_p, bp, pre_rows, 1, jnp.zeros((bp, 3 * D + LORA_PAD), f32),
                                   consts_p, *pre_args)
    tiles_s, rows_s, g = _rwkv_pre(proj, n_p, n_s, bs, pre_rows, HEADS, shift_init_s, consts_s, *pre_args, dst=g)

    y_p, sfin_p = _wkv_scan(tiles_p, rows_p, jnp.zeros((LANES, STATE_ROWS), f32), 64, pre_rows // bp)
    y_s, sfin_s = _wkv_scan(tiles_s, rows_s, state_wkv[0].reshape(bs, HEADS * STATE_ROWS), ts, pre_rows // bs)

    x1, t, selt, seln = _outproj(ml, g, proj, x, (y_p, tiles_p, rows_p, consts_p), (y_s, tiles_s, rows_s, consts_s),
                                 wo, row(norm2[0]), rh, rl, rb, bp, bs)
    meta = _route_meta(seln, TOKEN_TILE)
    nch = meta[:, 0, :N_EXPERTS].reshape(-1)
    off = meta[:, 1, :N_EXPERTS].reshape(-1)
    tri = jnp.triu(jnp.ones((TOKEN_TILE, TOKEN_TILE), bf16), k=1)
    ysort, pos = _moe_experts(nch, off, t, selt, tri, wgu, wd, TOKEN_TILE)
    y_prompt, y_sample = _moe_combine(nch, off, ysort, pos, x1, row(final_norm), x_prompt.shape, x_sample.shape,
                                      TOKEN_TILE)

    def last_rows(lo, hi, n_end, batch, steps):
        return proj[n_end - steps * batch:n_end, lo:hi].reshape(steps, batch, hi - lo)

    conv_p = jnp.transpose(last_rows(0, D, n_p, bp, 3), (1, 0, 2))[None]
    conv_s = jnp.transpose(last_rows(0, D, n_p + n_s, bs, 3), (1, 0, 2))[None]

    def shift_rows(n_end, batch):
        return jnp.concatenate([last_rows(2 * D, 5 * D, n_end, batch, 1)[0],
                                last_rows(7 * D, 7 * D + LORA, n_end, batch, 1)[0]], axis=1)[None]

    wkv_p = jnp.transpose(sfin_p.reshape(2, HEADS // 2, bp, HD, HD), (2, 1, 0, 3, 4)).reshape(1, bp, HEADS, HD, HD)
    wkv_s = sfin_s.reshape(1, bs, HEADS, HD, HD)
    return (y_prompt, y_sample, conv_p, hl_p[None], shift_rows(n_p, bp), wkv_p,
            conv_s, hl_s[None], shift_rows(n_p + n_s, bs), wkv_s)
```

```python
import functools

import jax
import jax.numpy as jnp
from jax import lax
from jax.experimental import pallas as pl
from jax.experimental.pallas import tpu as pltpu

f32 = jnp.float32
bf16 = jnp.bfloat16

D = 1024
HEADS = 16
HD = 64
LANES = 128
SUBLANES = 8
LORA_W = 64
LORA_A = 64
LORA_G = 160
LORA = LORA_W + LORA_A + LORA_G
LORA_PAD = 512
N_GROUPS = 4
EPG = 8
N_EXPERTS = N_GROUPS * EPG
D_EXPERT = 256
LRU_C = 8.0
GN_EPS = 64e-5
NORM_EPS = 1e-6
PROJ_COLS = 7 * D + LORA_PAD
COL_LRU_X, COL_LRU_Y, COL_R, COL_K, COL_V, COL_GL, COL_GR = range(7)
COL_LORA = 7 * D // LORA_PAD
ROUTER_ROWS = 48
MIB = 1024 * 1024
TOKEN_TILE = 1024


def _aligned(start, multiple):
    return start if isinstance(start, int) else pl.multiple_of(start, multiple)


def _full(shape, grid_rank=1):
    zeros = tuple(0 for _ in shape)
    if grid_rank == 1:
        return pl.BlockSpec(shape, lambda i: zeros)
    return pl.BlockSpec(shape, lambda i, j: zeros)


def _inproj_kernel(xp_ref, xs_ref, g_ref, w_ref, o_ref, xo_ref, xf_ref, xn_ref, *, prompt_tiles):
    i = pl.program_id(0)
    first_col = pl.program_id(1) == 0
    bp, tsteps = xp_ref.shape[0], xp_ref.shape[1]
    bs, ssteps = xs_ref.shape[0], xs_ref.shape[1]

    @pl.when(first_col & (i < prompt_tiles))
    def _():
        for b in range(bp):
            v = xp_ref[b]
            for cb in range(D // LANES):
                xf_ref[cb, pl.ds(b, tsteps, stride=bp), :] = v[:, cb * LANES:(cb + 1) * LANES]

    @pl.when(first_col & (i >= prompt_tiles))
    def _():
        for t in range(ssteps):
            v = xs_ref[:, t, :]
            for cb in range(D // LANES):
                xf_ref[cb, t * bs:(t + 1) * bs, :] = v[:, cb * LANES:(cb + 1) * LANES]

    @pl.when(first_col)
    def _():
        x = jnp.concatenate([xf_ref[cb] for cb in range(D // LANES)], axis=1)
        xo_ref[...] = x
        ms = jnp.mean(x * x, axis=-1, keepdims=True)
        xn_ref[...] = (x * lax.rsqrt(ms + NORM_EPS) * g_ref[...]).astype(bf16)

    o_ref[...] = jnp.dot(xn_ref[...], w_ref[...], preferred_element_type=f32)


def _inproj(x_prompt, x_sample, gain, w, tm, tn=1536):
    bp, tp, _ = x_prompt.shape
    bs, ts, _ = x_sample.shape
    assert tm % bp == 0 and tp % (tm // bp) == 0 and bs * ts == tm
    prompt_tiles = bp * tp // tm
    n = bp * tp + bs * ts
    kern = functools.partial(_inproj_kernel, prompt_tiles=prompt_tiles)
    return pl.pallas_call(
        kern,
        grid=(n // tm, PROJ_COLS // tn),
        in_specs=[
            pl.BlockSpec((bp, tm // bp, D), lambda i, j: (0, jnp.minimum(i, prompt_tiles - 1), 0)),
            pl.BlockSpec((bs, ts, D), lambda i, j: (0, 0, 0)),
            pl.BlockSpec((1, D), lambda i, j: (0, 0)),
            pl.BlockSpec((D, tn), lambda i, j: (0, j)),
        ],
        out_specs=[pl.BlockSpec((tm, tn), lambda i, j: (i, j)), pl.BlockSpec((tm, D), lambda i, j: (i, 0))],
        out_shape=[jax.ShapeDtypeStruct((n, PROJ_COLS), f32), jax.ShapeDtypeStruct((n, D), f32)],
        scratch_shapes=[pltpu.VMEM((D // LANES, tm, LANES), f32), pltpu.VMEM((tm, D), bf16)],
        compiler_params=pltpu.CompilerParams(dimension_semantics=("arbitrary", "arbitrary"),
                                             vmem_limit_bytes=56 * MIB),
        name="inproj",
    )(x_prompt, x_sample, gain, w)


def _lru_kernel(x_ref, y_ref, gl_ref, cinit_ref, h0_ref, cw_ref, cb_ref, wxa_ref, bx_ref, ba_ref, ap_ref,
                o_ref, hl_ref, xs_ref, a_ref, b_ref, h_ref, *, batch, reset_first, chunk):
    rows = x_ref.shape[0]
    nt = rows // batch
    hist = 3 * batch
    pid = pl.program_id(0)

    @pl.when(pid == 0)
    def _():
        xs_ref[0:hist, :] = cinit_ref[...]
        h_ref[...] = h0_ref[...]

    xs_ref[hist:hist + rows, :] = x_ref[...]
    logsig = -jax.nn.softplus(-ap_ref[...])

    def gates(c, _):
        r0 = pl.multiple_of(c * chunk, chunk)
        xc = (cb_ref[...]
              + cw_ref[3:4, :] * xs_ref[pl.ds(pl.multiple_of(r0 + hist, SUBLANES), chunk), :]
              + cw_ref[2:3, :] * xs_ref[pl.ds(pl.multiple_of(r0 + 2 * batch, SUBLANES), chunk), :]
              + cw_ref[1:2, :] * xs_ref[pl.ds(pl.multiple_of(r0 + batch, SUBLANES), chunk), :]
              + cw_ref[0:1, :] * xs_ref[pl.ds(r0, chunk), :])
        if reset_first:
            grow = lax.broadcasted_iota(jnp.int32, (chunk, LANES), 0) + (r0 + pid * rows)
            first = grow < batch
        for n in range(D // LANES):
            sl = slice(n * LANES, (n + 1) * LANES)
            xn = xc[:, sl]
            g2 = jnp.dot(xn.astype(bf16), wxa_ref[n], preferred_element_type=f32)
            gate_x = jax.nn.sigmoid(g2[:, :LANES] + bx_ref[:, sl])
            gate_a = jax.nn.sigmoid(g2[:, LANES:] + ba_ref[:, sl])
            log_a = LRU_C * gate_a * logsig[:, sl]
            a = jnp.exp(log_a)
            mult = jnp.sqrt(-jnp.tanh(log_a) * (a * a + 1.0))
            if reset_first:
                mult = jnp.where(first, 1.0, mult)
            a_ref[pl.ds(r0, chunk), sl] = a
            b_ref[pl.ds(r0, chunk), sl] = xn * gate_x * mult
        return 0

    lax.fori_loop(0, rows // chunk, gates, 0)

    def scan(t, h):
        r0 = pl.multiple_of(t * batch, batch)
        h = a_ref[pl.ds(r0, batch), :] * h + b_ref[pl.ds(r0, batch), :]
        b_ref[pl.ds(r0, batch), :] = h
        return h

    h = lax.fori_loop(0, nt, scan, h_ref[...], unroll=(8 if nt >= 8 and batch == SUBLANES else 1))
    h_ref[...] = h
    hl_ref[...] = h
    xs_ref[0:hist, :] = xs_ref[rows:rows + hist, :]

    def outp(c, _):
        r0 = pl.multiple_of(c * chunk, chunk)
        o_ref[pl.ds(r0, chunk), :] = (b_ref[pl.ds(r0, chunk), :] * jax.nn.gelu(y_ref[pl.ds(r0, chunk), :])
                                      * jax.nn.sigmoid(gl_ref[pl.ds(r0, chunk), :]))
        return 0

    lax.fori_loop(0, rows // chunk, outp, 0)


def _with_dst(kern, n_in):
    def wrapped(*refs):
        return kern(*refs[:n_in], *refs[n_in + 1:])
    return wrapped


def _shared_rows(dst, n_total):
    if dst is None:
        return [], [], jax.ShapeDtypeStruct((n_total, D), f32)
    return [pl.BlockSpec(memory_space=pl.ANY)], [dst], jax.ShapeDtypeStruct(dst.shape, dst.dtype)


def _lru(proj, row_off, n_rows, batch, rows_blk, reset_first, conv_init, h0, cw, cb, wxa, bx, ba, ap, dst=None):
    ob = row_off // rows_blk
    kern = functools.partial(_lru_kernel, batch=batch, reset_first=reset_first, chunk=128)
    n_in = 11
    dst_spec, dst_arg, out0 = _shared_rows(dst, proj.shape[0])
    return pl.pallas_call(
        kern if dst is None else _with_dst(kern, n_in),
        grid=(n_rows // rows_blk,),
        in_specs=[
            pl.BlockSpec((rows_blk, D), lambda i: (ob + i, COL_LRU_X)),
            pl.BlockSpec((rows_blk, D), lambda i: (ob + i, COL_LRU_Y)),
            pl.BlockSpec((rows_blk, D), lambda i: (ob + i, COL_GL)),
            _full((3 * batch, D)), _full((batch, D)), _full((4, D)), _full((1, D)),
            _full((D // LANES, LANES, 2 * LANES)), _full((1, D)), _full((1, D)), _full((1, D)),
        ] + dst_spec,
        out_specs=[pl.BlockSpec((rows_blk, D), lambda i: (ob + i, 0)), _full((batch, D))],
        out_shape=[out0, jax.ShapeDtypeStruct((batch, D), f32)],
        input_output_aliases={} if dst is None else {n_in: 0},
        scratch_shapes=[pltpu.VMEM((rows_blk + 3 * batch, D), f32), pltpu.VMEM((rows_blk, D), f32),
                        pltpu.VMEM((rows_blk, D), f32), pltpu.VMEM((batch, D), f32)],
        compiler_params=pltpu.CompilerParams(dimension_semantics=("arbitrary",), vmem_limit_bytes=48 * MIB),
        name="lru",
    )(proj, proj, proj, conv_init, h0, cw, cb, wxa, bx, ba, ap, *dst_arg)


TILE_ALPHA, TILE_BETA, TILE_W, TILE_KP, TILE_WR, TILE_V = range(6)
ROW_BR, ROW_KR, ROW_BONUS = range(3)
CONST_KK, CONST_KA, CONST_RK, CONST_LNW, CONST_LNB = range(5)
NAT_R, NAT_W, NAT_K, NAT_V, NAT_A = range(5)


def _wkv_prep(tiles_ref, rows_ref, gam_ref, t, g, r, w, k, v, a, const):
    kk = k * const(CONST_KK)
    nrm = jnp.sqrt(jnp.sum(kk * kk, axis=0, keepdims=True))
    kk = kk * (1.0 / jnp.maximum(nrm, 1e-12))
    beta = kk * a
    kp = k * (1.0 + (a - 1.0) * const(CONST_KA))
    gam_prev = gam_ref[g]
    gam = gam_prev * w
    gam_ref[g] = gam
    inv = 1.0 / gam
    tiles_ref[t, g, TILE_ALPHA] = -kk * gam_prev
    tiles_ref[t, g, TILE_BETA] = beta * inv
    tiles_ref[t, g, TILE_W] = gam
    tiles_ref[t, g, TILE_KP] = kp * inv
    tiles_ref[t, g, TILE_WR] = gam * r
    tiles_ref[t, g, TILE_V] = v
    rows_ref[t, g, ROW_BR:ROW_BR + 1, :] = jnp.sum(beta * r, axis=0, keepdims=True)
    rows_ref[t, g, ROW_KR:ROW_KR + 1, :] = jnp.sum(kp * r, axis=0, keepdims=True)
    rows_ref[t, g, ROW_BONUS:ROW_BONUS + 1, :] = jnp.sum(r * kp * const(CONST_RK), axis=0, keepdims=True)


def _rwkv_pre_kernel(r_ref, k_ref, v_ref, l_ref, sinit_ref, mu_ref, w0_ref, a0_ref, w2_ref, a2_ref, g2_ref, c_ref,
                     tiles_ref, rows_ref, go_ref, ps_ref, nat_ref, gam_ref, *, batch, chunk, chains_are_heads):
    rows = r_ref.shape[0]
    gam_ref[...] = jnp.ones_like(gam_ref)

    @pl.when(pl.program_id(0) == 0)
    def _():
        ps_ref[0:batch, :] = sinit_ref[...]

    ps_ref[batch:batch + rows, 0:D] = r_ref[...]
    ps_ref[batch:batch + rows, D:2 * D] = k_ref[...]
    ps_ref[batch:batch + rows, 2 * D:3 * D] = v_ref[...]
    ps_ref[batch:batch + rows, 3 * D:3 * D + LORA_PAD] = l_ref[...]

    def body(c, _):
        r0 = _aligned(c * chunk, chunk)

        def mixed(lo, hi):
            cur = ps_ref[pl.ds(_aligned(r0 + batch, SUBLANES), chunk), lo:hi]
            prev = ps_ref[pl.ds(r0, chunk), lo:hi]
            return cur + (prev - cur) * mu_ref[:, lo:hi]

        nat_ref[NAT_R, pl.ds(r0, chunk), :] = mixed(0, D)
        nat_ref[NAT_K, pl.ds(r0, chunk), :] = mixed(D, 2 * D)
        nat_ref[NAT_V, pl.ds(r0, chunk), :] = mixed(2 * D, 3 * D)
        lm = mixed(3 * D, 3 * D + LORA_PAD)
        xwa = lm[:, 0:LANES]
        xg = lm[:, LANES:3 * LANES]
        lw = jnp.dot(jnp.tanh(xwa).astype(bf16), w2_ref[...], preferred_element_type=f32)
        w_log = -jax.nn.softplus(-(w0_ref[...] + lw)) - 0.5
        nat_ref[NAT_W, pl.ds(r0, chunk), :] = jnp.exp(-jnp.exp(w_log))
        la = jnp.dot(xwa.astype(bf16), a2_ref[...], preferred_element_type=f32)
        nat_ref[NAT_A, pl.ds(r0, chunk), :] = jax.nn.sigmoid(a0_ref[...] + la)
        go_ref[pl.ds(r0, chunk), :] = jnp.dot(jax.nn.sigmoid(xg).astype(bf16), g2_ref[...],
                                              preferred_element_type=f32)
        return 0

    if chains_are_heads:
        lax.fori_loop(0, rows // chunk, body, 0)
        def tstep(t, _):
            r0 = pl.multiple_of(t * batch, batch)
            for hp in range(D // LANES):
                trs = [nat_ref[q, pl.ds(r0, batch), hp * LANES:(hp + 1) * LANES].T for q in range(5)]
                for h2 in range(2):
                    h = 2 * hp + h2
                    _wkv_prep(tiles_ref, rows_ref, gam_ref, t, h, *[tr[h2 * HD:(h2 + 1) * HD] for tr in trs],
                              lambda i, h=h: c_ref[h, i])
            return 0

        lax.fori_loop(0, rows // batch, tstep, 0)
    else:
        half = LANES // 2
        lane = lax.broadcasted_iota(jnp.int32, (HD, LANES), 1)

        def load_pair(q, tp):
            zz = jnp.concatenate(
                [nat_ref[q, pl.ds(_aligned(tp * 2 * SUBLANES + tt * SUBLANES, SUBLANES), SUBLANES),
                         hp * LANES:(hp + 1) * LANES] for tt in range(2) for hp in range(D // LANES)], axis=0)
            tr = zz.T
            top, bot = tr[0:HD], tr[HD:2 * HD]
            d0 = jnp.where(lane < half, top, pltpu.roll(bot, half, axis=1))
            d1 = jnp.where(lane < half, pltpu.roll(top, half, axis=1), bot)
            return d0, d1

        def pair(tp, _):
            pairs = [load_pair(q, tp) for q in range(5)]
            for tt in range(2):
                _wkv_prep(tiles_ref, rows_ref, gam_ref, 2 * tp + tt, 0, *[p[tt] for p in pairs], lambda i: c_ref[0, i])
            return 0

        n_chunks = rows // chunk
        pairs_per_chunk = chunk // (2 * SUBLANES)
        body(0, 0)
        for c in range(1, n_chunks):
            body(c, 0)
            for tp in range((c - 1) * pairs_per_chunk, c * pairs_per_chunk):
                pair(tp, 0)
        for tp in range((n_chunks - 1) * pairs_per_chunk, n_chunks * pairs_per_chunk):
            pair(tp, 0)

    ps_ref[0:batch, :] = ps_ref[rows:rows + batch, :]


def _rwkv_pre(proj, row_off, n_rows, batch, rows_blk, groups, shift_init, consts, mu, w0, a0, w2p, a2p, g2p,
              dst=None):
    ob = row_off // rows_blk
    width = 3 * D + LORA_PAD
    steps_blk = rows_blk // batch
    kern = functools.partial(_rwkv_pre_kernel, batch=batch, chunk=128, chains_are_heads=groups > 1)
    n_in = 12
    dst_spec, dst_arg, out_g = _shared_rows(dst, proj.shape[0])
    return pl.pallas_call(
        kern if dst is None else _with_dst(kern, n_in),
        grid=(n_rows // rows_blk,),
        in_specs=[
            pl.BlockSpec((rows_blk, D), lambda i: (ob + i, COL_R)),
            pl.BlockSpec((rows_blk, D), lambda i: (ob + i, COL_K)),
            pl.BlockSpec((rows_blk, D), lambda i: (ob + i, COL_V)),
            pl.BlockSpec((rows_blk, LORA_PAD), lambda i: (ob + i, COL_LORA)),
            _full((batch, width)), _full((1, width)), _full((1, D)), _full((1, D)),
            _full((LANES, D)), _full((LANES, D)), _full((2 * LANES, D)), _full((groups, 5, HD, LANES)),
        ] + dst_spec,
        out_specs=[pl.BlockSpec((steps_blk, groups, 6, HD, LANES), lambda i: (i, 0, 0, 0, 0)),
                   pl.BlockSpec((steps_blk, groups, SUBLANES, LANES), lambda i: (i, 0, 0, 0)),
                   pl.BlockSpec((rows_blk, D), lambda i: (ob + i, 0))],
        out_shape=[jax.ShapeDtypeStruct((n_rows // batch, groups, 6, HD, LANES), f32),
                   jax.ShapeDtypeStruct((n_rows // batch, groups, SUBLANES, LANES), f32),
                   out_g],
        input_output_aliases={} if dst is None else {n_in: 2},
        scratch_shapes=[pltpu.VMEM((rows_blk + batch, width), f32), pltpu.VMEM((5, rows_blk, D), f32),
                        pltpu.VMEM((groups, HD, LANES), f32)],
        compiler_params=pltpu.CompilerParams(dimension_semantics=("arbitrary",), vmem_limit_bytes=56 * MIB),
        name="rwkv_pre",
    )(proj, proj, proj, proj, shift_init, mu, w0, a0, w2p, a2p, g2p, consts, *dst_arg)


STATE_ROWS = HD * HD
KEY_UNROLL = 32


def _wkv_step(s_ref, tiles_ref, rows_ref, y_ref, t, window):
    def key_row(tile, j):
        return tiles_ref[t, 0, tile, pl.ds(j, 1), :]

    def reduce_keys(j, acc):
        sa, y0 = acc
        s = s_ref[pl.ds(pl.multiple_of(j * HD, HD), HD), :]
        return sa + s * key_row(TILE_ALPHA, j), y0 + s * key_row(TILE_WR, j)

    zero = jnp.zeros((HD, LANES), f32)
    sa, y0 = lax.fori_loop(0, HD, reduce_keys, (zero, zero), unroll=KEY_UNROLL)
    v = tiles_ref[t, 0, TILE_V]

    def update_keys(j, _):
        rows = pl.ds(pl.multiple_of(j * HD, HD), HD)
        s_ref[rows, :] = s_ref[rows, :] + sa * key_row(TILE_BETA, j) + v * key_row(TILE_KP, j)
        return 0

    lax.fori_loop(0, HD, update_keys, 0, unroll=KEY_UNROLL)
    y_ref[t, 0] = y0 + sa * rows_ref[t, 0, ROW_BR:ROW_BR + 1, :] + v * rows_ref[t, 0, ROW_KR:ROW_KR + 1, :]

    @pl.when((t + 1) % window == 0)
    def _():
        def rescale(j, _):
            rows = pl.ds(pl.multiple_of(j * HD, HD), HD)
            s_ref[rows, :] = s_ref[rows, :] * key_row(TILE_W, j)
            return 0

        lax.fori_loop(0, HD, rescale, 0, unroll=KEY_UNROLL)


def _wkv_scan_kernel(tiles_ref, rows_ref, st_ref, y_ref, sfin_ref, s_ref, *, steps, window):
    tb = pl.program_id(1)

    def value_rows(i):
        return pl.ds(i, HD, stride=HD)

    @pl.when(tb == 0)
    def _():
        def init(c, _):
            tr = st_ref[:, pl.ds(pl.multiple_of(c * LANES, LANES), LANES)].T
            for i2 in range(2):
                s_ref[value_rows(2 * c + i2), :] = tr[i2 * HD:(i2 + 1) * HD]
            return 0

        lax.fori_loop(0, STATE_ROWS // LANES, init, 0, unroll=8)

    def step(t, _):
        _wkv_step(s_ref, tiles_ref, rows_ref, y_ref, t, window)
        return 0

    lax.fori_loop(0, steps, step, 0)

    @pl.when(tb == pl.num_programs(1) - 1)
    def _():
        def fin(c, _):
            pair = jnp.concatenate([s_ref[value_rows(2 * c + i2), :] for i2 in range(2)], axis=0)
            sfin_ref[:, pl.ds(pl.multiple_of(c * LANES, LANES), LANES)] = pair.T
            return 0

        lax.fori_loop(0, STATE_ROWS // LANES, fin, 0, unroll=8)


def _wkv_scan(tiles, rows, state, steps_blk, window):
    n_steps, groups = tiles.shape[0], tiles.shape[1]
    assert steps_blk % window == 0
    kern = functools.partial(_wkv_scan_kernel, steps=steps_blk, window=window)
    sblk = pl.BlockSpec((LANES, STATE_ROWS), lambda g, i: (0, g))
    return pl.pallas_call(
        kern,
        grid=(groups, n_steps // steps_blk),
        in_specs=[pl.BlockSpec((steps_blk, 1, 6, HD, LANES), lambda g, i: (i, g, 0, 0, 0)),
                  pl.BlockSpec((steps_blk, 1, SUBLANES, LANES), lambda g, i: (i, g, 0, 0)),
                  sblk],
        out_specs=[pl.BlockSpec((steps_blk, 1, HD, LANES), lambda g, i: (i, g, 0, 0)), sblk],
        out_shape=[jax.ShapeDtypeStruct((n_steps, groups, HD, LANES), f32),
                   jax.ShapeDtypeStruct((LANES, groups * STATE_ROWS), f32)],
        scratch_shapes=[pltpu.VMEM((STATE_ROWS, LANES), f32)],
        compiler_params=pltpu.CompilerParams(dimension_semantics=("arbitrary", "arbitrary"),
                                             vmem_limit_bytes=48 * MIB),
        name="wkv_scan",
    )(tiles, rows, state)


def _wkv_norm(y, v, bonus, lnw, lnb):
    mean = jnp.sum(y, axis=0, keepdims=True) * (1.0 / HD)
    d = y - mean
    var = jnp.sum(d * d, axis=0, keepdims=True) * (1.0 / HD)
    return d * lax.rsqrt(var + GN_EPS) * lnw + lnb + bonus * v


def _wkv_post_kernel(y_ref, v_ref, rows_ref, c_ref, o_ref, *, batch, chains_are_heads):
    steps = y_ref.shape[0]

    def normed(t, g):
        return _wkv_norm(y_ref[t, g], v_ref[t, g, 0], rows_ref[t, g, ROW_BONUS:ROW_BONUS + 1, :],
                         c_ref[g, CONST_LNW], c_ref[g, CONST_LNB])

    if chains_are_heads:
        def tstep(t, _):
            r0 = pl.multiple_of(t * batch, batch)
            for hp in range(D // LANES):
                two = jnp.concatenate([normed(t, 2 * hp), normed(t, 2 * hp + 1)], axis=0)
                o_ref[pl.ds(r0, batch), hp * LANES:(hp + 1) * LANES] = two.T
            return 0

        lax.fori_loop(0, steps, tstep, 0)
    else:
        half = LANES // 2
        lane = lax.broadcasted_iota(jnp.int32, (HD, LANES), 1)

        def pair(tp, _):
            o0 = normed(2 * tp, 0)
            o1 = normed(2 * tp + 1, 0)
            top = jnp.where(lane < half, o0, pltpu.roll(o1, half, axis=1))
            bot = jnp.where(lane < half, pltpu.roll(o0, half, axis=1), o1)
            zz = jnp.concatenate([top, bot], axis=0).T
            for tt in range(2):
                row = pl.multiple_of(tp * 2 * SUBLANES + tt * SUBLANES, SUBLANES)
                for hp in range(D // LANES):
                    src = tt * HD + hp * SUBLANES
                    o_ref[pl.ds(row, SUBLANES), hp * LANES:(hp + 1) * LANES] = zz[src:src + SUBLANES, :]
            return 0

        lax.fori_loop(0, steps // 2, pair, 0, unroll=4)


def _outproj_kernel(ml_ref, g_ref, gr_ref, x_ref, yp_ref, vp_ref, rp_ref, cp_ref, ysm_ref, vs_ref, rs_ref, cs_ref,
                    wo_ref, n2_ref, rh_ref, rl_ref, rb_ref,
                    x1_ref, t_ref, selt_ref, seln_ref, op_ref, *, prompt_tiles, prompt_batch, sample_batch):
    tm = x_ref.shape[0]
    i = pl.program_id(0)

    @pl.when(i < prompt_tiles)
    def _():
        _wkv_post_kernel(yp_ref, vp_ref, rp_ref, cp_ref, op_ref, batch=prompt_batch, chains_are_heads=False)

    @pl.when(i >= prompt_tiles)
    def _():
        _wkv_post_kernel(ysm_ref, vs_ref, rs_ref, cs_ref, op_ref, batch=sample_batch, chains_are_heads=True)

    merged = ml_ref[...] + jax.nn.sigmoid(gr_ref[...]) * (op_ref[...] * g_ref[...])
    x1 = x_ref[...] + jnp.dot(merged.astype(bf16), wo_ref[...], preferred_element_type=f32)
    x1_ref[...] = x1
    t = x1 * lax.rsqrt(jnp.mean(x1 * x1, axis=-1, keepdims=True) + NORM_EPS) * n2_ref[...]
    th = t.astype(bf16)
    t_ref[...] = th
    tl = (t - th.astype(f32)).astype(bf16)
    nt_dims = (((1,), (1,)), ((), ()))
    lg = (lax.dot_general(rh_ref[...], th, nt_dims, preferred_element_type=f32)
          + lax.dot_general(rh_ref[...], tl, nt_dims, preferred_element_type=f32)
          + lax.dot_general(rl_ref[...], th, nt_dims, preferred_element_type=f32)) + rb_ref[...]
    row = lax.broadcasted_iota(jnp.int32, (EPG, tm), 0).astype(f32)
    neg = jnp.float32(-jnp.inf)
    glog = jnp.where(row < N_GROUPS, lg[0:EPG], neg)
    ge = jnp.exp(glog - jnp.max(glog, axis=0, keepdims=True))
    pg = ge / jnp.sum(ge, axis=0, keepdims=True)
    p_top = jnp.max(pg, axis=0, keepdims=True)
    g_idx = jnp.min(jnp.where(pg == p_top, row, EPG), axis=0, keepdims=True)
    le = jnp.zeros((EPG, tm), f32)
    for g in range(N_GROUPS):
        le = jnp.where(g_idx == g, lg[EPG * (g + 1):EPG * (g + 2)], le)
    qe = jnp.exp(le - jnp.max(le, axis=0, keepdims=True))
    q = qe / jnp.sum(qe, axis=0, keepdims=True)
    q1 = jnp.max(q, axis=0, keepdims=True)
    i1 = jnp.min(jnp.where(q == q1, row, EPG), axis=0, keepdims=True)
    qm = jnp.where(row == i1, -1.0, q)
    q2 = jnp.max(qm, axis=0, keepdims=True)
    i2 = jnp.min(jnp.where(qm == q2, row, EPG), axis=0, keepdims=True)
    qs = q1 + q2
    sel = jnp.concatenate([g_idx * EPG + i1, g_idx * EPG + i2, q1 / qs * p_top, q2 / qs * p_top,
                           jnp.zeros((SUBLANES - 4, tm), f32)], axis=0)
    selt_ref[...] = sel
    sel_pad = jnp.concatenate([sel, jnp.zeros((LANES - SUBLANES, tm), f32)], axis=0)
    for c in range(tm // LANES):
        seln_ref[c * LANES:(c + 1) * LANES, :] = sel_pad[:, c * LANES:(c + 1) * LANES].T


def _outproj(ml, g, proj, x, scan_p, scan_s, wo, n2, rh, rl, rb, prompt_batch, sample_batch, tm=512):
    n = x.shape[0]
    y_p, tiles_p, rows_p, consts_p = scan_p
    y_s, tiles_s, rows_s, consts_s = scan_s
    n_p = y_p.shape[0] * prompt_batch
    sp, ss = tm // prompt_batch, tm // sample_batch
    prompt_tiles = n_p // tm
    sample_tiles = (n - n_p) // tm
    assert n_p % tm == 0 and sp % 2 == 0 and y_s.shape[0] == sample_tiles * ss
    pidx = lambda i: jnp.minimum(i, prompt_tiles - 1)
    sidx = lambda i: jnp.clip(i - prompt_tiles, 0, sample_tiles - 1)
    groups = y_s.shape[1]
    blk = pl.BlockSpec((tm, D), lambda i: (i, 0))
    kern = functools.partial(_outproj_kernel, prompt_tiles=prompt_tiles, prompt_batch=prompt_batch,
                             sample_batch=sample_batch)
    return pl.pallas_call(
        kern,
        grid=(n // tm,),
        in_specs=[blk, blk, pl.BlockSpec((tm, D), lambda i: (i, COL_GR)), blk,
                  pl.BlockSpec((sp, 1, HD, LANES), lambda i: (pidx(i), 0, 0, 0)),
                  pl.BlockSpec((sp, 1, 1, HD, LANES), lambda i: (pidx(i), 0, TILE_V, 0, 0)),
                  pl.BlockSpec((sp, 1, SUBLANES, LANES), lambda i: (pidx(i), 0, 0, 0)),
                  _full((1, 5, HD, LANES)),
                  pl.BlockSpec((ss, groups, HD, LANES), lambda i: (sidx(i), 0, 0, 0)),
                  pl.BlockSpec((ss, groups, 1, HD, LANES), lambda i: (sidx(i), 0, TILE_V, 0, 0)),
                  pl.BlockSpec((ss, groups, SUBLANES, LANES), lambda i: (sidx(i), 0, 0, 0)),
                  _full((groups, 5, HD, LANES)),
                  _full((D, D)), _full((1, D)), _full((ROUTER_ROWS, D)), _full((ROUTER_ROWS, D)),
                  _full((ROUTER_ROWS, 1))],
        out_specs=[blk, blk, pl.BlockSpec((SUBLANES, tm), lambda i: (0, i)),
                   pl.BlockSpec((tm, LANES), lambda i: (i, 0))],
        out_shape=[jax.ShapeDtypeStruct((n, D), f32), jax.ShapeDtypeStruct((n, D), bf16),
                   jax.ShapeDtypeStruct((SUBLANES, n), f32), jax.ShapeDtypeStruct((n, LANES), f32)],
        scratch_shapes=[pltpu.VMEM((tm, D), f32)],
        compiler_params=pltpu.CompilerParams(dimension_semantics=("arbitrary",), vmem_limit_bytes=56 * MIB),
        name="outproj_router",
    )(ml, g, proj, x, y_p, tiles_p, rows_p, consts_p, y_s, tiles_s, rows_s, consts_s, wo, n2, rh, rl, rb)


MOE_CHUNK = 32
MOE_ROWS = 128
MOE_GATHER = 512
MOE_EXPERTS_PER_STEP = 4
MOE_OVERRUN = MOE_ROWS
MOE_CAP = -(-(2 * TOKEN_TILE + N_EXPERTS * (MOE_CHUNK - 1) + MOE_OVERRUN) // MOE_GATHER) * MOE_GATHER
SEL_E1, SEL_E2, SEL_C1, SEL_C2 = range(4)


def _route_meta_kernel(seln_ref, o_ref):
    tm = seln_ref.shape[0]
    lane = lax.broadcasted_iota(jnp.int32, (tm, LANES), 1).astype(f32)
    sel = seln_ref[...]
    hit = (lane == sel[:, SEL_E1:SEL_E1 + 1]) | (lane == sel[:, SEL_E2:SEL_E2 + 1])
    cnt = jnp.sum(jnp.where(hit, 1.0, 0.0), axis=0, keepdims=True)
    nchunk = jnp.floor((cnt + (MOE_CHUNK - 1)) * (1.0 / MOE_CHUNK))
    upper = (lax.broadcasted_iota(jnp.int32, (LANES, LANES), 0)
             < lax.broadcasted_iota(jnp.int32, (LANES, LANES), 1))
    offs = jnp.dot(jnp.broadcast_to(nchunk, (SUBLANES, LANES)).astype(bf16), jnp.where(upper, 1.0, 0.0).astype(bf16),
                   preferred_element_type=f32)
    row = lax.broadcasted_iota(jnp.int32, (SUBLANES, LANES), 0)
    o_ref[0] = jnp.where(row == 0, nchunk, jnp.where(row == 1, offs, 0.0)).astype(jnp.int32)


def _route_meta(seln, tm):
    tiles = seln.shape[0] // tm
    return pl.pallas_call(
        _route_meta_kernel,
        grid=(tiles,),
        in_specs=[pl.BlockSpec((tm, LANES), lambda i: (i, 0))],
        out_specs=pl.BlockSpec((1, SUBLANES, LANES), lambda i: (i, 0, 0)),
        out_shape=jax.ShapeDtypeStruct((tiles, SUBLANES, LANES), jnp.int32),
        compiler_params=pltpu.CompilerParams(dimension_semantics=("arbitrary",)),
        name="route_meta",
    )(seln)


def _moe_experts_kernel(nch_ref, off_ref, t_ref, selt_ref, tri_ref, wgu_ref, wd_ref, ys_ref, pos_ref,
                        xs_ref, cw_ref):
    i = pl.program_id(0)
    e = pl.program_id(1)
    tm = t_ref.shape[0]
    used_rows = (off_ref[i * N_EXPERTS + N_EXPERTS - 1] + nch_ref[i * N_EXPERTS + N_EXPERTS - 1]) * MOE_CHUNK

    @pl.when(e == 0)
    def _():
        ys_ref[...] = jnp.zeros_like(ys_ref)
        e1 = selt_ref[SEL_E1:SEL_E1 + 1, :]
        e2 = selt_ref[SEL_E2:SEL_E2 + 1, :]
        erow = lax.broadcasted_iota(jnp.int32, (N_EXPERTS, tm), 0).astype(f32)
        oh1 = jnp.where(erow == e1, 1.0, 0.0)
        oh2 = jnp.where(erow == e2, 1.0, 0.0)
        before1 = jnp.dot(oh1.astype(bf16), tri_ref[...], preferred_element_type=f32)
        before2 = jnp.dot(oh2.astype(bf16), tri_ref[...], preferred_element_type=f32)
        cnt1 = jnp.sum(oh1, axis=1, keepdims=True)
        cnt2 = jnp.sum(oh2, axis=1, keepdims=True)
        nchunk = jnp.floor((cnt1 + cnt2 + (MOE_CHUNK - 1)) * (1.0 / MOE_CHUNK))
        lower = (lax.broadcasted_iota(jnp.int32, (N_EXPERTS, LANES), 1)
                 < lax.broadcasted_iota(jnp.int32, (N_EXPERTS, LANES), 0))
        nchunk_rows = jnp.concatenate([jnp.broadcast_to(nchunk, (N_EXPERTS, LANES)),
                                       jnp.zeros((LANES - N_EXPERTS, LANES), f32)], axis=0)
        start = jnp.dot(jnp.where(lower, 1.0, 0.0).astype(bf16), nchunk_rows.astype(bf16),
                        preferred_element_type=f32)[:, 0:1] * MOE_CHUNK
        pos1 = jnp.sum(oh1 * (start + before1), axis=0, keepdims=True)
        pos2 = jnp.sum(oh2 * (start + cnt1 + before2), axis=0, keepdims=True)
        pos_ref[...] = jnp.concatenate([pos1, pos2, jnp.zeros((SUBLANES - 2, tm), f32)], axis=0)
        c1 = selt_ref[SEL_C1:SEL_C1 + 1, :]
        c2 = selt_ref[SEL_C2:SEL_C2 + 1, :]
        for k in range(MOE_CAP // MOE_GATHER):
            @pl.when(k * MOE_GATHER < used_rows + MOE_OVERRUN)
            def _(k=k):
                ridx = (lax.broadcasted_iota(jnp.int32, (MOE_GATHER, tm), 0) + k * MOE_GATHER).astype(f32)
                p1 = ridx == pos1
                p2 = ridx == pos2
                onehot = jnp.where(p1 | p2, 1.0, 0.0).astype(bf16)
                xs_ref[k * MOE_GATHER:(k + 1) * MOE_GATHER, :] = jnp.dot(
                    onehot, t_ref[...], preferred_element_type=f32).astype(bf16)
                w = jnp.sum(jnp.where(p1, c1, 0.0) + jnp.where(p2, c2, 0.0), axis=1, keepdims=True)
                cw_ref[k * MOE_GATHER:(k + 1) * MOE_GATHER, :] = jnp.broadcast_to(w, (MOE_GATHER, LANES))

    def expert_rows(u, rows):
        gu = jnp.dot(xs_ref[rows, :], wgu_ref[u], preferred_element_type=f32)
        w = cw_ref[rows, :]
        h = jax.nn.silu(gu[:, :D_EXPERT]) * gu[:, D_EXPERT:] * jnp.concatenate([w, w], axis=1)
        return jnp.dot(h.astype(bf16), wd_ref[u], preferred_element_type=f32).astype(bf16)

    per_step = wgu_ref.shape[0]
    per_block = MOE_ROWS // MOE_CHUNK
    experts = [i * N_EXPERTS + e * per_step + u for u in range(per_step)]
    bases = [off_ref[x] * MOE_CHUNK for x in experts]

    for u in range(per_step):
        rows = pl.ds(pl.multiple_of(bases[u], MOE_CHUNK), MOE_ROWS)
        ys_ref[0, rows, :] = expert_rows(u, rows)

    for u in range(per_step):
        seg_end = bases[u] + nch_ref[experts[u]] * MOE_CHUNK

        def more(c, _, u=u, seg_end=seg_end):
            start = bases[u] + (c + 1) * MOE_ROWS
            rows = pl.ds(pl.multiple_of(start, MOE_CHUNK), MOE_ROWS)
            ridx = lax.broadcasted_iota(jnp.int32, (MOE_ROWS, D), 0) + start
            ys_ref[0, rows, :] = jnp.where(ridx < seg_end, expert_rows(u, rows), ys_ref[0, rows, :])
            return 0

        lax.fori_loop(0, jnp.maximum(nch_ref[experts[u]] - 1, 0) // per_block, more, 0)


def _moe_experts(nch, off, t, selt, tri, wgu, wd, tm):
    n = t.shape[0]
    tiles = n // tm
    grid_spec = pltpu.PrefetchScalarGridSpec(
        num_scalar_prefetch=2,
        grid=(tiles, N_EXPERTS // MOE_EXPERTS_PER_STEP),
        in_specs=[
            pl.BlockSpec((tm, D), lambda i, e, nch, off: (i, 0)),
            pl.BlockSpec((SUBLANES, tm), lambda i, e, nch, off: (0, i)),
            pl.BlockSpec((tm, tm), lambda i, e, nch, off: (0, 0)),
            pl.BlockSpec((MOE_EXPERTS_PER_STEP, D, 2 * D_EXPERT), lambda i, e, nch, off: (e, 0, 0)),
            pl.BlockSpec((MOE_EXPERTS_PER_STEP, D_EXPERT, D), lambda i, e, nch, off: (e, 0, 0)),
        ],
        out_specs=[pl.BlockSpec((1, MOE_CAP, D), lambda i, e, nch, off: (i, 0, 0)),
                   pl.BlockSpec((SUBLANES, tm), lambda i, e, nch, off: (0, i))],
        scratch_shapes=[pltpu.VMEM((MOE_CAP, D), bf16), pltpu.VMEM((MOE_CAP, LANES), f32)],
    )
    return pl.pallas_call(
        _moe_experts_kernel,
        grid_spec=grid_spec,
        out_shape=[jax.ShapeDtypeStruct((tiles, MOE_CAP, D), bf16), jax.ShapeDtypeStruct((SUBLANES, n), f32)],
        compiler_params=pltpu.CompilerParams(dimension_semantics=("arbitrary", "arbitrary"),
                                             vmem_limit_bytes=56 * MIB),
        name="moe_experts",
    )(nch, off, t, selt, tri, wgu, wd)


def _moe_combine_kernel(nch_ref, off_ref, ysort_ref, pos_ref, x1_ref, fn_ref, yp_ref, ys_ref, acc_ref, yf_ref,
                        *, prompt_tiles, parts):
    i = pl.program_id(0)
    part = pl.program_id(1)
    rows = x1_ref.shape[0]
    bp, tsteps = yp_ref.shape[0], yp_ref.shape[1]
    bs = ys_ref.shape[0]
    ssteps = ys_ref.shape[1] // parts
    used_rows = (off_ref[i * N_EXPERTS + N_EXPERTS - 1] + nch_ref[i * N_EXPERTS + N_EXPERTS - 1]) * MOE_CHUNK

    pos_t = jnp.concatenate([pos_ref[...], jnp.zeros((LANES - SUBLANES, rows), f32)], axis=0)
    pos_n = jnp.concatenate([pos_t[:, c * LANES:(c + 1) * LANES].T for c in range(rows // LANES)], axis=0)
    pos1 = pos_n[:, 0:1]
    pos2 = pos_n[:, 1:2]
    acc_ref[...] = x1_ref[...]
    for k in range(MOE_CAP // MOE_GATHER):
        @pl.when(k * MOE_GATHER < used_rows)
        def _(k=k):
            cidx = (lax.broadcasted_iota(jnp.int32, (rows, MOE_GATHER), 1) + k * MOE_GATHER).astype(f32)
            onehot = jnp.where((cidx == pos1) | (cidx == pos2), 1.0, 0.0).astype(bf16)
            acc_ref[...] += jnp.dot(onehot, ysort_ref[0, k * MOE_GATHER:(k + 1) * MOE_GATHER, :],
                                    preferred_element_type=f32)

    xo = acc_ref[...]
    y = xo * lax.rsqrt(jnp.mean(xo * xo, axis=-1, keepdims=True) + NORM_EPS) * fn_ref[...]
    for cb in range(D // LANES):
        yf_ref[cb] = y[:, cb * LANES:(cb + 1) * LANES]

    @pl.when(i < prompt_tiles)
    def _():
        for b in range(bp):
            yp_ref[b] = jnp.concatenate([yf_ref[cb, pl.ds(b, tsteps, stride=bp), :] for cb in range(D // LANES)],
                                        axis=1)

    for p in range(parts):
        @pl.when((i >= prompt_tiles) & (part == p))
        def _(p=p):
            for t in range(ssteps):
                ys_ref[:, p * ssteps + t, :] = jnp.concatenate(
                    [yf_ref[cb, t * bs:(t + 1) * bs, :] for cb in range(D // LANES)], axis=1)


def _moe_combine(nch, off, ysort, pos, x1, fn, prompt_shape, sample_shape, tm, parts=2):
    n = x1.shape[0]
    bp, tp, _ = prompt_shape
    bs, ts, _ = sample_shape
    rows = tm // parts
    assert rows % bp == 0 and tp % (rows // bp) == 0 and bs * ts == tm and ts % parts == 0
    prompt_tiles = bp * tp // tm
    last_prompt_blk = prompt_tiles * parts - 1
    kern = functools.partial(_moe_combine_kernel, prompt_tiles=prompt_tiles, parts=parts)
    grid_spec = pltpu.PrefetchScalarGridSpec(
        num_scalar_prefetch=2,
        grid=(n // tm, parts),
        in_specs=[
            pl.BlockSpec((1, MOE_CAP, D), lambda i, p, nch, off: (i, 0, 0)),
            pl.BlockSpec((SUBLANES, rows), lambda i, p, nch, off: (0, i * parts + p)),
            pl.BlockSpec((rows, D), lambda i, p, nch, off: (i * parts + p, 0)),
            pl.BlockSpec((1, D), lambda i, p, nch, off: (0, 0)),
        ],
        out_specs=[pl.BlockSpec((bp, rows // bp, D),
                                lambda i, p, nch, off: (0, jnp.minimum(i * parts + p, last_prompt_blk), 0)),
                   pl.BlockSpec((bs, ts, D), lambda i, p, nch, off: (0, 0, 0))],
        scratch_shapes=[pltpu.VMEM((rows, D), f32), pltpu.VMEM((D // LANES, rows, LANES), f32)],
    )
    return pl.pallas_call(
        kern,
        grid_spec=grid_spec,
        out_shape=[jax.ShapeDtypeStruct(prompt_shape, f32), jax.ShapeDtypeStruct(sample_shape, f32)],
        compiler_params=pltpu.CompilerParams(dimension_semantics=("arbitrary", "arbitrary"),
                                             vmem_limit_bytes=56 * MIB),
        name="moe_combine",
    )(nch, off, ysort, pos, x1, fn)


def _chain_tiles_prompt(vec, batch):
    t = vec.reshape(HEADS // 2, 2, HD)
    t = jnp.transpose(t, (2, 1, 0))
    return jnp.broadcast_to(t[..., None], (HD, 2, HEADS // 2, batch)).reshape(HD, LANES)


def _chain_tiles_sample(vec):
    return jnp.broadcast_to(vec.reshape(HEADS, HD)[..., None], (HEADS, HD, LANES))


def kernel(x_prompt, x_sample, state_conv, state_lru, state_shift, state_wkv, norm1, w_in, conv_w, conv_b, lru_wx, lru_bx, lru_wa, lru_ba, lru_a_param, rwkv_mu, rwkv_w0, rwkv_w2, rwkv_a0, rwkv_a2, rwkv_g2, rwkv_k_k, rwkv_k_a, rwkv_r_k, rwkv_ln_w, rwkv_ln_b, w_out, norm2, router_group, router_group_b, router_expert, router_expert_b, exp_gate, exp_up, exp_down, final_norm):
    bp, tp, _ = x_prompt.shape
    bs, ts, _ = x_sample.shape
    assert norm1.shape[0] == 1 and bp * HEADS == LANES and bs == LANES and tp % 64 == 0 and ts % 2 == 0
    n_p, n_s = bp * tp, bs * ts
    c_rw = 2 * D
    c_lora = c_rw + 3 * D
    c_gl = c_lora + LORA

    w = w_in[0]
    w_all = jnp.concatenate([w[:, :c_lora], w[:, c_gl:], w[:, c_lora:c_gl],
                             jnp.zeros((D, LORA_PAD - LORA), f32)], axis=1).astype(bf16)
    mu = rwkv_mu[0]
    mu_all = jnp.concatenate([mu[:3 * D], mu[3 * D:], jnp.zeros((LORA_PAD - LORA,), f32)])[None]
    w2p = jnp.concatenate([rwkv_w2[0], jnp.zeros((LANES - LORA_W, D), f32)]).astype(bf16)
    a2p = jnp.concatenate([jnp.zeros((LORA_W, D), f32), rwkv_a2[0]]).astype(bf16)
    g2p = jnp.concatenate([rwkv_g2[0], jnp.zeros((2 * LANES - LORA_G, D), f32)]).astype(bf16)
    wxa = jnp.concatenate([lru_wx[0], lru_wa[0]], axis=-1).astype(bf16)
    row = lambda v: v.reshape(1, -1)
    rk = rwkv_r_k[0].reshape(D)
    cvecs = (rwkv_k_k[0], rwkv_k_a[0], rk, rwkv_ln_w[0], rwkv_ln_b[0])
    consts_p = jnp.stack([_chain_tiles_prompt(v, bp) for v in cvecs])[None]
    consts_s = jnp.stack([_chain_tiles_sample(v) for v in cvecs], axis=1)
    rw = jnp.zeros((ROUTER_ROWS, D), f32)
    rw = rw.at[0:N_GROUPS].set(router_group[0].T).at[EPG:EPG + N_EXPERTS].set(router_expert[0].T)
    rh = rw.astype(bf16)
    rl = (rw - rh.astype(f32)).astype(bf16)
    rb = jnp.zeros((ROUTER_ROWS, 1), f32)
    rb = rb.at[0:N_GROUPS, 0].set(router_group_b[0]).at[EPG:EPG + N_EXPERTS, 0].set(router_expert_b[0])
    wgu = jnp.concatenate([exp_gate[0], exp_up[0]], axis=-1).astype(bf16)
    wd = exp_down[0].astype(bf16)
    wo = w_out[0].astype(bf16)

    proj, x = _inproj(x_prompt, x_sample, row(norm1[0]), w_all, tm=TOKEN_TILE)

    lru_args = (conv_w[0], row(conv_b[0]), wxa, row(lru_bx[0]), row(lru_ba[0]), row(lru_a_param[0]))
    conv_init_s = jnp.transpose(state_conv[0], (1, 0, 2)).reshape(3 * bs, D)
    ml, hl_p = _lru(proj, 0, n_p, bp, 512, True, jnp.zeros((3 * bp, D), f32), jnp.zeros((bp, D), f32), *lru_args)
    ml, hl_s = _lru(proj, n_p, n_s, bs, 512, False, conv_init_s, state_lru[0], *lru_args, dst=ml)

    sh = state_shift[0]
    shift_init_s = jnp.concatenate([sh, jnp.zeros((bs, LORA_PAD - LORA), f32)], axis=1)
    pre_args = (mu_all, row(rwkv_w0[0]), row(rwkv_a0[0]), w2p, a2p, g2p)
    pre_rows = 256
    tiles_p, rows_p, g = _rwkv_pre(proj, 0, n_p, bp, pre_rows, 1, jnp.zeros((bp, 3 * D + LORA_PAD), f32),
                                   consts_p, *pre_args)
    tiles_s, rows_s, g = _rwkv_pre(proj, n_p, n_s, bs, pre_rows, HEADS, shift_init_s, consts_s, *pre_args, dst=g)

    y_p, sfin_p = _wkv_scan(tiles_p, rows_p, jnp.zeros((LANES, STATE_ROWS), f32), 64, pre_rows // bp)
    y_s, sfin_s = _wkv_scan(tiles_s, rows_s, state_wkv[0].reshape(bs, HEADS * STATE_ROWS), ts, pre_rows // bs)

    x1, t, selt, seln = _outproj(ml, g, proj, x, (y_p, tiles_p, rows_p, consts_p), (y_s, tiles_s, rows_s, consts_s),
                                 wo, row(norm2[0]), rh, rl, rb, bp, bs)
    meta = _route_meta(seln, TOKEN_TILE)
    nch = meta[:, 0, :N_EXPERTS].reshape(-1)
    off = meta[:, 1, :N_EXPERTS].reshape(-1)
    tri = jnp.triu(jnp.ones((TOKEN_TILE, TOKEN_TILE), bf16), k=1)
    ysort, pos = _moe_experts(nch, off, t, selt, tri, wgu, wd, TOKEN_TILE)
    y_prompt, y_sample = _moe_combine(nch, off, ysort, pos, x1, row(final_norm), x_prompt.shape, x_sample.shape,
                                      TOKEN_TILE)

    def last_rows(lo, hi, n_end, batch, steps):
        return proj[n_end - steps * batch:n_end, lo:hi].reshape(steps, batch, hi - lo)

    conv_p = jnp.transpose(last_rows(0, D, n_p, bp, 3), (1, 0, 2))[None]
    conv_s = jnp.transpose(last_rows(0, D, n_p + n_s, bs, 3), (1, 0, 2))[None]

    def shift_rows(n_end, batch):
        return jnp.concatenate([last_rows(2 * D, 5 * D, n_end, batch, 1)[0],
                                last_rows(7 * D, 7 * D + LORA, n_end, batch, 1)[0]], axis=1)[None]

    wkv_p = jnp.transpose(sfin_p.reshape(2, HEADS // 2, bp, HD, HD), (2, 1, 0, 3, 4)).reshape(1, bp, HEADS, HD, HD)
    wkv_s = sfin_s.reshape(1, bs, HEADS, HD, HD)
    return (y_prompt, y_sample, conv_p, hl_p[None], shift_rows(n_p, bp), wkv_p,
            conv_s, hl_s[None], shift_rows(n_p + n_s, bs), wkv_s)
```

```python
import functools

import jax
import jax.numpy as jnp
from jax import lax
from jax.experimental import pallas as pl
from jax.experimental.pallas import tpu as pltpu

f32 = jnp.float32
bf16 = jnp.bfloat16

D = 1024
HEADS = 16
HD = 64
LANES = 128
SUBLANES = 8
LORA_W = 64
LORA_A = 64
LORA_G = 160
LORA = LORA_W + LORA_A + LORA_G
LORA_PAD = 512
N_GROUPS = 4
EPG = 8
N_EXPERTS = N_GROUPS * EPG
D_EXPERT = 256
LRU_C = 8.0
GN_EPS = 64e-5
NORM_EPS = 1e-6
PROJ_COLS = 7 * D + LORA_PAD
COL_LRU_X, COL_LRU_Y, COL_R, COL_K, COL_V, COL_GL, COL_GR = range(7)
COL_LORA = 7 * D // LORA_PAD
ROUTER_ROWS = 48
MIB = 1024 * 1024
TOKEN_TILE = 1024


def _aligned(start, multiple):
    return start if isinstance(start, int) else pl.multiple_of(start, multiple)


def _full(shape, grid_rank=1):
    zeros = tuple(0 for _ in shape)
    if grid_rank == 1:
        return pl.BlockSpec(shape, lambda i: zeros)
    return pl.BlockSpec(shape, lambda i, j: zeros)


def _inproj_kernel(xp_ref, xs_ref, g_ref, w_ref, o_ref, xo_ref, xf_ref, xn_ref, *, prompt_tiles):
    i = pl.program_id(0)
    first_col = pl.program_id(1) == 0
    bp, tsteps = xp_ref.shape[0], xp_ref.shape[1]
    bs, ssteps = xs_ref.shape[0], xs_ref.shape[1]

    @pl.when(first_col & (i < prompt_tiles))
    def _():
        for b in range(bp):
            v = xp_ref[b]
            for cb in range(D // LANES):
                xf_ref[cb, pl.ds(b, tsteps, stride=bp), :] = v[:, cb * LANES:(cb + 1) * LANES]

    @pl.when(first_col & (i >= prompt_tiles))
    def _():
        for t in range(ssteps):
            v = xs_ref[:, t, :]
            for cb in range(D // LANES):
                xf_ref[cb, t * bs:(t + 1) * bs, :] = v[:, cb * LANES:(cb + 1) * LANES]

    @pl.when(first_col)
    def _():
        x = jnp.concatenate([xf_ref[cb] for cb in range(D // LANES)], axis=1)
        xo_ref[...] = x
        ms = jnp.mean(x * x, axis=-1, keepdims=True)
        xn_ref[...] = (x * lax.rsqrt(ms + NORM_EPS) * g_ref[...]).astype(bf16)

    o_ref[...] = jnp.dot(xn_ref[...], w_ref[...], preferred_element_type=f32)


def _inproj(x_prompt, x_sample, gain, w, tm, tn=1536):
    bp, tp, _ = x_prompt.shape
    bs, ts, _ = x_sample.shape
    assert tm % bp == 0 and tp % (tm // bp) == 0 and bs * ts == tm
    prompt_tiles = bp * tp // tm
    n = bp * tp + bs * ts
    kern = functools.partial(_inproj_kernel, prompt_tiles=prompt_tiles)
    return pl.pallas_call(
        kern,
        grid=(n // tm, PROJ_COLS // tn),
        in_specs=[
            pl.BlockSpec((bp, tm // bp, D), lambda i, j: (0, jnp.minimum(i, prompt_tiles - 1), 0)),
            pl.BlockSpec((bs, ts, D), lambda i, j: (0, 0, 0)),
            pl.BlockSpec((1, D), lambda i, j: (0, 0)),
            pl.BlockSpec((D, tn), lambda i, j: (0, j)),
        ],
        out_specs=[pl.BlockSpec((tm, tn), lambda i, j: (i, j)), pl.BlockSpec((tm, D), lambda i, j: (i, 0))],
        out_shape=[jax.ShapeDtypeStruct((n, PROJ_COLS), f32), jax.ShapeDtypeStruct((n, D), f32)],
        scratch_shapes=[pltpu.VMEM((D // LANES, tm, LANES), f32), pltpu.VMEM((tm, D), bf16)],
        compiler_params=pltpu.CompilerParams(dimension_semantics=("arbitrary", "arbitrary"),
                                             vmem_limit_bytes=56 * MIB),
        name="inproj",
    )(x_prompt, x_sample, gain, w)


def _lru_kernel(x_ref, y_ref, gl_ref, cinit_ref, h0_ref, cw_ref, cb_ref, wxa_ref, bx_ref, ba_ref, ap_ref,
                o_ref, hl_ref, xs_ref, a_ref, b_ref, h_ref, *, batch, reset_first, chunk):
    rows = x_ref.shape[0]
    nt = rows // batch
    hist = 3 * batch
    pid = pl.program_id(0)

    @pl.when(pid == 0)
    def _():
        xs_ref[0:hist, :] = cinit_ref[...]
        h_ref[...] = h0_ref[...]

    xs_ref[hist:hist + rows, :] = x_ref[...]
    logsig = -jax.nn.softplus(-ap_ref[...])

    def gates(c, _):
        r0 = pl.multiple_of(c * chunk, chunk)
        xc = (cb_ref[...]
              + cw_ref[3:4, :] * xs_ref[pl.ds(pl.multiple_of(r0 + hist, SUBLANES), chunk), :]
              + cw_ref[2:3, :] * xs_ref[pl.ds(pl.multiple_of(r0 + 2 * batch, SUBLANES), chunk), :]
              + cw_ref[1:2, :] * xs_ref[pl.ds(pl.multiple_of(r0 + batch, SUBLANES), chunk), :]
              + cw_ref[0:1, :] * xs_ref[pl.ds(r0, chunk), :])
        if reset_first:
            grow = lax.broadcasted_iota(jnp.int32, (chunk, LANES), 0) + (r0 + pid * rows)
            first = grow < batch
        for n in range(D // LANES):
            sl = slice(n * LANES, (n + 1) * LANES)
            xn = xc[:, sl]
            g2 = jnp.dot(xn.astype(bf16), wxa_ref[n], preferred_element_type=f32)
            gate_x = jax.nn.sigmoid(g2[:, :LANES] + bx_ref[:, sl])
            gate_a = jax.nn.sigmoid(g2[:, LANES:] + ba_ref[:, sl])
            log_a = LRU_C * gate_a * logsig[:, sl]
            a = jnp.exp(log_a)
            mult = jnp.sqrt(-jnp.tanh(log_a) * (a * a + 1.0))
            if reset_first:
                mult = jnp.where(first, 1.0, mult)
            a_ref[pl.ds(r0, chunk), sl] = a
            b_ref[pl.ds(r0, chunk), sl] = xn * gate_x * mult
        return 0

    lax.fori_loop(0, rows // chunk, gates, 0)

    def scan(t, h):
        r0 = pl.multiple_of(t * batch, batch)
        h = a_ref[pl.ds(r0, batch), :] * h + b_ref[pl.ds(r0, batch), :]
        b_ref[pl.ds(r0, batch), :] = h
        return h

    h = lax.fori_loop(0, nt, scan, h_ref[...], unroll=(8 if nt >= 8 and batch == SUBLANES else 1))
    h_ref[...] = h
    hl_ref[...] = h
    xs_ref[0:hist, :] = xs_ref[rows:rows + hist, :]

    def outp(c, _):
        r0 = pl.multiple_of(c * chunk, chunk)
        o_ref[pl.ds(r0, chunk), :] = (b_ref[pl.ds(r0, chunk), :] * jax.nn.gelu(y_ref[pl.ds(r0, chunk), :])
                                      * jax.nn.sigmoid(gl_ref[pl.ds(r0, chunk), :]))
        return 0

    lax.fori_loop(0, rows // chunk, outp, 0)


def _with_dst(kern, n_in):
    def wrapped(*refs):
        return kern(*refs[:n_in], *refs[n_in + 1:])
    return wrapped


def _shared_rows(dst, n_total):
    if dst is None:
        return [], [], jax.ShapeDtypeStruct((n_total, D), f32)
    return [pl.BlockSpec(memory_space=pl.ANY)], [dst], jax.ShapeDtypeStruct(dst.shape, dst.dtype)


def _lru(proj, row_off, n_rows, batch, rows_blk, reset_first, conv_init, h0, cw, cb, wxa, bx, ba, ap, dst=None):
    ob = row_off // rows_blk
    kern = functools.partial(_lru_kernel, batch=batch, reset_first=reset_first, chunk=128)
    n_in = 11
    dst_spec, dst_arg, out0 = _shared_rows(dst, proj.shape[0])
    return pl.pallas_call(
        kern if dst is None else _with_dst(kern, n_in),
        grid=(n_rows // rows_blk,),
        in_specs=[
            pl.BlockSpec((rows_blk, D), lambda i: (ob + i, COL_LRU_X)),
            pl.BlockSpec((rows_blk, D), lambda i: (ob + i, COL_LRU_Y)),
            pl.BlockSpec((rows_blk, D), lambda i: (ob + i, COL_GL)),
            _full((3 * batch, D)), _full((batch, D)), _full((4, D)), _full((1, D)),
            _full((D // LANES, LANES, 2 * LANES)), _full((1, D)), _full((1, D)), _full((1, D)),
        ] + dst_spec,
        out_specs=[pl.BlockSpec((rows_blk, D), lambda i: (ob + i, 0)), _full((batch, D))],
        out_shape=[out0, jax.ShapeDtypeStruct((batch, D), f32)],
        input_output_aliases={} if dst is None else {n_in: 0},
        scratch_shapes=[pltpu.VMEM((rows_blk + 3 * batch, D), f32), pltpu.VMEM((rows_blk, D), f32),
                        pltpu.VMEM((rows_blk, D), f32), pltpu.VMEM((batch, D), f32)],
        compiler_params=pltpu.CompilerParams(dimension_semantics=("arbitrary",), vmem_limit_bytes=48 * MIB),
        name="lru",
    )(proj, proj, proj, conv_init, h0, cw, cb, wxa, bx, ba, ap, *dst_arg)


TILE_ALPHA, TILE_BETA, TILE_W, TILE_KP, TILE_WR, TILE_V = range(6)
ROW_BR, ROW_KR, ROW_BONUS = range(3)
CONST_KK, CONST_KA, CONST_RK, CONST_LNW, CONST_LNB = range(5)
NAT_R, NAT_W, NAT_K, NAT_V, NAT_A = range(5)


def _wkv_prep(tiles_ref, rows_ref, gam_ref, t, g, r, w, k, v, a, const):
    kk = k * const(CONST_KK)
    nrm = jnp.sqrt(jnp.sum(kk * kk, axis=0, keepdims=True))
    kk = kk * (1.0 / jnp.maximum(nrm, 1e-12))
    beta = kk * a
    kp = k * (1.0 + (a - 1.0) * const(CONST_KA))
    gam_prev = gam_ref[g]
    gam = gam_prev * w
    gam_ref[g] = gam
    inv = 1.0 / gam
    tiles_ref[t, g, TILE_ALPHA] = -kk * gam_prev
    tiles_ref[t, g, TILE_BETA] = beta * inv
    tiles_ref[t, g, TILE_W] = gam
    tiles_ref[t, g, TILE_KP] = kp * inv
    tiles_ref[t, g, TILE_WR] = gam * r
    tiles_ref[t, g, TILE_V] = v
    rows_ref[t, g, ROW_BR:ROW_BR + 1, :] = jnp.sum(beta * r, axis=0, keepdims=True)
    rows_ref[t, g, ROW_KR:ROW_KR + 1, :] = jnp.sum(kp * r, axis=0, keepdims=True)
    rows_ref[t, g, ROW_BONUS:ROW_BONUS + 1, :] = jnp.sum(r * kp * const(CONST_RK), axis=0, keepdims=True)


def _rwkv_pre_kernel(r_ref, k_ref, v_ref, l_ref, sinit_ref, mu_ref, w0_ref, a0_ref, w2_ref, a2_ref, g2_ref, c_ref,
                     tiles_ref, rows_ref, go_ref, ps_ref, nat_ref, gam_ref, *, batch, chunk, chains_are_heads):
    rows = r_ref.shape[0]
    gam_ref[...] = jnp.ones_like(gam_ref)

    @pl.when(pl.program_id(0) == 0)
    def _():
        ps_ref[0:batch, :] = sinit_ref[...]

    ps_ref[batch:batch + rows, 0:D] = r_ref[...]
    ps_ref[batch:batch + rows, D:2 * D] = k_ref[...]
    ps_ref[batch:batch + rows, 2 * D:3 * D] = v_ref[...]
    ps_ref[batch:batch + rows, 3 * D:3 * D + LORA_PAD] = l_ref[...]

    def body(c, _):
        r0 = _aligned(c * chunk, chunk)

        def mixed(lo, hi):
            cur = ps_ref[pl.ds(_aligned(r0 + batch, SUBLANES), chunk), lo:hi]
            prev = ps_ref[pl.ds(r0, chunk), lo:hi]
            return cur + (prev - cur) * mu_ref[:, lo:hi]

        nat_ref[NAT_R, pl.ds(r0, chunk), :] = mixed(0, D)
        nat_ref[NAT_K, pl.ds(r0, chunk), :] = mixed(D, 2 * D)
        nat_ref[NAT_V, pl.ds(r0, chunk), :] = mixed(2 * D, 3 * D)
        lm = mixed(3 * D, 3 * D + LORA_PAD)
        xwa = lm[:, 0:LANES]
        xg = lm[:, LANES:3 * LANES]
        lw = jnp.dot(jnp.tanh(xwa).astype(bf16), w2_ref[...], preferred_element_type=f32)
        w_log = -jax.nn.softplus(-(w0_ref[...] + lw)) - 0.5
        nat_ref[NAT_W, pl.ds(r0, chunk), :] = jnp.exp(-jnp.exp(w_log))
        la = jnp.dot(xwa.astype(bf16), a2_ref[...], preferred_element_type=f32)
        nat_ref[NAT_A, pl.ds(r0, chunk), :] = jax.nn.sigmoid(a0_ref[...] + la)
        go_ref[pl.ds(r0, chunk), :] = jnp.dot(jax.nn.sigmoid(xg).astype(bf16), g2_ref[...],
                                              preferred_element_type=f32)
        return 0

    if chains_are_heads:
        lax.fori_loop(0, rows // chunk, body, 0)
        def tstep(t, _):
            r0 = pl.multiple_of(t * batch, batch)
            for hp in range(D // LANES):
                trs = [nat_ref[q, pl.ds(r0, batch), hp * LANES:(hp + 1) * LANES].T for q in range(5)]
                for h2 in range(2):
                    h = 2 * hp + h2
                    _wkv_prep(tiles_ref, rows_ref, gam_ref, t, h, *[tr[h2 * HD:(h2 + 1) * HD] for tr in trs],
                              lambda i, h=h: c_ref[h, i])
            return 0

        lax.fori_loop(0, rows // batch, tstep, 0)
    else:
        half = LANES // 2
        lane = lax.broadcasted_iota(jnp.int32, (HD, LANES), 1)

        def load_pair(q, tp):
            zz = jnp.concatenate(
                [nat_ref[q, pl.ds(_aligned(tp * 2 * SUBLANES + tt * SUBLANES, SUBLANES), SUBLANES),
                         hp * LANES:(hp + 1) * LANES] for tt in range(2) for hp in range(D // LANES)], axis=0)
            tr = zz.T
            top, bot = tr[0:HD], tr[HD:2 * HD]
            d0 = jnp.where(lane < half, top, pltpu.roll(bot, half, axis=1))
            d1 = jnp.where(lane < half, pltpu.roll(top, half, axis=1), bot)
            return d0, d1

        def pair(tp, _):
            pairs = [load_pair(q, tp) for q in range(5)]
            for tt in range(2):
                _wkv_prep(tiles_ref, rows_ref, gam_ref, 2 * tp + tt, 0, *[p[tt] for p in pairs], lambda i: c_ref[0, i])
            return 0

        n_chunks = rows // chunk
        pairs_per_chunk = chunk // (2 * SUBLANES)
        body(0, 0)
        for c in range(1, n_chunks):
            body(c, 0)
            for tp in range((c - 1) * pairs_per_chunk, c * pairs_per_chunk):
                pair(tp, 0)
        for tp in range((n_chunks - 1) * pairs_per_chunk, n_chunks * pairs_per_chunk):
            pair(tp, 0)

    ps_ref[0:batch, :] = ps_ref[rows:rows + batch, :]


def _rwkv_pre(proj, row_off, n_rows, batch, rows_blk, groups, shift_init, consts, mu, w0, a0, w2p, a2p, g2p,
              dst=None):
    ob = row_off // rows_blk
    width = 3 * D + LORA_PAD
    steps_blk = rows_blk // batch
    kern = functools.partial(_rwkv_pre_kernel, batch=batch, chunk=128, chains_are_heads=groups > 1)
    n_in = 12
    dst_spec, dst_arg, out_g = _shared_rows(dst, proj.shape[0])
    return pl.pallas_call(
        kern if dst is None else _with_dst(kern, n_in),
        grid=(n_rows // rows_blk,),
        in_specs=[
            pl.BlockSpec((rows_blk, D), lambda i: (ob + i, COL_R)),
            pl.BlockSpec((rows_blk, D), lambda i: (ob + i, COL_K)),
            pl.BlockSpec((rows_blk, D), lambda i: (ob + i, COL_V)),
            pl.BlockSpec((rows_blk, LORA_PAD), lambda i: (ob + i, COL_LORA)),
            _full((batch, width)), _full((1, width)), _full((1, D)), _full((1, D)),
            _full((LANES, D)), _full((LANES, D)), _full((2 * LANES, D)), _full((groups, 5, HD, LANES)),
        ] + dst_spec,
        out_specs=[pl.BlockSpec((steps_blk, groups, 6, HD, LANES), lambda i: (i, 0, 0, 0, 0)),
                   pl.BlockSpec((steps_blk, groups, SUBLANES, LANES), lambda i: (i, 0, 0, 0)),
                   pl.BlockSpec((rows_blk, D), lambda i: (ob + i, 0))],
        out_shape=[jax.ShapeDtypeStruct((n_rows // batch, groups, 6, HD, LANES), f32),
                   jax.ShapeDtypeStruct((n_rows // batch, groups, SUBLANES, LANES), f32),
                   out_g],
        input_output_aliases={} if dst is None else {n_in: 2},
        scratch_shapes=[pltpu.VMEM((rows_blk + batch, width), f32), pltpu.VMEM((5, rows_blk, D), f32),
                        pltpu.VMEM((groups, HD, LANES), f32)],
        compiler_params=pltpu.CompilerParams(dimension_semantics=("arbitrary",), vmem_limit_bytes=56 * MIB),
        name="rwkv_pre",
    )(proj, proj, proj, proj, shift_init, mu, w0, a0, w2p, a2p, g2p, consts, *dst_arg)


STATE_ROWS = HD * HD
KEY_UNROLL = 32


def _wkv_step(s_ref, tiles_ref, rows_ref, y_ref, t, window):
    def key_row(tile, j):
        return tiles_ref[t, 0, tile, pl.ds(j, 1), :]

    def reduce_keys(j, acc):
        sa, y0 = acc
        s = s_ref[pl.ds(pl.multiple_of(j * HD, HD), HD), :]
        return sa + s * key_row(TILE_ALPHA, j), y0 + s * key_row(TILE_WR, j)

    zero = jnp.zeros((HD, LANES), f32)
    sa, y0 = lax.fori_loop(0, HD, reduce_keys, (zero, zero), unroll=KEY_UNROLL)
    v = tiles_ref[t, 0, TILE_V]

    def update_keys(j, _):
        rows = pl.ds(pl.multiple_of(j * HD, HD), HD)
        s_ref[rows, :] = s_ref[rows, :] + sa * key_row(TILE_BETA, j) + v * key_row(TILE_KP, j)
        return 0

    lax.fori_loop(0, HD, update_keys, 0, unroll=KEY_UNROLL)
    y_ref[t, 0] = y0 + sa * rows_ref[t, 0, ROW_BR:ROW_BR + 1, :] + v * rows_ref[t, 0, ROW_KR:ROW_KR + 1, :]

    @pl.when((t + 1) % window == 0)
    def _():
        def rescale(j, _):
            rows = pl.ds(pl.multiple_of(j * HD, HD), HD)
            s_ref[rows, :] = s_ref[rows, :] * key_row(TILE_W, j)
            return 0

        lax.fori_loop(0, HD, rescale, 0, unroll=KEY_UNROLL)


def _wkv_scan_kernel(tiles_ref, rows_ref, st_ref, y_ref, sfin_ref, s_ref, *, steps, window):
    tb = pl.program_id(1)

    def value_rows(i):
        return pl.ds(i, HD, stride=HD)

    @pl.when(tb == 0)
    def _():
        def init(c, _):
            tr = st_ref[:, pl.ds(pl.multiple_of(c * LANES, LANES), LANES)].T
            for i2 in range(2):
                s_ref[value_rows(2 * c + i2), :] = tr[i2 * HD:(i2 + 1) * HD]
            return 0

        lax.fori_loop(0, STATE_ROWS // LANES, init, 0, unroll=8)

    def step(t, _):
        _wkv_step(s_ref, tiles_ref, rows_ref, y_ref, t, window)
        return 0

    lax.fori_loop(0, steps, step, 0)

    @pl.when(tb == pl.num_programs(1) - 1)
    def _():
        def fin(c, _):
            pair = jnp.concatenate([s_ref[value_rows(2 * c + i2), :] for i2 in range(2)], axis=0)
            sfin_ref[:, pl.ds(pl.multiple_of(c * LANES, LANES), LANES)] = pair.T
            return 0

        lax.fori_loop(0, STATE_ROWS // LANES, fin, 0, unroll=8)


def _wkv_scan(tiles, rows, state, steps_blk, window):
    n_steps, groups = tiles.shape[0], tiles.shape[1]
    assert steps_blk % window == 0
    kern = functools.partial(_wkv_scan_kernel, steps=steps_blk, window=window)
    sblk = pl.BlockSpec((LANES, STATE_ROWS), lambda g, i: (0, g))
    return pl.pallas_call(
        kern,
        grid=(groups, n_steps // steps_blk),
        in_specs=[pl.BlockSpec((steps_blk, 1, 6, HD, LANES), lambda g, i: (i, g, 0, 0, 0)),
                  pl.BlockSpec((steps_blk, 1, SUBLANES, LANES), lambda g, i: (i, g, 0, 0)),
                  sblk],
        out_specs=[pl.BlockSpec((steps_blk, 1, HD, LANES), lambda g, i: (i, g, 0, 0)), sblk],
        out_shape=[jax.ShapeDtypeStruct((n_steps, groups, HD, LANES), f32),
                   jax.ShapeDtypeStruct((LANES, groups * STATE_ROWS), f32)],
        scratch_shapes=[pltpu.VMEM((STATE_ROWS, LANES), f32)],
        compiler_params=pltpu.CompilerParams(dimension_semantics=("arbitrary", "arbitrary"),
                                             vmem_limit_bytes=48 * MIB),
        name="wkv_scan",
    )(tiles, rows, state)


def _wkv_norm(y, v, bonus, lnw, lnb):
    mean = jnp.sum(y, axis=0, keepdims=True) * (1.0 / HD)
    d = y - mean
    var = jnp.sum(d * d, axis=0, keepdims=True) * (1.0 / HD)
    return d * lax.rsqrt(var + GN_EPS) * lnw + lnb + bonus * v


def _wkv_post_kernel(y_ref, v_ref, rows_ref, c_ref, o_ref, *, batch, chains_are_heads):
    steps = y_ref.shape[0]

    def normed(t, g):
        return _wkv_norm(y_ref[t, g], v_ref[t, g, 0], rows_ref[t, g, ROW_BONUS:ROW_BONUS + 1, :],
                         c_ref[g, CONST_LNW], c_ref[g, CONST_LNB])

    if chains_are_heads:
        def tstep(t, _):
            r0 = pl.multiple_of(t * batch, batch)
            for hp in range(D // LANES):
                two = jnp.concatenate([normed(t, 2 * hp), normed(t, 2 * hp + 1)], axis=0)
                o_ref[pl.ds(r0, batch), hp * LANES:(hp + 1) * LANES] = two.T
            return 0

        lax.fori_loop(0, steps, tstep, 0)
    else:
        half = LANES // 2
        lane = lax.broadcasted_iota(jnp.int32, (HD, LANES), 1)

        def pair(tp, _):
            o0 = normed(2 * tp, 0)
            o1 = normed(2 * tp + 1, 0)
            top = jnp.where(lane < half, o0, pltpu.roll(o1, half, axis=1))
            bot = jnp.where(lane < half, pltpu.roll(o0, half, axis=1), o1)
            zz = jnp.concatenate([top, bot], axis=0).T
            for tt in range(2):
                row = pl.multiple_of(tp * 2 * SUBLANES + tt * SUBLANES, SUBLANES)
                for hp in range(D // LANES):
                    src = tt * HD + hp * SUBLANES
                    o_ref[pl.ds(row, SUBLANES), hp * LANES:(hp + 1) * LANES] = zz[src:src + SUBLANES, :]
            return 0

        lax.fori_loop(0, steps // 2, pair, 0, unroll=4)


def _outproj_kernel(ml_ref, g_ref, gr_ref, x_ref, yp_ref, vp_ref, rp_ref, cp_ref, ysm_ref, vs_ref, rs_ref, cs_ref,
                    wo_ref, n2_ref, rh_ref, rl_ref, rb_ref,
                    x1_ref, t_ref, selt_ref, seln_ref, op_ref, *, prompt_tiles, prompt_batch, sample_batch):
    tm = x_ref.shape[0]
    i = pl.program_id(0)

    @pl.when(i < prompt_tiles)
    def _():
        _wkv_post_kernel(yp_ref, vp_ref, rp_ref, cp_ref, op_ref, batch=prompt_batch, chains_are_heads=False)

    @pl.when(i >= prompt_tiles)
    def _():
        _wkv_post_kernel(ysm_ref, vs_ref, rs_ref, cs_ref, op_ref, batch=sample_batch, chains_are_heads=True)

    merged = ml_ref[...] + jax.nn.sigmoid(gr_ref[...]) * (op_ref[...] * g_ref[...])
    x1 = x_ref[...] + jnp.dot(merged.astype(bf16), wo_ref[...], preferred_element_type=f32)
    x1_ref[...] = x1
    t = x1 * lax.rsqrt(jnp.mean(x1 * x1, axis=-1, keepdims=True) + NORM_EPS) * n2_ref[...]
    th = t.astype(bf16)
    t_ref[...] = th
    tl = (t - th.astype(f32)).astype(bf16)
    nt_dims = (((1,), (1,)), ((), ()))
    lg = (lax.dot_general(rh_ref[...], th, nt_dims, preferred_element_type=f32)
          + lax.dot_general(rh_ref[...], tl, nt_dims, preferred_element_type=f32)
          + lax.dot_general(rl_ref[...], th, nt_dims, preferred_element_type=f32)) + rb_ref[...]
    row = lax.broadcasted_iota(jnp.int32, (EPG, tm), 0).astype(f32)
    neg = jnp.float32(-jnp.inf)
    glog = jnp.where(row < N_GROUPS, lg[0:EPG], neg)
    ge = jnp.exp(glog - jnp.max(glog, axis=0, keepdims=True))
    pg = ge / jnp.sum(ge, axis=0, keepdims=True)
    p_top = jnp.max(pg, axis=0, keepdims=True)
    g_idx = jnp.min(jnp.where(pg == p_top, row, EPG), axis=0, keepdims=True)
    le = jnp.zeros((EPG, tm), f32)
    for g in range(N_GROUPS):
        le = jnp.where(g_idx == g, lg[EPG * (g + 1):EPG * (g + 2)], le)
    qe = jnp.exp(le - jnp.max(le, axis=0, keepdims=True))
    q = qe / jnp.sum(qe, axis=0, keepdims=True)
    q1 = jnp.max(q, axis=0, keepdims=True)
    i1 = jnp.min(jnp.where(q == q1, row, EPG), axis=0, keepdims=True)
    qm = jnp.where(row == i1, -1.0, q)
    q2 = jnp.max(qm, axis=0, keepdims=True)
    i2 = jnp.min(jnp.where(qm == q2, row, EPG), axis=0, keepdims=True)
    qs = q1 + q2
    sel = jnp.concatenate([g_idx * EPG + i1, g_idx * EPG + i2, q1 / qs * p_top, q2 / qs * p_top,
                           jnp.zeros((SUBLANES - 4, tm), f32)], axis=0)
    selt_ref[...] = sel
    sel_pad = jnp.concatenate([sel, jnp.zeros((LANES - SUBLANES, tm), f32)], axis=0)
    for c in range(tm // LANES):
        seln_ref[c * LANES:(c + 1) * LANES, :] = sel_pad[:, c * LANES:(c + 1) * LANES].T


def _outproj(ml, g, proj, x, scan_p, scan_s, wo, n2, rh, rl, rb, prompt_batch, sample_batch, tm=512):
    n = x.shape[0]
    y_p, tiles_p, rows_p, consts_p = scan_p
    y_s, tiles_s, rows_s, consts_s = scan_s
    n_p = y_p.shape[0] * prompt_batch
    sp, ss = tm // prompt_batch, tm // sample_batch
    prompt_tiles = n_p // tm
    sample_tiles = (n - n_p) // tm
    assert n_p % tm == 0 and sp % 2 == 0 and y_s.shape[0] == sample_tiles * ss
    pidx = lambda i: jnp.minimum(i, prompt_tiles - 1)
    sidx = lambda i: jnp.clip(i - prompt_tiles, 0, sample_tiles - 1)
    groups = y_s.shape[1]
    blk = pl.BlockSpec((tm, D), lambda i: (i, 0))
    kern = functools.partial(_outproj_kernel, prompt_tiles=prompt_tiles, prompt_batch=prompt_batch,
                             sample_batch=sample_batch)
    return pl.pallas_call(
        kern,
        grid=(n // tm,),
        in_specs=[blk, blk, pl.BlockSpec((tm, D), lambda i: (i, COL_GR)), blk,
                  pl.BlockSpec((sp, 1, HD, LANES), lambda i: (pidx(i), 0, 0, 0)),
                  pl.BlockSpec((sp, 1, 1, HD, LANES), lambda i: (pidx(i), 0, TILE_V, 0, 0)),
                  pl.BlockSpec((sp, 1, SUBLANES, LANES), lambda i: (pidx(i), 0, 0, 0)),
                  _full((1, 5, HD, LANES)),
                  pl.BlockSpec((ss, groups, HD, LANES), lambda i: (sidx(i), 0, 0, 0)),
                  pl.BlockSpec((ss, groups, 1, HD, LANES), lambda i: (sidx(i), 0, TILE_V, 0, 0)),
                  pl.BlockSpec((ss, groups, SUBLANES, LANES), lambda i: (sidx(i), 0, 0, 0)),
                  _full((groups, 5, HD, LANES)),
                  _full((D, D)), _full((1, D)), _full((ROUTER_ROWS, D)), _full((ROUTER_ROWS, D)),
                  _full((ROUTER_ROWS, 1))],
        out_specs=[blk, blk, pl.BlockSpec((SUBLANES, tm), lambda i: (0, i)),
                   pl.BlockSpec((tm, LANES), lambda i: (i, 0))],
        out_shape=[jax.ShapeDtypeStruct((n, D), f32), jax.ShapeDtypeStruct((n, D), bf16),
                   jax.ShapeDtypeStruct((SUBLANES, n), f32), jax.ShapeDtypeStruct((n, LANES), f32)],
        scratch_shapes=[pltpu.VMEM((tm, D), f32)],
        compiler_params=pltpu.CompilerParams(dimension_semantics=("arbitrary",), vmem_limit_bytes=56 * MIB),
        name="outproj_router",
    )(ml, g, proj, x, y_p, tiles_p, rows_p, consts_p, y_s, tiles_s, rows_s, consts_s, wo, n2, rh, rl, rb)


MOE_CHUNK = 32
MOE_ROWS = 128
MOE_GATHER = 512
MOE_EXPERTS_PER_STEP = 4
MOE_OVERRUN = MOE_ROWS
MOE_CAP = -(-(2 * TOKEN_TILE + N_EXPERTS * (MOE_CHUNK - 1) + MOE_OVERRUN) // MOE_GATHER) * MOE_GATHER
SEL_E1, SEL_E2, SEL_C1, SEL_C2 = range(4)


def _route_meta_kernel(seln_ref, o_ref):
    tm = seln_ref.shape[0]
    lane = lax.broadcasted_iota(jnp.int32, (tm, LANES), 1).astype(f32)
    sel = seln_ref[...]
    hit = (lane == sel[:, SEL_E1:SEL_E1 + 1]) | (lane == sel[:, SEL_E2:SEL_E2 + 1])
    cnt = jnp.sum(jnp.where(hit, 1.0, 0.0), axis=0, keepdims=True)
    nchunk = jnp.floor((cnt + (MOE_CHUNK - 1)) * (1.0 / MOE_CHUNK))
    upper = (lax.broadcasted_iota(jnp.int32, (LANES, LANES), 0)
             < lax.broadcasted_iota(jnp.int32, (LANES, LANES), 1))
    offs = jnp.dot(jnp.broadcast_to(nchunk, (SUBLANES, LANES)).astype(bf16), jnp.where(upper, 1.0, 0.0).astype(bf16),
                   preferred_element_type=f32)
    row = lax.broadcasted_iota(jnp.int32, (SUBLANES, LANES), 0)
    o_ref[0] = jnp.where(row == 0, nchunk, jnp.where(row == 1, offs, 0.0)).astype(jnp.int32)


def _route_meta(seln, tm):
    tiles = seln.shape[0] // tm
    return pl.pallas_call(
        _route_meta_kernel,
        grid=(tiles,),
        in_specs=[pl.BlockSpec((tm, LANES), lambda i: (i, 0))],
        out_specs=pl.BlockSpec((1, SUBLANES, LANES), lambda i: (i, 0, 0)),
        out_shape=jax.ShapeDtypeStruct((tiles, SUBLANES, LANES), jnp.int32),
        compiler_params=pltpu.CompilerParams(dimension_semantics=("arbitrary",)),
        name="route_meta",
    )(seln)


def _moe_experts_kernel(nch_ref, off_ref, t_ref, selt_ref, tri_ref, wgu_ref, wd_ref, ys_ref, pos_ref,
                        xs_ref, cw_ref):
    i = pl.program_id(0)
    e = pl.program_id(1)
    tm = t_ref.shape[0]

    @pl.when(e == 0)
    def _():
        ys_ref[...] = jnp.zeros_like(ys_ref)
        e1 = selt_ref[SEL_E1:SEL_E1 + 1, :]
        e2 = selt_ref[SEL_E2:SEL_E2 + 1, :]
        erow = lax.broadcasted_iota(jnp.int32, (N_EXPERTS, tm), 0).astype(f32)
        oh1 = jnp.where(erow == e1, 1.0, 0.0)
        oh2 = jnp.where(erow == e2, 1.0, 0.0)
        before1 = jnp.dot(oh1.astype(bf16), tri_ref[...], preferred_element_type=f32)
        before2 = jnp.dot(oh2.astype(bf16), tri_ref[...], preferred_element_type=f32)
        cnt1 = jnp.sum(oh1, axis=1, keepdims=True)
        cnt2 = jnp.sum(oh2, axis=1, keepdims=True)
        nchunk = jnp.floor((cnt1 + cnt2 + (MOE_CHUNK - 1)) * (1.0 / MOE_CHUNK))
        lower = (lax.broadcasted_iota(jnp.int32, (N_EXPERTS, LANES), 1)
                 < lax.broadcasted_iota(jnp.int32, (N_EXPERTS, LANES), 0))
        nchunk_rows = jnp.concatenate([jnp.broadcast_to(nchunk, (N_EXPERTS, LANES)),
                                       jnp.zeros((LANES - N_EXPERTS, LANES), f32)], axis=0)
        start = jnp.dot(jnp.where(lower, 1.0, 0.0).astype(bf16), nchunk_rows.astype(bf16),
                        preferred_element_type=f32)[:, 0:1] * MOE_CHUNK
        pos1 = jnp.sum(oh1 * (start + before1), axis=0, keepdims=True)
        pos2 = jnp.sum(oh2 * (start + cnt1 + before2), axis=0, keepdims=True)
        pos_ref[...] = jnp.concatenate([pos1, pos2, jnp.zeros((SUBLANES - 2, tm), f32)], axis=0)

    per_step = wgu_ref.shape[0]

    def gathered_chunks(step):
        last = i * N_EXPERTS + jnp.maximum(step * per_step + per_step - 1, 0)
        end_row = (off_ref[last] + nch_ref[last]) * MOE_CHUNK + MOE_OVERRUN
        chunks = jnp.minimum((end_row + MOE_GATHER - 1) // MOE_GATHER, MOE_CAP // MOE_GATHER)
        return jnp.where(step < 0, 0, chunks)

    def gather_chunk(k, _):
        r0 = pl.multiple_of(k * MOE_GATHER, MOE_GATHER)
        ridx = (lax.broadcasted_iota(jnp.int32, (MOE_GATHER, tm), 0) + r0).astype(f32)
        p1 = ridx == pos_ref[0:1, :]
        p2 = ridx == pos_ref[1:2, :]
        onehot = jnp.where(p1 | p2, 1.0, 0.0).astype(bf16)
        xs_ref[pl.ds(r0, MOE_GATHER), :] = jnp.dot(onehot, t_ref[...], preferred_element_type=f32).astype(bf16)
        w = jnp.sum(jnp.where(p1, selt_ref[SEL_C1:SEL_C1 + 1, :], 0.0)
                    + jnp.where(p2, selt_ref[SEL_C2:SEL_C2 + 1, :], 0.0), axis=1, keepdims=True)
        cw_ref[pl.ds(r0, MOE_GATHER), :] = jnp.broadcast_to(w, (MOE_GATHER, LANES))
        return 0

    lax.fori_loop(gathered_chunks(e - 1), gathered_chunks(e), gather_chunk, 0)

    def expert_rows(u, rows):
        gu = jnp.dot(xs_ref[rows, :], wgu_ref[u], preferred_element_type=f32)
        w = cw_ref[rows, :]
        h = jax.nn.silu(gu[:, :D_EXPERT]) * gu[:, D_EXPERT:] * jnp.concatenate([w, w], axis=1)
        return jnp.dot(h.astype(bf16), wd_ref[u], preferred_element_type=f32).astype(bf16)

    per_block = MOE_ROWS // MOE_CHUNK
    experts = [i * N_EXPERTS + e * per_step + u for u in range(per_step)]
    bases = [off_ref[x] * MOE_CHUNK for x in experts]

    for u in range(per_step):
        rows = pl.ds(pl.multiple_of(bases[u], MOE_CHUNK), MOE_ROWS)
        ys_ref[0, rows, :] = expert_rows(u, rows)

    for u in range(per_step):
        seg_end = bases[u] + nch_ref[experts[u]] * MOE_CHUNK

        def more(c, _, u=u, seg_end=seg_end):
            start = bases[u] + (c + 1) * MOE_ROWS
            rows = pl.ds(pl.multiple_of(start, MOE_CHUNK), MOE_ROWS)
            ridx = lax.broadcasted_iota(jnp.int32, (MOE_ROWS, D), 0) + start
            ys_ref[0, rows, :] = jnp.where(ridx < seg_end, expert_rows(u, rows), ys_ref[0, rows, :])
            return 0

        lax.fori_loop(0, jnp.maximum(nch_ref[experts[u]] - 1, 0) // per_block, more, 0)


def _moe_experts(nch, off, t, selt, tri, wgu, wd, tm):
    n = t.shape[0]
    tiles = n // tm
    grid_spec = pltpu.PrefetchScalarGridSpec(
        num_scalar_prefetch=2,
        grid=(tiles, N_EXPERTS // MOE_EXPERTS_PER_STEP),
        in_specs=[
            pl.BlockSpec((tm, D), lambda i, e, nch, off: (i, 0)),
            pl.BlockSpec((SUBLANES, tm), lambda i, e, nch, off: (0, i)),
            pl.BlockSpec((tm, tm), lambda i, e, nch, off: (0, 0)),
            pl.BlockSpec((MOE_EXPERTS_PER_STEP, D, 2 * D_EXPERT), lambda i, e, nch, off: (e, 0, 0)),
            pl.BlockSpec((MOE_EXPERTS_PER_STEP, D_EXPERT, D), lambda i, e, nch, off: (e, 0, 0)),
        ],
        out_specs=[pl.BlockSpec((1, MOE_CAP, D), lambda i, e, nch, off: (i, 0, 0)),
                   pl.BlockSpec((SUBLANES, tm), lambda i, e, nch, off: (0, i))],
        scratch_shapes=[pltpu.VMEM((MOE_CAP, D), bf16), pltpu.VMEM((MOE_CAP, LANES), f32)],
    )
    return pl.pallas_call(
        _moe_experts_kernel,
        grid_spec=grid_spec,
        out_shape=[jax.ShapeDtypeStruct((tiles, MOE_CAP, D), bf16), jax.ShapeDtypeStruct((SUBLANES, n), f32)],
        compiler_params=pltpu.CompilerParams(dimension_semantics=("arbitrary", "arbitrary"),
                                             vmem_limit_bytes=56 * MIB),
        name="moe_experts",
    )(nch, off, t, selt, tri, wgu, wd)


def _moe_combine_kernel(nch_ref, off_ref, ysort_ref, pos_ref, x1_ref, fn_ref, yp_ref, ys_ref, acc_ref, yf_ref,
                        *, prompt_tiles, parts):
    i = pl.program_id(0)
    part = pl.program_id(1)
    rows = x1_ref.shape[0]
    bp, tsteps = yp_ref.shape[0], yp_ref.shape[1]
    bs = ys_ref.shape[0]
    ssteps = ys_ref.shape[1] // parts
    used_rows = (off_ref[i * N_EXPERTS + N_EXPERTS - 1] + nch_ref[i * N_EXPERTS + N_EXPERTS - 1]) * MOE_CHUNK

    pos_t = jnp.concatenate([pos_ref[...], jnp.zeros((LANES - SUBLANES, rows), f32)], axis=0)
    pos_n = jnp.concatenate([pos_t[:, c * LANES:(c + 1) * LANES].T for c in range(rows // LANES)], axis=0)
    pos1 = pos_n[:, 0:1]
    pos2 = pos_n[:, 1:2]
    acc_ref[...] = x1_ref[...]
    for k in range(MOE_CAP // MOE_GATHER):
        @pl.when(k * MOE_GATHER < used_rows)
        def _(k=k):
            cidx = (lax.broadcasted_iota(jnp.int32, (rows, MOE_GATHER), 1) + k * MOE_GATHER).astype(f32)
            onehot = jnp.where((cidx == pos1) | (cidx == pos2), 1.0, 0.0).astype(bf16)
            acc_ref[...] += jnp.dot(onehot, ysort_ref[0, k * MOE_GATHER:(k + 1) * MOE_GATHER, :],
                                    preferred_element_type=f32)

    xo = acc_ref[...]
    y = xo * lax.rsqrt(jnp.mean(xo * xo, axis=-1, keepdims=True) + NORM_EPS) * fn_ref[...]
    for cb in range(D // LANES):
        yf_ref[cb] = y[:, cb * LANES:(cb + 1) * LANES]

    @pl.when(i < prompt_tiles)
    def _():
        for b in range(bp):
            yp_ref[b] = jnp.concatenate([yf_ref[cb, pl.ds(b, tsteps, stride=bp), :] for cb in range(D // LANES)],
                                        axis=1)

    for p in range(parts):
        @pl.when((i >= prompt_tiles) & (part == p))
        def _(p=p):
            for t in range(ssteps):
                ys_ref[:, p * ssteps + t, :] = jnp.concatenate(
                    [yf_ref[cb, t * bs:(t + 1) * bs, :] for cb in range(D // LANES)], axis=1)


def _moe_combine(nch, off, ysort, pos, x1, fn, prompt_shape, sample_shape, tm, parts=2):
    n = x1.shape[0]
    bp, tp, _ = prompt_shape
    bs, ts, _ = sample_shape
    rows = tm // parts
    assert rows % bp == 0 and tp % (rows // bp) == 0 and bs * ts == tm and ts % parts == 0
    prompt_tiles = bp * tp // tm
    last_prompt_blk = prompt_tiles * parts - 1
    kern = functools.partial(_moe_combine_kernel, prompt_tiles=prompt_tiles, parts=parts)
    grid_spec = pltpu.PrefetchScalarGridSpec(
        num_scalar_prefetch=2,
        grid=(n // tm, parts),
        in_specs=[
            pl.BlockSpec((1, MOE_CAP, D), lambda i, p, nch, off: (i, 0, 0)),
            pl.BlockSpec((SUBLANES, rows), lambda i, p, nch, off: (0, i * parts + p)),
            pl.BlockSpec((rows, D), lambda i, p, nch, off: (i * parts + p, 0)),
            pl.BlockSpec((1, D), lambda i, p, nch, off: (0, 0)),
        ],
        out_specs=[pl.BlockSpec((bp, rows // bp, D),
                                lambda i, p, nch, off: (0, jnp.minimum(i * parts + p, last_prompt_blk), 0)),
                   pl.BlockSpec((bs, ts, D), lambda i, p, nch, off: (0, 0, 0))],
        scratch_shapes=[pltpu.VMEM((rows, D), f32), pltpu.VMEM((D // LANES, rows, LANES), f32)],
    )
    return pl.pallas_call(
        kern,
        grid_spec=grid_spec,
        out_shape=[jax.ShapeDtypeStruct(prompt_shape, f32), jax.ShapeDtypeStruct(sample_shape, f32)],
        compiler_params=pltpu.CompilerParams(dimension_semantics=("arbitrary", "arbitrary"),
                                             vmem_limit_bytes=56 * MIB),
        name="moe_combine",
    )(nch, off, ysort, pos, x1, fn)


def _chain_tiles_prompt(vec, batch):
    t = vec.reshape(HEADS // 2, 2, HD)
    t = jnp.transpose(t, (2, 1, 0))
    return jnp.broadcast_to(t[..., None], (HD, 2, HEADS // 2, batch)).reshape(HD, LANES)


def _chain_tiles_sample(vec):
    return jnp.broadcast_to(vec.reshape(HEADS, HD)[..., None], (HEADS, HD, LANES))


def kernel(x_prompt, x_sample, state_conv, state_lru, state_shift, state_wkv, norm1, w_in, conv_w, conv_b, lru_wx, lru_bx, lru_wa, lru_ba, lru_a_param, rwkv_mu, rwkv_w0, rwkv_w2, rwkv_a0, rwkv_a2, rwkv_g2, rwkv_k_k, rwkv_k_a, rwkv_r_k, rwkv_ln_w, rwkv_ln_b, w_out, norm2, router_group, router_group_b, router_expert, router_expert_b, exp_gate, exp_up, exp_down, final_norm):
    bp, tp, _ = x_prompt.shape
    bs, ts, _ = x_sample.shape
    assert norm1.shape[0] == 1 and bp * HEADS == LANES and bs == LANES and tp % 64 == 0 and ts % 2 == 0
    n_p, n_s = bp * tp, bs * ts
    c_rw = 2 * D
    c_lora = c_rw + 3 * D
    c_gl = c_lora + LORA

    w = w_in[0]
    w_all = jnp.concatenate([w[:, :c_lora], w[:, c_gl:], w[:, c_lora:c_gl],
                             jnp.zeros((D, LORA_PAD - LORA), f32)], axis=1).astype(bf16)
    mu = rwkv_mu[0]
    mu_all = jnp.concatenate([mu[:3 * D], mu[3 * D:], jnp.zeros((LORA_PAD - LORA,), f32)])[None]
    w2p = jnp.concatenate([rwkv_w2[0], jnp.zeros((LANES - LORA_W, D), f32)]).astype(bf16)
    a2p = jnp.concatenate([jnp.zeros((LORA_W, D), f32), rwkv_a2[0]]).astype(bf16)
    g2p = jnp.concatenate([rwkv_g2[0], jnp.zeros((2 * LANES - LORA_G, D), f32)]).astype(bf16)
    wxa = jnp.concatenate([lru_wx[0], lru_wa[0]], axis=-1).astype(bf16)
    row = lambda v: v.reshape(1, -1)
    rk = rwkv_r_k[0].reshape(D)
    cvecs = (rwkv_k_k[0], rwkv_k_a[0], rk, rwkv_ln_w[0], rwkv_ln_b[0])
    consts_p = jnp.stack([_chain_tiles_prompt(v, bp) for v in cvecs])[None]
    consts_s = jnp.stack([_chain_tiles_sample(v) for v in cvecs], axis=1)
    rw = jnp.zeros((ROUTER_ROWS, D), f32)
    rw = rw.at[0:N_GROUPS].set(router_group[0].T).at[EPG:EPG + N_EXPERTS].set(router_expert[0].T)
    rh = rw.astype(bf16)
    rl = (rw - rh.astype(f32)).astype(bf16)
    rb = jnp.zeros((ROUTER_ROWS, 1), f32)
    rb = rb.at[0:N_GROUPS, 0].set(router_group_b[0]).at[EPG:EPG + N_EXPERTS, 0].set(router_expert_b[0])
    wgu = jnp.concatenate([exp_gate[0], exp_up[0]], axis=-1).astype(bf16)
    wd = exp_down[0].astype(bf16)
    wo = w_out[0].astype(bf16)

    proj, x = _inproj(x_prompt, x_sample, row(norm1[0]), w_all, tm=TOKEN_TILE)

    lru_args = (conv_w[0], row(conv_b[0]), wxa, row(lru_bx[0]), row(lru_ba[0]), row(lru_a_param[0]))
    conv_init_s = jnp.transpose(state_conv[0], (1, 0, 2)).reshape(3 * bs, D)
    ml, hl_p = _lru(proj, 0, n_p, bp, 512, True, jnp.zeros((3 * bp, D), f32), jnp.zeros((bp, D), f32), *lru_args)
    ml, hl_s = _lru(proj, n_p, n_s, bs, 512, False, conv_init_s, state_lru[0], *lru_args, dst=ml)

    sh = state_shift[0]
    shift_init_s = jnp.concatenate([sh, jnp.zeros((bs, LORA_PAD - LORA), f32)], axis=1)
    pre_args = (mu_all, row(rwkv_w0[0]), row(rwkv_a0[0]), w2p, a2p, g2p)
    pre_rows = 256
    tiles_p, rows_p, g = _rwkv_pre(proj, 0, n_p, bp, pre_rows, 1, jnp.zeros((bp, 3 * D + LORA_PAD), f32),
                                   consts_p, *pre_args)
    tiles_s, rows_s, g = _rwkv_pre(proj, n_p, n_s, bs, pre_rows, HEADS, shift_init_s, consts_s, *pre_args, dst=g)

    y_p, sfin_p = _wkv_scan(tiles_p, rows_p, jnp.zeros((LANES, STATE_ROWS), f32), 64, pre_rows // bp)
    y_s, sfin_s = _wkv_scan(tiles_s, rows_s, state_wkv[0].reshape(bs, HEADS * STATE_ROWS), ts, pre_rows // bs)

    x1, t, selt, seln = _outproj(ml, g, proj, x, (y_p, tiles_p, rows_p, consts_p), (y_s, tiles_s, rows_s, consts_s),
                                 wo, row(norm2[0]), rh, rl, rb, bp, bs)
    meta = _route_meta(seln, TOKEN_TILE)
    nch = meta[:, 0, :N_EXPERTS].reshape(-1)
    off = meta[:, 1, :N_EXPERTS].reshape(-1)
    tri = jnp.triu(jnp.ones((TOKEN_TILE, TOKEN_TILE), bf16), k=1)
    ysort, pos = _moe_experts(nch, off, t, selt, tri, wgu, wd, TOKEN_TILE)
    y_prompt, y_sample = _moe_combine(nch, off, ysort, pos, x1, row(final_norm), x_prompt.shape, x_sample.shape,
                                      TOKEN_TILE)

    def last_rows(lo, hi, n_end, batch, steps):
        return proj[n_end - steps * batch:n_end, lo:hi].reshape(steps, batch, hi - lo)

    conv_p = jnp.transpose(last_rows(0, D, n_p, bp, 3), (1, 0, 2))[None]
    conv_s = jnp.transpose(last_rows(0, D, n_p + n_s, bs, 3), (1, 0, 2))[None]

    def shift_rows(n_end, batch):
        return jnp.concatenate([last_rows(2 * D, 5 * D, n_end, batch, 1)[0],
                                last_rows(7 * D, 7 * D + LORA, n_end, batch, 1)[0]], axis=1)[None]

    wkv_p = jnp.transpose(sfin_p.reshape(2, HEADS // 2, bp, HD, HD), (2, 1, 0, 3, 4)).reshape(1, bp, HEADS, HD, HD)
    wkv_s = sfin_s.reshape(1, bs, HEADS, HD, HD)
    return (y_prompt, y_sample, conv_p, hl_p[None], shift_rows(n_p, bp), wkv_p,
            conv_s, hl_s[None], shift_rows(n_p + n_s, bs), wkv_s)
```

```python
import functools

import jax
import jax.numpy as jnp
from jax import lax
from jax.experimental import pallas as pl
from jax.experimental.pallas import tpu as pltpu

f32 = jnp.float32
bf16 = jnp.bfloat16

D = 1024
HEADS = 16
HD = 64
LANES = 128
SUBLANES = 8
LORA_W = 64
LORA_A = 64
LORA_G = 160
LORA = LORA_W + LORA_A + LORA_G
LORA_PAD = 512
N_GROUPS = 4
EPG = 8
N_EXPERTS = N_GROUPS * EPG
D_EXPERT = 256
LRU_C = 8.0
GN_EPS = 64e-5
NORM_EPS = 1e-6
PROJ_COLS = 7 * D + LORA_PAD
COL_LRU_X, COL_LRU_Y, COL_R, COL_K, COL_V, COL_GL, COL_GR = range(7)
COL_LORA = 7 * D // LORA_PAD
ROUTER_ROWS = 48
MIB = 1024 * 1024
TOKEN_TILE = 1024


def _aligned(start, multiple):
    return start if isinstance(start, int) else pl.multiple_of(start, multiple)


def _full(shape, grid_rank=1):
    zeros = tuple(0 for _ in shape)
    if grid_rank == 1:
        return pl.BlockSpec(shape, lambda i: zeros)
    return pl.BlockSpec(shape, lambda i, j: zeros)


def _inproj_kernel(xp_ref, xs_ref, g_ref, w_ref, o_ref, xo_ref, xf_ref, xn_ref, *, prompt_tiles):
    i = pl.program_id(0)
    first_col = pl.program_id(1) == 0
    bp, tsteps = xp_ref.shape[0], xp_ref.shape[1]
    bs, ssteps = xs_ref.shape[0], xs_ref.shape[1]

    @pl.when(first_col & (i < prompt_tiles))
    def _():
        for b in range(bp):
            v = xp_ref[b]
            for cb in range(D // LANES):
                xf_ref[cb, pl.ds(b, tsteps, stride=bp), :] = v[:, cb * LANES:(cb + 1) * LANES]

    @pl.when(first_col & (i >= prompt_tiles))
    def _():
        for t in range(ssteps):
            v = xs_ref[:, t, :]
            for cb in range(D // LANES):
                xf_ref[cb, t * bs:(t + 1) * bs, :] = v[:, cb * LANES:(cb + 1) * LANES]

    @pl.when(first_col)
    def _():
        x = jnp.concatenate([xf_ref[cb] for cb in range(D // LANES)], axis=1)
        xo_ref[...] = x
        ms = jnp.mean(x * x, axis=-1, keepdims=True)
        xn_ref[...] = (x * lax.rsqrt(ms + NORM_EPS) * g_ref[...]).astype(bf16)

    o_ref[...] = jnp.dot(xn_ref[...], w_ref[...], preferred_element_type=f32)


def _inproj(x_prompt, x_sample, gain, w, tm, tn=1536):
    bp, tp, _ = x_prompt.shape
    bs, ts, _ = x_sample.shape
    assert tm % bp == 0 and tp % (tm // bp) == 0 and bs * ts == tm
    prompt_tiles = bp * tp // tm
    n = bp * tp + bs * ts
    kern = functools.partial(_inproj_kernel, prompt_tiles=prompt_tiles)
    return pl.pallas_call(
        kern,
        grid=(n // tm, PROJ_COLS // tn),
        in_specs=[
            pl.BlockSpec((bp, tm // bp, D), lambda i, j: (0, jnp.minimum(i, prompt_tiles - 1), 0)),
            pl.BlockSpec((bs, ts, D), lambda i, j: (0, 0, 0)),
            pl.BlockSpec((1, D), lambda i, j: (0, 0)),
            pl.BlockSpec((D, tn), lambda i, j: (0, j)),
        ],
        out_specs=[pl.BlockSpec((tm, tn), lambda i, j: (i, j)), pl.BlockSpec((tm, D), lambda i, j: (i, 0))],
        out_shape=[jax.ShapeDtypeStruct((n, PROJ_COLS), f32), jax.ShapeDtypeStruct((n, D), f32)],
        scratch_shapes=[pltpu.VMEM((D // LANES, tm, LANES), f32), pltpu.VMEM((tm, D), bf16)],
        compiler_params=pltpu.CompilerParams(dimension_semantics=("arbitrary", "arbitrary"),
                                             vmem_limit_bytes=56 * MIB),
        name="inproj",
    )(x_prompt, x_sample, gain, w)


def _lru_kernel(x_ref, y_ref, gl_ref, cinit_ref, h0_ref, cw_ref, cb_ref, wxa_ref, bx_ref, ba_ref, ap_ref,
                o_ref, hl_ref, xs_ref, a_ref, b_ref, h_ref, *, batch, reset_first, chunk):
    rows = x_ref.shape[0]
    nt = rows // batch
    hist = 3 * batch
    pid = pl.program_id(0)

    @pl.when(pid == 0)
    def _():
        xs_ref[0:hist, :] = cinit_ref[...]
        h_ref[...] = h0_ref[...]

    xs_ref[hist:hist + rows, :] = x_ref[...]
    logsig = -jax.nn.softplus(-ap_ref[...])

    def gates(c, _):
        r0 = pl.multiple_of(c * chunk, chunk)
        xc = (cb_ref[...]
              + cw_ref[3:4, :] * xs_ref[pl.ds(pl.multiple_of(r0 + hist, SUBLANES), chunk), :]
              + cw_ref[2:3, :] * xs_ref[pl.ds(pl.multiple_of(r0 + 2 * batch, SUBLANES), chunk), :]
              + cw_ref[1:2, :] * xs_ref[pl.ds(pl.multiple_of(r0 + batch, SUBLANES), chunk), :]
              + cw_ref[0:1, :] * xs_ref[pl.ds(r0, chunk), :])
        if reset_first:
            grow = lax.broadcasted_iota(jnp.int32, (chunk, LANES), 0) + (r0 + pid * rows)
            first = grow < batch
        for n in range(D // LANES):
            sl = slice(n * LANES, (n + 1) * LANES)
            xn = xc[:, sl]
            g2 = jnp.dot(xn.astype(bf16), wxa_ref[n], preferred_element_type=f32)
            gate_x = jax.nn.sigmoid(g2[:, :LANES] + bx_ref[:, sl])
            gate_a = jax.nn.sigmoid(g2[:, LANES:] + ba_ref[:, sl])
            log_a = LRU_C * gate_a * logsig[:, sl]
            a = jnp.exp(log_a)
            mult = jnp.sqrt(-jnp.tanh(log_a) * (a * a + 1.0))
            if reset_first:
                mult = jnp.where(first, 1.0, mult)
            a_ref[pl.ds(r0, chunk), sl] = a
            b_ref[pl.ds(r0, chunk), sl] = xn * gate_x * mult
        return 0

    lax.fori_loop(0, rows // chunk, gates, 0)

    def scan(t, h):
        r0 = pl.multiple_of(t * batch, batch)
        h = a_ref[pl.ds(r0, batch), :] * h + b_ref[pl.ds(r0, batch), :]
        b_ref[pl.ds(r0, batch), :] = h
        return h

    h = lax.fori_loop(0, nt, scan, h_ref[...], unroll=(8 if nt >= 8 and batch == SUBLANES else 1))
    h_ref[...] = h
    hl_ref[...] = h
    xs_ref[0:hist, :] = xs_ref[rows:rows + hist, :]

    def outp(c, _):
        r0 = pl.multiple_of(c * chunk, chunk)
        o_ref[pl.ds(r0, chunk), :] = (b_ref[pl.ds(r0, chunk), :] * jax.nn.gelu(y_ref[pl.ds(r0, chunk), :])
                                      * jax.nn.sigmoid(gl_ref[pl.ds(r0, chunk), :]))
        return 0

    lax.fori_loop(0, rows // chunk, outp, 0)


def _with_dst(kern, n_in):
    def wrapped(*refs):
        return kern(*refs[:n_in], *refs[n_in + 1:])
    return wrapped


def _shared_rows(dst, n_total):
    if dst is None:
        return [], [], jax.ShapeDtypeStruct((n_total, D), f32)
    return [pl.BlockSpec(memory_space=pl.ANY)], [dst], jax.ShapeDtypeStruct(dst.shape, dst.dtype)


def _lru(proj, row_off, n_rows, batch, rows_blk, reset_first, conv_init, h0, cw, cb, wxa, bx, ba, ap, dst=None):
    ob = row_off // rows_blk
    kern = functools.partial(_lru_kernel, batch=batch, reset_first=reset_first, chunk=128)
    n_in = 11
    dst_spec, dst_arg, out0 = _shared_rows(dst, proj.shape[0])
    return pl.pallas_call(
        kern if dst is None else _with_dst(kern, n_in),
        grid=(n_rows // rows_blk,),
        in_specs=[
            pl.BlockSpec((rows_blk, D), lambda i: (ob + i, COL_LRU_X)),
            pl.BlockSpec((rows_blk, D), lambda i: (ob + i, COL_LRU_Y)),
            pl.BlockSpec((rows_blk, D), lambda i: (ob + i, COL_GL)),
            _full((3 * batch, D)), _full((batch, D)), _full((4, D)), _full((1, D)),
            _full((D // LANES, LANES, 2 * LANES)), _full((1, D)), _full((1, D)), _full((1, D)),
        ] + dst_spec,
        out_specs=[pl.BlockSpec((rows_blk, D), lambda i: (ob + i, 0)), _full((batch, D))],
        out_shape=[out0, jax.ShapeDtypeStruct((batch, D), f32)],
        input_output_aliases={} if dst is None else {n_in: 0},
        scratch_shapes=[pltpu.VMEM((rows_blk + 3 * batch, D), f32), pltpu.VMEM((rows_blk, D), f32),
                        pltpu.VMEM((rows_blk, D), f32), pltpu.VMEM((batch, D), f32)],
        compiler_params=pltpu.CompilerParams(dimension_semantics=("arbitrary",), vmem_limit_bytes=48 * MIB),
        name="lru",
    )(proj, proj, proj, conv_init, h0, cw, cb, wxa, bx, ba, ap, *dst_arg)


TILE_ALPHA, TILE_BETA, TILE_GAMMA, TILE_KP, TILE_WR, TILE_V = range(6)
ROW_BR, ROW_KR, ROW_BONUS = range(3)
CONST_KK, CONST_KA, CONST_RK, CONST_LNW, CONST_LNB = range(5)
NAT_R, NAT_W, NAT_K, NAT_V, NAT_A = range(5)


def _wkv_prep(tiles_ref, rows_ref, gam_ref, t, g, r, w, k, v, a, const):
    kk = k * const(CONST_KK)
    nrm = jnp.sqrt(jnp.sum(kk * kk, axis=0, keepdims=True))
    kk = kk * (1.0 / jnp.maximum(nrm, 1e-12))
    beta = kk * a
    kp = k * (1.0 + (a - 1.0) * const(CONST_KA))
    gam_prev = gam_ref[g]
    gam = gam_prev * w
    gam_ref[g] = gam
    inv = 1.0 / gam
    tiles_ref[t, g, TILE_ALPHA] = -kk * gam_prev
    tiles_ref[t, g, TILE_BETA] = beta * inv
    tiles_ref[t, g, TILE_GAMMA] = gam
    tiles_ref[t, g, TILE_KP] = kp * inv
    tiles_ref[t, g, TILE_WR] = gam * r
    tiles_ref[t, g, TILE_V] = v
    rows_ref[t, g, ROW_BR:ROW_BR + 1, :] = jnp.sum(beta * r, axis=0, keepdims=True)
    rows_ref[t, g, ROW_KR:ROW_KR + 1, :] = jnp.sum(kp * r, axis=0, keepdims=True)
    rows_ref[t, g, ROW_BONUS:ROW_BONUS + 1, :] = jnp.sum(r * kp * const(CONST_RK), axis=0, keepdims=True)


def _rwkv_pre_kernel(r_ref, k_ref, v_ref, l_ref, sinit_ref, mu_ref, w0_ref, a0_ref, w2_ref, a2_ref, g2_ref, c_ref,
                     tiles_ref, rows_ref, go_ref, ps_ref, nat_ref, gam_ref, *, batch, chunk, chains_are_heads):
    rows = r_ref.shape[0]
    gam_ref[...] = jnp.ones_like(gam_ref)

    @pl.when(pl.program_id(0) == 0)
    def _():
        ps_ref[0:batch, :] = sinit_ref[...]

    ps_ref[batch:batch + rows, 0:D] = r_ref[...]
    ps_ref[batch:batch + rows, D:2 * D] = k_ref[...]
    ps_ref[batch:batch + rows, 2 * D:3 * D] = v_ref[...]
    ps_ref[batch:batch + rows, 3 * D:3 * D + LORA_PAD] = l_ref[...]

    def body(c, _):
        r0 = _aligned(c * chunk, chunk)

        def mixed(lo, hi):
            cur = ps_ref[pl.ds(_aligned(r0 + batch, SUBLANES), chunk), lo:hi]
            prev = ps_ref[pl.ds(r0, chunk), lo:hi]
            return cur + (prev - cur) * mu_ref[:, lo:hi]

        nat_ref[NAT_R, pl.ds(r0, chunk), :] = mixed(0, D)
        nat_ref[NAT_K, pl.ds(r0, chunk), :] = mixed(D, 2 * D)
        nat_ref[NAT_V, pl.ds(r0, chunk), :] = mixed(2 * D, 3 * D)
        lm = mixed(3 * D, 3 * D + LORA_PAD)
        xwa = lm[:, 0:LANES]
        xg = lm[:, LANES:3 * LANES]
        lw = jnp.dot(jnp.tanh(xwa).astype(bf16), w2_ref[...], preferred_element_type=f32)
        w_log = -jax.nn.softplus(-(w0_ref[...] + lw)) - 0.5
        nat_ref[NAT_W, pl.ds(r0, chunk), :] = jnp.exp(-jnp.exp(w_log))
        la = jnp.dot(xwa.astype(bf16), a2_ref[...], preferred_element_type=f32)
        nat_ref[NAT_A, pl.ds(r0, chunk), :] = jax.nn.sigmoid(a0_ref[...] + la)
        go_ref[pl.ds(r0, chunk), :] = jnp.dot(jax.nn.sigmoid(xg).astype(bf16), g2_ref[...],
                                              preferred_element_type=f32)
        return 0

    if chains_are_heads:
        lax.fori_loop(0, rows // chunk, body, 0)
        def tstep(t, _):
            r0 = pl.multiple_of(t * batch, batch)
            for hp in range(D // LANES):
                trs = [nat_ref[q, pl.ds(r0, batch), hp * LANES:(hp + 1) * LANES].T for q in range(5)]
                for h2 in range(2):
                    h = 2 * hp + h2
                    _wkv_prep(tiles_ref, rows_ref, gam_ref, t, h, *[tr[h2 * HD:(h2 + 1) * HD] for tr in trs],
                              lambda i, h=h: c_ref[h, i])
            return 0

        lax.fori_loop(0, rows // batch, tstep, 0)
    else:
        half = LANES // 2
        lane = lax.broadcasted_iota(jnp.int32, (HD, LANES), 1)

        def load_pair(q, tp):
            zz = jnp.concatenate(
                [nat_ref[q, pl.ds(_aligned(tp * 2 * SUBLANES + tt * SUBLANES, SUBLANES), SUBLANES),
                         hp * LANES:(hp + 1) * LANES] for tt in range(2) for hp in range(D // LANES)], axis=0)
            tr = zz.T
            top, bot = tr[0:HD], tr[HD:2 * HD]
            d0 = jnp.where(lane < half, top, pltpu.roll(bot, half, axis=1))
            d1 = jnp.where(lane < half, pltpu.roll(top, half, axis=1), bot)
            return d0, d1

        def pair(tp, _):
            pairs = [load_pair(q, tp) for q in range(5)]
            for tt in range(2):
                _wkv_prep(tiles_ref, rows_ref, gam_ref, 2 * tp + tt, 0, *[p[tt] for p in pairs], lambda i: c_ref[0, i])
            return 0

        n_chunks = rows // chunk
        pairs_per_chunk = chunk // (2 * SUBLANES)
        body(0, 0)
        for c in range(1, n_chunks):
            body(c, 0)
            for tp in range((c - 1) * pairs_per_chunk, c * pairs_per_chunk):
                pair(tp, 0)
        for tp in range((n_chunks - 1) * pairs_per_chunk, n_chunks * pairs_per_chunk):
            pair(tp, 0)

    ps_ref[0:batch, :] = ps_ref[rows:rows + batch, :]


def _rwkv_pre(proj, row_off, n_rows, batch, rows_blk, groups, shift_init, consts, mu, w0, a0, w2p, a2p, g2p,
              dst=None):
    ob = row_off // rows_blk
    width = 3 * D + LORA_PAD
    steps_blk = rows_blk // batch
    kern = functools.partial(_rwkv_pre_kernel, batch=batch, chunk=128, chains_are_heads=groups > 1)
    n_in = 12
    dst_spec, dst_arg, out_g = _shared_rows(dst, proj.shape[0])
    return pl.pallas_call(
        kern if dst is None else _with_dst(kern, n_in),
        grid=(n_rows // rows_blk,),
        in_specs=[
            pl.BlockSpec((rows_blk, D), lambda i: (ob + i, COL_R)),
            pl.BlockSpec((rows_blk, D), lambda i: (ob + i, COL_K)),
            pl.BlockSpec((rows_blk, D), lambda i: (ob + i, COL_V)),
            pl.BlockSpec((rows_blk, LORA_PAD), lambda i: (ob + i, COL_LORA)),
            _full((batch, width)), _full((1, width)), _full((1, D)), _full((1, D)),
            _full((LANES, D)), _full((LANES, D)), _full((2 * LANES, D)), _full((groups, 5, HD, LANES)),
        ] + dst_spec,
        out_specs=[pl.BlockSpec((steps_blk, groups, 6, HD, LANES), lambda i: (i, 0, 0, 0, 0)),
                   pl.BlockSpec((steps_blk, groups, SUBLANES, LANES), lambda i: (i, 0, 0, 0)),
                   pl.BlockSpec((rows_blk, D), lambda i: (ob + i, 0))],
        out_shape=[jax.ShapeDtypeStruct((n_rows // batch, groups, 6, HD, LANES), f32),
                   jax.ShapeDtypeStruct((n_rows // batch, groups, SUBLANES, LANES), f32),
                   out_g],
        input_output_aliases={} if dst is None else {n_in: 2},
        scratch_shapes=[pltpu.VMEM((rows_blk + batch, width), f32), pltpu.VMEM((5, rows_blk, D), f32),
                        pltpu.VMEM((groups, HD, LANES), f32)],
        compiler_params=pltpu.CompilerParams(dimension_semantics=("arbitrary",), vmem_limit_bytes=56 * MIB),
        name="rwkv_pre",
    )(proj, proj, proj, proj, shift_init, mu, w0, a0, w2p, a2p, g2p, consts, *dst_arg)


STATE_ROWS = HD * HD
KEY_UNROLL = 32


def _wkv_step(s_ref, tiles_ref, rows_ref, y_ref, t, window):
    def key_row(tile, j):
        return tiles_ref[t, 0, tile, pl.ds(j, 1), :]

    def reduce_keys(j, acc):
        sa, y0 = acc
        s = s_ref[pl.ds(pl.multiple_of(j * HD, HD), HD), :]
        return sa + s * key_row(TILE_ALPHA, j), y0 + s * key_row(TILE_WR, j)

    zero = jnp.zeros((HD, LANES), f32)
    sa, y0 = lax.fori_loop(0, HD, reduce_keys, (zero, zero), unroll=KEY_UNROLL)
    v = tiles_ref[t, 0, TILE_V]

    def update_keys(j, _):
        rows = pl.ds(pl.multiple_of(j * HD, HD), HD)
        s_ref[rows, :] = s_ref[rows, :] + sa * key_row(TILE_BETA, j) + v * key_row(TILE_KP, j)
        return 0

    lax.fori_loop(0, HD, update_keys, 0, unroll=KEY_UNROLL)
    y_ref[t, 0] = y0 + sa * rows_ref[t, 0, ROW_BR:ROW_BR + 1, :] + v * rows_ref[t, 0, ROW_KR:ROW_KR + 1, :]

    @pl.when((t + 1) % window == 0)
    def _():
        def rescale(j, _):
            rows = pl.ds(pl.multiple_of(j * HD, HD), HD)
            s_ref[rows, :] = s_ref[rows, :] * key_row(TILE_GAMMA, j)
            return 0

        lax.fori_loop(0, HD, rescale, 0, unroll=KEY_UNROLL)


def _wkv_scan_kernel(tiles_ref, rows_ref, st_ref, y_ref, sfin_ref, s_ref, *, steps, window):
    tb = pl.program_id(1)

    def value_rows(i):
        return pl.ds(i, HD, stride=HD)

    @pl.when(tb == 0)
    def _():
        def init(c, _):
            tr = st_ref[:, pl.ds(pl.multiple_of(c * LANES, LANES), LANES)].T
            for i2 in range(2):
                s_ref[value_rows(2 * c + i2), :] = tr[i2 * HD:(i2 + 1) * HD]
            return 0

        lax.fori_loop(0, STATE_ROWS // LANES, init, 0, unroll=8)

    def step(t, _):
        _wkv_step(s_ref, tiles_ref, rows_ref, y_ref, t, window)
        return 0

    lax.fori_loop(0, steps, step, 0)

    @pl.when(tb == pl.num_programs(1) - 1)
    def _():
        def fin(c, _):
            pair = jnp.concatenate([s_ref[value_rows(2 * c + i2), :] for i2 in range(2)], axis=0)
            sfin_ref[:, pl.ds(pl.multiple_of(c * LANES, LANES), LANES)] = pair.T
            return 0

        lax.fori_loop(0, STATE_ROWS // LANES, fin, 0, unroll=8)


def _wkv_scan(tiles, rows, state, steps_blk, window):
    n_steps, groups = tiles.shape[0], tiles.shape[1]
    assert steps_blk % window == 0
    kern = functools.partial(_wkv_scan_kernel, steps=steps_blk, window=window)
    sblk = pl.BlockSpec((LANES, STATE_ROWS), lambda g, i: (0, g))
    return pl.pallas_call(
        kern,
        grid=(groups, n_steps // steps_blk),
        in_specs=[pl.BlockSpec((steps_blk, 1, 6, HD, LANES), lambda g, i: (i, g, 0, 0, 0)),
                  pl.BlockSpec((steps_blk, 1, SUBLANES, LANES), lambda g, i: (i, g, 0, 0)),
                  sblk],
        out_specs=[pl.BlockSpec((steps_blk, 1, HD, LANES), lambda g, i: (i, g, 0, 0)), sblk],
        out_shape=[jax.ShapeDtypeStruct((n_steps, groups, HD, LANES), f32),
                   jax.ShapeDtypeStruct((LANES, groups * STATE_ROWS), f32)],
        scratch_shapes=[pltpu.VMEM((STATE_ROWS, LANES), f32)],
        compiler_params=pltpu.CompilerParams(dimension_semantics=("arbitrary", "arbitrary"),
                                             vmem_limit_bytes=48 * MIB),
        name="wkv_scan",
    )(tiles, rows, state)


def _wkv_norm(y, v, bonus, lnw, lnb):
    mean = jnp.sum(y, axis=0, keepdims=True) * (1.0 / HD)
    d = y - mean
    var = jnp.sum(d * d, axis=0, keepdims=True) * (1.0 / HD)
    return d * lax.rsqrt(var + GN_EPS) * lnw + lnb + bonus * v


def _wkv_post_kernel(y_ref, v_ref, rows_ref, c_ref, o_ref, *, batch, chains_are_heads):
    steps = y_ref.shape[0]

    def normed(t, g):
        return _wkv_norm(y_ref[t, g], v_ref[t, g, 0], rows_ref[t, g, ROW_BONUS:ROW_BONUS + 1, :],
                         c_ref[g, CONST_LNW], c_ref[g, CONST_LNB])

    if chains_are_heads:
        def tstep(t, _):
            r0 = pl.multiple_of(t * batch, batch)
            for hp in range(D // LANES):
                two = jnp.concatenate([normed(t, 2 * hp), normed(t, 2 * hp + 1)], axis=0)
                o_ref[pl.ds(r0, batch), hp * LANES:(hp + 1) * LANES] = two.T
            return 0

        lax.fori_loop(0, steps, tstep, 0)
    else:
        half = LANES // 2
        lane = lax.broadcasted_iota(jnp.int32, (HD, LANES), 1)

        def pair(tp, _):
            o0 = normed(2 * tp, 0)
            o1 = normed(2 * tp + 1, 0)
            top = jnp.where(lane < half, o0, pltpu.roll(o1, half, axis=1))
            bot = jnp.where(lane < half, pltpu.roll(o0, half, axis=1), o1)
            zz = jnp.concatenate([top, bot], axis=0).T
            for tt in range(2):
                row = pl.multiple_of(tp * 2 * SUBLANES + tt * SUBLANES, SUBLANES)
                for hp in range(D // LANES):
                    src = tt * HD + hp * SUBLANES
                    o_ref[pl.ds(row, SUBLANES), hp * LANES:(hp + 1) * LANES] = zz[src:src + SUBLANES, :]
            return 0

        lax.fori_loop(0, steps // 2, pair, 0, unroll=4)


def _outproj_kernel(ml_ref, g_ref, gr_ref, x_ref, yp_ref, vp_ref, rp_ref, cp_ref, ysm_ref, vs_ref, rs_ref, cs_ref,
                    wo_ref, n2_ref, rh_ref, rl_ref, rb_ref,
                    x1_ref, t_ref, selt_ref, seln_ref, op_ref, *, prompt_tiles, prompt_batch, sample_batch):
    tm = x_ref.shape[0]
    i = pl.program_id(0)

    @pl.when(i < prompt_tiles)
    def _():
        _wkv_post_kernel(yp_ref, vp_ref, rp_ref, cp_ref, op_ref, batch=prompt_batch, chains_are_heads=False)

    @pl.when(i >= prompt_tiles)
    def _():
        _wkv_post_kernel(ysm_ref, vs_ref, rs_ref, cs_ref, op_ref, batch=sample_batch, chains_are_heads=True)

    merged = ml_ref[...] + jax.nn.sigmoid(gr_ref[...]) * (op_ref[...] * g_ref[...])
    x1 = x_ref[...] + jnp.dot(merged.astype(bf16), wo_ref[...], preferred_element_type=f32)
    x1_ref[...] = x1
    t = x1 * lax.rsqrt(jnp.mean(x1 * x1, axis=-1, keepdims=True) + NORM_EPS) * n2_ref[...]
    th = t.astype(bf16)
    t_ref[...] = th
    tl = (t - th.astype(f32)).astype(bf16)
    nt_dims = (((1,), (1,)), ((), ()))
    lg = (lax.dot_general(rh_ref[...], th, nt_dims, preferred_element_type=f32)
          + lax.dot_general(rh_ref[...], tl, nt_dims, preferred_element_type=f32)
          + lax.dot_general(rl_ref[...], th, nt_dims, preferred_element_type=f32)) + rb_ref[...]
    row = lax.broadcasted_iota(jnp.int32, (EPG, tm), 0).astype(f32)
    neg = jnp.float32(-jnp.inf)
    glog = jnp.where(row < N_GROUPS, lg[0:EPG], neg)
    ge = jnp.exp(glog - jnp.max(glog, axis=0, keepdims=True))
    pg = ge / jnp.sum(ge, axis=0, keepdims=True)
    p_top = jnp.max(pg, axis=0, keepdims=True)
    g_idx = jnp.min(jnp.where(pg == p_top, row, EPG), axis=0, keepdims=True)
    le = jnp.zeros((EPG, tm), f32)
    for g in range(N_GROUPS):
        le = jnp.where(g_idx == g, lg[EPG * (g + 1):EPG * (g + 2)], le)
    qe = jnp.exp(le - jnp.max(le, axis=0, keepdims=True))
    q = qe / jnp.sum(qe, axis=0, keepdims=True)
    q1 = jnp.max(q, axis=0, keepdims=True)
    i1 = jnp.min(jnp.where(q == q1, row, EPG), axis=0, keepdims=True)
    qm = jnp.where(row == i1, -1.0, q)
    q2 = jnp.max(qm, axis=0, keepdims=True)
    i2 = jnp.min(jnp.where(qm == q2, row, EPG), axis=0, keepdims=True)
    qs = q1 + q2
    sel = jnp.concatenate([g_idx * EPG + i1, g_idx * EPG + i2, q1 / qs * p_top, q2 / qs * p_top,
                           jnp.zeros((SUBLANES - 4, tm), f32)], axis=0)
    selt_ref[...] = sel
    sel_pad = jnp.concatenate([sel, jnp.zeros((LANES - SUBLANES, tm), f32)], axis=0)
    for c in range(tm // LANES):
        seln_ref[c * LANES:(c + 1) * LANES, :] = sel_pad[:, c * LANES:(c + 1) * LANES].T


def _outproj(ml, g, proj, x, scan_p, scan_s, wo, n2, rh, rl, rb, prompt_batch, sample_batch, tm=512):
    n = x.shape[0]
    y_p, tiles_p, rows_p, consts_p = scan_p
    y_s, tiles_s, rows_s, consts_s = scan_s
    n_p = y_p.shape[0] * prompt_batch
    sp, ss = tm // prompt_batch, tm // sample_batch
    prompt_tiles = n_p // tm
    sample_tiles = (n - n_p) // tm
    assert n_p % tm == 0 and sp % 2 == 0 and y_s.shape[0] == sample_tiles * ss
    pidx = lambda i: jnp.minimum(i, prompt_tiles - 1)
    sidx = lambda i: jnp.clip(i - prompt_tiles, 0, sample_tiles - 1)
    groups = y_s.shape[1]
    blk = pl.BlockSpec((tm, D), lambda i: (i, 0))
    kern = functools.partial(_outproj_kernel, prompt_tiles=prompt_tiles, prompt_batch=prompt_batch,
                             sample_batch=sample_batch)
    return pl.pallas_call(
        kern,
        grid=(n // tm,),
        in_specs=[blk, blk, pl.BlockSpec((tm, D), lambda i: (i, COL_GR)), blk,
                  pl.BlockSpec((sp, 1, HD, LANES), lambda i: (pidx(i), 0, 0, 0)),
                  pl.BlockSpec((sp, 1, 1, HD, LANES), lambda i: (pidx(i), 0, TILE_V, 0, 0)),
                  pl.BlockSpec((sp, 1, SUBLANES, LANES), lambda i: (pidx(i), 0, 0, 0)),
                  _full((1, 5, HD, LANES)),
                  pl.BlockSpec((ss, groups, HD, LANES), lambda i: (sidx(i), 0, 0, 0)),
                  pl.BlockSpec((ss, groups, 1, HD, LANES), lambda i: (sidx(i), 0, TILE_V, 0, 0)),
                  pl.BlockSpec((ss, groups, SUBLANES, LANES), lambda i: (sidx(i), 0, 0, 0)),
                  _full((groups, 5, HD, LANES)),
                  _full((D, D)), _full((1, D)), _full((ROUTER_ROWS, D)), _full((ROUTER_ROWS, D)),
                  _full((ROUTER_ROWS, 1))],
        out_specs=[blk, blk, pl.BlockSpec((SUBLANES, tm), lambda i: (0, i)),
                   pl.BlockSpec((tm, LANES), lambda i: (i, 0))],
        out_shape=[jax.ShapeDtypeStruct((n, D), f32), jax.ShapeDtypeStruct((n, D), bf16),
                   jax.ShapeDtypeStruct((SUBLANES, n), f32), jax.ShapeDtypeStruct((n, LANES), f32)],
        scratch_shapes=[pltpu.VMEM((tm, D), f32)],
        compiler_params=pltpu.CompilerParams(dimension_semantics=("arbitrary",), vmem_limit_bytes=56 * MIB),
        name="outproj_router",
    )(ml, g, proj, x, y_p, tiles_p, rows_p, consts_p, y_s, tiles_s, rows_s, consts_s, wo, n2, rh, rl, rb)


MOE_CHUNK = 32
MOE_ROWS = 128
MOE_GATHER = 512
MOE_EXPERTS_PER_STEP = 4
MOE_OVERRUN = MOE_ROWS
MOE_CAP = -(-(2 * TOKEN_TILE + N_EXPERTS * (MOE_CHUNK - 1) + MOE_OVERRUN) // MOE_GATHER) * MOE_GATHER
SEL_E1, SEL_E2, SEL_C1, SEL_C2 = range(4)


def _route_meta_kernel(seln_ref, o_ref):
    tm = seln_ref.shape[0]
    lane = lax.broadcasted_iota(jnp.int32, (tm, LANES), 1).astype(f32)
    sel = seln_ref[...]
    hit = (lane == sel[:, SEL_E1:SEL_E1 + 1]) | (lane == sel[:, SEL_E2:SEL_E2 + 1])
    cnt = jnp.sum(jnp.where(hit, 1.0, 0.0), axis=0, keepdims=True)
    nchunk = jnp.floor((cnt + (MOE_CHUNK - 1)) * (1.0 / MOE_CHUNK))
    upper = (lax.broadcasted_iota(jnp.int32, (LANES, LANES), 0)
             < lax.broadcasted_iota(jnp.int32, (LANES, LANES), 1))
    offs = jnp.dot(jnp.broadcast_to(nchunk, (SUBLANES, LANES)).astype(bf16), jnp.where(upper, 1.0, 0.0).astype(bf16),
                   preferred_element_type=f32)
    row = lax.broadcasted_iota(jnp.int32, (SUBLANES, LANES), 0)
    o_ref[0] = jnp.where(row == 0, nchunk, jnp.where(row == 1, offs, 0.0)).astype(jnp.int32)


def _route_meta(seln, tm):
    tiles = seln.shape[0] // tm
    return pl.pallas_call(
        _route_meta_kernel,
        grid=(tiles,),
        in_specs=[pl.BlockSpec((tm, LANES), lambda i: (i, 0))],
        out_specs=pl.BlockSpec((1, SUBLANES, LANES), lambda i: (i, 0, 0)),
        out_shape=jax.ShapeDtypeStruct((tiles, SUBLANES, LANES), jnp.int32),
        compiler_params=pltpu.CompilerParams(dimension_semantics=("arbitrary",)),
        name="route_meta",
    )(seln)


def _moe_experts_kernel(nch_ref, off_ref, t_ref, selt_ref, tri_ref, wgu_ref, wd_ref, ys_ref, pos_ref,
                        xs_ref, cw_ref):
    i = pl.program_id(0)
    e = pl.program_id(1)
    tm = t_ref.shape[0]

    @pl.when(e == 0)
    def _():
        ys_ref[...] = jnp.zeros_like(ys_ref)
        e1 = selt_ref[SEL_E1:SEL_E1 + 1, :]
        e2 = selt_ref[SEL_E2:SEL_E2 + 1, :]
        erow = lax.broadcasted_iota(jnp.int32, (N_EXPERTS, tm), 0).astype(f32)
        oh1 = jnp.where(erow == e1, 1.0, 0.0)
        oh2 = jnp.where(erow == e2, 1.0, 0.0)
        before1 = jnp.dot(oh1.astype(bf16), tri_ref[...], preferred_element_type=f32)
        before2 = jnp.dot(oh2.astype(bf16), tri_ref[...], preferred_element_type=f32)
        cnt1 = jnp.sum(oh1, axis=1, keepdims=True)
        cnt2 = jnp.sum(oh2, axis=1, keepdims=True)
        nchunk = jnp.floor((cnt1 + cnt2 + (MOE_CHUNK - 1)) * (1.0 / MOE_CHUNK))
        lower = (lax.broadcasted_iota(jnp.int32, (N_EXPERTS, LANES), 1)
                 < lax.broadcasted_iota(jnp.int32, (N_EXPERTS, LANES), 0))
        nchunk_rows = jnp.concatenate([jnp.broadcast_to(nchunk, (N_EXPERTS, LANES)),
                                       jnp.zeros((LANES - N_EXPERTS, LANES), f32)], axis=0)
        start = jnp.dot(jnp.where(lower, 1.0, 0.0).astype(bf16), nchunk_rows.astype(bf16),
                        preferred_element_type=f32)[:, 0:1] * MOE_CHUNK
        pos1 = jnp.sum(oh1 * (start + before1), axis=0, keepdims=True)
        pos2 = jnp.sum(oh2 * (start + cnt1 + before2), axis=0, keepdims=True)
        pos_ref[...] = jnp.concatenate([pos1, pos2, jnp.zeros((SUBLANES - 2, tm), f32)], axis=0)

    per_step = wgu_ref.shape[0]

    def gathered_chunks(step):
        last = i * N_EXPERTS + jnp.maximum(step * per_step + per_step - 1, 0)
        end_row = (off_ref[last] + nch_ref[last]) * MOE_CHUNK + MOE_OVERRUN
        chunks = jnp.minimum((end_row + MOE_GATHER - 1) // MOE_GATHER, MOE_CAP // MOE_GATHER)
        return jnp.where(step < 0, 0, chunks)

    def gather_chunk(k, _):
        r0 = pl.multiple_of(k * MOE_GATHER, MOE_GATHER)
        ridx = (lax.broadcasted_iota(jnp.int32, (MOE_GATHER, tm), 0) + r0).astype(f32)
        p1 = ridx == pos_ref[0:1, :]
        p2 = ridx == pos_ref[1:2, :]
        onehot = jnp.where(p1 | p2, 1.0, 0.0).astype(bf16)
        xs_ref[pl.ds(r0, MOE_GATHER), :] = jnp.dot(onehot, t_ref[...], preferred_element_type=f32).astype(bf16)
        w = jnp.sum(jnp.where(p1, selt_ref[SEL_C1:SEL_C1 + 1, :], 0.0)
                    + jnp.where(p2, selt_ref[SEL_C2:SEL_C2 + 1, :], 0.0), axis=1, keepdims=True)
        cw_ref[pl.ds(r0, MOE_GATHER), :] = jnp.broadcast_to(w, (MOE_GATHER, LANES))
        return 0

    lax.fori_loop(gathered_chunks(e - 1), gathered_chunks(e), gather_chunk, 0)

    def expert_rows(u, rows):
        gu = jnp.dot(xs_ref[rows, :], wgu_ref[u], preferred_element_type=f32)
        w = cw_ref[rows, :]
        h = jax.nn.silu(gu[:, :D_EXPERT]) * gu[:, D_EXPERT:] * jnp.concatenate([w, w], axis=1)
        return jnp.dot(h.astype(bf16), wd_ref[u], preferred_element_type=f32).astype(bf16)

    per_block = MOE_ROWS // MOE_CHUNK
    experts = [i * N_EXPERTS + e * per_step + u for u in range(per_step)]
    bases = [off_ref[x] * MOE_CHUNK for x in experts]

    for u in range(per_step):
        rows = pl.ds(pl.multiple_of(bases[u], MOE_CHUNK), MOE_ROWS)
        ys_ref[0, rows, :] = expert_rows(u, rows)

    for u in range(per_step):
        seg_end = bases[u] + nch_ref[experts[u]] * MOE_CHUNK

        def more(c, _, u=u, seg_end=seg_end):
            start = bases[u] + (c + 1) * MOE_ROWS
            rows = pl.ds(pl.multiple_of(start, MOE_CHUNK), MOE_ROWS)
            ridx = lax.broadcasted_iota(jnp.int32, (MOE_ROWS, D), 0) + start
            ys_ref[0, rows, :] = jnp.where(ridx < seg_end, expert_rows(u, rows), ys_ref[0, rows, :])
            return 0

        lax.fori_loop(0, jnp.maximum(nch_ref[experts[u]] - 1, 0) // per_block, more, 0)


def _moe_experts(nch, off, t, selt, tri, wgu, wd, tm):
    n = t.shape[0]
    tiles = n // tm
    grid_spec = pltpu.PrefetchScalarGridSpec(
        num_scalar_prefetch=2,
        grid=(tiles, N_EXPERTS // MOE_EXPERTS_PER_STEP),
        in_specs=[
            pl.BlockSpec((tm, D), lambda i, e, nch, off: (i, 0)),
            pl.BlockSpec((SUBLANES, tm), lambda i, e, nch, off: (0, i)),
            pl.BlockSpec((tm, tm), lambda i, e, nch, off: (0, 0)),
            pl.BlockSpec((MOE_EXPERTS_PER_STEP, D, 2 * D_EXPERT), lambda i, e, nch, off: (e, 0, 0)),
            pl.BlockSpec((MOE_EXPERTS_PER_STEP, D_EXPERT, D), lambda i, e, nch, off: (e, 0, 0)),
        ],
        out_specs=[pl.BlockSpec((1, MOE_CAP, D), lambda i, e, nch, off: (i, 0, 0)),
                   pl.BlockSpec((SUBLANES, tm), lambda i, e, nch, off: (0, i))],
        scratch_shapes=[pltpu.VMEM((MOE_CAP, D), bf16), pltpu.VMEM((MOE_CAP, LANES), f32)],
    )
    return pl.pallas_call(
        _moe_experts_kernel,
        grid_spec=grid_spec,
        out_shape=[jax.ShapeDtypeStruct((tiles, MOE_CAP, D), bf16), jax.ShapeDtypeStruct((SUBLANES, n), f32)],
        compiler_params=pltpu.CompilerParams(dimension_semantics=("arbitrary", "arbitrary"),
                                             vmem_limit_bytes=56 * MIB),
        name="moe_experts",
    )(nch, off, t, selt, tri, wgu, wd)


def _moe_combine_kernel(nch_ref, off_ref, ysort_ref, pos_ref, x1_ref, fn_ref, yp_ref, ys_ref, acc_ref, yf_ref,
                        *, prompt_tiles, parts):
    i = pl.program_id(0)
    part = pl.program_id(1)
    rows = x1_ref.shape[0]
    bp, tsteps = yp_ref.shape[0], yp_ref.shape[1]
    bs = ys_ref.shape[0]
    ssteps = ys_ref.shape[1] // parts
    used_rows = (off_ref[i * N_EXPERTS + N_EXPERTS - 1] + nch_ref[i * N_EXPERTS + N_EXPERTS - 1]) * MOE_CHUNK

    pos_t = jnp.concatenate([pos_ref[...], jnp.zeros((LANES - SUBLANES, rows), f32)], axis=0)
    pos_n = jnp.concatenate([pos_t[:, c * LANES:(c + 1) * LANES].T for c in range(rows // LANES)], axis=0)
    pos1 = pos_n[:, 0:1]
    pos2 = pos_n[:, 1:2]
    acc_ref[...] = x1_ref[...]
    for k in range(MOE_CAP // MOE_GATHER):
        @pl.when(k * MOE_GATHER < used_rows)
        def _(k=k):
            cidx = (lax.broadcasted_iota(jnp.int32, (rows, MOE_GATHER), 1) + k * MOE_GATHER).astype(f32)
            onehot = jnp.where((cidx == pos1) | (cidx == pos2), 1.0, 0.0).astype(bf16)
            acc_ref[...] += jnp.dot(onehot, ysort_ref[0, k * MOE_GATHER:(k + 1) * MOE_GATHER, :],
                                    preferred_element_type=f32)

    xo = acc_ref[...]
    y = xo * lax.rsqrt(jnp.mean(xo * xo, axis=-1, keepdims=True) + NORM_EPS) * fn_ref[...]
    for cb in range(D // LANES):
        yf_ref[cb] = y[:, cb * LANES:(cb + 1) * LANES]

    @pl.when(i < prompt_tiles)
    def _():
        for b in range(bp):
            yp_ref[b] = jnp.concatenate([yf_ref[cb, pl.ds(b, tsteps, stride=bp), :] for cb in range(D // LANES)],
                                        axis=1)

    for p in range(parts):
        @pl.when((i >= prompt_tiles) & (part == p))
        def _(p=p):
            for t in range(ssteps):
                ys_ref[:, p * ssteps + t, :] = jnp.concatenate(
                    [yf_ref[cb, t * bs:(t + 1) * bs, :] for cb in range(D // LANES)], axis=1)


def _moe_combine(nch, off, ysort, pos, x1, fn, prompt_shape, sample_shape, tm, parts=2):
    n = x1.shape[0]
    bp, tp, _ = prompt_shape
    bs, ts, _ = sample_shape
    rows = tm // parts
    assert rows % bp == 0 and tp % (rows // bp) == 0 and bs * ts == tm and ts % parts == 0
    prompt_tiles = bp * tp // tm
    last_prompt_blk = prompt_tiles * parts - 1
    kern = functools.partial(_moe_combine_kernel, prompt_tiles=prompt_tiles, parts=parts)
    grid_spec = pltpu.PrefetchScalarGridSpec(
        num_scalar_prefetch=2,
        grid=(n // tm, parts),
        in_specs=[
            pl.BlockSpec((1, MOE_CAP, D), lambda i, p, nch, off: (i, 0, 0)),
            pl.BlockSpec((SUBLANES, rows), lambda i, p, nch, off: (0, i * parts + p)),
            pl.BlockSpec((rows, D), lambda i, p, nch, off: (i * parts + p, 0)),
            pl.BlockSpec((1, D), lambda i, p, nch, off: (0, 0)),
        ],
        out_specs=[pl.BlockSpec((bp, rows // bp, D),
                                lambda i, p, nch, off: (0, jnp.minimum(i * parts + p, last_prompt_blk), 0)),
                   pl.BlockSpec((bs, ts, D), lambda i, p, nch, off: (0, 0, 0))],
        scratch_shapes=[pltpu.VMEM((rows, D), f32), pltpu.VMEM((D // LANES, rows, LANES), f32)],
    )
    return pl.pallas_call(
        kern,
        grid_spec=grid_spec,
        out_shape=[jax.ShapeDtypeStruct(prompt_shape, f32), jax.ShapeDtypeStruct(sample_shape, f32)],
        compiler_params=pltpu.CompilerParams(dimension_semantics=("arbitrary", "arbitrary"),
                                             vmem_limit_bytes=56 * MIB),
        name="moe_combine",
    )(nch, off, ysort, pos, x1, fn)


def _chain_tiles_prompt(vec, batch):
    t = vec.reshape(HEADS // 2, 2, HD)
    t = jnp.transpose(t, (2, 1, 0))
    return jnp.broadcast_to(t[..., None], (HD, 2, HEADS // 2, batch)).reshape(HD, LANES)


def _chain_tiles_sample(vec):
    return jnp.broadcast_to(vec.reshape(HEADS, HD)[..., None], (HEADS, HD, LANES))


def kernel(x_prompt, x_sample, state_conv, state_lru, state_shift, state_wkv, norm1, w_in, conv_w, conv_b, lru_wx, lru_bx, lru_wa, lru_ba, lru_a_param, rwkv_mu, rwkv_w0, rwkv_w2, rwkv_a0, rwkv_a2, rwkv_g2, rwkv_k_k, rwkv_k_a, rwkv_r_k, rwkv_ln_w, rwkv_ln_b, w_out, norm2, router_group, router_group_b, router_expert, router_expert_b, exp_gate, exp_up, exp_down, final_norm):
    bp, tp, _ = x_prompt.shape
    bs, ts, _ = x_sample.shape
    assert norm1.shape[0] == 1 and bp * HEADS == LANES and bs == LANES and tp % 64 == 0 and ts % 2 == 0
    n_p, n_s = bp * tp, bs * ts
    c_rw = 2 * D
    c_lora = c_rw + 3 * D
    c_gl = c_lora + LORA

    w = w_in[0]
    w_all = jnp.concatenate([w[:, :c_lora], w[:, c_gl:], w[:, c_lora:c_gl],
                             jnp.zeros((D, LORA_PAD - LORA), f32)], axis=1).astype(bf16)
    mu = rwkv_mu[0]
    mu_all = jnp.concatenate([mu[:3 * D], mu[3 * D:], jnp.zeros((LORA_PAD - LORA,), f32)])[None]
    w2p = jnp.concatenate([rwkv_w2[0], jnp.zeros((LANES - LORA_W, D), f32)]).astype(bf16)
    a2p = jnp.concatenate([jnp.zeros((LORA_W, D), f32), rwkv_a2[0]]).astype(bf16)
    g2p = jnp.concatenate([rwkv_g2[0], jnp.zeros((2 * LANES - LORA_G, D), f32)]).astype(bf16)
    wxa = jnp.concatenate([lru_wx[0], lru_wa[0]], axis=-1).astype(bf16)
    row = lambda v: v.reshape(1, -1)
    rk = rwkv_r_k[0].reshape(D)
    cvecs = (rwkv_k_k[0], rwkv_k_a[0], rk, rwkv_ln_w[0], rwkv_ln_b[0])
    consts_p = jnp.stack([_chain_tiles_prompt(v, bp) for v in cvecs])[None]
    consts_s = jnp.stack([_chain_tiles_sample(v) for v in cvecs], axis=1)
    rw = jnp.zeros((ROUTER_ROWS, D), f32)
    rw = rw.at[0:N_GROUPS].set(router_group[0].T).at[EPG:EPG + N_EXPERTS].set(router_expert[0].T)
    rh = rw.astype(bf16)
    rl = (rw - rh.astype(f32)).astype(bf16)
    rb = jnp.zeros((ROUTER_ROWS, 1), f32)
    rb = rb.at[0:N_GROUPS, 0].set(router_group_b[0]).at[EPG:EPG + N_EXPERTS, 0].set(router_expert_b[0])
    wgu = jnp.concatenate([exp_gate[0], exp_up[0]], axis=-1).astype(bf16)
    wd = exp_down[0].astype(bf16)
    wo = w_out[0].astype(bf16)

    proj, x = _inproj(x_prompt, x_sample, row(norm1[0]), w_all, tm=TOKEN_TILE)

    lru_args = (conv_w[0], row(conv_b[0]), wxa, row(lru_bx[0]), row(lru_ba[0]), row(lru_a_param[0]))
    conv_init_s = jnp.transpose(state_conv[0], (1, 0, 2)).reshape(3 * bs, D)
    ml, hl_p = _lru(proj, 0, n_p, bp, 512, True, jnp.zeros((3 * bp, D), f32), jnp.zeros((bp, D), f32), *lru_args)
    ml, hl_s = _lru(proj, n_p, n_s, bs, 512, False, conv_init_s, state_lru[0], *lru_args, dst=ml)

    sh = state_shift[0]
    shift_init_s = jnp.concatenate([sh, jnp.zeros((bs, LORA_PAD - LORA), f32)], axis=1)
    pre_args = (mu_all, row(rwkv_w0[0]), row(rwkv_a0[0]), w2p, a2p, g2p)
    pre_rows = 256
    tiles_p, rows_p, g = _rwkv_pre(proj, 0, n_p, bp, pre_rows, 1, jnp.zeros((bp, 3 * D + LORA_PAD), f32),
                                   consts_p, *pre_args)
    tiles_s, rows_s, g = _rwkv_pre(proj, n_p, n_s, bs, pre_rows, HEADS, shift_init_s, consts_s, *pre_args, dst=g)

    y_p, sfin_p = _wkv_scan(tiles_p, rows_p, jnp.zeros((LANES, STATE_ROWS), f32), 64, pre_rows // bp)
    y_s, sfin_s = _wkv_scan(tiles_s, rows_s, state_wkv[0].reshape(bs, HEADS * STATE_ROWS), ts, pre_rows // bs)

    x1, t, selt, seln = _outproj(ml, g, proj, x, (y_p, tiles_p, rows_p, consts_p), (y_s, tiles_s, rows_s, consts_s),
                                 wo, row(norm2[0]), rh, rl, rb, bp, bs)
    meta = _route_meta(seln, TOKEN_TILE)
    nch = meta[:, 0, :N_EXPERTS].reshape(-1)
    off = meta[:, 1, :N_EXPERTS].reshape(-1)
    tri = jnp.triu(jnp.ones((TOKEN_TILE, TOKEN_TILE), bf16), k=1)
    ysort, pos = _moe_experts(nch, off, t, selt, tri, wgu, wd, TOKEN_TILE)
    y_prompt, y_sample = _moe_combine(nch, off, ysort, pos, x1, row(final_norm), x_prompt.shape, x_sample.shape,
                                      TOKEN_TILE)

    def last_rows(lo, hi, n_end, batch, steps):
        return proj[n_end - steps * batch:n_end, lo:hi].reshape(steps, batch, hi - lo)

    conv_p = jnp.transpose(last_rows(0, D, n_p, bp, 3), (1, 0, 2))[None]
    conv_s = jnp.transpose(last_rows(0, D, n_p + n_s, bs, 3), (1, 0, 2))[None]

    def shift_rows(n_end, batch):
        return jnp.concatenate([last_rows(2 * D, 5 * D, n_end, batch, 1)[0],
                                last_rows(7 * D, 7 * D + LORA, n_end, batch, 1)[0]], axis=1)[None]

    wkv_p = jnp.transpose(sfin_p.reshape(2, HEADS // 2, bp, HD, HD), (2, 1, 0, 3, 4)).reshape(1, bp, HEADS, HD, HD)
    wkv_s = sfin_s.reshape(1, bs, HEADS, HD, HD)
    return (y_prompt, y_sample, conv_p, hl_p[None], shift_rows(n_p, bp), wkv_p,
            conv_s, hl_s[None], shift_rows(n_p + n_s, bs), wkv_s)
```

```python
import functools

import jax
import jax.numpy as jnp
from jax import lax
from jax.experimental import pallas as pl
from jax.experimental.pallas import tpu as pltpu

f32 = jnp.float32
bf16 = jnp.bfloat16

D = 1024
HEADS = 16
HD = 64
LANES = 128
SUBLANES = 8
LORA_W = 64
LORA_A = 64
LORA_G = 160
LORA = LORA_W + LORA_A + LORA_G
LORA_PAD = 512
N_GROUPS = 4
EPG = 8
N_EXPERTS = N_GROUPS * EPG
D_EXPERT = 256
LRU_C = 8.0
GN_EPS = 64e-5
NORM_EPS = 1e-6
PROJ_COLS = 7 * D + LORA_PAD
COL_LRU_X, COL_LRU_Y, COL_R, COL_K, COL_V, COL_GL, COL_GR = range(7)
COL_LORA = 7 * D // LORA_PAD
ROUTER_ROWS = 48
MIB = 1024 * 1024
TOKEN_TILE = 1024


def _aligned(start, multiple):
    return start if isinstance(start, int) else pl.multiple_of(start, multiple)


def _full(shape, grid_rank=1):
    zeros = tuple(0 for _ in shape)
    if grid_rank == 1:
        return pl.BlockSpec(shape, lambda i: zeros)
    return pl.BlockSpec(shape, lambda i, j: zeros)


def _inproj_kernel(xp_ref, xs_ref, g_ref, w_ref, o_ref, xo_ref, xf_ref, xn_ref, *, prompt_tiles):
    i = pl.program_id(0)
    first_col = pl.program_id(1) == 0
    bp, tsteps = xp_ref.shape[0], xp_ref.shape[1]
    bs, ssteps = xs_ref.shape[0], xs_ref.shape[1]

    @pl.when(first_col & (i < prompt_tiles))
    def _():
        for b in range(bp):
            v = xp_ref[b]
            for cb in range(D // LANES):
                xf_ref[cb, pl.ds(b, tsteps, stride=bp), :] = v[:, cb * LANES:(cb + 1) * LANES]

    @pl.when(first_col & (i >= prompt_tiles))
    def _():
        for t in range(ssteps):
            v = xs_ref[:, t, :]
            for cb in range(D // LANES):
                xf_ref[cb, t * bs:(t + 1) * bs, :] = v[:, cb * LANES:(cb + 1) * LANES]

    @pl.when(first_col)
    def _():
        x = jnp.concatenate([xf_ref[cb] for cb in range(D // LANES)], axis=1)
        xo_ref[...] = x
        ms = jnp.mean(x * x, axis=-1, keepdims=True)
        xn_ref[...] = (x * lax.rsqrt(ms + NORM_EPS) * g_ref[...]).astype(bf16)

    o_ref[...] = jnp.dot(xn_ref[...], w_ref[...], preferred_element_type=f32)


def _inproj(x_prompt, x_sample, gain, w, tm, tn=1536):
    bp, tp, _ = x_prompt.shape
    bs, ts, _ = x_sample.shape
    assert tm % bp == 0 and tp % (tm // bp) == 0 and bs * ts == tm
    prompt_tiles = bp * tp // tm
    n = bp * tp + bs * ts
    kern = functools.partial(_inproj_kernel, prompt_tiles=prompt_tiles)
    return pl.pallas_call(
        kern,
        grid=(n // tm, PROJ_COLS // tn),
        in_specs=[
            pl.BlockSpec((bp, tm // bp, D), lambda i, j: (0, jnp.minimum(i, prompt_tiles - 1), 0)),
            pl.BlockSpec((bs, ts, D), lambda i, j: (0, 0, 0)),
            pl.BlockSpec((1, D), lambda i, j: (0, 0)),
            pl.BlockSpec((D, tn), lambda i, j: (0, j)),
        ],
        out_specs=[pl.BlockSpec((tm, tn), lambda i, j: (i, j)), pl.BlockSpec((tm, D), lambda i, j: (i, 0))],
        out_shape=[jax.ShapeDtypeStruct((n, PROJ_COLS), f32), jax.ShapeDtypeStruct((n, D), f32)],
        scratch_shapes=[pltpu.VMEM((D // LANES, tm, LANES), f32), pltpu.VMEM((tm, D), bf16)],
        compiler_params=pltpu.CompilerParams(dimension_semantics=("arbitrary", "arbitrary"),
                                             vmem_limit_bytes=56 * MIB),
        name="inproj",
    )(x_prompt, x_sample, gain, w)


def _lru_kernel(x_ref, y_ref, gl_ref, cinit_ref, h0_ref, cw_ref, cb_ref, wxa_ref, bx_ref, ba_ref, ap_ref,
                o_ref, hl_ref, xs_ref, a_ref, b_ref, h_ref, *, batch, reset_first, chunk):
    rows = x_ref.shape[0]
    nt = rows // batch
    hist = 3 * batch
    pid = pl.program_id(0)

    @pl.when(pid == 0)
    def _():
        xs_ref[0:hist, :] = cinit_ref[...]
        h_ref[...] = h0_ref[...]

    xs_ref[hist:hist + rows, :] = x_ref[...]
    logsig = -jax.nn.softplus(-ap_ref[...])

    def gates(c, _):
        r0 = pl.multiple_of(c * chunk, chunk)
        xc = (cb_ref[...]
              + cw_ref[3:4, :] * xs_ref[pl.ds(pl.multiple_of(r0 + hist, SUBLANES), chunk), :]
              + cw_ref[2:3, :] * xs_ref[pl.ds(pl.multiple_of(r0 + 2 * batch, SUBLANES), chunk), :]
              + cw_ref[1:2, :] * xs_ref[pl.ds(pl.multiple_of(r0 + batch, SUBLANES), chunk), :]
              + cw_ref[0:1, :] * xs_ref[pl.ds(r0, chunk), :])
        if reset_first:
            grow = lax.broadcasted_iota(jnp.int32, (chunk, LANES), 0) + (r0 + pid * rows)
            first = grow < batch
        for n in range(D // LANES):
            sl = slice(n * LANES, (n + 1) * LANES)
            xn = xc[:, sl]
            g2 = jnp.dot(xn.astype(bf16), wxa_ref[n], preferred_element_type=f32)
            gate_x = jax.nn.sigmoid(g2[:, :LANES] + bx_ref[:, sl])
            gate_a = jax.nn.sigmoid(g2[:, LANES:] + ba_ref[:, sl])
            log_a = LRU_C * gate_a * logsig[:, sl]
            a = jnp.exp(log_a)
            mult = jnp.sqrt(-jnp.tanh(log_a) * (a * a + 1.0))
            if reset_first:
                mult = jnp.where(first, 1.0, mult)
            a_ref[pl.ds(r0, chunk), sl] = a
            b_ref[pl.ds(r0, chunk), sl] = xn * gate_x * mult
        return 0

    lax.fori_loop(0, rows // chunk, gates, 0)

    def scan(t, h):
        r0 = pl.multiple_of(t * batch, batch)
        h = a_ref[pl.ds(r0, batch), :] * h + b_ref[pl.ds(r0, batch), :]
        b_ref[pl.ds(r0, batch), :] = h
        return h

    h = lax.fori_loop(0, nt, scan, h_ref[...], unroll=(8 if nt >= 8 and batch == SUBLANES else 1))
    h_ref[...] = h
    hl_ref[...] = h
    xs_ref[0:hist, :] = xs_ref[rows:rows + hist, :]

    def outp(c, _):
        r0 = pl.multiple_of(c * chunk, chunk)
        o_ref[pl.ds(r0, chunk), :] = (b_ref[pl.ds(r0, chunk), :] * jax.nn.gelu(y_ref[pl.ds(r0, chunk), :])
                                      * jax.nn.sigmoid(gl_ref[pl.ds(r0, chunk), :]))
        return 0

    lax.fori_loop(0, rows // chunk, outp, 0)


def _with_dst(kern, n_in):
    def wrapped(*refs):
        return kern(*refs[:n_in], *refs[n_in + 1:])
    return wrapped


def _shared_rows(dst, n_total):
    if dst is None:
        return [], [], jax.ShapeDtypeStruct((n_total, D), f32)
    return [pl.BlockSpec(memory_space=pl.ANY)], [dst], jax.ShapeDtypeStruct(dst.shape, dst.dtype)


def _lru(proj, row_off, n_rows, batch, rows_blk, reset_first, conv_init, h0, cw, cb, wxa, bx, ba, ap, dst=None):
    ob = row_off // rows_blk
    kern = functools.partial(_lru_kernel, batch=batch, reset_first=reset_first, chunk=128)
    n_in = 11
    dst_spec, dst_arg, out0 = _shared_rows(dst, proj.shape[0])
    return pl.pallas_call(
        kern if dst is None else _with_dst(kern, n_in),
        grid=(n_rows // rows_blk,),
        in_specs=[
            pl.BlockSpec((rows_blk, D), lambda i: (ob + i, COL_LRU_X)),
            pl.BlockSpec((rows_blk, D), lambda i: (ob + i, COL_LRU_Y)),
            pl.BlockSpec((rows_blk, D), lambda i: (ob + i, COL_GL)),
            _full((3 * batch, D)), _full((batch, D)), _full((4, D)), _full((1, D)),
            _full((D // LANES, LANES, 2 * LANES)), _full((1, D)), _full((1, D)), _full((1, D)),
        ] + dst_spec,
        out_specs=[pl.BlockSpec((rows_blk, D), lambda i: (ob + i, 0)), _full((batch, D))],
        out_shape=[out0, jax.ShapeDtypeStruct((batch, D), f32)],
        input_output_aliases={} if dst is None else {n_in: 0},
        scratch_shapes=[pltpu.VMEM((rows_blk + 3 * batch, D), f32), pltpu.VMEM((rows_blk, D), f32),
                        pltpu.VMEM((rows_blk, D), f32), pltpu.VMEM((batch, D), f32)],
        compiler_params=pltpu.CompilerParams(dimension_semantics=("arbitrary",), vmem_limit_bytes=48 * MIB),
        name="lru",
    )(proj, proj, proj, conv_init, h0, cw, cb, wxa, bx, ba, ap, *dst_arg)


TILE_ALPHA, TILE_BETA, TILE_GAMMA, TILE_KP, TILE_WR, TILE_V = range(6)
ROW_BR, ROW_KR, ROW_BONUS = range(3)
CONST_KK, CONST_KA, CONST_RK, CONST_LNW, CONST_LNB = range(5)
NAT_R, NAT_W, NAT_K, NAT_V, NAT_A = range(5)


def _wkv_prep(tiles_ref, rows_ref, gam_ref, t, g, r, w, k, v, a, const):
    kk = k * const(CONST_KK)
    nrm = jnp.sqrt(jnp.sum(kk * kk, axis=0, keepdims=True))
    kk = kk * (1.0 / jnp.maximum(nrm, 1e-12))
    beta = kk * a
    kp = k * (1.0 + (a - 1.0) * const(CONST_KA))
    gam_prev = gam_ref[g]
    gam = gam_prev * w
    gam_ref[g] = gam
    inv = 1.0 / gam
    tiles_ref[t, g, TILE_ALPHA] = -kk * gam_prev
    tiles_ref[t, g, TILE_BETA] = beta * inv
    tiles_ref[t, g, TILE_GAMMA] = gam
    tiles_ref[t, g, TILE_KP] = kp * inv
    tiles_ref[t, g, TILE_WR] = gam * r
    tiles_ref[t, g, TILE_V] = v
    rows_ref[t, g, ROW_BR:ROW_BR + 1, :] = jnp.sum(beta * r, axis=0, keepdims=True)
    rows_ref[t, g, ROW_KR:ROW_KR + 1, :] = jnp.sum(kp * r, axis=0, keepdims=True)
    rows_ref[t, g, ROW_BONUS:ROW_BONUS + 1, :] = jnp.sum(r * kp * const(CONST_RK), axis=0, keepdims=True)


def _rwkv_pre_kernel(r_ref, k_ref, v_ref, l_ref, sinit_ref, mu_ref, w0_ref, a0_ref, w2_ref, a2_ref, g2_ref, c_ref,
                     tiles_ref, rows_ref, go_ref, ps_ref, nat_ref, gam_ref, *, batch, chunk, chains_are_heads):
    rows = r_ref.shape[0]
    gam_ref[...] = jnp.ones_like(gam_ref)

    @pl.when(pl.program_id(0) == 0)
    def _():
        ps_ref[0:batch, :] = sinit_ref[...]

    ps_ref[batch:batch + rows, 0:D] = r_ref[...]
    ps_ref[batch:batch + rows, D:2 * D] = k_ref[...]
    ps_ref[batch:batch + rows, 2 * D:3 * D] = v_ref[...]
    ps_ref[batch:batch + rows, 3 * D:3 * D + LORA_PAD] = l_ref[...]

    def body(c, _):
        r0 = _aligned(c * chunk, chunk)

        def mixed(lo, hi):
            cur = ps_ref[pl.ds(_aligned(r0 + batch, SUBLANES), chunk), lo:hi]
            prev = ps_ref[pl.ds(r0, chunk), lo:hi]
            return cur + (prev - cur) * mu_ref[:, lo:hi]

        nat_ref[NAT_R, pl.ds(r0, chunk), :] = mixed(0, D)
        nat_ref[NAT_K, pl.ds(r0, chunk), :] = mixed(D, 2 * D)
        nat_ref[NAT_V, pl.ds(r0, chunk), :] = mixed(2 * D, 3 * D)
        lm = mixed(3 * D, 3 * D + LORA_PAD)
        xwa = lm[:, 0:LANES]
        xg = lm[:, LANES:3 * LANES]
        lw = jnp.dot(jnp.tanh(xwa).astype(bf16), w2_ref[...], preferred_element_type=f32)
        w_log = -jax.nn.softplus(-(w0_ref[...] + lw)) - 0.5
        nat_ref[NAT_W, pl.ds(r0, chunk), :] = jnp.exp(-jnp.exp(w_log))
        la = jnp.dot(xwa.astype(bf16), a2_ref[...], preferred_element_type=f32)
        nat_ref[NAT_A, pl.ds(r0, chunk), :] = jax.nn.sigmoid(a0_ref[...] + la)
        go_ref[pl.ds(r0, chunk), :] = jnp.dot(jax.nn.sigmoid(xg).astype(bf16), g2_ref[...],
                                              preferred_element_type=f32)
        return 0

    if chains_are_heads:
        lax.fori_loop(0, rows // chunk, body, 0)
        def tstep(t, _):
            r0 = pl.multiple_of(t * batch, batch)
            for hp in range(D // LANES):
                trs = [nat_ref[q, pl.ds(r0, batch), hp * LANES:(hp + 1) * LANES].T for q in range(5)]
                for h2 in range(2):
                    h = 2 * hp + h2
                    _wkv_prep(tiles_ref, rows_ref, gam_ref, t, h, *[tr[h2 * HD:(h2 + 1) * HD] for tr in trs],
                              lambda i, h=h: c_ref[h, i])
            return 0

        lax.fori_loop(0, rows // batch, tstep, 0)
    else:
        half = LANES // 2
        lane = lax.broadcasted_iota(jnp.int32, (HD, LANES), 1)

        def load_pair(q, tp):
            zz = jnp.concatenate(
                [nat_ref[q, pl.ds(_aligned(tp * 2 * SUBLANES + tt * SUBLANES, SUBLANES), SUBLANES),
                         hp * LANES:(hp + 1) * LANES] for tt in range(2) for hp in range(D // LANES)], axis=0)
            tr = zz.T
            top, bot = tr[0:HD], tr[HD:2 * HD]
            d0 = jnp.where(lane < half, top, pltpu.roll(bot, half, axis=1))
            d1 = jnp.where(lane < half, pltpu.roll(top, half, axis=1), bot)
            return d0, d1

        def pair(tp, _):
            pairs = [load_pair(q, tp) for q in range(5)]
            for tt in range(2):
                _wkv_prep(tiles_ref, rows_ref, gam_ref, 2 * tp + tt, 0, *[p[tt] for p in pairs], lambda i: c_ref[0, i])
            return 0

        n_chunks = rows // chunk
        pairs_per_chunk = chunk // (2 * SUBLANES)
        body(0, 0)
        for c in range(1, n_chunks):
            body(c, 0)
            for tp in range((c - 1) * pairs_per_chunk, c * pairs_per_chunk):
                pair(tp, 0)
        for tp in range((n_chunks - 1) * pairs_per_chunk, n_chunks * pairs_per_chunk):
            pair(tp, 0)

    ps_ref[0:batch, :] = ps_ref[rows:rows + batch, :]


def _rwkv_pre(proj, row_off, n_rows, batch, rows_blk, groups, shift_init, consts, mu, w0, a0, w2p, a2p, g2p,
              dst=None):
    ob = row_off // rows_blk
    width = 3 * D + LORA_PAD
    steps_blk = rows_blk // batch
    kern = functools.partial(_rwkv_pre_kernel, batch=batch, chunk=128, chains_are_heads=groups > 1)
    n_in = 12
    dst_spec, dst_arg, out_g = _shared_rows(dst, proj.shape[0])
    return pl.pallas_call(
        kern if dst is None else _with_dst(kern, n_in),
        grid=(n_rows // rows_blk,),
        in_specs=[
            pl.BlockSpec((rows_blk, D), lambda i: (ob + i, COL_R)),
            pl.BlockSpec((rows_blk, D), lambda i: (ob + i, COL_K)),
            pl.BlockSpec((rows_blk, D), lambda i: (ob + i, COL_V)),
            pl.BlockSpec((rows_blk, LORA_PAD), lambda i: (ob + i, COL_LORA)),
            _full((batch, width)), _full((1, width)), _full((1, D)), _full((1, D)),
            _full((LANES, D)), _full((LANES, D)), _full((2 * LANES, D)), _full((groups, 5, HD, LANES)),
        ] + dst_spec,
        out_specs=[pl.BlockSpec((steps_blk, groups, 6, HD, LANES), lambda i: (i, 0, 0, 0, 0)),
                   pl.BlockSpec((steps_blk, groups, SUBLANES, LANES), lambda i: (i, 0, 0, 0)),
                   pl.BlockSpec((rows_blk, D), lambda i: (ob + i, 0))],
        out_shape=[jax.ShapeDtypeStruct((n_rows // batch, groups, 6, HD, LANES), f32),
                   jax.ShapeDtypeStruct((n_rows // batch, groups, SUBLANES, LANES), f32),
                   out_g],
        input_output_aliases={} if dst is None else {n_in: 2},
        scratch_shapes=[pltpu.VMEM((rows_blk + batch, width), f32), pltpu.VMEM((5, rows_blk, D), f32),
                        pltpu.VMEM((groups, HD, LANES), f32)],
        compiler_params=pltpu.CompilerParams(dimension_semantics=("arbitrary",), vmem_limit_bytes=56 * MIB),
        name="rwkv_pre",
    )(proj, proj, proj, proj, shift_init, mu, w0, a0, w2p, a2p, g2p, consts, *dst_arg)


STATE_ROWS = HD * HD
KEY_UNROLL = 32


def _wkv_step(s_ref, tiles_ref, rows_ref, y_ref, t, window):
    def key_row(tile, j):
        return tiles_ref[t, 0, tile, pl.ds(j, 1), :]

    def reduce_keys(j, acc):
        sa, y0 = acc
        s = s_ref[pl.ds(pl.multiple_of(j * HD, HD), HD), :]
        return sa + s * key_row(TILE_ALPHA, j), y0 + s * key_row(TILE_WR, j)

    zero = jnp.zeros((HD, LANES), f32)
    sa, y0 = lax.fori_loop(0, HD, reduce_keys, (zero, zero), unroll=KEY_UNROLL)
    v = tiles_ref[t, 0, TILE_V]

    def update_keys(j, _):
        rows = pl.ds(pl.multiple_of(j * HD, HD), HD)
        s_ref[rows, :] = s_ref[rows, :] + sa * key_row(TILE_BETA, j) + v * key_row(TILE_KP, j)
        return 0

    lax.fori_loop(0, HD, update_keys, 0, unroll=KEY_UNROLL)
    y_ref[t, 0] = y0 + sa * rows_ref[t, 0, ROW_BR:ROW_BR + 1, :] + v * rows_ref[t, 0, ROW_KR:ROW_KR + 1, :]

    @pl.when((t + 1) % window == 0)
    def _():
        def rescale(j, _):
            rows = pl.ds(pl.multiple_of(j * HD, HD), HD)
            s_ref[rows, :] = s_ref[rows, :] * key_row(TILE_GAMMA, j)
            return 0

        lax.fori_loop(0, HD, rescale, 0, unroll=KEY_UNROLL)


def _wkv_scan_kernel(tiles_ref, rows_ref, st_ref, y_ref, sfin_ref, s_ref, *, steps, window):
    tb = pl.program_id(1)

    def value_rows(i):
        return pl.ds(i, HD, stride=HD)

    @pl.when(tb == 0)
    def _():
        def init(c, _):
            tr = st_ref[:, pl.ds(pl.multiple_of(c * LANES, LANES), LANES)].T
            for i2 in range(2):
                s_ref[value_rows(2 * c + i2), :] = tr[i2 * HD:(i2 + 1) * HD]
            return 0

        lax.fori_loop(0, STATE_ROWS // LANES, init, 0, unroll=8)

    def step(t, _):
        _wkv_step(s_ref, tiles_ref, rows_ref, y_ref, t, window)
        return 0

    lax.fori_loop(0, steps, step, 0)

    @pl.when(tb == pl.num_programs(1) - 1)
    def _():
        def fin(c, _):
            pair = jnp.concatenate([s_ref[value_rows(2 * c + i2), :] for i2 in range(2)], axis=0)
            sfin_ref[:, pl.ds(pl.multiple_of(c * LANES, LANES), LANES)] = pair.T
            return 0

        lax.fori_loop(0, STATE_ROWS // LANES, fin, 0, unroll=8)


def _wkv_scan(tiles, rows, state, steps_blk, window):
    n_steps, groups = tiles.shape[0], tiles.shape[1]
    assert steps_blk % window == 0
    kern = functools.partial(_wkv_scan_kernel, steps=steps_blk, window=window)
    sblk = pl.BlockSpec((LANES, STATE_ROWS), lambda g, i: (0, g))
    return pl.pallas_call(
        kern,
        grid=(groups, n_steps // steps_blk),
        in_specs=[pl.BlockSpec((steps_blk, 1, 6, HD, LANES), lambda g, i: (i, g, 0, 0, 0)),
                  pl.BlockSpec((steps_blk, 1, SUBLANES, LANES), lambda g, i: (i, g, 0, 0)),
                  sblk],
        out_specs=[pl.BlockSpec((steps_blk, 1, HD, LANES), lambda g, i: (i, g, 0, 0)), sblk],
        out_shape=[jax.ShapeDtypeStruct((n_steps, groups, HD, LANES), f32),
                   jax.ShapeDtypeStruct((LANES, groups * STATE_ROWS), f32)],
        scratch_shapes=[pltpu.VMEM((STATE_ROWS, LANES), f32)],
        compiler_params=pltpu.CompilerParams(dimension_semantics=("arbitrary", "arbitrary"),
                                             vmem_limit_bytes=48 * MIB),
        name="wkv_scan",
    )(tiles, rows, state)


def _wkv_norm(y, v, bonus, lnw, lnb):
    mean = jnp.sum(y, axis=0, keepdims=True) * (1.0 / HD)
    d = y - mean
    var = jnp.sum(d * d, axis=0, keepdims=True) * (1.0 / HD)
    return d * lax.rsqrt(var + GN_EPS) * lnw + lnb + bonus * v


def _wkv_post_kernel(y_ref, v_ref, rows_ref, c_ref, o_ref, *, batch, chains_are_heads):
    steps = y_ref.shape[0]

    def normed(t, g):
        return _wkv_norm(y_ref[t, g], v_ref[t, g, 0], rows_ref[t, g, ROW_BONUS:ROW_BONUS + 1, :],
                         c_ref[g, CONST_LNW], c_ref[g, CONST_LNB])

    if chains_are_heads:
        def tstep(t, _):
            r0 = pl.multiple_of(t * batch, batch)
            for hp in range(D // LANES):
                two = jnp.concatenate([normed(t, 2 * hp), normed(t, 2 * hp + 1)], axis=0)
                o_ref[pl.ds(r0, batch), hp * LANES:(hp + 1) * LANES] = two.T
            return 0

        lax.fori_loop(0, steps, tstep, 0)
    else:
        half = LANES // 2
        lane = lax.broadcasted_iota(jnp.int32, (HD, LANES), 1)

        def pair(tp, _):
            o0 = normed(2 * tp, 0)
            o1 = normed(2 * tp + 1, 0)
            top = jnp.where(lane < half, o0, pltpu.roll(o1, half, axis=1))
            bot = jnp.where(lane < half, pltpu.roll(o0, half, axis=1), o1)
            zz = jnp.concatenate([top, bot], axis=0).T
            for tt in range(2):
                row = pl.multiple_of(tp * 2 * SUBLANES + tt * SUBLANES, SUBLANES)
                for hp in range(D // LANES):
                    src = tt * HD + hp * SUBLANES
                    o_ref[pl.ds(row, SUBLANES), hp * LANES:(hp + 1) * LANES] = zz[src:src + SUBLANES, :]
            return 0

        lax.fori_loop(0, steps // 2, pair, 0, unroll=4)


def _outproj_kernel(ml_ref, g_ref, gr_ref, x_ref, yp_ref, vp_ref, rp_ref, cp_ref, ysm_ref, vs_ref, rs_ref, cs_ref,
                    wo_ref, n2_ref, rh_ref, rl_ref, rb_ref,
                    x1_ref, t_ref, selt_ref, seln_ref, op_ref, *, prompt_tiles, prompt_batch, sample_batch):
    tm = x_ref.shape[0]
    i = pl.program_id(0)

    @pl.when(i < prompt_tiles)
    def _():
        _wkv_post_kernel(yp_ref, vp_ref, rp_ref, cp_ref, op_ref, batch=prompt_batch, chains_are_heads=False)

    @pl.when(i >= prompt_tiles)
    def _():
        _wkv_post_kernel(ysm_ref, vs_ref, rs_ref, cs_ref, op_ref, batch=sample_batch, chains_are_heads=True)

    merged = ml_ref[...] + jax.nn.sigmoid(gr_ref[...]) * (op_ref[...] * g_ref[...])
    x1 = x_ref[...] + jnp.dot(merged.astype(bf16), wo_ref[...], preferred_element_type=f32)
    x1_ref[...] = x1
    t = x1 * lax.rsqrt(jnp.mean(x1 * x1, axis=-1, keepdims=True) + NORM_EPS) * n2_ref[...]
    th = t.astype(bf16)
    t_ref[...] = th
    tl = (t - th.astype(f32)).astype(bf16)
    nt_dims = (((1,), (1,)), ((), ()))
    lg = (lax.dot_general(rh_ref[...], th, nt_dims, preferred_element_type=f32)
          + lax.dot_general(rh_ref[...], tl, nt_dims, preferred_element_type=f32)
          + lax.dot_general(rl_ref[...], th, nt_dims, preferred_element_type=f32)) + rb_ref[...]
    row = lax.broadcasted_iota(jnp.int32, (EPG, tm), 0).astype(f32)
    neg = jnp.float32(-jnp.inf)
    glog = jnp.where(row < N_GROUPS, lg[0:EPG], neg)
    ge = jnp.exp(glog - jnp.max(glog, axis=0, keepdims=True))
    pg = ge / jnp.sum(ge, axis=0, keepdims=True)
    p_top = jnp.max(pg, axis=0, keepdims=True)
    g_idx = jnp.min(jnp.where(pg == p_top, row, EPG), axis=0, keepdims=True)
    le = jnp.zeros((EPG, tm), f32)
    for g in range(N_GROUPS):
        le = jnp.where(g_idx == g, lg[EPG * (g + 1):EPG * (g + 2)], le)
    qe = jnp.exp(le - jnp.max(le, axis=0, keepdims=True))
    q = qe / jnp.sum(qe, axis=0, keepdims=True)
    q1 = jnp.max(q, axis=0, keepdims=True)
    i1 = jnp.min(jnp.where(q == q1, row, EPG), axis=0, keepdims=True)
    qm = jnp.where(row == i1, -1.0, q)
    q2 = jnp.max(qm, axis=0, keepdims=True)
    i2 = jnp.min(jnp.where(qm == q2, row, EPG), axis=0, keepdims=True)
    qs = q1 + q2
    sel = jnp.concatenate([g_idx * EPG + i1, g_idx * EPG + i2, q1 / qs * p_top, q2 / qs * p_top,
                           jnp.zeros((SUBLANES - 4, tm), f32)], axis=0)
    selt_ref[...] = sel
    sel_pad = jnp.concatenate([sel, jnp.zeros((LANES - SUBLANES, tm), f32)], axis=0)
    for c in range(tm // LANES):
        seln_ref[c * LANES:(c + 1) * LANES, :] = sel_pad[:, c * LANES:(c + 1) * LANES].T


def _outproj(ml, g, proj, x, scan_p, scan_s, wo, n2, rh, rl, rb, prompt_batch, sample_batch, tm=512):
    n = x.shape[0]
    y_p, tiles_p, rows_p, consts_p = scan_p
    y_s, tiles_s, rows_s, consts_s = scan_s
    n_p = y_p.shape[0] * prompt_batch
    sp, ss = tm // prompt_batch, tm // sample_batch
    prompt_tiles = n_p // tm
    sample_tiles = (n - n_p) // tm
    assert n_p % tm == 0 and sp % 2 == 0 and y_s.shape[0] == sample_tiles * ss
    pidx = lambda i: jnp.minimum(i, prompt_tiles - 1)
    sidx = lambda i: jnp.clip(i - prompt_tiles, 0, sample_tiles - 1)
    groups = y_s.shape[1]
    blk = pl.BlockSpec((tm, D), lambda i: (i, 0))
    kern = functools.partial(_outproj_kernel, prompt_tiles=prompt_tiles, prompt_batch=prompt_batch,
                             sample_batch=sample_batch)
    return pl.pallas_call(
        kern,
        grid=(n // tm,),
        in_specs=[blk, blk, pl.BlockSpec((tm, D), lambda i: (i, COL_GR)), blk,
                  pl.BlockSpec((sp, 1, HD, LANES), lambda i: (pidx(i), 0, 0, 0)),
                  pl.BlockSpec((sp, 1, 1, HD, LANES), lambda i: (pidx(i), 0, TILE_V, 0, 0)),
                  pl.BlockSpec((sp, 1, SUBLANES, LANES), lambda i: (pidx(i), 0, 0, 0)),
                  _full((1, 5, HD, LANES)),
                  pl.BlockSpec((ss, groups, HD, LANES), lambda i: (sidx(i), 0, 0, 0)),
                  pl.BlockSpec((ss, groups, 1, HD, LANES), lambda i: (sidx(i), 0, TILE_V, 0, 0)),
                  pl.BlockSpec((ss, groups, SUBLANES, LANES), lambda i: (sidx(i), 0, 0, 0)),
                  _full((groups, 5, HD, LANES)),
                  _full((D, D)), _full((1, D)), _full((ROUTER_ROWS, D)), _full((ROUTER_ROWS, D)),
                  _full((ROUTER_ROWS, 1))],
        out_specs=[blk, blk, pl.BlockSpec((SUBLANES, tm), lambda i: (0, i)),
                   pl.BlockSpec((tm, LANES), lambda i: (i, 0))],
        out_shape=[jax.ShapeDtypeStruct((n, D), f32), jax.ShapeDtypeStruct((n, D), bf16),
                   jax.ShapeDtypeStruct((SUBLANES, n), f32), jax.ShapeDtypeStruct((n, LANES), f32)],
        scratch_shapes=[pltpu.VMEM((tm, D), f32)],
        compiler_params=pltpu.CompilerParams(dimension_semantics=("arbitrary",), vmem_limit_bytes=56 * MIB),
        name="outproj_router",
    )(ml, g, proj, x, y_p, tiles_p, rows_p, consts_p, y_s, tiles_s, rows_s, consts_s, wo, n2, rh, rl, rb)


MOE_CHUNK = 32
MOE_ROWS = 128
MOE_GATHER = 512
MOE_EXPERTS_PER_STEP = 4
MOE_OVERRUN = MOE_ROWS
MOE_CAP = -(-(2 * TOKEN_TILE + N_EXPERTS * (MOE_CHUNK - 1) + MOE_OVERRUN) // MOE_GATHER) * MOE_GATHER
SEL_E1, SEL_E2, SEL_C1, SEL_C2 = range(4)


def _route_meta_kernel(seln_ref, o_ref):
    tm = seln_ref.shape[0]
    lane = lax.broadcasted_iota(jnp.int32, (tm, LANES), 1).astype(f32)
    sel = seln_ref[...]
    hit = (lane == sel[:, SEL_E1:SEL_E1 + 1]) | (lane == sel[:, SEL_E2:SEL_E2 + 1])
    cnt = jnp.sum(jnp.where(hit, 1.0, 0.0), axis=0, keepdims=True)
    nchunk = jnp.floor((cnt + (MOE_CHUNK - 1)) * (1.0 / MOE_CHUNK))
    upper = (lax.broadcasted_iota(jnp.int32, (LANES, LANES), 0)
             < lax.broadcasted_iota(jnp.int32, (LANES, LANES), 1))
    offs = jnp.dot(jnp.broadcast_to(nchunk, (SUBLANES, LANES)).astype(bf16), jnp.where(upper, 1.0, 0.0).astype(bf16),
                   preferred_element_type=f32)
    row = lax.broadcasted_iota(jnp.int32, (SUBLANES, LANES), 0)
    o_ref[0] = jnp.where(row == 0, nchunk, jnp.where(row == 1, offs, 0.0)).astype(jnp.int32)


def _route_meta(seln, tm):
    tiles = seln.shape[0] // tm
    return pl.pallas_call(
        _route_meta_kernel,
        grid=(tiles,),
        in_specs=[pl.BlockSpec((tm, LANES), lambda i: (i, 0))],
        out_specs=pl.BlockSpec((1, SUBLANES, LANES), lambda i: (i, 0, 0)),
        out_shape=jax.ShapeDtypeStruct((tiles, SUBLANES, LANES), jnp.int32),
        compiler_params=pltpu.CompilerParams(dimension_semantics=("arbitrary",)),
        name="route_meta",
    )(seln)


def _moe_experts_kernel(nch_ref, off_ref, t_ref, selt_ref, tri_ref, wgu_ref, wd_ref, ys_ref, pos_ref,
                        xs_ref, cw_ref):
    i = pl.program_id(0)
    e = pl.program_id(1)
    tm = t_ref.shape[0]

    @pl.when(e == 0)
    def _():
        ys_ref[...] = jnp.zeros_like(ys_ref)
        e1 = selt_ref[SEL_E1:SEL_E1 + 1, :]
        e2 = selt_ref[SEL_E2:SEL_E2 + 1, :]
        erow = lax.broadcasted_iota(jnp.int32, (N_EXPERTS, tm), 0).astype(f32)
        oh1 = jnp.where(erow == e1, 1.0, 0.0)
        oh2 = jnp.where(erow == e2, 1.0, 0.0)
        before1 = jnp.dot(oh1.astype(bf16), tri_ref[...], preferred_element_type=f32)
        before2 = jnp.dot(oh2.astype(bf16), tri_ref[...], preferred_element_type=f32)
        cnt1 = jnp.sum(oh1, axis=1, keepdims=True)
        cnt2 = jnp.sum(oh2, axis=1, keepdims=True)
        nchunk = jnp.floor((cnt1 + cnt2 + (MOE_CHUNK - 1)) * (1.0 / MOE_CHUNK))
        lower = (lax.broadcasted_iota(jnp.int32, (N_EXPERTS, LANES), 1)
                 < lax.broadcasted_iota(jnp.int32, (N_EXPERTS, LANES), 0))
        nchunk_rows = jnp.concatenate([jnp.broadcast_to(nchunk, (N_EXPERTS, LANES)),
                                       jnp.zeros((LANES - N_EXPERTS, LANES), f32)], axis=0)
        start = jnp.dot(jnp.where(lower, 1.0, 0.0).astype(bf16), nchunk_rows.astype(bf16),
                        preferred_element_type=f32)[:, 0:1] * MOE_CHUNK
        pos1 = jnp.sum(oh1 * (start + before1), axis=0, keepdims=True)
        pos2 = jnp.sum(oh2 * (start + cnt1 + before2), axis=0, keepdims=True)
        pos_ref[...] = jnp.concatenate([pos1, pos2, jnp.zeros((SUBLANES - 2, tm), f32)], axis=0)

    per_step = wgu_ref.shape[0]

    def gathered_chunks(step):
        last = i * N_EXPERTS + jnp.maximum(step * per_step + per_step - 1, 0)
        end_row = (off_ref[last] + nch_ref[last]) * MOE_CHUNK + MOE_OVERRUN
        chunks = jnp.minimum((end_row + MOE_GATHER - 1) // MOE_GATHER, MOE_CAP // MOE_GATHER)
        return jnp.where(step < 0, 0, chunks)

    def gather_chunk(k, _):
        r0 = pl.multiple_of(k * MOE_GATHER, MOE_GATHER)
        ridx = (lax.broadcasted_iota(jnp.int32, (MOE_GATHER, tm), 0) + r0).astype(f32)
        p1 = ridx == pos_ref[0:1, :]
        p2 = ridx == pos_ref[1:2, :]
        onehot = jnp.where(p1 | p2, 1.0, 0.0).astype(bf16)
        xs_ref[pl.ds(r0, MOE_GATHER), :] = jnp.dot(onehot, t_ref[...], preferred_element_type=f32).astype(bf16)
        w = jnp.sum(jnp.where(p1, selt_ref[SEL_C1:SEL_C1 + 1, :], 0.0)
                    + jnp.where(p2, selt_ref[SEL_C2:SEL_C2 + 1, :], 0.0), axis=1, keepdims=True)
        cw_ref[pl.ds(r0, MOE_GATHER), :] = jnp.broadcast_to(w, (MOE_GATHER, LANES))
        return 0

    lax.fori_loop(gathered_chunks(e - 1), gathered_chunks(e), gather_chunk, 0)

    def expert_rows(u, rows):
        gu = jnp.dot(xs_ref[rows, :], wgu_ref[u], preferred_element_type=f32)
        w = cw_ref[rows, :]
        h = jax.nn.silu(gu[:, :D_EXPERT]) * gu[:, D_EXPERT:] * jnp.concatenate([w, w], axis=1)
        return jnp.dot(h.astype(bf16), wd_ref[u], preferred_element_type=f32).astype(bf16)

    per_block = MOE_ROWS // MOE_CHUNK
    experts = [i * N_EXPERTS + e * per_step + u for u in range(per_step)]
    bases = [off_ref[x] * MOE_CHUNK for x in experts]

    for u in range(per_step):
        rows = pl.ds(pl.multiple_of(bases[u], MOE_CHUNK), MOE_ROWS)
        ys_ref[0, rows, :] = expert_rows(u, rows)

    for u in range(per_step):
        seg_end = bases[u] + nch_ref[experts[u]] * MOE_CHUNK

        def more(c, _, u=u, seg_end=seg_end):
            start = bases[u] + (c + 1) * MOE_ROWS
            rows = pl.ds(pl.multiple_of(start, MOE_CHUNK), MOE_ROWS)
            ridx = lax.broadcasted_iota(jnp.int32, (MOE_ROWS, D), 0) + start
            ys_ref[0, rows, :] = jnp.where(ridx < seg_end, expert_rows(u, rows), ys_ref[0, rows, :])
            return 0

        lax.fori_loop(0, jnp.maximum(nch_ref[experts[u]] - 1, 0) // per_block, more, 0)


def _moe_experts(nch, off, t, selt, tri, wgu, wd, tm):
    n = t.shape[0]
    tiles = n // tm
    grid_spec = pltpu.PrefetchScalarGridSpec(
        num_scalar_prefetch=2,
        grid=(tiles, N_EXPERTS // MOE_EXPERTS_PER_STEP),
        in_specs=[
            pl.BlockSpec((tm, D), lambda i, e, nch, off: (i, 0)),
            pl.BlockSpec((SUBLANES, tm), lambda i, e, nch, off: (0, i)),
            pl.BlockSpec((tm, tm), lambda i, e, nch, off: (0, 0)),
            pl.BlockSpec((MOE_EXPERTS_PER_STEP, D, 2 * D_EXPERT), lambda i, e, nch, off: (e, 0, 0)),
            pl.BlockSpec((MOE_EXPERTS_PER_STEP, D_EXPERT, D), lambda i, e, nch, off: (e, 0, 0)),
        ],
        out_specs=[pl.BlockSpec((1, MOE_CAP, D), lambda i, e, nch, off: (i, 0, 0)),
                   pl.BlockSpec((SUBLANES, tm), lambda i, e, nch, off: (0, i))],
        scratch_shapes=[pltpu.VMEM((MOE_CAP, D), bf16), pltpu.VMEM((MOE_CAP, LANES), f32)],
    )
    return pl.pallas_call(
        _moe_experts_kernel,
        grid_spec=grid_spec,
        out_shape=[jax.ShapeDtypeStruct((tiles, MOE_CAP, D), bf16), jax.ShapeDtypeStruct((SUBLANES, n), f32)],
        compiler_params=pltpu.CompilerParams(dimension_semantics=("arbitrary", "arbitrary"),
                                             vmem_limit_bytes=56 * MIB),
        name="moe_experts",
    )(nch, off, t, selt, tri, wgu, wd)


def _moe_combine_kernel(nch_ref, off_ref, ysort_ref, pos_ref, x1_ref, fn_ref, yp_ref, ys_ref, acc_ref, yf_ref,
                        *, prompt_tiles, parts):
    i = pl.program_id(0)
    part = pl.program_id(1)
    rows = x1_ref.shape[0]
    bp, tsteps = yp_ref.shape[0], yp_ref.shape[1]
    bs = ys_ref.shape[0]
    ssteps = ys_ref.shape[1] // parts
    used_rows = (off_ref[i * N_EXPERTS + N_EXPERTS - 1] + nch_ref[i * N_EXPERTS + N_EXPERTS - 1]) * MOE_CHUNK

    pos_t = jnp.concatenate([pos_ref[...], jnp.zeros((LANES - SUBLANES, rows), f32)], axis=0)
    pos_n = jnp.concatenate([pos_t[:, c * LANES:(c + 1) * LANES].T for c in range(rows // LANES)], axis=0)
    pos1 = pos_n[:, 0:1]
    pos2 = pos_n[:, 1:2]
    acc_ref[...] = x1_ref[...]
    for k in range(MOE_CAP // MOE_GATHER):
        @pl.when(k * MOE_GATHER < used_rows)
        def _(k=k):
            cidx = (lax.broadcasted_iota(jnp.int32, (rows, MOE_GATHER), 1) + k * MOE_GATHER).astype(f32)
            onehot = jnp.where((cidx == pos1) | (cidx == pos2), 1.0, 0.0).astype(bf16)
            acc_ref[...] += jnp.dot(onehot, ysort_ref[0, k * MOE_GATHER:(k + 1) * MOE_GATHER, :],
                                    preferred_element_type=f32)

    xo = acc_ref[...]
    y = xo * lax.rsqrt(jnp.mean(xo * xo, axis=-1, keepdims=True) + NORM_EPS) * fn_ref[...]
    for cb in range(D // LANES):
        yf_ref[cb] = y[:, cb * LANES:(cb + 1) * LANES]

    @pl.when(i < prompt_tiles)
    def _():
        for b in range(bp):
            yp_ref[b] = jnp.concatenate([yf_ref[cb, pl.ds(b, tsteps, stride=bp), :] for cb in range(D // LANES)],
                                        axis=1)

    for p in range(parts):
        @pl.when((i >= prompt_tiles) & (part == p))
        def _(p=p):
            for t in range(ssteps):
                ys_ref[:, p * ssteps + t, :] = jnp.concatenate(
                    [yf_ref[cb, t * bs:(t + 1) * bs, :] for cb in range(D // LANES)], axis=1)


def _moe_combine(nch, off, ysort, pos, x1, fn, prompt_shape, sample_shape, tm, parts=1):
    n = x1.shape[0]
    bp, tp, _ = prompt_shape
    bs, ts, _ = sample_shape
    rows = tm // parts
    assert rows % bp == 0 and tp % (rows // bp) == 0 and bs * ts == tm and ts % parts == 0
    prompt_tiles = bp * tp // tm
    last_prompt_blk = prompt_tiles * parts - 1
    kern = functools.partial(_moe_combine_kernel, prompt_tiles=prompt_tiles, parts=parts)
    grid_spec = pltpu.PrefetchScalarGridSpec(
        num_scalar_prefetch=2,
        grid=(n // tm, parts),
        in_specs=[
            pl.BlockSpec((1, MOE_CAP, D), lambda i, p, nch, off: (i, 0, 0)),
            pl.BlockSpec((SUBLANES, rows), lambda i, p, nch, off: (0, i * parts + p)),
            pl.BlockSpec((rows, D), lambda i, p, nch, off: (i * parts + p, 0)),
            pl.BlockSpec((1, D), lambda i, p, nch, off: (0, 0)),
        ],
        out_specs=[pl.BlockSpec((bp, rows // bp, D),
                                lambda i, p, nch, off: (0, jnp.minimum(i * parts + p, last_prompt_blk), 0)),
                   pl.BlockSpec((bs, ts, D), lambda i, p, nch, off: (0, 0, 0))],
        scratch_shapes=[pltpu.VMEM((rows, D), f32), pltpu.VMEM((D // LANES, rows, LANES), f32)],
    )
    return pl.pallas_call(
        kern,
        grid_spec=grid_spec,
        out_shape=[jax.ShapeDtypeStruct(prompt_shape, f32), jax.ShapeDtypeStruct(sample_shape, f32)],
        compiler_params=pltpu.CompilerParams(dimension_semantics=("arbitrary", "arbitrary"),
                                             vmem_limit_bytes=56 * MIB),
        name="moe_combine",
    )(nch, off, ysort, pos, x1, fn)


def _chain_tiles_prompt(vec, batch):
    t = vec.reshape(HEADS // 2, 2, HD)
    t = jnp.transpose(t, (2, 1, 0))
    return jnp.broadcast_to(t[..., None], (HD, 2, HEADS // 2, batch)).reshape(HD, LANES)


def _chain_tiles_sample(vec):
    return jnp.broadcast_to(vec.reshape(HEADS, HD)[..., None], (HEADS, HD, LANES))


def kernel(x_prompt, x_sample, state_conv, state_lru, state_shift, state_wkv, norm1, w_in, conv_w, conv_b, lru_wx, lru_bx, lru_wa, lru_ba, lru_a_param, rwkv_mu, rwkv_w0, rwkv_w2, rwkv_a0, rwkv_a2, rwkv_g2, rwkv_k_k, rwkv_k_a, rwkv_r_k, rwkv_ln_w, rwkv_ln_b, w_out, norm2, router_group, router_group_b, router_expert, router_expert_b, exp_gate, exp_up, exp_down, final_norm):
    bp, tp, _ = x_prompt.shape
    bs, ts, _ = x_sample.shape
    assert norm1.shape[0] == 1 and bp * HEADS == LANES and bs == LANES and tp % 64 == 0 and ts % 2 == 0
    n_p, n_s = bp * tp, bs * ts
    c_rw = 2 * D
    c_lora = c_rw + 3 * D
    c_gl = c_lora + LORA

    w = w_in[0]
    w_all = jnp.concatenate([w[:, :c_lora], w[:, c_gl:], w[:, c_lora:c_gl],
                             jnp.zeros((D, LORA_PAD - LORA), f32)], axis=1).astype(bf16)
    mu = rwkv_mu[0]
    mu_all = jnp.concatenate([mu[:3 * D], mu[3 * D:], jnp.zeros((LORA_PAD - LORA,), f32)])[None]
    w2p = jnp.concatenate([rwkv_w2[0], jnp.zeros((LANES - LORA_W, D), f32)]).astype(bf16)
    a2p = jnp.concatenate([jnp.zeros((LORA_W, D), f32), rwkv_a2[0]]).astype(bf16)
    g2p = jnp.concatenate([rwkv_g2[0], jnp.zeros((2 * LANES - LORA_G, D), f32)]).astype(bf16)
    wxa = jnp.concatenate([lru_wx[0], lru_wa[0]], axis=-1).astype(bf16)
    row = lambda v: v.reshape(1, -1)
    rk = rwkv_r_k[0].reshape(D)
    cvecs = (rwkv_k_k[0], rwkv_k_a[0], rk, rwkv_ln_w[0], rwkv_ln_b[0])
    consts_p = jnp.stack([_chain_tiles_prompt(v, bp) for v in cvecs])[None]
    consts_s = jnp.stack([_chain_tiles_sample(v) for v in cvecs], axis=1)
    rw = jnp.zeros((ROUTER_ROWS, D), f32)
    rw = rw.at[0:N_GROUPS].set(router_group[0].T).at[EPG:EPG + N_EXPERTS].set(router_expert[0].T)
    rh = rw.astype(bf16)
    rl = (rw - rh.astype(f32)).astype(bf16)
    rb = jnp.zeros((ROUTER_ROWS, 1), f32)
    rb = rb.at[0:N_GROUPS, 0].set(router_group_b[0]).at[EPG:EPG + N_EXPERTS, 0].set(router_expert_b[0])
    wgu = jnp.concatenate([exp_gate[0], exp_up[0]], axis=-1).astype(bf16)
    wd = exp_down[0].astype(bf16)
    wo = w_out[0].astype(bf16)

    proj, x = _inproj(x_prompt, x_sample, row(norm1[0]), w_all, tm=TOKEN_TILE)

    lru_args = (conv_w[0], row(conv_b[0]), wxa, row(lru_bx[0]), row(lru_ba[0]), row(lru_a_param[0]))
    conv_init_s = jnp.transpose(state_conv[0], (1, 0, 2)).reshape(3 * bs, D)
    ml, hl_p = _lru(proj, 0, n_p, bp, 512, True, jnp.zeros((3 * bp, D), f32), jnp.zeros((bp, D), f32), *lru_args)
    ml, hl_s = _lru(proj, n_p, n_s, bs, 512, False, conv_init_s, state_lru[0], *lru_args, dst=ml)

    sh = state_shift[0]
    shift_init_s = jnp.concatenate([sh, jnp.zeros((bs, LORA_PAD - LORA), f32)], axis=1)
    pre_args = (mu_all, row(rwkv_w0[0]), row(rwkv_a0[0]), w2p, a2p, g2p)
    pre_rows = 256
    tiles_p, rows_p, g = _rwkv_pre(proj, 0, n_p, bp, pre_rows, 1, jnp.zeros((bp, 3 * D + LORA_PAD), f32),
                                   consts_p, *pre_args)
    tiles_s, rows_s, g = _rwkv_pre(proj, n_p, n_s, bs, pre_rows, HEADS, shift_init_s, consts_s, *pre_args, dst=g)

    y_p, sfin_p = _wkv_scan(tiles_p, rows_p, jnp.zeros((LANES, STATE_ROWS), f32), 64, pre_rows // bp)
    y_s, sfin_s = _wkv_scan(tiles_s, rows_s, state_wkv[0].reshape(bs, HEADS * STATE_ROWS), ts, pre_rows // bs)

    x1, t, selt, seln = _outproj(ml, g, proj, x, (y_p, tiles_p, rows_p, consts_p), (y_s, tiles_s, rows_s, consts_s),
                                 wo, row(norm2[0]), rh, rl, rb, bp, bs)
    meta = _route_meta(seln, TOKEN_TILE)
    nch = meta[:, 0, :N_EXPERTS].reshape(-1)
    off = meta[:, 1, :N_EXPERTS].reshape(-1)
    tri = jnp.triu(jnp.ones((TOKEN_TILE, TOKEN_TILE), bf16), k=1)
    ysort, pos = _moe_experts(nch, off, t, selt, tri, wgu, wd, TOKEN_TILE)
    y_prompt, y_sample = _moe_combine(nch, off, ysort, pos, x1, row(final_norm), x_prompt.shape, x_sample.shape,
                                      TOKEN_TILE)

    def last_rows(lo, hi, n_end, batch, steps):
        return proj[n_end - steps * batch:n_end, lo:hi].reshape(steps, batch, hi - lo)

    conv_p = jnp.transpose(last_rows(0, D, n_p, bp, 3), (1, 0, 2))[None]
    conv_s = jnp.transpose(last_rows(0, D, n_p + n_s, bs, 3), (1, 0, 2))[None]

    def shift_rows(n_end, batch):
        return jnp.concatenate([last_rows(2 * D, 5 * D, n_end, batch, 1)[0],
                                last_rows(7 * D, 7 * D + LORA, n_end, batch, 1)[0]], axis=1)[None]

    wkv_p = jnp.transpose(sfin_p.reshape(2, HEADS // 2, bp, HD, HD), (2, 1, 0, 3, 4)).reshape(1, bp, HEADS, HD, HD)
    wkv_s = sfin_s.reshape(1, bs, HEADS, HD, HD)
    return (y_prompt, y_sample, conv_p, hl_p[None], shift_rows(n_p, bp), wkv_p,
            conv_s, hl_s[None], shift_rows(n_p + n_s, bs), wkv_s)
```

```python
import functools

import jax
import jax.numpy as jnp
from jax import lax
from jax.experimental import pallas as pl
from jax.experimental.pallas import tpu as pltpu

f32 = jnp.float32
bf16 = jnp.bfloat16

D = 1024
HEADS = 16
HD = 64
LANES = 128
SUBLANES = 8
LORA_W = 64
LORA_A = 64
LORA_G = 160
LORA = LORA_W + LORA_A + LORA_G
LORA_PAD = 512
N_GROUPS = 4
EPG = 8
N_EXPERTS = N_GROUPS * EPG
D_EXPERT = 256
LRU_C = 8.0
GN_EPS = 64e-5
NORM_EPS = 1e-6
PROJ_COLS = 7 * D + LORA_PAD
COL_LRU_X, COL_LRU_Y, COL_R, COL_K, COL_V, COL_GL, COL_GR = range(7)
COL_LORA = 7 * D // LORA_PAD
ROUTER_ROWS = 48
MIB = 1024 * 1024
TOKEN_TILE = 1024


def _aligned(start, multiple):
    return start if isinstance(start, int) else pl.multiple_of(start, multiple)


def _full(shape, grid_rank=1):
    zeros = tuple(0 for _ in shape)
    if grid_rank == 1:
        return pl.BlockSpec(shape, lambda i: zeros)
    return pl.BlockSpec(shape, lambda i, j: zeros)


def _inproj_kernel(xp_ref, xs_ref, g_ref, w_ref, o_ref, xo_ref, xf_ref, xn_ref, *, prompt_tiles):
    i = pl.program_id(0)
    first_col = pl.program_id(1) == 0
    bp, tsteps = xp_ref.shape[0], xp_ref.shape[1]
    bs, ssteps = xs_ref.shape[0], xs_ref.shape[1]

    @pl.when(first_col & (i < prompt_tiles))
    def _():
        for b in range(bp):
            v = xp_ref[b]
            for cb in range(D // LANES):
                xf_ref[cb, pl.ds(b, tsteps, stride=bp), :] = v[:, cb * LANES:(cb + 1) * LANES]

    @pl.when(first_col & (i >= prompt_tiles))
    def _():
        for t in range(ssteps):
            v = xs_ref[:, t, :]
            for cb in range(D // LANES):
                xf_ref[cb, t * bs:(t + 1) * bs, :] = v[:, cb * LANES:(cb + 1) * LANES]

    @pl.when(first_col)
    def _():
        x = jnp.concatenate([xf_ref[cb] for cb in range(D // LANES)], axis=1)
        xo_ref[...] = x
        ms = jnp.mean(x * x, axis=-1, keepdims=True)
        xn_ref[...] = (x * lax.rsqrt(ms + NORM_EPS) * g_ref[...]).astype(bf16)

    o_ref[...] = jnp.dot(xn_ref[...], w_ref[...], preferred_element_type=f32)


def _inproj(x_prompt, x_sample, gain, w, tm, tn=1536):
    bp, tp, _ = x_prompt.shape
    bs, ts, _ = x_sample.shape
    assert tm % bp == 0 and tp % (tm // bp) == 0 and bs * ts == tm
    prompt_tiles = bp * tp // tm
    n = bp * tp + bs * ts
    kern = functools.partial(_inproj_kernel, prompt_tiles=prompt_tiles)
    return pl.pallas_call(
        kern,
        grid=(n // tm, PROJ_COLS // tn),
        in_specs=[
            pl.BlockSpec((bp, tm // bp, D), lambda i, j: (0, jnp.minimum(i, prompt_tiles - 1), 0)),
            pl.BlockSpec((bs, ts, D), lambda i, j: (0, 0, 0)),
            pl.BlockSpec((1, D), lambda i, j: (0, 0)),
            pl.BlockSpec((D, tn), lambda i, j: (0, j)),
        ],
        out_specs=[pl.BlockSpec((tm, tn), lambda i, j: (i, j)), pl.BlockSpec((tm, D), lambda i, j: (i, 0))],
        out_shape=[jax.ShapeDtypeStruct((n, PROJ_COLS), f32), jax.ShapeDtypeStruct((n, D), f32)],
        scratch_shapes=[pltpu.VMEM((D // LANES, tm, LANES), f32), pltpu.VMEM((tm, D), bf16)],
        compiler_params=pltpu.CompilerParams(dimension_semantics=("arbitrary", "arbitrary"),
                                             vmem_limit_bytes=56 * MIB),
        name="inproj",
    )(x_prompt, x_sample, gain, w)


def _lru_kernel(x_ref, y_ref, gl_ref, cinit_ref, h0_ref, cw_ref, cb_ref, wxa_ref, bx_ref, ba_ref, ap_ref,
                o_ref, hl_ref, xs_ref, a_ref, b_ref, h_ref, *, batch, reset_first, chunk):
    rows = x_ref.shape[0]
    nt = rows // batch
    hist = 3 * batch
    pid = pl.program_id(0)

    @pl.when(pid == 0)
    def _():
        xs_ref[0:hist, :] = cinit_ref[...]
        h_ref[...] = h0_ref[...]

    xs_ref[hist:hist + rows, :] = x_ref[...]
    logsig = -jax.nn.softplus(-ap_ref[...])

    def gates(c, _):
        r0 = pl.multiple_of(c * chunk, chunk)
        xc = (cb_ref[...]
              + cw_ref[3:4, :] * xs_ref[pl.ds(pl.multiple_of(r0 + hist, SUBLANES), chunk), :]
              + cw_ref[2:3, :] * xs_ref[pl.ds(pl.multiple_of(r0 + 2 * batch, SUBLANES), chunk), :]
              + cw_ref[1:2, :] * xs_ref[pl.ds(pl.multiple_of(r0 + batch, SUBLANES), chunk), :]
              + cw_ref[0:1, :] * xs_ref[pl.ds(r0, chunk), :])
        if reset_first:
            grow = lax.broadcasted_iota(jnp.int32, (chunk, LANES), 0) + (r0 + pid * rows)
            first = grow < batch
        for n in range(D // LANES):
            sl = slice(n * LANES, (n + 1) * LANES)
            xn = xc[:, sl]
            g2 = jnp.dot(xn.astype(bf16), wxa_ref[n], preferred_element_type=f32)
            gate_x = jax.nn.sigmoid(g2[:, :LANES] + bx_ref[:, sl])
            gate_a = jax.nn.sigmoid(g2[:, LANES:] + ba_ref[:, sl])
            log_a = LRU_C * gate_a * logsig[:, sl]
            a = jnp.exp(log_a)
            mult = jnp.sqrt(-jnp.tanh(log_a) * (a * a + 1.0))
            if reset_first:
                mult = jnp.where(first, 1.0, mult)
            a_ref[pl.ds(r0, chunk), sl] = a
            b_ref[pl.ds(r0, chunk), sl] = xn * gate_x * mult
        return 0

    lax.fori_loop(0, rows // chunk, gates, 0)

    def scan(t, h):
        r0 = pl.multiple_of(t * batch, batch)
        h = a_ref[pl.ds(r0, batch), :] * h + b_ref[pl.ds(r0, batch), :]
        b_ref[pl.ds(r0, batch), :] = h
        return h

    h = lax.fori_loop(0, nt, scan, h_ref[...], unroll=(8 if nt >= 8 and batch == SUBLANES else 1))
    h_ref[...] = h
    hl_ref[...] = h
    xs_ref[0:hist, :] = xs_ref[rows:rows + hist, :]

    def outp(c, _):
        r0 = pl.multiple_of(c * chunk, chunk)
        o_ref[pl.ds(r0, chunk), :] = (b_ref[pl.ds(r0, chunk), :] * jax.nn.gelu(y_ref[pl.ds(r0, chunk), :])
                                      * jax.nn.sigmoid(gl_ref[pl.ds(r0, chunk), :]))
        return 0

    lax.fori_loop(0, rows // chunk, outp, 0)


def _with_dst(kern, n_in):
    def wrapped(*refs):
        return kern(*refs[:n_in], *refs[n_in + 1:])
    return wrapped


def _shared_rows(dst, n_total):
    if dst is None:
        return [], [], jax.ShapeDtypeStruct((n_total, D), f32)
    return [pl.BlockSpec(memory_space=pl.ANY)], [dst], jax.ShapeDtypeStruct(dst.shape, dst.dtype)


def _lru(proj, row_off, n_rows, batch, rows_blk, reset_first, conv_init, h0, cw, cb, wxa, bx, ba, ap, dst=None):
    ob = row_off // rows_blk
    kern = functools.partial(_lru_kernel, batch=batch, reset_first=reset_first, chunk=128)
    n_in = 11
    dst_spec, dst_arg, out0 = _shared_rows(dst, proj.shape[0])
    return pl.pallas_call(
        kern if dst is None else _with_dst(kern, n_in),
        grid=(n_rows // rows_blk,),
        in_specs=[
            pl.BlockSpec((rows_blk, D), lambda i: (ob + i, COL_LRU_X)),
            pl.BlockSpec((rows_blk, D), lambda i: (ob + i, COL_LRU_Y)),
            pl.BlockSpec((rows_blk, D), lambda i: (ob + i, COL_GL)),
            _full((3 * batch, D)), _full((batch, D)), _full((4, D)), _full((1, D)),
            _full((D // LANES, LANES, 2 * LANES)), _full((1, D)), _full((1, D)), _full((1, D)),
        ] + dst_spec,
        out_specs=[pl.BlockSpec((rows_blk, D), lambda i: (ob + i, 0)), _full((batch, D))],
        out_shape=[out0, jax.ShapeDtypeStruct((batch, D), f32)],
        input_output_aliases={} if dst is None else {n_in: 0},
        scratch_shapes=[pltpu.VMEM((rows_blk + 3 * batch, D), f32), pltpu.VMEM((rows_blk, D), f32),
                        pltpu.VMEM((rows_blk, D), f32), pltpu.VMEM((batch, D), f32)],
        compiler_params=pltpu.CompilerParams(dimension_semantics=("arbitrary",), vmem_limit_bytes=48 * MIB),
        name="lru",
    )(proj, proj, proj, conv_init, h0, cw, cb, wxa, bx, ba, ap, *dst_arg)


TILE_ALPHA, TILE_BETA, TILE_GAMMA, TILE_KP, TILE_WR, TILE_V = range(6)
ROW_BR, ROW_KR, ROW_BONUS = range(3)
CONST_KK, CONST_KA, CONST_RK, CONST_LNW, CONST_LNB = range(5)
NAT_R, NAT_W, NAT_K, NAT_V, NAT_A = range(5)


def _wkv_prep(tiles_ref, rows_ref, gam_ref, t, g, r, w, k, v, a, const):
    kk = k * const(CONST_KK)
    nrm = jnp.sqrt(jnp.sum(kk * kk, axis=0, keepdims=True))
    kk = kk * (1.0 / jnp.maximum(nrm, 1e-12))
    beta = kk * a
    kp = k * (1.0 + (a - 1.0) * const(CONST_KA))
    gam_prev = gam_ref[g]
    gam = gam_prev * w
    gam_ref[g] = gam
    inv = 1.0 / gam
    tiles_ref[t, g, TILE_ALPHA] = -kk * gam_prev
    tiles_ref[t, g, TILE_BETA] = beta * inv
    tiles_ref[t, g, TILE_GAMMA] = gam
    tiles_ref[t, g, TILE_KP] = kp * inv
    tiles_ref[t, g, TILE_WR] = gam * r
    tiles_ref[t, g, TILE_V] = v
    rows_ref[t, g, ROW_BR:ROW_BR + 1, :] = jnp.sum(beta * r, axis=0, keepdims=True)
    rows_ref[t, g, ROW_KR:ROW_KR + 1, :] = jnp.sum(kp * r, axis=0, keepdims=True)
    rows_ref[t, g, ROW_BONUS:ROW_BONUS + 1, :] = jnp.sum(r * kp * const(CONST_RK), axis=0, keepdims=True)


def _rwkv_pre_kernel(r_ref, k_ref, v_ref, l_ref, sinit_ref, mu_ref, w0_ref, a0_ref, w2_ref, a2_ref, g2_ref, c_ref,
                     tiles_ref, rows_ref, go_ref, ps_ref, nat_ref, gam_ref, *, batch, chunk, chains_are_heads):
    rows = r_ref.shape[0]
    gam_ref[...] = jnp.ones_like(gam_ref)

    @pl.when(pl.program_id(0) == 0)
    def _():
        ps_ref[0:batch, :] = sinit_ref[...]

    ps_ref[batch:batch + rows, 0:D] = r_ref[...]
    ps_ref[batch:batch + rows, D:2 * D] = k_ref[...]
    ps_ref[batch:batch + rows, 2 * D:3 * D] = v_ref[...]
    ps_ref[batch:batch + rows, 3 * D:3 * D + LORA_PAD] = l_ref[...]

    def body(c, _):
        r0 = _aligned(c * chunk, chunk)

        def mixed(lo, hi):
            cur = ps_ref[pl.ds(_aligned(r0 + batch, SUBLANES), chunk), lo:hi]
            prev = ps_ref[pl.ds(r0, chunk), lo:hi]
            return cur + (prev - cur) * mu_ref[:, lo:hi]

        nat_ref[NAT_R, pl.ds(r0, chunk), :] = mixed(0, D)
        nat_ref[NAT_K, pl.ds(r0, chunk), :] = mixed(D, 2 * D)
        nat_ref[NAT_V, pl.ds(r0, chunk), :] = mixed(2 * D, 3 * D)
        lm = mixed(3 * D, 3 * D + LORA_PAD)
        xwa = lm[:, 0:LANES]
        xg = lm[:, LANES:3 * LANES]
        lw = jnp.dot(jnp.tanh(xwa).astype(bf16), w2_ref[...], preferred_element_type=f32)
        w_log = -jax.nn.softplus(-(w0_ref[...] + lw)) - 0.5
        nat_ref[NAT_W, pl.ds(r0, chunk), :] = jnp.exp(-jnp.exp(w_log))
        la = jnp.dot(xwa.astype(bf16), a2_ref[...], preferred_element_type=f32)
        nat_ref[NAT_A, pl.ds(r0, chunk), :] = jax.nn.sigmoid(a0_ref[...] + la)
        go_ref[pl.ds(r0, chunk), :] = jnp.dot(jax.nn.sigmoid(xg).astype(bf16), g2_ref[...],
                                              preferred_element_type=f32)
        return 0

    if chains_are_heads:
        lax.fori_loop(0, rows // chunk, body, 0)
        def tstep(t, _):
            r0 = pl.multiple_of(t * batch, batch)
            for hp in range(D // LANES):
                trs = [nat_ref[q, pl.ds(r0, batch), hp * LANES:(hp + 1) * LANES].T for q in range(5)]
                for h2 in range(2):
                    h = 2 * hp + h2
                    _wkv_prep(tiles_ref, rows_ref, gam_ref, t, h, *[tr[h2 * HD:(h2 + 1) * HD] for tr in trs],
                              lambda i, h=h: c_ref[h, i])
            return 0

        lax.fori_loop(0, rows // batch, tstep, 0)
    else:
        half = LANES // 2
        lane = lax.broadcasted_iota(jnp.int32, (HD, LANES), 1)

        def load_pair(q, tp):
            zz = jnp.concatenate(
                [nat_ref[q, pl.ds(_aligned(tp * 2 * SUBLANES + tt * SUBLANES, SUBLANES), SUBLANES),
                         hp * LANES:(hp + 1) * LANES] for tt in range(2) for hp in range(D // LANES)], axis=0)
            tr = zz.T
            top, bot = tr[0:HD], tr[HD:2 * HD]
            d0 = jnp.where(lane < half, top, pltpu.roll(bot, half, axis=1))
            d1 = jnp.where(lane < half, pltpu.roll(top, half, axis=1), bot)
            return d0, d1

        def pair(tp, _):
            pairs = [load_pair(q, tp) for q in range(5)]
            for tt in range(2):
                _wkv_prep(tiles_ref, rows_ref, gam_ref, 2 * tp + tt, 0, *[p[tt] for p in pairs], lambda i: c_ref[0, i])
            return 0

        n_chunks = rows // chunk
        pairs_per_chunk = chunk // (2 * SUBLANES)
        body(0, 0)
        for c in range(1, n_chunks):
            body(c, 0)
            for tp in range((c - 1) * pairs_per_chunk, c * pairs_per_chunk):
                pair(tp, 0)
        for tp in range((n_chunks - 1) * pairs_per_chunk, n_chunks * pairs_per_chunk):
            pair(tp, 0)

    ps_ref[0:batch, :] = ps_ref[rows:rows + batch, :]


def _rwkv_pre(proj, row_off, n_rows, batch, rows_blk, groups, shift_init, consts, mu, w0, a0, w2p, a2p, g2p,
              dst=None):
    ob = row_off // rows_blk
    width = 3 * D + LORA_PAD
    steps_blk = rows_blk // batch
    kern = functools.partial(_rwkv_pre_kernel, batch=batch, chunk=128, chains_are_heads=groups > 1)
    n_in = 12
    dst_spec, dst_arg, out_g = _shared_rows(dst, proj.shape[0])
    return pl.pallas_call(
        kern if dst is None else _with_dst(kern, n_in),
        grid=(n_rows // rows_blk,),
        in_specs=[
            pl.BlockSpec((rows_blk, D), lambda i: (ob + i, COL_R)),
            pl.BlockSpec((rows_blk, D), lambda i: (ob + i, COL_K)),
            pl.BlockSpec((rows_blk, D), lambda i: (ob + i, COL_V)),
            pl.BlockSpec((rows_blk, LORA_PAD), lambda i: (ob + i, COL_LORA)),
            _full((batch, width)), _full((1, width)), _full((1, D)), _full((1, D)),
            _full((LANES, D)), _full((LANES, D)), _full((2 * LANES, D)), _full((groups, 5, HD, LANES)),
        ] + dst_spec,
        out_specs=[pl.BlockSpec((steps_blk, groups, 6, HD, LANES), lambda i: (i, 0, 0, 0, 0)),
                   pl.BlockSpec((steps_blk, groups, SUBLANES, LANES), lambda i: (i, 0, 0, 0)),
                   pl.BlockSpec((rows_blk, D), lambda i: (ob + i, 0))],
        out_shape=[jax.ShapeDtypeStruct((n_rows // batch, groups, 6, HD, LANES), f32),
                   jax.ShapeDtypeStruct((n_rows // batch, groups, SUBLANES, LANES), f32),
                   out_g],
        input_output_aliases={} if dst is None else {n_in: 2},
        scratch_shapes=[pltpu.VMEM((rows_blk + batch, width), f32), pltpu.VMEM((5, rows_blk, D), f32),
                        pltpu.VMEM((groups, HD, LANES), f32)],
        compiler_params=pltpu.CompilerParams(dimension_semantics=("arbitrary",), vmem_limit_bytes=56 * MIB),
        name="rwkv_pre",
    )(proj, proj, proj, proj, shift_init, mu, w0, a0, w2p, a2p, g2p, consts, *dst_arg)


STATE_ROWS = HD * HD
KEY_UNROLL = 32


def _wkv_step(s_ref, tiles_ref, rows_ref, y_ref, t, window):
    def key_row(tile, j):
        return tiles_ref[t, 0, tile, pl.ds(j, 1), :]

    def reduce_keys(j, acc):
        sa, y0 = acc
        s = s_ref[pl.ds(pl.multiple_of(j * HD, HD), HD), :]
        return sa + s * key_row(TILE_ALPHA, j), y0 + s * key_row(TILE_WR, j)

    zero = jnp.zeros((HD, LANES), f32)
    sa, y0 = lax.fori_loop(0, HD, reduce_keys, (zero, zero), unroll=KEY_UNROLL)
    v = tiles_ref[t, 0, TILE_V]

    def update_keys(j, _):
        rows = pl.ds(pl.multiple_of(j * HD, HD), HD)
        s_ref[rows, :] = s_ref[rows, :] + sa * key_row(TILE_BETA, j) + v * key_row(TILE_KP, j)
        return 0

    lax.fori_loop(0, HD, update_keys, 0, unroll=KEY_UNROLL)
    y_ref[t, 0] = y0 + sa * rows_ref[t, 0, ROW_BR:ROW_BR + 1, :] + v * rows_ref[t, 0, ROW_KR:ROW_KR + 1, :]

    @pl.when((t + 1) % window == 0)
    def _():
        def rescale(j, _):
            rows = pl.ds(pl.multiple_of(j * HD, HD), HD)
            s_ref[rows, :] = s_ref[rows, :] * key_row(TILE_GAMMA, j)
            return 0

        lax.fori_loop(0, HD, rescale, 0, unroll=KEY_UNROLL)


def _wkv_scan_kernel(tiles_ref, rows_ref, st_ref, y_ref, sfin_ref, s_ref, *, steps, window):
    tb = pl.program_id(1)

    def value_rows(i):
        return pl.ds(i, HD, stride=HD)

    @pl.when(tb == 0)
    def _():
        def init(c, _):
            tr = st_ref[:, pl.ds(pl.multiple_of(c * LANES, LANES), LANES)].T
            for i2 in range(2):
                s_ref[value_rows(2 * c + i2), :] = tr[i2 * HD:(i2 + 1) * HD]
            return 0

        lax.fori_loop(0, STATE_ROWS // LANES, init, 0, unroll=8)

    def step(t, _):
        _wkv_step(s_ref, tiles_ref, rows_ref, y_ref, t, window)
        return 0

    lax.fori_loop(0, steps, step, 0)

    @pl.when(tb == pl.num_programs(1) - 1)
    def _():
        def fin(c, _):
            pair = jnp.concatenate([s_ref[value_rows(2 * c + i2), :] for i2 in range(2)], axis=0)
            sfin_ref[:, pl.ds(pl.multiple_of(c * LANES, LANES), LANES)] = pair.T
            return 0

        lax.fori_loop(0, STATE_ROWS // LANES, fin, 0, unroll=8)


def _wkv_scan(tiles, rows, state, steps_blk, window):
    n_steps, groups = tiles.shape[0], tiles.shape[1]
    assert steps_blk % window == 0
    kern = functools.partial(_wkv_scan_kernel, steps=steps_blk, window=window)
    sblk = pl.BlockSpec((LANES, STATE_ROWS), lambda g, i: (0, g))
    return pl.pallas_call(
        kern,
        grid=(groups, n_steps // steps_blk),
        in_specs=[pl.BlockSpec((steps_blk, 1, 6, HD, LANES), lambda g, i: (i, g, 0, 0, 0)),
                  pl.BlockSpec((steps_blk, 1, SUBLANES, LANES), lambda g, i: (i, g, 0, 0)),
                  sblk],
        out_specs=[pl.BlockSpec((steps_blk, 1, HD, LANES), lambda g, i: (i, g, 0, 0)), sblk],
        out_shape=[jax.ShapeDtypeStruct((n_steps, groups, HD, LANES), f32),
                   jax.ShapeDtypeStruct((LANES, groups * STATE_ROWS), f32)],
        scratch_shapes=[pltpu.VMEM((STATE_ROWS, LANES), f32)],
        compiler_params=pltpu.CompilerParams(dimension_semantics=("arbitrary", "arbitrary"),
                                             vmem_limit_bytes=48 * MIB),
        name="wkv_scan",
    )(tiles, rows, state)


def _wkv_norm(y, v, bonus, lnw, lnb):
    mean = jnp.sum(y, axis=0, keepdims=True) * (1.0 / HD)
    d = y - mean
    var = jnp.sum(d * d, axis=0, keepdims=True) * (1.0 / HD)
    return d * lax.rsqrt(var + GN_EPS) * lnw + lnb + bonus * v


def _wkv_post_kernel(y_ref, v_ref, rows_ref, c_ref, o_ref, *, batch, chains_are_heads):
    steps = y_ref.shape[0]

    def normed(t, g):
        return _wkv_norm(y_ref[t, g], v_ref[t, g, 0], rows_ref[t, g, ROW_BONUS:ROW_BONUS + 1, :],
                         c_ref[g, CONST_LNW], c_ref[g, CONST_LNB])

    if chains_are_heads:
        def tstep(t, _):
            r0 = pl.multiple_of(t * batch, batch)
            for hp in range(D // LANES):
                two = jnp.concatenate([normed(t, 2 * hp), normed(t, 2 * hp + 1)], axis=0)
                o_ref[pl.ds(r0, batch), hp * LANES:(hp + 1) * LANES] = two.T
            return 0

        lax.fori_loop(0, steps, tstep, 0)
    else:
        half = LANES // 2
        lane = lax.broadcasted_iota(jnp.int32, (HD, LANES), 1)

        def pair(tp, _):
            o0 = normed(2 * tp, 0)
            o1 = normed(2 * tp + 1, 0)
            top = jnp.where(lane < half, o0, pltpu.roll(o1, half, axis=1))
            bot = jnp.where(lane < half, pltpu.roll(o0, half, axis=1), o1)
            zz = jnp.concatenate([top, bot], axis=0).T
            for tt in range(2):
                row = pl.multiple_of(tp * 2 * SUBLANES + tt * SUBLANES, SUBLANES)
                for hp in range(D // LANES):
                    src = tt * HD + hp * SUBLANES
                    o_ref[pl.ds(row, SUBLANES), hp * LANES:(hp + 1) * LANES] = zz[src:src + SUBLANES, :]
            return 0

        lax.fori_loop(0, steps // 2, pair, 0, unroll=4)


def _outproj_kernel(ml_ref, g_ref, gr_ref, x_ref, yp_ref, vp_ref, rp_ref, cp_ref, ysm_ref, vs_ref, rs_ref, cs_ref,
                    wo_ref, n2_ref, rh_ref, rl_ref, rb_ref,
                    x1_ref, t_ref, selt_ref, seln_ref, op_ref, *, prompt_tiles, prompt_batch, sample_batch):
    tm = x_ref.shape[0]
    i = pl.program_id(0)

    @pl.when(i < prompt_tiles)
    def _():
        _wkv_post_kernel(yp_ref, vp_ref, rp_ref, cp_ref, op_ref, batch=prompt_batch, chains_are_heads=False)

    @pl.when(i >= prompt_tiles)
    def _():
        _wkv_post_kernel(ysm_ref, vs_ref, rs_ref, cs_ref, op_ref, batch=sample_batch, chains_are_heads=True)

    merged = ml_ref[...] + jax.nn.sigmoid(gr_ref[...]) * (op_ref[...] * g_ref[...])
    x1 = x_ref[...] + jnp.dot(merged.astype(bf16), wo_ref[...], preferred_element_type=f32)
    x1_ref[...] = x1
    t = x1 * lax.rsqrt(jnp.mean(x1 * x1, axis=-1, keepdims=True) + NORM_EPS) * n2_ref[...]
    th = t.astype(bf16)
    t_ref[...] = th
    tl = (t - th.astype(f32)).astype(bf16)
    nt_dims = (((1,), (1,)), ((), ()))
    lg = (lax.dot_general(rh_ref[...], th, nt_dims, preferred_element_type=f32)
          + lax.dot_general(rh_ref[...], tl, nt_dims, preferred_element_type=f32)
          + lax.dot_general(rl_ref[...], th, nt_dims, preferred_element_type=f32)) + rb_ref[...]
    row = lax.broadcasted_iota(jnp.int32, (EPG, tm), 0).astype(f32)
    neg = jnp.float32(-jnp.inf)
    glog = jnp.where(row < N_GROUPS, lg[0:EPG], neg)
    ge = jnp.exp(glog - jnp.max(glog, axis=0, keepdims=True))
    pg = ge / jnp.sum(ge, axis=0, keepdims=True)
    p_top = jnp.max(pg, axis=0, keepdims=True)
    g_idx = jnp.min(jnp.where(pg == p_top, row, EPG), axis=0, keepdims=True)
    le = jnp.zeros((EPG, tm), f32)
    for g in range(N_GROUPS):
        le = jnp.where(g_idx == g, lg[EPG * (g + 1):EPG * (g + 2)], le)
    qe = jnp.exp(le - jnp.max(le, axis=0, keepdims=True))
    q = qe / jnp.sum(qe, axis=0, keepdims=True)
    q1 = jnp.max(q, axis=0, keepdims=True)
    i1 = jnp.min(jnp.where(q == q1, row, EPG), axis=0, keepdims=True)
    qm = jnp.where(row == i1, -1.0, q)
    q2 = jnp.max(qm, axis=0, keepdims=True)
    i2 = jnp.min(jnp.where(qm == q2, row, EPG), axis=0, keepdims=True)
    qs = q1 + q2
    sel = jnp.concatenate([g_idx * EPG + i1, g_idx * EPG + i2, q1 / qs * p_top, q2 / qs * p_top,
                           jnp.zeros((SUBLANES - 4, tm), f32)], axis=0)
    selt_ref[...] = sel
    sel_pad = jnp.concatenate([sel, jnp.zeros((LANES - SUBLANES, tm), f32)], axis=0)
    for c in range(tm // LANES):
        seln_ref[c * LANES:(c + 1) * LANES, :] = sel_pad[:, c * LANES:(c + 1) * LANES].T


def _outproj(ml, g, proj, x, scan_p, scan_s, wo, n2, rh, rl, rb, prompt_batch, sample_batch, tm=512):
    n = x.shape[0]
    y_p, tiles_p, rows_p, consts_p = scan_p
    y_s, tiles_s, rows_s, consts_s = scan_s
    n_p = y_p.shape[0] * prompt_batch
    sp, ss = tm // prompt_batch, tm // sample_batch
    prompt_tiles = n_p // tm
    sample_tiles = (n - n_p) // tm
    assert n_p % tm == 0 and sp % 2 == 0 and y_s.shape[0] == sample_tiles * ss
    pidx = lambda i: jnp.minimum(i, prompt_tiles - 1)
    sidx = lambda i: jnp.clip(i - prompt_tiles, 0, sample_tiles - 1)
    groups = y_s.shape[1]
    blk = pl.BlockSpec((tm, D), lambda i: (i, 0))
    kern = functools.partial(_outproj_kernel, prompt_tiles=prompt_tiles, prompt_batch=prompt_batch,
                             sample_batch=sample_batch)
    return pl.pallas_call(
        kern,
        grid=(n // tm,),
        in_specs=[blk, blk, pl.BlockSpec((tm, D), lambda i: (i, COL_GR)), blk,
                  pl.BlockSpec((sp, 1, HD, LANES), lambda i: (pidx(i), 0, 0, 0)),
                  pl.BlockSpec((sp, 1, 1, HD, LANES), lambda i: (pidx(i), 0, TILE_V, 0, 0)),
                  pl.BlockSpec((sp, 1, SUBLANES, LANES), lambda i: (pidx(i), 0, 0, 0)),
                  _full((1, 5, HD, LANES)),
                  pl.BlockSpec((ss, groups, HD, LANES), lambda i: (sidx(i), 0, 0, 0)),
                  pl.BlockSpec((ss, groups, 1, HD, LANES), lambda i: (sidx(i), 0, TILE_V, 0, 0)),
                  pl.BlockSpec((ss, groups, SUBLANES, LANES), lambda i: (sidx(i), 0, 0, 0)),
                  _full((groups, 5, HD, LANES)),
                  _full((D, D)), _full((1, D)), _full((ROUTER_ROWS, D)), _full((ROUTER_ROWS, D)),
                  _full((ROUTER_ROWS, 1))],
        out_specs=[blk, blk, pl.BlockSpec((SUBLANES, tm), lambda i: (0, i)),
                   pl.BlockSpec((tm, LANES), lambda i: (i, 0))],
        out_shape=[jax.ShapeDtypeStruct((n, D), f32), jax.ShapeDtypeStruct((n, D), bf16),
                   jax.ShapeDtypeStruct((SUBLANES, n), f32), jax.ShapeDtypeStruct((n, LANES), f32)],
        scratch_shapes=[pltpu.VMEM((tm, D), f32)],
        compiler_params=pltpu.CompilerParams(dimension_semantics=("arbitrary",), vmem_limit_bytes=56 * MIB),
        name="outproj_router",
    )(ml, g, proj, x, y_p, tiles_p, rows_p, consts_p, y_s, tiles_s, rows_s, consts_s, wo, n2, rh, rl, rb)


MOE_CHUNK = 32
MOE_ROWS = 128
MOE_GATHER = 512
MOE_EXPERTS_PER_STEP = 8
MOE_OVERRUN = MOE_ROWS
MOE_CAP = -(-(2 * TOKEN_TILE + N_EXPERTS * (MOE_CHUNK - 1) + MOE_OVERRUN) // MOE_GATHER) * MOE_GATHER
SEL_E1, SEL_E2, SEL_C1, SEL_C2 = range(4)


def _route_meta_kernel(seln_ref, o_ref):
    tm = seln_ref.shape[0]
    lane = lax.broadcasted_iota(jnp.int32, (tm, LANES), 1).astype(f32)
    sel = seln_ref[...]
    hit = (lane == sel[:, SEL_E1:SEL_E1 + 1]) | (lane == sel[:, SEL_E2:SEL_E2 + 1])
    cnt = jnp.sum(jnp.where(hit, 1.0, 0.0), axis=0, keepdims=True)
    nchunk = jnp.floor((cnt + (MOE_CHUNK - 1)) * (1.0 / MOE_CHUNK))
    upper = (lax.broadcasted_iota(jnp.int32, (LANES, LANES), 0)
             < lax.broadcasted_iota(jnp.int32, (LANES, LANES), 1))
    offs = jnp.dot(jnp.broadcast_to(nchunk, (SUBLANES, LANES)).astype(bf16), jnp.where(upper, 1.0, 0.0).astype(bf16),
                   preferred_element_type=f32)
    row = lax.broadcasted_iota(jnp.int32, (SUBLANES, LANES), 0)
    o_ref[0] = jnp.where(row == 0, nchunk, jnp.where(row == 1, offs, 0.0)).astype(jnp.int32)


def _route_meta(seln, tm):
    tiles = seln.shape[0] // tm
    return pl.pallas_call(
        _route_meta_kernel,
        grid=(tiles,),
        in_specs=[pl.BlockSpec((tm, LANES), lambda i: (i, 0))],
        out_specs=pl.BlockSpec((1, SUBLANES, LANES), lambda i: (i, 0, 0)),
        out_shape=jax.ShapeDtypeStruct((tiles, SUBLANES, LANES), jnp.int32),
        compiler_params=pltpu.CompilerParams(dimension_semantics=("arbitrary",)),
        name="route_meta",
    )(seln)


def _moe_experts_kernel(nch_ref, off_ref, t_ref, selt_ref, tri_ref, wgu_ref, wd_ref, ys_ref, pos_ref,
                        xs_ref, cw_ref):
    i = pl.program_id(0)
    e = pl.program_id(1)
    tm = t_ref.shape[0]

    @pl.when(e == 0)
    def _():
        ys_ref[...] = jnp.zeros_like(ys_ref)
        e1 = selt_ref[SEL_E1:SEL_E1 + 1, :]
        e2 = selt_ref[SEL_E2:SEL_E2 + 1, :]
        erow = lax.broadcasted_iota(jnp.int32, (N_EXPERTS, tm), 0).astype(f32)
        oh1 = jnp.where(erow == e1, 1.0, 0.0)
        oh2 = jnp.where(erow == e2, 1.0, 0.0)
        before1 = jnp.dot(oh1.astype(bf16), tri_ref[...], preferred_element_type=f32)
        before2 = jnp.dot(oh2.astype(bf16), tri_ref[...], preferred_element_type=f32)
        cnt1 = jnp.sum(oh1, axis=1, keepdims=True)
        cnt2 = jnp.sum(oh2, axis=1, keepdims=True)
        nchunk = jnp.floor((cnt1 + cnt2 + (MOE_CHUNK - 1)) * (1.0 / MOE_CHUNK))
        lower = (lax.broadcasted_iota(jnp.int32, (N_EXPERTS, LANES), 1)
                 < lax.broadcasted_iota(jnp.int32, (N_EXPERTS, LANES), 0))
        nchunk_rows = jnp.concatenate([jnp.broadcast_to(nchunk, (N_EXPERTS, LANES)),
                                       jnp.zeros((LANES - N_EXPERTS, LANES), f32)], axis=0)
        start = jnp.dot(jnp.where(lower, 1.0, 0.0).astype(bf16), nchunk_rows.astype(bf16),
                        preferred_element_type=f32)[:, 0:1] * MOE_CHUNK
        pos1 = jnp.sum(oh1 * (start + before1), axis=0, keepdims=True)
        pos2 = jnp.sum(oh2 * (start + cnt1 + before2), axis=0, keepdims=True)
        pos_ref[...] = jnp.concatenate([pos1, pos2, jnp.zeros((SUBLANES - 2, tm), f32)], axis=0)

    per_step = wgu_ref.shape[0]

    def gathered_chunks(step):
        last = i * N_EXPERTS + jnp.maximum(step * per_step + per_step - 1, 0)
        end_row = (off_ref[last] + nch_ref[last]) * MOE_CHUNK + MOE_OVERRUN
        chunks = jnp.minimum((end_row + MOE_GATHER - 1) // MOE_GATHER, MOE_CAP // MOE_GATHER)
        return jnp.where(step < 0, 0, chunks)

    def gather_chunk(k, _):
        r0 = pl.multiple_of(k * MOE_GATHER, MOE_GATHER)
        ridx = (lax.broadcasted_iota(jnp.int32, (MOE_GATHER, tm), 0) + r0).astype(f32)
        p1 = ridx == pos_ref[0:1, :]
        p2 = ridx == pos_ref[1:2, :]
        onehot = jnp.where(p1 | p2, 1.0, 0.0).astype(bf16)
        xs_ref[pl.ds(r0, MOE_GATHER), :] = jnp.dot(onehot, t_ref[...], preferred_element_type=f32).astype(bf16)
        w = jnp.sum(jnp.where(p1, selt_ref[SEL_C1:SEL_C1 + 1, :], 0.0)
                    + jnp.where(p2, selt_ref[SEL_C2:SEL_C2 + 1, :], 0.0), axis=1, keepdims=True)
        cw_ref[pl.ds(r0, MOE_GATHER), :] = jnp.broadcast_to(w, (MOE_GATHER, LANES))
        return 0

    lax.fori_loop(gathered_chunks(e - 1), gathered_chunks(e), gather_chunk, 0)

    def expert_rows(u, rows):
        gu = jnp.dot(xs_ref[rows, :], wgu_ref[u], preferred_element_type=f32)
        w = cw_ref[rows, :]
        h = jax.nn.silu(gu[:, :D_EXPERT]) * gu[:, D_EXPERT:] * jnp.concatenate([w, w], axis=1)
        return jnp.dot(h.astype(bf16), wd_ref[u], preferred_element_type=f32).astype(bf16)

    per_block = MOE_ROWS // MOE_CHUNK
    experts = [i * N_EXPERTS + e * per_step + u for u in range(per_step)]
    bases = [off_ref[x] * MOE_CHUNK for x in experts]

    for u in range(per_step):
        rows = pl.ds(pl.multiple_of(bases[u], MOE_CHUNK), MOE_ROWS)
        ys_ref[0, rows, :] = expert_rows(u, rows)

    for u in range(per_step):
        seg_end = bases[u] + nch_ref[experts[u]] * MOE_CHUNK

        def more(c, _, u=u, seg_end=seg_end):
            start = bases[u] + (c + 1) * MOE_ROWS
            rows = pl.ds(pl.multiple_of(start, MOE_CHUNK), MOE_ROWS)
            ridx = lax.broadcasted_iota(jnp.int32, (MOE_ROWS, D), 0) + start
            ys_ref[0, rows, :] = jnp.where(ridx < seg_end, expert_rows(u, rows), ys_ref[0, rows, :])
            return 0

        lax.fori_loop(0, jnp.maximum(nch_ref[experts[u]] - 1, 0) // per_block, more, 0)


def _moe_experts(nch, off, t, selt, tri, wgu, wd, tm):
    n = t.shape[0]
    tiles = n // tm
    grid_spec = pltpu.PrefetchScalarGridSpec(
        num_scalar_prefetch=2,
        grid=(tiles, N_EXPERTS // MOE_EXPERTS_PER_STEP),
        in_specs=[
            pl.BlockSpec((tm, D), lambda i, e, nch, off: (i, 0)),
            pl.BlockSpec((SUBLANES, tm), lambda i, e, nch, off: (0, i)),
            pl.BlockSpec((tm, tm), lambda i, e, nch, off: (0, 0), pipeline_mode=pl.Buffered(1)),
            pl.BlockSpec((MOE_EXPERTS_PER_STEP, D, 2 * D_EXPERT), lambda i, e, nch, off: (e, 0, 0)),
            pl.BlockSpec((MOE_EXPERTS_PER_STEP, D_EXPERT, D), lambda i, e, nch, off: (e, 0, 0)),
        ],
        out_specs=[pl.BlockSpec((1, MOE_CAP, D), lambda i, e, nch, off: (i, 0, 0)),
                   pl.BlockSpec((SUBLANES, tm), lambda i, e, nch, off: (0, i))],
        scratch_shapes=[pltpu.VMEM((MOE_CAP, D), bf16), pltpu.VMEM((MOE_CAP, LANES), f32)],
    )
    return pl.pallas_call(
        _moe_experts_kernel,
        grid_spec=grid_spec,
        out_shape=[jax.ShapeDtypeStruct((tiles, MOE_CAP, D), bf16), jax.ShapeDtypeStruct((SUBLANES, n), f32)],
        compiler_params=pltpu.CompilerParams(dimension_semantics=("arbitrary", "arbitrary"),
                                             vmem_limit_bytes=56 * MIB),
        name="moe_experts",
    )(nch, off, t, selt, tri, wgu, wd)


def _moe_combine_kernel(nch_ref, off_ref, ysort_ref, pos_ref, x1_ref, fn_ref, yp_ref, ys_ref, acc_ref, yf_ref,
                        *, prompt_tiles, parts):
    i = pl.program_id(0)
    part = pl.program_id(1)
    rows = x1_ref.shape[0]
    bp, tsteps = yp_ref.shape[0], yp_ref.shape[1]
    bs = ys_ref.shape[0]
    ssteps = ys_ref.shape[1] // parts
    used_rows = (off_ref[i * N_EXPERTS + N_EXPERTS - 1] + nch_ref[i * N_EXPERTS + N_EXPERTS - 1]) * MOE_CHUNK

    pos_t = jnp.concatenate([pos_ref[...], jnp.zeros((LANES - SUBLANES, rows), f32)], axis=0)
    pos_n = jnp.concatenate([pos_t[:, c * LANES:(c + 1) * LANES].T for c in range(rows // LANES)], axis=0)
    pos1 = pos_n[:, 0:1]
    pos2 = pos_n[:, 1:2]
    acc_ref[...] = x1_ref[...]
    for k in range(MOE_CAP // MOE_GATHER):
        @pl.when(k * MOE_GATHER < used_rows)
        def _(k=k):
            cidx = (lax.broadcasted_iota(jnp.int32, (rows, MOE_GATHER), 1) + k * MOE_GATHER).astype(f32)
            onehot = jnp.where((cidx == pos1) | (cidx == pos2), 1.0, 0.0).astype(bf16)
            acc_ref[...] += jnp.dot(onehot, ysort_ref[0, k * MOE_GATHER:(k + 1) * MOE_GATHER, :],
                                    preferred_element_type=f32)

    xo = acc_ref[...]
    y = xo * lax.rsqrt(jnp.mean(xo * xo, axis=-1, keepdims=True) + NORM_EPS) * fn_ref[...]
    for cb in range(D // LANES):
        yf_ref[cb] = y[:, cb * LANES:(cb + 1) * LANES]

    @pl.when(i < prompt_tiles)
    def _():
        for b in range(bp):
            yp_ref[b] = jnp.concatenate([yf_ref[cb, pl.ds(b, tsteps, stride=bp), :] for cb in range(D // LANES)],
                                        axis=1)

    for p in range(parts):
        @pl.when((i >= prompt_tiles) & (part == p))
        def _(p=p):
            for t in range(ssteps):
                ys_ref[:, p * ssteps + t, :] = jnp.concatenate(
                    [yf_ref[cb, t * bs:(t + 1) * bs, :] for cb in range(D // LANES)], axis=1)


def _moe_combine(nch, off, ysort, pos, x1, fn, prompt_shape, sample_shape, tm, parts=1):
    n = x1.shape[0]
    bp, tp, _ = prompt_shape
    bs, ts, _ = sample_shape
    rows = tm // parts
    assert rows % bp == 0 and tp % (rows // bp) == 0 and bs * ts == tm and ts % parts == 0
    prompt_tiles = bp * tp // tm
    last_prompt_blk = prompt_tiles * parts - 1
    kern = functools.partial(_moe_combine_kernel, prompt_tiles=prompt_tiles, parts=parts)
    grid_spec = pltpu.PrefetchScalarGridSpec(
        num_scalar_prefetch=2,
        grid=(n // tm, parts),
        in_specs=[
            pl.BlockSpec((1, MOE_CAP, D), lambda i, p, nch, off: (i, 0, 0)),
            pl.BlockSpec((SUBLANES, rows), lambda i, p, nch, off: (0, i * parts + p)),
            pl.BlockSpec((rows, D), lambda i, p, nch, off: (i * parts + p, 0)),
            pl.BlockSpec((1, D), lambda i, p, nch, off: (0, 0)),
        ],
        out_specs=[pl.BlockSpec((bp, rows // bp, D),
                                lambda i, p, nch, off: (0, jnp.minimum(i * parts + p, last_prompt_blk), 0)),
                   pl.BlockSpec((bs, ts, D), lambda i, p, nch, off: (0, 0, 0))],
        scratch_shapes=[pltpu.VMEM((rows, D), f32), pltpu.VMEM((D // LANES, rows, LANES), f32)],
    )
    return pl.pallas_call(
        kern,
        grid_spec=grid_spec,
        out_shape=[jax.ShapeDtypeStruct(prompt_shape, f32), jax.ShapeDtypeStruct(sample_shape, f32)],
        compiler_params=pltpu.CompilerParams(dimension_semantics=("arbitrary", "arbitrary"),
                                             vmem_limit_bytes=56 * MIB),
        name="moe_combine",
    )(nch, off, ysort, pos, x1, fn)


def _chain_tiles_prompt(vec, batch):
    t = vec.reshape(HEADS // 2, 2, HD)
    t = jnp.transpose(t, (2, 1, 0))
    return jnp.broadcast_to(t[..., None], (HD, 2, HEADS // 2, batch)).reshape(HD, LANES)


def _chain_tiles_sample(vec):
    return jnp.broadcast_to(vec.reshape(HEADS, HD)[..., None], (HEADS, HD, LANES))


def kernel(x_prompt, x_sample, state_conv, state_lru, state_shift, state_wkv, norm1, w_in, conv_w, conv_b, lru_wx, lru_bx, lru_wa, lru_ba, lru_a_param, rwkv_mu, rwkv_w0, rwkv_w2, rwkv_a0, rwkv_a2, rwkv_g2, rwkv_k_k, rwkv_k_a, rwkv_r_k, rwkv_ln_w, rwkv_ln_b, w_out, norm2, router_group, router_group_b, router_expert, router_expert_b, exp_gate, exp_up, exp_down, final_norm):
    bp, tp, _ = x_prompt.shape
    bs, ts, _ = x_sample.shape
    assert norm1.shape[0] == 1 and bp * HEADS == LANES and bs == LANES and tp % 64 == 0 and ts % 2 == 0
    n_p, n_s = bp * tp, bs * ts
    c_rw = 2 * D
    c_lora = c_rw + 3 * D
    c_gl = c_lora + LORA

    w = w_in[0]
    w_all = jnp.concatenate([w[:, :c_lora], w[:, c_gl:], w[:, c_lora:c_gl],
                             jnp.zeros((D, LORA_PAD - LORA), f32)], axis=1).astype(bf16)
    mu = rwkv_mu[0]
    mu_all = jnp.concatenate([mu[:3 * D], mu[3 * D:], jnp.zeros((LORA_PAD - LORA,), f32)])[None]
    w2p = jnp.concatenate([rwkv_w2[0], jnp.zeros((LANES - LORA_W, D), f32)]).astype(bf16)
    a2p = jnp.concatenate([jnp.zeros((LORA_W, D), f32), rwkv_a2[0]]).astype(bf16)
    g2p = jnp.concatenate([rwkv_g2[0], jnp.zeros((2 * LANES - LORA_G, D), f32)]).astype(bf16)
    wxa = jnp.concatenate([lru_wx[0], lru_wa[0]], axis=-1).astype(bf16)
    row = lambda v: v.reshape(1, -1)
    rk = rwkv_r_k[0].reshape(D)
    cvecs = (rwkv_k_k[0], rwkv_k_a[0], rk, rwkv_ln_w[0], rwkv_ln_b[0])
    consts_p = jnp.stack([_chain_tiles_prompt(v, bp) for v in cvecs])[None]
    consts_s = jnp.stack([_chain_tiles_sample(v) for v in cvecs], axis=1)
    rw = jnp.zeros((ROUTER_ROWS, D), f32)
    rw = rw.at[0:N_GROUPS].set(router_group[0].T).at[EPG:EPG + N_EXPERTS].set(router_expert[0].T)
    rh = rw.astype(bf16)
    rl = (rw - rh.astype(f32)).astype(bf16)
    rb = jnp.zeros((ROUTER_ROWS, 1), f32)
    rb = rb.at[0:N_GROUPS, 0].set(router_group_b[0]).at[EPG:EPG + N_EXPERTS, 0].set(router_expert_b[0])
    wgu = jnp.concatenate([exp_gate[0], exp_up[0]], axis=-1).astype(bf16)
    wd = exp_down[0].astype(bf16)
    wo = w_out[0].astype(bf16)

    proj, x = _inproj(x_prompt, x_sample, row(norm1[0]), w_all, tm=TOKEN_TILE)

    lru_args = (conv_w[0], row(conv_b[0]), wxa, row(lru_bx[0]), row(lru_ba[0]), row(lru_a_param[0]))
    conv_init_s = jnp.transpose(state_conv[0], (1, 0, 2)).reshape(3 * bs, D)
    ml, hl_p = _lru(proj, 0, n_p, bp, 512, True, jnp.zeros((3 * bp, D), f32), jnp.zeros((bp, D), f32), *lru_args)
    ml, hl_s = _lru(proj, n_p, n_s, bs, 512, False, conv_init_s, state_lru[0], *lru_args, dst=ml)

    sh = state_shift[0]
    shift_init_s = jnp.concatenate([sh, jnp.zeros((bs, LORA_PAD - LORA), f32)], axis=1)
    pre_args = (mu_all, row(rwkv_w0[0]), row(rwkv_a0[0]), w2p, a2p, g2p)
    pre_rows = 256
    tiles_p, rows_p, g = _rwkv_pre(proj, 0, n_p, bp, pre_rows, 1, jnp.zeros((bp, 3 * D + LORA_PAD), f32),
                                   consts_p, *pre_args)
    tiles_s, rows_s, g = _rwkv_pre(proj, n_p, n_s, bs, pre_rows, HEADS, shift_init_s, consts_s, *pre_args, dst=g)

    y_p, sfin_p = _wkv_scan(tiles_p, rows_p, jnp.zeros((LANES, STATE_ROWS), f32), 64, pre_rows // bp)
    y_s, sfin_s = _wkv_scan(tiles_s, rows_s, state_wkv[0].reshape(bs, HEADS * STATE_ROWS), ts, pre_rows // bs)

    x1, t, selt, seln = _outproj(ml, g, proj, x, (y_p, tiles_p, rows_p, consts_p), (y_s, tiles_s, rows_s, consts_s),
                                 wo, row(norm2[0]), rh, rl, rb, bp, bs)
    meta = _route_meta(seln, TOKEN_TILE)
    nch = meta[:, 0, :N_EXPERTS].reshape(-1)
    off = meta[:, 1, :N_EXPERTS].reshape(-1)
    tri = jnp.triu(jnp.ones((TOKEN_TILE, TOKEN_TILE), bf16), k=1)
    ysort, pos = _moe_experts(nch, off, t, selt, tri, wgu, wd, TOKEN_TILE)
    y_prompt, y_sample = _moe_combine(nch, off, ysort, pos, x1, row(final_norm), x_prompt.shape, x_sample.shape,
                                      TOKEN_TILE)

    def last_rows(lo, hi, n_end, batch, steps):
        return proj[n_end - steps * batch:n_end, lo:hi].reshape(steps, batch, hi - lo)

    conv_p = jnp.transpose(last_rows(0, D, n_p, bp, 3), (1, 0, 2))[None]
    conv_s = jnp.transpose(last_rows(0, D, n_p + n_s, bs, 3), (1, 0, 2))[None]

    def shift_rows(n_end, batch):
        return jnp.concatenate([last_rows(2 * D, 5 * D, n_end, batch, 1)[0],
                                last_rows(7 * D, 7 * D + LORA, n_end, batch, 1)[0]], axis=1)[None]

    wkv_p = jnp.transpose(sfin_p.reshape(2, HEADS // 2, bp, HD, HD), (2, 1, 0, 3, 4)).reshape(1, bp, HEADS, HD, HD)
    wkv_s = sfin_s.reshape(1, bs, HEADS, HD, HD)
    return (y_prompt, y_sample, conv_p, hl_p[None], shift_rows(n_p, bp), wkv_p,
            conv_s, hl_s[None], shift_rows(n_p + n_s, bs), wkv_s)
```
